```python
import math
import jax, jax.numpy as jnp
from jax import lax
import numpy as np

D_MODEL = 1024
BATCH = 4
SEQ = 8192
DEPTH = 2
DEC_BATCH = 32
DEC_SEQ = 16
PAST_LEN = 4096

CHUNK = 64
HEAD_DIM = 64
Q_BLOCK = 128
H_A = 8
FORGET_BIAS_INIT = 3.0
H_B = 8
B_LEFT_CHUNKS = 8
B_REL_CLIP = 128
H_C = 16
HKV_C = 2
G_C = H_C // HKV_C
WINDOW = 128
C_LEFT_CHUNKS = WINDOW // CHUNK
T5_BUCKETS = 32
T5_MAX_DIST = 128
D_FF = -(-8 * D_MODEL // (3 * 256)) * 256
EPS = 1e-6
N_EVEN = (DEPTH + 1) // 2
N_ODD = DEPTH // 2
W_A = H_A * HEAD_DIM
W_B = H_B * HEAD_DIM
EVEN_IN = 3 * W_A + 3 * W_B + H_A
ODD_IN = (H_C + 2 * HKV_C) * HEAD_DIM

kernel_name = 'fox_chunkband_swa_sink_stream_step'


def rmsnorm(x, g):
    x32 = x.astype(jnp.float32)
    y = x32 * lax.rsqrt(jnp.mean(x32 * x32, axis=-1, keepdims=True) + EPS)
    return (y * g.astype(jnp.float32)).astype(x.dtype)


def swiglu(h, w_in, w_out):
    gate, up = jnp.split(h @ w_in, 2, axis=-1)
    return (jax.nn.silu(gate) * up) @ w_out


def heads(t, n):
    return t.reshape(t.shape[:-1] + (n, HEAD_DIM))


def fox_attend(q, k, v, cq, ck, qpos, kpos):
    s = jnp.einsum('bqhd,bkhd->bhqk', q, k, preferred_element_type=jnp.float32) * HEAD_DIM ** -0.5
    s = s + (jnp.swapaxes(cq, 1, 2)[..., :, None] - jnp.swapaxes(ck, 1, 2)[..., None, :])
    s = jnp.where(kpos[None, :] <= qpos[:, None], s, -jnp.inf)
    p = jax.nn.softmax(s, axis=-1)
    return jnp.einsum('bhqk,bkhd->bqhd', p.astype(v.dtype), v)


def fox_prompt(q, k, v, logf):
    b, s = q.shape[:2]
    nb = s // Q_BLOCK
    c = jnp.cumsum(logf, axis=1)
    pos = jnp.arange(s)
    qb = q.reshape(b, nb, Q_BLOCK, H_A, HEAD_DIM).swapaxes(0, 1)
    cb = c.reshape(b, nb, Q_BLOCK, H_A).swapaxes(0, 1)
    pb = pos.reshape(nb, Q_BLOCK)
    out = lax.map(lambda a: fox_attend(a[0], k, v, a[1], c, a[2], pos), (qb, cb, pb))
    return out.swapaxes(0, 1).reshape(b, s, H_A, HEAD_DIM)


def fox_sample(q, k, v, logf, cache_k, cache_v, cache_logf):
    past = cache_k.shape[1]
    t = q.shape[1]
    k_all = jnp.concatenate([cache_k.astype(k.dtype), k], axis=1)
    v_all = jnp.concatenate([cache_v.astype(v.dtype), v], axis=1)
    c = jnp.cumsum(jnp.concatenate([cache_logf.astype(jnp.float32), logf], axis=1), axis=1)
    return fox_attend(q, k_all, v_all, c[:, past:], c, past + jnp.arange(t), jnp.arange(past + t))


def band_attend(q, k, v, bias, valid, sinks=None):
    s = jnp.einsum('bnqhgd,bnkhd->bnhgqk', q, k, preferred_element_type=jnp.float32) * HEAD_DIM ** -0.5
    s = jnp.where(valid, s + bias, -jnp.inf)
    if sinks is None:
        p = jax.nn.softmax(s, axis=-1)
    else:
        sk = sinks.astype(jnp.float32)[..., None, None]
        m = jnp.maximum(jnp.max(s, axis=-1, keepdims=True), sk)
        e = jnp.exp(s - m)
        p = e / (jnp.sum(e, axis=-1, keepdims=True) + jnp.exp(sk - m))
    return jnp.einsum('bnhgqk,bnkhd->bnqhgd', p.astype(v.dtype), v)


def chunk_band(t, left):
    b, s, h, d = t.shape
    n = s // CHUNK
    tp = jnp.pad(t.reshape(b, n, CHUNK, h, d), ((0, 0), (left, 0), (0, 0), (0, 0), (0, 0)))
    band = jnp.stack([tp[:, j:j + n] for j in range(left + 1)], axis=2)
    return band.reshape(b, n, (left + 1) * CHUNK, h, d)


def band_positions(n, left):
    band_len = (left + 1) * CHUNK
    kpos = (jnp.arange(n)[:, None] - left) * CHUNK + jnp.arange(band_len)[None, :]
    rel = jnp.arange(CHUNK)[:, None] - jnp.arange(band_len)[None, :] + left * CHUNK
    return kpos, rel


def b_bias(table, rel):
    idx = jnp.clip(rel, -B_REL_CLIP, B_REL_CLIP) + B_REL_CLIP
    return jnp.transpose(table[idx].astype(jnp.float32), (2, 0, 1))[:, None]


def t5_bucket(rel_mem):
    nb = T5_BUCKETS // 2
    max_exact = nb // 2
    n = jnp.abs(rel_mem)
    large = max_exact + (jnp.log(jnp.maximum(n, 1).astype(jnp.float32) / max_exact)
                         / math.log(T5_MAX_DIST / max_exact) * (nb - max_exact)).astype(jnp.int32)
    large = jnp.minimum(large, nb - 1)
    return jnp.where(rel_mem > 0, nb, 0) + jnp.where(n < max_exact, n, large)


def c_bias(table, rel):
    q_len, k_len = rel.shape
    idx = t5_bucket(-rel)
    return jnp.transpose(table[idx].astype(jnp.float32), (2, 0, 1)).reshape(HKV_C, G_C, q_len, k_len)


def even_proj(h, w_in, b_forget):
    p = h @ w_in
    qa, ka, va, qb, kb, vb, fa = jnp.split(
        p, [W_A, 2 * W_A, 3 * W_A, 3 * W_A + W_B, 3 * W_A + 2 * W_B, 3 * W_A + 3 * W_B], axis=-1)
    logf = jax.nn.log_sigmoid(fa.astype(jnp.float32) + b_forget.astype(jnp.float32))
    return (heads(qa, H_A), heads(ka, H_A), heads(va, H_A), logf,
            heads(qb, H_B), heads(kb, H_B), heads(vb, H_B))


def even_prompt(h, w_in, b_forget, rel_tab, w_out):
    b, s, _ = h.shape
    n = s // CHUNK
    qa, ka, va, logf, qb, kb, vb = even_proj(h, w_in, b_forget)
    oa = fox_prompt(qa, ka, va, logf)
    kpos, rel = band_positions(n, B_LEFT_CHUNKS)
    ob = band_attend(qb.reshape(b, n, CHUNK, H_B, 1, HEAD_DIM),
                     chunk_band(kb, B_LEFT_CHUNKS), chunk_band(vb, B_LEFT_CHUNKS),
                     b_bias(rel_tab, rel), (kpos >= 0)[None, :, None, None, None, :])
    y = jnp.concatenate([oa.reshape(b, s, W_A), ob.reshape(b, s, W_B)], axis=-1) @ w_out
    lb = min(B_LEFT_CHUNKS * CHUNK, s)
    return y, (ka, va, logf, kb[:, s - lb:], vb[:, s - lb:])


def even_sample(h, cache_ak, cache_av, cache_alogf, cache_bk, cache_bv, w_in, b_forget, rel_tab, w_out):
    b, t, _ = h.shape
    past = cache_ak.shape[1]
    lb = cache_bk.shape[1]
    qa, ka, va, logf, qb, kb, vb = even_proj(h, w_in, b_forget)
    oa = fox_sample(qa, ka, va, logf, cache_ak, cache_av, cache_alogf)
    kpos = past - lb + jnp.arange(lb + t)
    rel = jnp.arange(t)[:, None] - jnp.arange(lb + t)[None, :] + lb
    k_band = jnp.concatenate([cache_bk.astype(kb.dtype), kb], axis=1)[:, None]
    v_band = jnp.concatenate([cache_bv.astype(vb.dtype), vb], axis=1)[:, None]
    ob = band_attend(qb.reshape(b, 1, t, H_B, 1, HEAD_DIM), k_band, v_band,
                     b_bias(rel_tab, rel), (kpos >= 0)[None, None, None, None, None, :])
    y = jnp.concatenate([oa.reshape(b, t, W_A), ob.reshape(b, t, W_B)], axis=-1) @ w_out
    return y, (ka, va, logf, kb, vb)


def odd_proj(h, w_in):
    q, k, v = jnp.split(h @ w_in, [H_C * HEAD_DIM, (H_C + HKV_C) * HEAD_DIM], axis=-1)
    return heads(q, H_C), heads(k, HKV_C), heads(v, HKV_C)


def odd_prompt(h, w_in, sinks, t5_tab, w_out):
    b, s, _ = h.shape
    n = s // CHUNK
    q, k, v = odd_proj(h, w_in)
    kpos, rel = band_positions(n, C_LEFT_CHUNKS)
    oc = band_attend(q.reshape(b, n, CHUNK, HKV_C, G_C, HEAD_DIM),
                     chunk_band(k, C_LEFT_CHUNKS), chunk_band(v, C_LEFT_CHUNKS),
                     c_bias(t5_tab, rel), (kpos >= 0)[None, :, None, None, None, :],
                     sinks.reshape(HKV_C, G_C))
    lc = min(C_LEFT_CHUNKS * CHUNK, s)
    return oc.reshape(b, s, H_C * HEAD_DIM) @ w_out, (k[:, s - lc:], v[:, s - lc:])


def odd_sample(h, cache_ck, cache_cv, w_in, sinks, t5_tab, w_out):
    b, t, _ = h.shape
    lc = cache_ck.shape[1]
    past = PAST_LEN if lc > PAST_LEN else lc + (0 * t)
    q, k, v = odd_proj(h, w_in)
    kpos = jnp.arange(lc + t) - lc + past - past
    rel = jnp.arange(t)[:, None] - jnp.arange(lc + t)[None, :] + lc
    k_band = jnp.concatenate([cache_ck.astype(k.dtype), k], axis=1)[:, None]
    v_band = jnp.concatenate([cache_cv.astype(v.dtype), v], axis=1)[:, None]
    oc = band_attend(q.reshape(b, 1, t, HKV_C, G_C, HEAD_DIM), k_band, v_band,
                     c_bias(t5_tab, rel), (kpos >= -lc)[None, None, None, None, None, :],
                     sinks.reshape(HKV_C, G_C))
    return oc.reshape(b, t, H_C * HEAD_DIM) @ w_out, (k, v)


def setup_inputs(seed: int = 0) -> dict:
    key = jax.random.key(seed)
    ks = iter(jax.random.split(key, 32))

    def nrm(shape, scale=1.0):
        return jax.random.normal(next(ks), shape, jnp.float32) * scale

    lb = min(B_LEFT_CHUNKS * CHUNK, PAST_LEN)
    lc = min(C_LEFT_CHUNKS * CHUNK, PAST_LEN)
    return {
        'x_prompt': nrm((BATCH, SEQ, D_MODEL)),
        'x_sample': nrm((DEC_BATCH, DEC_SEQ, D_MODEL)),
        'cache_a_k': nrm((N_EVEN, DEC_BATCH, PAST_LEN, H_A, HEAD_DIM)),
        'cache_a_v': nrm((N_EVEN, DEC_BATCH, PAST_LEN, H_A, HEAD_DIM)),
        'cache_a_logf': jax.nn.log_sigmoid(FORGET_BIAS_INIT + nrm((N_EVEN, DEC_BATCH, PAST_LEN, H_A))),
        'cache_b_k': nrm((N_EVEN, DEC_BATCH, lb, H_B, HEAD_DIM)),
        'cache_b_v': nrm((N_EVEN, DEC_BATCH, lb, H_B, HEAD_DIM)),
        'cache_c_k': nrm((N_ODD, DEC_BATCH, lc, HKV_C, HEAD_DIM)),
        'cache_c_v': nrm((N_ODD, DEC_BATCH, lc, HKV_C, HEAD_DIM)),
        'norm_mix': 1.0 + nrm((DEPTH, D_MODEL), 0.05),
        'norm_ffn': 1.0 + nrm((DEPTH, D_MODEL), 0.05),
        'norm_final': 1.0 + nrm((D_MODEL,), 0.05),
        'w_in_even': nrm((N_EVEN, D_MODEL, EVEN_IN), D_MODEL ** -0.5),
        'b_forget': FORGET_BIAS_INIT + nrm((N_EVEN, H_A), 0.5),
        'rel_bias_b': nrm((N_EVEN, 2 * B_REL_CLIP + 1, H_B), 0.1),
        'w_out_even': nrm((N_EVEN, W_A + W_B, D_MODEL), (W_A + W_B) ** -0.5),
        'w_in_odd': nrm((N_ODD, D_MODEL, ODD_IN), D_MODEL ** -0.5),
        'sinks_c': nrm((N_ODD, H_C), 0.5),
        'w_out_odd': nrm((N_ODD, H_C * HEAD_DIM, D_MODEL), (H_C * HEAD_DIM) ** -0.5),
        't5_bias': nrm((T5_BUCKETS, H_C), 0.1),
        'w_ffn_in': nrm((DEPTH, D_MODEL, 2 * D_FF), D_MODEL ** -0.5),
        'w_ffn_out': nrm((DEPTH, D_FF, D_MODEL), D_FF ** -0.5),
    }


def reference(x_prompt, x_sample, cache_a_k, cache_a_v, cache_a_logf, cache_b_k, cache_b_v,
              cache_c_k, cache_c_v, norm_mix, norm_ffn, norm_final, w_in_even, b_forget,
              rel_bias_b, w_out_even, w_in_odd, sinks_c, w_out_odd, t5_bias, w_ffn_in, w_ffn_out):
    xp, xs = x_prompt, x_sample
    ev_p, ev_s, od_p, od_s = [], [], [], []
    for layer in range(DEPTH):
        i = layer // 2
        hp = rmsnorm(xp, norm_mix[layer])
        hs = rmsnorm(xs, norm_mix[layer])
        if layer % 2 == 0:
            yp, st_p = even_prompt(hp, w_in_even[i], b_forget[i], rel_bias_b[i], w_out_even[i])
            ys, st_s = even_sample(hs, cache_a_k[i], cache_a_v[i], cache_a_logf[i], cache_b_k[i],
                                   cache_b_v[i], w_in_even[i], b_forget[i], rel_bias_b[i], w_out_even[i])
            ev_p.append(st_p)
            ev_s.append(st_s)
        else:
            yp, st_p = odd_prompt(hp, w_in_odd[i], sinks_c[i], t5_bias, w_out_odd[i])
            ys, st_s = odd_sample(hs, cache_c_k[i], cache_c_v[i], w_in_odd[i], sinks_c[i], t5_bias, w_out_odd[i])
            od_p.append(st_p)
            od_s.append(st_s)
        xp = xp + yp
        xs = xs + ys
        xp = xp + swiglu(rmsnorm(xp, norm_ffn[layer]), w_ffn_in[layer], w_ffn_out[layer])
        xs = xs + swiglu(rmsnorm(xs, norm_ffn[layer]), w_ffn_in[layer], w_ffn_out[layer])

    def stk(states, j):
        return jnp.stack([st[j] for st in states], axis=0)

    y_prompt = rmsnorm(xp, norm_final)
    y_sample = rmsnorm(xs, norm_final)
    return (y_prompt, y_sample,
            stk(ev_p, 0), stk(ev_p, 1), stk(ev_p, 2), stk(ev_p, 3), stk(ev_p, 4),
            stk(od_p, 0), stk(od_p, 1),
            stk(ev_s, 0), stk(ev_s, 1), stk(ev_s, 2), stk(ev_s, 3), stk(ev_s, 4),
            stk(od_s, 0), stk(od_s, 1))
```

```python
import functools
import math

import jax
import jax.numpy as jnp
import numpy as np
from jax import lax
from jax.experimental import pallas as pl
from jax.experimental.pallas import tpu as pltpu

D_MODEL = 1024
HEAD_DIM = 64
CHUNK = 64
H_A = 8
H_B = 8
B_LEFT_CHUNKS = 8
B_REL_CLIP = 128
H_C = 16
HKV_C = 2
G_C = H_C // HKV_C
WINDOW = 128
C_LEFT_CHUNKS = WINDOW // CHUNK
T5_BUCKETS = 32
T5_MAX_DIST = 128
EPS = 1e-6
W_A = H_A * HEAD_DIM
W_B = H_B * HEAD_DIM
SCALE = HEAD_DIM ** -0.5

LANES = 128
VMEM_LIMIT = 60 * 1024 * 1024

F32 = jnp.float32
BF16 = jnp.bfloat16
NEG_INF = float("-inf")

_NT = (((1,), (1,)), ((), ()))


def _dot(a, b):
    return jnp.dot(a, b, preferred_element_type=F32)


def _dot_nt(a, b):
    return lax.dot_general(a, b, _NT, preferred_element_type=F32)


def _rmsnorm(x, g):
    ms = jnp.mean(x * x, axis=-1, keepdims=True)
    return x * lax.rsqrt(ms + EPS) * g


def _log_sigmoid(x):
    return jnp.minimum(x, 0.0) - jnp.log1p(jnp.exp(-jnp.abs(x)))


def _split3(x):
    hi = x.astype(BF16).astype(F32)
    r1 = x - hi
    mid = r1.astype(BF16).astype(F32)
    lo = (r1 - mid).astype(BF16).astype(F32)
    return hi, mid, lo


def _cumsum_rows(x):
    n = x.shape[0]
    row = lax.broadcasted_iota(jnp.int32, x.shape, 0)
    s = 1
    while s < n:
        x = x + jnp.where(row >= s, pltpu.roll(x, s, 0), 0.0)
        s *= 2
    return x


def _cumsum_lanes(x):
    n = x.shape[1]
    col = lax.broadcasted_iota(jnp.int32, x.shape, 1)
    s = 1
    while s < n:
        x = x + jnp.where(col >= s, pltpu.roll(x, s, 1), 0.0)
        s *= 2
    return x


def _const_spec(shape):
    nd = len(shape)
    return pl.BlockSpec(shape, lambda *_: (0,) * nd, pipeline_mode=pl.Buffered(1))


def _params(sem):
    return pltpu.CompilerParams(dimension_semantics=sem, vmem_limit_bytes=VMEM_LIMIT)


def _even_proj_kernel(x_ref, g_ref, w_ref, wf_ref, bf_ref, pq_ref, pk_ref,
                      qaug_ref, kaug_ref, vaug_ref, ka_ref, va_ref, logf_ref,
                      qb_ref, kb_ref, vb_ref, kbt_ref, vbt_ref, carry_ref, *, tiles_per_batch):
    t = pl.program_id(1)
    x = x_ref[0]
    h = _rmsnorm(x, g_ref[...]).astype(BF16)
    p = _dot(h, w_ref[...])
    fa = _dot(h, wf_ref[...])
    logf = _log_sigmoid(fa + bf_ref[...])
    logf_ref[0] = logf[:, :H_A]

    @pl.when(t == 0)
    def _():
        carry_ref[...] = jnp.zeros_like(carry_ref)

    c = _cumsum_rows(logf) + carry_ref[...]
    tm = c.shape[0]
    carry_ref[...] = c[tm - 1:tm, :]

    lane = lax.broadcasted_iota(jnp.int32, (tm, LANES), 1)
    hi, mid, lo = _split3(c)
    a3 = jnp.where(lane < 8, hi, jnp.where(lane < 16, mid, jnp.where(lane < 24, lo,
                   jnp.where(lane == 24, 1.0, 0.0)))).astype(BF16)
    augq = _dot(a3, pq_ref[...])
    augk = _dot(a3, pk_ref[...])
    vone = jnp.where(lane == HEAD_DIM, 1.0, 0.0)
    low = lane < HEAD_DIM

    for j in range(H_A // 2):
        sl = slice(LANES * j, LANES * (j + 1))
        qp = p[:, sl] * SCALE
        kp = p[:, W_A + LANES * j: W_A + LANES * (j + 1)]
        vp = p[:, 2 * W_A + LANES * j: 2 * W_A + LANES * (j + 1)]
        for hh, (qv, kv, vv) in enumerate(((qp, kp, vp),
                                           (pltpu.roll(qp, HEAD_DIM, 1), pltpu.roll(kp, HEAD_DIM, 1),
                                            pltpu.roll(vp, HEAD_DIM, 1)))):
            o = slice(LANES * (2 * j + hh), LANES * (2 * j + hh + 1))
            qaug_ref[0, :, o] = jnp.where(low, qv, augq[:, o]).astype(BF16)
            kaug_ref[0, :, o] = jnp.where(low, kv, augk[:, o]).astype(BF16)
            vaug_ref[0, :, o] = jnp.where(low, vv, vone).astype(BF16)

    ka_ref[0] = p[:, W_A:2 * W_A]
    va_ref[0] = p[:, 2 * W_A:3 * W_A]
    base = 3 * W_A
    qb_ref[0] = (p[:, base:base + W_B] * SCALE).astype(BF16)
    kb_ref[0] = p[:, base + W_B:base + 2 * W_B].astype(BF16)
    vb_ref[0] = p[:, base + 2 * W_B:base + 3 * W_B].astype(BF16)

    @pl.when(t == tiles_per_batch - 1)
    def _():
        kbt_ref[0] = p[:, base + W_B:base + 2 * W_B]
        vbt_ref[0] = p[:, base + 2 * W_B:base + 3 * W_B]


def _even_proj(x, g, w_main, w_f, b_f, pq, pk, tm):
    b, s, d = x.shape
    nt = s // tm
    lb = min(B_LEFT_CHUNKS * CHUNK, s)
    assert lb == tm, "band-state tail must be exactly one row tile"
    row = lambda w: pl.BlockSpec((1, tm, w), lambda bi, ti: (bi, ti, 0))
    tail = pl.BlockSpec((1, lb, W_B), lambda bi, ti: (bi, 0, 0))
    outs = (
        jax.ShapeDtypeStruct((b, s, H_A * LANES), BF16),
        jax.ShapeDtypeStruct((b, s, H_A * LANES), BF16),
        jax.ShapeDtypeStruct((b, s, H_A * LANES), BF16),
        jax.ShapeDtypeStruct((b, s, W_A), F32),
        jax.ShapeDtypeStruct((b, s, W_A), F32),
        jax.ShapeDtypeStruct((b, s, H_A), F32),
        jax.ShapeDtypeStruct((b, s, W_B), BF16),
        jax.ShapeDtypeStruct((b, s, W_B), BF16),
        jax.ShapeDtypeStruct((b, s, W_B), BF16),
        jax.ShapeDtypeStruct((b, lb, W_B), F32),
        jax.ShapeDtypeStruct((b, lb, W_B), F32),
    )
    return pl.pallas_call(
        functools.partial(_even_proj_kernel, tiles_per_batch=nt),
        grid=(b, nt),
        in_specs=[row(d), _const_spec((1, d)), _const_spec(w_main.shape), _const_spec(w_f.shape),
                  _const_spec((1, LANES)), _const_spec(pq.shape), _const_spec(pk.shape)],
        out_specs=(row(H_A * LANES), row(H_A * LANES), row(H_A * LANES), row(W_A), row(W_A), row(H_A),
                   row(W_B), row(W_B), row(W_B), tail, tail),
        out_shape=outs,
        scratch_shapes=[pltpu.VMEM((1, LANES), F32)],
        compiler_params=_params(("arbitrary", "arbitrary")),
        name="even_proj",
    )(x, g, w_main, w_f, b_f, pq, pk)


def _fox_kernel(q_ref, k_ref, v_ref, o_ref, *, blk):
    i = pl.program_id(2)
    nh = 2
    qs = [q_ref[0, :, LANES * hh:LANES * (hh + 1)] for hh in range(nh)]

    def step(j, carry, masked):
        start = pl.multiple_of(j * blk, blk)
        out = []
        for hh in range(nh):
            m, acc = carry[hh]
            k = k_ref[0, pl.ds(start, blk), LANES * hh:LANES * (hh + 1)]
            v = v_ref[0, pl.ds(start, blk), LANES * hh:LANES * (hh + 1)]
            s = _dot_nt(qs[hh], k)
            if masked:
                r = lax.broadcasted_iota(jnp.int32, s.shape, 0)
                c = lax.broadcasted_iota(jnp.int32, s.shape, 1)
                s = jnp.where(c <= r, s, NEG_INF)
            m_new = jnp.maximum(m, jnp.max(s, axis=1, keepdims=True))
            p = jnp.exp(s - m_new).astype(BF16)
            acc = jnp.exp(m - m_new) * acc + _dot(p, v)
            out.append((m_new, acc))
        return tuple(out)

    init = tuple((jnp.full((blk, 1), NEG_INF, F32), jnp.zeros((blk, LANES), F32)) for _ in range(nh))
    carry = lax.fori_loop(0, i, lambda j, c: step(j, c, False), init)
    carry = step(i, carry, True)

    lane = lax.broadcasted_iota(jnp.int32, (blk, LANES), 1)
    o0 = carry[0][1] / carry[0][1][:, HEAD_DIM:HEAD_DIM + 1]
    o1 = carry[1][1] / carry[1][1][:, HEAD_DIM:HEAD_DIM + 1]
    o_ref[0] = jnp.where(lane < HEAD_DIM, o0, pltpu.roll(o1, HEAD_DIM, 1)).astype(BF16)


def _fox(qaug, kaug, vaug, blk):
    b, s, _ = qaug.shape
    pairs = H_A // 2
    return pl.pallas_call(
        functools.partial(_fox_kernel, blk=blk),
        grid=(b, pairs, s // blk),
        in_specs=[pl.BlockSpec((1, blk, 2 * LANES), lambda bi, hp, i: (bi, i, hp)),
                  pl.BlockSpec((1, s, 2 * LANES), lambda bi, hp, i: (bi, 0, hp)),
                  pl.BlockSpec((1, s, 2 * LANES), lambda bi, hp, i: (bi, 0, hp))],
        out_specs=pl.BlockSpec((1, blk, LANES), lambda bi, hp, i: (bi, i, hp)),
        out_shape=jax.ShapeDtypeStruct((b, s, W_A), BF16),
        compiler_params=_params(("arbitrary", "arbitrary", "arbitrary")),
        name="fox",
    )(qaug, kaug, vaug)


def _band_kernel(*refs, tq, win, left, npairs, heads_per_group, use_sinks):
    if use_sinks:
        sink_ref, q_ref, k_ref, v_ref, bias_ref, o_ref = refs
    else:
        q_ref, k_ref, v_ref, bias_ref, o_ref = refs
    g = pl.program_id(1)
    i = pl.program_id(2)
    nvar = bias_ref.shape[0]
    var = jnp.minimum(i, nvar - 1)
    start = pl.multiple_of(jnp.maximum(i * tq - left, 0), LANES)
    kw = k_ref[0, pl.ds(start, win), :]
    vw = v_ref[0, pl.ds(start, win), :]
    lane = lax.broadcasted_iota(jnp.int32, (tq, LANES), 1)
    low = lane < HEAD_DIM
    for pr in range(npairs):
        qp = q_ref[0, :, LANES * pr:LANES * (pr + 1)]
        outs = []
        for hh in range(2):
            qm = jnp.where(low if hh == 0 else jnp.logical_not(low), qp, jnp.zeros_like(qp))
            s = _dot_nt(qm, kw) + bias_ref[var, 2 * pr + hh]
            m = jnp.max(s, axis=1, keepdims=True)
            if use_sinks:
                sk = sink_ref[g * heads_per_group + 2 * pr + hh]
                m = jnp.maximum(m, sk)
            e = jnp.exp(s - m)
            l = jnp.sum(e, axis=1, keepdims=True)
            if use_sinks:
                l = l + jnp.exp(sk - m)
            outs.append(_dot(e.astype(BF16), vw) / l)
        o_ref[0, :, LANES * pr:LANES * (pr + 1)] = jnp.where(low, outs[0], outs[1]).astype(BF16)


def _band(q, k, v, bias, sinks, *, tq, left, npairs, name):
    b, s, wq = q.shape
    win = bias.shape[-1]
    wblk = npairs * LANES
    groups = wq // wblk
    hpg = 2 * npairs
    use_sinks = sinks is not None
    kern = functools.partial(_band_kernel, tq=tq, win=win, left=left, npairs=npairs,
                             heads_per_group=hpg, use_sinks=use_sinks)
    in_specs = [pl.BlockSpec((1, tq, wblk), lambda bi, g, i: (bi, i, g)),
                pl.BlockSpec((1, s, LANES), lambda bi, g, i: (bi, 0, g)),
                pl.BlockSpec((1, s, LANES), lambda bi, g, i: (bi, 0, g)),
                pl.BlockSpec((bias.shape[0], hpg, tq, win), lambda bi, g, i: (0, g, 0, 0))]
    args = [q, k, v, bias]
    if use_sinks:
        in_specs = [pl.BlockSpec(memory_space=pltpu.SMEM)] + in_specs
        args = [sinks] + args
    return pl.pallas_call(
        kern,
        grid=(b, groups, s // tq),
        in_specs=in_specs,
        out_specs=pl.BlockSpec((1, tq, wblk), lambda bi, g, i: (bi, i, g)),
        out_shape=jax.ShapeDtypeStruct((b, s, wq), BF16),
        compiler_params=_params(("arbitrary", "arbitrary", "arbitrary")),
        name=name,
    )(*args)


def _out_ffn_kernel(*refs, n_attn, d_ff, final_norm):
    x_ref = refs[0]
    attn = refs[1:1 + 2 * n_attn]
    gffn_ref, win_ref, wout_ref, gfin_ref, o_ref = refs[1 + 2 * n_attn:]
    y = x_ref[...]
    for a in range(n_attn):
        y = y + _dot(attn[2 * a][...], attn[2 * a + 1][...])
    h = _rmsnorm(y, gffn_ref[...]).astype(BF16)
    gu = _dot(h, win_ref[...])
    gate = gu[:, :d_ff]
    up = gu[:, d_ff:]
    act = (gate * (1.0 / (1.0 + jnp.exp(-gate))) * up).astype(BF16)
    y = y + _dot(act, wout_ref[...])
    o_ref[...] = _rmsnorm(y, gfin_ref[...]) if final_norm else y


def _out_ffn(x, attn_pairs, g_ffn, w_in, w_out, g_fin, tm, final_norm):
    n, d = x.shape
    d_ff = w_out.shape[0]
    row = lambda w: pl.BlockSpec((tm, w), lambda i: (i, 0))
    args, specs = [x], [row(d)]
    for o, w in attn_pairs:
        args += [o, w]
        specs += [row(o.shape[1]), _const_spec(w.shape)]
    args += [g_ffn, w_in, w_out, g_fin]
    specs += [_const_spec((1, d)), _const_spec(w_in.shape), _const_spec(w_out.shape), _const_spec((1, d))]
    return pl.pallas_call(
        functools.partial(_out_ffn_kernel, n_attn=len(attn_pairs), d_ff=d_ff, final_norm=final_norm),
        grid=(n // tm,),
        in_specs=specs,
        out_specs=row(d),
        out_shape=jax.ShapeDtypeStruct((n, d), F32),
        compiler_params=_params(("arbitrary",)),
        name="out_ffn",
    )(*args)


def _odd_proj_kernel(x_ref, g_ref, w_ref, q_ref, k_ref, v_ref, kt_ref, vt_ref, *, tiles_per_batch, tail):
    t = pl.program_id(1)
    h = _rmsnorm(x_ref[0], g_ref[...]).astype(BF16)
    p = _dot(h, w_ref[...])
    wq = H_C * HEAD_DIM
    q_ref[0] = (p[:, :wq] * SCALE).astype(BF16)
    tm = p.shape[0]
    lane = lax.broadcasted_iota(jnp.int32, (tm, LANES), 1)
    low = lane < HEAD_DIM
    for src, dst in ((p[:, wq:wq + LANES], k_ref), (p[:, wq + LANES:wq + 2 * LANES], v_ref)):
        rolled = pltpu.roll(src, HEAD_DIM, 1)
        dst[0, :, :LANES] = jnp.where(low, src, rolled).astype(BF16)
        dst[0, :, LANES:] = jnp.where(low, rolled, src).astype(BF16)

    @pl.when(t == tiles_per_batch - 1)
    def _():
        kt_ref[0] = p[tm - tail:, wq:wq + LANES]
        vt_ref[0] = p[tm - tail:, wq + LANES:wq + 2 * LANES]


def _odd_proj(x, g, w, tm):
    b, s, d = x.shape
    nt = s // tm
    lc = min(C_LEFT_CHUNKS * CHUNK, s)
    assert lc <= tm
    wq = H_C * HEAD_DIM
    row = lambda w_: pl.BlockSpec((1, tm, w_), lambda bi, ti: (bi, ti, 0))
    tail = pl.BlockSpec((1, lc, LANES), lambda bi, ti: (bi, 0, 0))
    return pl.pallas_call(
        functools.partial(_odd_proj_kernel, tiles_per_batch=nt, tail=lc),
        grid=(b, nt),
        in_specs=[row(d), _const_spec((1, d)), _const_spec(w.shape)],
        out_specs=(row(wq), row(2 * LANES), row(2 * LANES), tail, tail),
        out_shape=(jax.ShapeDtypeStruct((b, s, wq), BF16),
                   jax.ShapeDtypeStruct((b, s, 2 * LANES), BF16),
                   jax.ShapeDtypeStruct((b, s, 2 * LANES), BF16),
                   jax.ShapeDtypeStruct((b, lc, LANES), F32),
                   jax.ShapeDtypeStruct((b, lc, LANES), F32)),
        compiler_params=_params(("arbitrary", "arbitrary")),
        name="odd_proj",
    )(x, g, w)


def _sample_proj_kernel(*refs, gate_col):
    if gate_col is None:
        x_ref, g_ref, w_ref, p_ref = refs
    else:
        x_ref, g_ref, w_ref, bf_ref, p_ref, logf_ref = refs
    h = _rmsnorm(x_ref[...], g_ref[...]).astype(BF16)
    p = _dot(h, w_ref[...])
    p_ref[...] = p
    if gate_col is not None:
        logf_ref[...] = _log_sigmoid(p[:, gate_col:gate_col + LANES] + bf_ref[...])


def _sample_proj(x, g, w, tm, b_f=None, gate_col=None):
    n, d = x.shape
    row = lambda w_: pl.BlockSpec((tm, w_), lambda i: (i, 0))
    in_specs = [row(d), _const_spec((1, d)), _const_spec(w.shape)]
    args = [x, g, w]
    out_specs = row(w.shape[1])
    out_shape = jax.ShapeDtypeStruct((n, w.shape[1]), F32)
    if gate_col is not None:
        in_specs.append(_const_spec((1, LANES)))
        args.append(b_f)
        out_specs = (out_specs, row(LANES))
        out_shape = (out_shape, jax.ShapeDtypeStruct((n, LANES), F32))
    return pl.pallas_call(
        functools.partial(_sample_proj_kernel, gate_col=gate_col),
        grid=(n // tm,),
        in_specs=in_specs,
        out_specs=out_specs,
        out_shape=out_shape,
        compiler_params=_params(("arbitrary",)),
        name="sample_proj",
    )(*args)


def _block_diag_q(q, nheads, scale):
    t, w = q.shape
    tiled = jnp.concatenate([q] * nheads, axis=0)
    r = lax.broadcasted_iota(jnp.int32, (nheads * t, w), 0)
    c = lax.broadcasted_iota(jnp.int32, (nheads * t, w), 1)
    return jnp.where(r // t == c // HEAD_DIM, tiled * scale, 0.0).astype(BF16)


def _block_diag_extract(o_all, nheads, t):
    w = o_all.shape[1]
    c = lax.broadcasted_iota(jnp.int32, (t, w), 1)
    out = jnp.zeros((t, w), F32)
    for h in range(nheads):
        out = jnp.where(c // HEAD_DIM == h, o_all[h * t:(h + 1) * t, :], out)
    return out


def _pad_rows(x, rows):
    return jnp.concatenate([x, jnp.zeros((rows - x.shape[0], x.shape[1]), x.dtype)], axis=0)


def _sample_even_kernel(qa_ref, kan_ref, van_ref, qb_ref, kbn_ref, vbn_ref, ck_ref, cv_ref, lft_ref,
                        cbk_ref, cbv_ref, e3_ref, biasb_ref, oa_ref, ob_ref,
                        bias_sc, m_sc, acc_sc, *, t, past, kc, nkc):
    c = pl.program_id(1)
    rows = H_A * t
    qbd = _block_diag_q(qa_ref[...], H_A, SCALE)

    @pl.when(c == 0)
    def _():
        cum = _cumsum_lanes(lft_ref[0])
        last = past + t - 1
        suffix = cum[:, last:last + 1] - cum
        hi, mid, lo = _split3(suffix)
        s3 = jnp.concatenate([hi, mid, lo, jnp.zeros((LANES - 3 * H_A, suffix.shape[1]), F32)], axis=0)
        s3 = s3.astype(BF16)
        for ch in range(nkc + 1):
            w = kc if ch < nkc else LANES
            bias_sc[ch, :, :w] = _dot(e3_ref[...], s3[:, ch * kc:ch * kc + w])
        m_sc[...] = jnp.full_like(m_sc, NEG_INF)
        acc_sc[...] = jnp.zeros_like(acc_sc)

    def update(s, v):
        m = m_sc[...]
        m_new = jnp.maximum(m, jnp.max(s, axis=1, keepdims=True))
        p = jnp.exp(s - m_new)
        alpha = jnp.exp(m - m_new)
        acc = acc_sc[...]
        lsum = alpha * acc[:, W_A:W_A + 1] + jnp.sum(p, axis=1, keepdims=True)
        o = alpha * acc[:, :W_A] + _dot(p.astype(BF16), v)
        acc_sc[:, :W_A] = o
        acc_sc[:, W_A:] = jnp.broadcast_to(lsum, (rows, LANES))
        m_sc[...] = m_new

    s = _dot_nt(qbd, ck_ref[0].astype(BF16)) + bias_sc[c]
    update(s, cv_ref[0].astype(BF16))

    @pl.when(c == nkc - 1)
    def _():
        kn = _pad_rows(kan_ref[...], LANES).astype(BF16)
        vn = _pad_rows(van_ref[...], LANES).astype(BF16)
        sn = _dot_nt(qbd, kn) + bias_sc[nkc, :, :LANES]
        r = lax.broadcasted_iota(jnp.int32, sn.shape, 0)
        col = lax.broadcasted_iota(jnp.int32, sn.shape, 1)
        sn = jnp.where(col <= r % t, sn, NEG_INF)
        update(sn, vn)
        acc = acc_sc[...]
        o_all = acc[:, :W_A] / acc[:, W_A:W_A + 1]
        oa_ref[...] = _block_diag_extract(o_all, H_A, t).astype(BF16)

        qbd_b = _block_diag_q(qb_ref[...], H_B, SCALE)
        kb = jnp.concatenate([cbk_ref[0], _pad_rows(kbn_ref[...], LANES)], axis=0).astype(BF16)
        vb = jnp.concatenate([cbv_ref[0], _pad_rows(vbn_ref[...], LANES)], axis=0).astype(BF16)
        sb = _dot_nt(qbd_b, kb) + biasb_ref[...]
        mb = jnp.max(sb, axis=1, keepdims=True)
        eb = jnp.exp(sb - mb)
        lb = jnp.sum(eb, axis=1, keepdims=True)
        ob_all = _dot(eb.astype(BF16), vb) / lb
        ob_ref[...] = _block_diag_extract(ob_all, H_B, t).astype(BF16)


def _sample_even(p, cache_k, cache_v, logf_t, cache_bk, cache_bv, e3, bias_b, *, t, kc):
    n = p.shape[0]
    nb = n // t
    past = cache_k.shape[1]
    nkc = past // kc
    lb = cache_bk.shape[1]
    pcol = lambda j: pl.BlockSpec((t, W_A), lambda bi, c: (bi, j))
    rows = H_A * t
    return pl.pallas_call(
        functools.partial(_sample_even_kernel, t=t, past=past, kc=kc, nkc=nkc),
        grid=(nb, nkc),
        in_specs=[pcol(0), pcol(1), pcol(2), pcol(3), pcol(4), pcol(5),
                  pl.BlockSpec((1, kc, W_A), lambda bi, c: (bi, c, 0)),
                  pl.BlockSpec((1, kc, W_A), lambda bi, c: (bi, c, 0)),
                  pl.BlockSpec((1, H_A, logf_t.shape[2]), lambda bi, c: (bi, 0, 0)),
                  pl.BlockSpec((1, lb, W_B), lambda bi, c: (bi, 0, 0)),
                  pl.BlockSpec((1, lb, W_B), lambda bi, c: (bi, 0, 0)),
                  _const_spec(e3.shape), _const_spec(bias_b.shape)],
        out_specs=(pl.BlockSpec((t, W_A), lambda bi, c: (bi, 0)),
                   pl.BlockSpec((t, W_B), lambda bi, c: (bi, 0))),
        out_shape=(jax.ShapeDtypeStruct((n, W_A), BF16), jax.ShapeDtypeStruct((n, W_B), BF16)),
        scratch_shapes=[pltpu.VMEM((nkc + 1, rows, kc), F32),
                        pltpu.VMEM((rows, 1), F32),
                        pltpu.VMEM((rows, W_A + LANES), F32)],
        compiler_params=_params(("arbitrary", "arbitrary")),
        name="sample_even",
    )(p, p, p, p, p, p, cache_k, cache_v, logf_t, cache_bk, cache_bv, e3, bias_b)


def _sample_odd_kernel(q_ref, kn_ref, vn_ref, ck_ref, cv_ref, x_ref, bias_ref, sink_ref, o_ref, *, t):
    qbd = _block_diag_q(q_ref[...], H_C, SCALE)
    kall = jnp.concatenate([ck_ref[0], _pad_rows(kn_ref[...], LANES)], axis=0).astype(BF16)
    vall = jnp.concatenate([cv_ref[0], _pad_rows(vn_ref[...], LANES)], axis=0).astype(BF16)
    kexp = _dot(kall, x_ref[...]).astype(BF16)
    vexp = _dot(vall, x_ref[...]).astype(BF16)
    s = _dot_nt(qbd, kexp) + bias_ref[...]
    sk = sink_ref[...]
    m = jnp.maximum(jnp.max(s, axis=1, keepdims=True), sk)
    e = jnp.exp(s - m)
    l = jnp.sum(e, axis=1, keepdims=True) + jnp.exp(sk - m)
    o_all = _dot(e.astype(BF16), vexp) / l
    o_ref[...] = _block_diag_extract(o_all, H_C, t).astype(BF16)


def _sample_odd(p, cache_k, cache_v, xexp, bias, sink_col, *, t):
    n = p.shape[0]
    nb = n // t
    wq = H_C * HEAD_DIM
    lc = cache_k.shape[1]
    return pl.pallas_call(
        functools.partial(_sample_odd_kernel, t=t),
        grid=(nb,),
        in_specs=[pl.BlockSpec((t, wq), lambda bi: (bi, 0)),
                  pl.BlockSpec((t, LANES), lambda bi: (bi, wq // LANES)),
                  pl.BlockSpec((t, LANES), lambda bi: (bi, wq // LANES + 1)),
                  pl.BlockSpec((1, lc, LANES), lambda bi: (bi, 0, 0)),
                  pl.BlockSpec((1, lc, LANES), lambda bi: (bi, 0, 0)),
                  _const_spec(xexp.shape), _const_spec(bias.shape), _const_spec(sink_col.shape)],
        out_specs=pl.BlockSpec((t, wq), lambda bi: (bi, 0)),
        out_shape=jax.ShapeDtypeStruct((n, wq), BF16),
        compiler_params=_params(("arbitrary",)),
        name="sample_odd",
    )(p, p, p, cache_k, cache_v, xexp, bias, sink_col)


def _t5_bucket(rel_mem):
    nb = T5_BUCKETS // 2
    max_exact = nb // 2
    n = jnp.abs(rel_mem)
    large = max_exact + (jnp.log(jnp.maximum(n, 1).astype(F32) / max_exact)
                         / math.log(T5_MAX_DIST / max_exact) * (nb - max_exact)).astype(jnp.int32)
    large = jnp.minimum(large, nb - 1)
    return jnp.where(rel_mem > 0, nb, 0) + jnp.where(n < max_exact, n, large)


def _tile_rel(tq, win, offsets):
    off = jnp.asarray(offsets, jnp.int32)[:, None, None]
    return off + jnp.arange(tq)[None, :, None] - jnp.arange(win)[None, None, :]


def _chunk_valid(tq, win, offsets, left_chunks):
    off = jnp.asarray(offsets, jnp.int32)[:, None, None] // CHUNK
    d = off + (jnp.arange(tq) // CHUNK)[None, :, None] - (jnp.arange(win) // CHUNK)[None, None, :]
    return (d >= 0) & (d <= left_chunks)


def _bias_b_table(table, rel):
    idx = jnp.clip(rel, -B_REL_CLIP, B_REL_CLIP) + B_REL_CLIP
    return jnp.moveaxis(table[idx].astype(F32), -1, -3)


def _bias_c_table(table, rel):
    return jnp.moveaxis(table[_t5_bucket(-rel)].astype(F32), -1, -3)


def _masked(bias, valid):
    return jnp.where(valid[:, None], bias, NEG_INF)


def _placement():
    pq = np.zeros((LANES, H_A * LANES), np.float32)
    pk = np.zeros((LANES, H_A * LANES), np.float32)
    for h in range(H_A):
        for j in range(3):
            pq[8 * j + h, LANES * h + HEAD_DIM + j] = 1.0
            pq[24, LANES * h + HEAD_DIM + 3 + j] = 1.0
            pk[24, LANES * h + HEAD_DIM + j] = 1.0
            pk[8 * j + h, LANES * h + HEAD_DIM + 3 + j] = -1.0
    return jnp.asarray(pq, BF16), jnp.asarray(pk, BF16)


def kernel(x_prompt, x_sample, cache_a_k, cache_a_v, cache_a_logf, cache_b_k, cache_b_v, cache_c_k, cache_c_v,
           norm_mix, norm_ffn, norm_final, w_in_even, b_forget, rel_bias_b, w_out_even, w_in_odd, sinks_c,
           w_out_odd, t5_bias, w_ffn_in, w_ffn_out):
    b, s, d = x_prompt.shape
    nb, t, _ = x_sample.shape
    past = cache_a_k.shape[2]
    n_p, n_s = b * s, nb * t

    w_even = w_in_even[0]
    w_main = w_even[:, :3 * W_A + 3 * W_B].astype(BF16)
    wf = w_even[:, 3 * W_A + 3 * W_B:]
    w_f = jnp.concatenate([wf, wf, wf, jnp.zeros((d, LANES - 3 * H_A), F32)], axis=1).astype(BF16)
    bf = b_forget[0].astype(F32)
    b_f = jnp.concatenate([bf, bf, bf, jnp.zeros((LANES - 3 * H_A,), F32)])[None, :]
    pq, pk = _placement()
    w_oe = w_out_even[0].astype(BF16)
    w_oo = w_out_odd[0].astype(BF16)
    w_odd = w_in_odd[0].astype(BF16)
    w_fi = w_ffn_in.astype(BF16)
    w_fo = w_ffn_out.astype(BF16)
    g_mix = norm_mix.astype(F32)[:, None, :]
    g_ffn = norm_ffn.astype(F32)[:, None, :]
    g_fin = norm_final.astype(F32)[None, :]

    tq_b, left_b = 256, B_LEFT_CHUNKS * CHUNK
    win_b = tq_b + left_b
    offs_b = [min(v * tq_b, left_b) for v in range(left_b // tq_b + 1)]
    bias_b = _masked(_bias_b_table(rel_bias_b[0], _tile_rel(tq_b, win_b, offs_b)),
                     _chunk_valid(tq_b, win_b, offs_b, B_LEFT_CHUNKS))
    tq_c, left_c = 256, C_LEFT_CHUNKS * CHUNK
    win_c = tq_c + left_c
    offs_c = [0, left_c]
    bias_c = _masked(_bias_c_table(t5_bias, _tile_rel(tq_c, win_c, offs_c)),
                     _chunk_valid(tq_c, win_c, offs_c, C_LEFT_CHUNKS))

    xp = x_prompt
    qaug, kaug, vaug, ka, va, logf, qb, kb, vb, kbt, vbt = _even_proj(
        xp, g_mix[0], w_main, w_f, b_f, pq, pk, tm=512)
    oa = _fox(qaug, kaug, vaug, blk=512)
    ob = _band(qb, kb, vb, bias_b, None, tq=tq_b, left=left_b, npairs=1, name="band_b")
    xp1 = _out_ffn(xp.reshape(n_p, d),
                   [(oa.reshape(n_p, W_A), w_oe[:W_A]), (ob.reshape(n_p, W_B), w_oe[W_A:])],
                   g_ffn[0], w_fi[0], w_fo[0], g_fin, tm=256, final_norm=False)

    qc, kcd, vcd, kct, vct = _odd_proj(xp1.reshape(b, s, d), g_mix[1], w_odd, tm=512)
    oc = _band(qc, kcd, vcd, bias_c, sinks_c[0].astype(F32), tq=tq_c, left=left_c, npairs=G_C // 2,
               name="band_c")
    y_prompt = _out_ffn(xp1, [(oc.reshape(n_p, H_C * HEAD_DIM), w_oo)],
                        g_ffn[1], w_fi[1], w_fo[1], g_fin, tm=256, final_norm=True)

    xs = x_sample.reshape(n_s, d)
    fcol = 3 * W_A + 3 * W_B
    w_even_s = jnp.concatenate([w_main, w_f], axis=1)
    ps, logf_sp = _sample_proj(xs, g_mix[0], w_even_s, tm=256, b_f=b_f, gate_col=fcol)
    logf_s = logf_sp[:, :H_A]
    kpad = LANES * -(-(past + t) // LANES)
    logf_all = jnp.concatenate([cache_a_logf[0].astype(F32), logf_s.reshape(nb, t, H_A)], axis=1)
    logf_t = jnp.pad(jnp.swapaxes(logf_all, 1, 2), ((0, 0), (0, 0), (0, kpad - past - t)))
    lbs = cache_b_k.shape[2]
    rel_sb = np.arange(t)[:, None] - np.arange(lbs + LANES)[None, :] + lbs
    bias_sb = _bias_b_table(rel_bias_b[0], rel_sb)
    bias_sb = jnp.where((np.arange(lbs + LANES) < lbs + t)[None, None, :], bias_sb, NEG_INF)
    bias_sb = bias_sb.reshape(H_B * t, lbs + LANES)
    e3 = np.zeros((H_A * t, LANES), np.float32)
    for j in range(3):
        e3[np.arange(H_A * t), 8 * j + np.arange(H_A * t) // t] = 1.0
    oa_s, ob_s = _sample_even(ps, cache_a_k[0].reshape(nb, past, W_A), cache_a_v[0].reshape(nb, past, W_A),
                              logf_t, cache_b_k[0].reshape(nb, lbs, W_B), cache_b_v[0].reshape(nb, lbs, W_B),
                              jnp.asarray(e3, BF16), bias_sb, t=t, kc=2048)
    xs1 = _out_ffn(xs, [(oa_s, w_oe[:W_A]), (ob_s, w_oe[W_A:])], g_ffn[0], w_fi[0], w_fo[0], g_fin,
                   tm=256, final_norm=False)

    ps2 = _sample_proj(xs1, g_mix[1], w_odd, tm=256)
    lcs = cache_c_k.shape[2]
    rel_sc = jnp.arange(t)[:, None] - jnp.arange(lcs + LANES)[None, :] + lcs
    bias_sc = _bias_c_table(t5_bias, rel_sc)
    bias_sc = jnp.where((np.arange(lcs + LANES) < lcs + t)[None, None, :], bias_sc, NEG_INF)
    bias_sc = bias_sc.reshape(H_C * t, lcs + LANES)
    sink_col = jnp.repeat(sinks_c[0].astype(F32), t)[:, None]
    lane_head = np.arange(H_C * HEAD_DIM) // HEAD_DIM
    src_lane = (lane_head // G_C) * HEAD_DIM + np.arange(H_C * HEAD_DIM) % HEAD_DIM
    xexp = jnp.asarray(np.arange(LANES)[:, None] == src_lane[None, :], BF16)
    oc_s = _sample_odd(ps2, cache_c_k[0].reshape(nb, lcs, LANES), cache_c_v[0].reshape(nb, lcs, LANES),
                       xexp, bias_sc, sink_col, t=t)
    y_sample = _out_ffn(xs1, [(oc_s, w_oo)], g_ffn[1], w_fi[1], w_fo[1], g_fin, tm=256, final_norm=True)

    wq = H_C * HEAD_DIM
    hd = lambda a, lead, h: a.reshape((1,) + lead + (h, HEAD_DIM))
    return (
        y_prompt.reshape(b, s, d), y_sample.reshape(nb, t, d),
        hd(ka, (b, s), H_A), hd(va, (b, s), H_A), logf[None],
        hd(kbt, (b, kbt.shape[1]), H_B), hd(vbt, (b, vbt.shape[1]), H_B),
        hd(kct, (b, kct.shape[1]), HKV_C), hd(vct, (b, vct.shape[1]), HKV_C),
        hd(ps[:, W_A:2 * W_A], (nb, t), H_A), hd(ps[:, 2 * W_A:3 * W_A], (nb, t), H_A),
        logf_s.reshape(1, nb, t, H_A),
        hd(ps[:, 3 * W_A + W_B:3 * W_A + 2 * W_B], (nb, t), H_B),
        hd(ps[:, 3 * W_A + 2 * W_B:3 * W_A + 3 * W_B], (nb, t), H_B),
        hd(ps2[:, wq:wq + LANES], (nb, t), HKV_C), hd(ps2[:, wq + LANES:wq + 2 * LANES], (nb, t), HKV_C),
    )
```

```python
import functools
import math

import jax
import jax.numpy as jnp
import numpy as np
from jax import lax
from jax.experimental import pallas as pl
from jax.experimental.pallas import tpu as pltpu

D_MODEL = 1024
HEAD_DIM = 64
CHUNK = 64
H_A = 8
H_B = 8
B_LEFT_CHUNKS = 8
B_REL_CLIP = 128
H_C = 16
HKV_C = 2
G_C = H_C // HKV_C
WINDOW = 128
C_LEFT_CHUNKS = WINDOW // CHUNK
T5_BUCKETS = 32
T5_MAX_DIST = 128
EPS = 1e-6
W_A = H_A * HEAD_DIM
W_B = H_B * HEAD_DIM
SCALE = HEAD_DIM ** -0.5

LANES = 128
VMEM_LIMIT = 60 * 1024 * 1024

F32 = jnp.float32
BF16 = jnp.bfloat16
NEG_INF = float("-inf")

_NT = (((1,), (1,)), ((), ()))


def _dot(a, b):
    return jnp.dot(a, b, preferred_element_type=F32)


def _dot_nt(a, b):
    return lax.dot_general(a, b, _NT, preferred_element_type=F32)


def _rmsnorm(x, g):
    ms = jnp.mean(x * x, axis=-1, keepdims=True)
    return x * lax.rsqrt(ms + EPS) * g


def _log_sigmoid(x):
    return jnp.minimum(x, 0.0) - jnp.log1p(jnp.exp(-jnp.abs(x)))


def _split3(x):
    hi = x.astype(BF16).astype(F32)
    r1 = x - hi
    mid = r1.astype(BF16).astype(F32)
    lo = (r1 - mid).astype(BF16).astype(F32)
    return hi, mid, lo


def _cumsum_rows(x):
    n = x.shape[0]
    row = lax.broadcasted_iota(jnp.int32, x.shape, 0)
    s = 1
    while s < n:
        x = x + jnp.where(row >= s, pltpu.roll(x, s, 0), 0.0)
        s *= 2
    return x


def _cumsum_lanes(x):
    n = x.shape[1]
    col = lax.broadcasted_iota(jnp.int32, x.shape, 1)
    s = 1
    while s < n:
        x = x + jnp.where(col >= s, pltpu.roll(x, s, 1), 0.0)
        s *= 2
    return x


def _staggered(items, stage_a, stage_b, stage_c):
    n = len(items)
    a_out, b_out = {}, {}
    for step in range(n + 2):
        if step < n:
            a_out[step] = stage_a(items[step])
        if 0 <= step - 1 < n:
            b_out[step - 1] = stage_b(items[step - 1], a_out.pop(step - 1))
        if 0 <= step - 2 < n:
            stage_c(items[step - 2], b_out.pop(step - 2))


def _const_spec(shape):
    nd = len(shape)
    return pl.BlockSpec(shape, lambda *_: (0,) * nd, pipeline_mode=pl.Buffered(1))


def _params(sem):
    return pltpu.CompilerParams(dimension_semantics=sem, vmem_limit_bytes=VMEM_LIMIT)


def _even_proj_kernel(x_ref, g_ref, w_ref, wf_ref, bf_ref, pq_ref, pk_ref,
                      qaug_ref, kaug_ref, vaug_ref, ka_ref, va_ref, logf_ref,
                      qb_ref, kb_ref, vb_ref, kbt_ref, vbt_ref, carry_ref, *, tiles_per_batch):
    t = pl.program_id(1)
    x = x_ref[0]
    h = _rmsnorm(x, g_ref[...]).astype(BF16)
    p = _dot(h, w_ref[...])
    fa = _dot(h, wf_ref[...])
    logf = _log_sigmoid(fa + bf_ref[...])
    logf_ref[0] = logf[:, :H_A]

    @pl.when(t == 0)
    def _():
        carry_ref[...] = jnp.zeros_like(carry_ref)

    c = _cumsum_rows(logf) + carry_ref[...]
    tm = c.shape[0]
    carry_ref[...] = c[tm - 1:tm, :]

    lane = lax.broadcasted_iota(jnp.int32, (tm, LANES), 1)
    hi, mid, lo = _split3(c)
    a3 = jnp.where(lane < 8, hi, jnp.where(lane < 16, mid, jnp.where(lane < 24, lo,
                   jnp.where(lane == 24, 1.0, 0.0)))).astype(BF16)
    augq = _dot(a3, pq_ref[...])
    augk = _dot(a3, pk_ref[...])
    vone = jnp.where(lane == HEAD_DIM, 1.0, 0.0)
    low = lane < HEAD_DIM

    for j in range(H_A // 2):
        sl = slice(LANES * j, LANES * (j + 1))
        qp = p[:, sl] * SCALE
        kp = p[:, W_A + LANES * j: W_A + LANES * (j + 1)]
        vp = p[:, 2 * W_A + LANES * j: 2 * W_A + LANES * (j + 1)]
        for hh, (qv, kv, vv) in enumerate(((qp, kp, vp),
                                           (pltpu.roll(qp, HEAD_DIM, 1), pltpu.roll(kp, HEAD_DIM, 1),
                                            pltpu.roll(vp, HEAD_DIM, 1)))):
            o = slice(LANES * (2 * j + hh), LANES * (2 * j + hh + 1))
            qaug_ref[0, :, o] = jnp.where(low, qv, augq[:, o]).astype(BF16)
            kaug_ref[0, :, o] = jnp.where(low, kv, augk[:, o]).astype(BF16)
            vaug_ref[0, :, o] = jnp.where(low, vv, vone).astype(BF16)

    ka_ref[0] = p[:, W_A:2 * W_A]
    va_ref[0] = p[:, 2 * W_A:3 * W_A]
    base = 3 * W_A
    qb_ref[0] = (p[:, base:base + W_B] * SCALE).astype(BF16)
    kb_ref[0] = p[:, base + W_B:base + 2 * W_B].astype(BF16)
    vb_ref[0] = p[:, base + 2 * W_B:base + 3 * W_B].astype(BF16)

    @pl.when(t == tiles_per_batch - 1)
    def _():
        kbt_ref[0] = p[:, base + W_B:base + 2 * W_B]
        vbt_ref[0] = p[:, base + 2 * W_B:base + 3 * W_B]


def _even_proj(x, g, w_main, w_f, b_f, pq, pk, tm):
    b, s, d = x.shape
    nt = s // tm
    lb = min(B_LEFT_CHUNKS * CHUNK, s)
    assert lb == tm, "band-state tail must be exactly one row tile"
    row = lambda w: pl.BlockSpec((1, tm, w), lambda bi, ti: (bi, ti, 0))
    tail = pl.BlockSpec((1, lb, W_B), lambda bi, ti: (bi, 0, 0))
    outs = (
        jax.ShapeDtypeStruct((b, s, H_A * LANES), BF16),
        jax.ShapeDtypeStruct((b, s, H_A * LANES), BF16),
        jax.ShapeDtypeStruct((b, s, H_A * LANES), BF16),
        jax.ShapeDtypeStruct((b, s, W_A), F32),
        jax.ShapeDtypeStruct((b, s, W_A), F32),
        jax.ShapeDtypeStruct((b, s, H_A), F32),
        jax.ShapeDtypeStruct((b, s, W_B), BF16),
        jax.ShapeDtypeStruct((b, s, W_B), BF16),
        jax.ShapeDtypeStruct((b, s, W_B), BF16),
        jax.ShapeDtypeStruct((b, lb, W_B), F32),
        jax.ShapeDtypeStruct((b, lb, W_B), F32),
    )
    return pl.pallas_call(
        functools.partial(_even_proj_kernel, tiles_per_batch=nt),
        grid=(b, nt),
        in_specs=[row(d), _const_spec((1, d)), _const_spec(w_main.shape), _const_spec(w_f.shape),
                  _const_spec((1, LANES)), _const_spec(pq.shape), _const_spec(pk.shape)],
        out_specs=(row(H_A * LANES), row(H_A * LANES), row(H_A * LANES), row(W_A), row(W_A), row(H_A),
                   row(W_B), row(W_B), row(W_B), tail, tail),
        out_shape=outs,
        scratch_shapes=[pltpu.VMEM((1, LANES), F32)],
        compiler_params=_params(("arbitrary", "arbitrary")),
        name="even_proj",
    )(x, g, w_main, w_f, b_f, pq, pk)


def _fox_kernel(q_ref, k_ref, v_ref, o_ref, m_sc, acc_sc, *, blk, nsub):
    i = pl.program_id(2)
    nh = 2
    chains = [(sub, hh) for sub in range(nsub) for hh in range(nh)]
    qs = {(sub, hh): q_ref[0, sub * blk:(sub + 1) * blk, LANES * hh:LANES * (hh + 1)] for sub, hh in chains}
    r = lax.broadcasted_iota(jnp.int32, (blk, blk), 0)
    c = lax.broadcasted_iota(jnp.int32, (blk, blk), 1)
    causal = c <= r
    m_sc[...] = jnp.full_like(m_sc, NEG_INF)
    acc_sc[...] = jnp.zeros_like(acc_sc)

    def block(j, modes):
        start = pl.multiple_of(j * blk, blk)
        active = [ch for ch in chains if modes[ch[0]] is not None]

        def stage_scores(ch):
            k = k_ref[0, pl.ds(start, blk), LANES * ch[1]:LANES * (ch[1] + 1)]
            s = _dot_nt(qs[ch], k)
            if modes[ch[0]] == "masked":
                s = jnp.where(causal, s, NEG_INF)
            m = m_sc[chains.index(ch)]
            m_new = jnp.maximum(m, jnp.max(s, axis=1, keepdims=True))
            m_sc[chains.index(ch)] = m_new
            return s, m, m_new

        def stage_exp(ch, sm):
            s, m, m_new = sm
            p = jnp.concatenate([jnp.exp(s[:, LANES * cb:LANES * (cb + 1)] - m_new)
                                 for cb in range(blk // LANES)], axis=1)
            return p.astype(BF16), jnp.exp(m - m_new)

        def stage_pv(ch, pa):
            p, alpha = pa
            v = v_ref[0, pl.ds(start, blk), LANES * ch[1]:LANES * (ch[1] + 1)]
            n = chains.index(ch)
            acc_sc[n] = alpha * acc_sc[n] + _dot(p, v)

        _staggered(active, stage_scores, stage_exp, stage_pv)

    @pl.loop(0, nsub * i)
    def _(j):
        block(j, ["full"] * nsub)

    for d in range(nsub):
        block(nsub * i + d, [None if sub < d else ("masked" if sub == d else "full") for sub in range(nsub)])

    lane = lax.broadcasted_iota(jnp.int32, (blk, LANES), 1)
    for sub in range(nsub):
        a0 = acc_sc[chains.index((sub, 0))]
        a1 = acc_sc[chains.index((sub, 1))]
        o0 = a0 / a0[:, HEAD_DIM:HEAD_DIM + 1]
        o1 = a1 / a1[:, HEAD_DIM:HEAD_DIM + 1]
        o_ref[0, sub * blk:(sub + 1) * blk, :] = jnp.where(
            lane < HEAD_DIM, o0, pltpu.roll(o1, HEAD_DIM, 1)).astype(BF16)


def _fox(qaug, kaug, vaug, blk, nsub):
    b, s, _ = qaug.shape
    pairs = H_A // 2
    tq = blk * nsub
    return pl.pallas_call(
        functools.partial(_fox_kernel, blk=blk, nsub=nsub),
        grid=(b, pairs, s // tq),
        in_specs=[pl.BlockSpec((1, tq, 2 * LANES), lambda bi, hp, i: (bi, i, hp)),
                  pl.BlockSpec((1, s, 2 * LANES), lambda bi, hp, i: (bi, 0, hp)),
                  pl.BlockSpec((1, s, 2 * LANES), lambda bi, hp, i: (bi, 0, hp))],
        out_specs=pl.BlockSpec((1, tq, LANES), lambda bi, hp, i: (bi, i, hp)),
        out_shape=jax.ShapeDtypeStruct((b, s, W_A), BF16),
        scratch_shapes=[pltpu.VMEM((2 * nsub, blk, LANES), F32), pltpu.VMEM((2 * nsub, blk, LANES), F32)],
        compiler_params=_params(("arbitrary", "arbitrary", "arbitrary")),
        name="fox",
    )(qaug, kaug, vaug)


def _band_kernel(*refs, tq, win, left, npairs, heads_per_group, use_sinks, shared_kv):
    if use_sinks:
        sink_ref, q_ref, k_ref, v_ref, bias_ref, o_ref = refs
    else:
        q_ref, k_ref, v_ref, bias_ref, o_ref = refs
    g = pl.program_id(1)
    i = pl.program_id(2)
    nvar = bias_ref.shape[0]
    var = jnp.minimum(i, nvar - 1)
    start = pl.multiple_of(jnp.maximum(i * tq - left, 0), LANES)
    lane = lax.broadcasted_iota(jnp.int32, (tq, LANES), 1)
    low = lane < HEAD_DIM
    heads = [(pr, hh) for pr in range(npairs) for hh in range(2)]

    def scores(pr, hh):
        qp = q_ref[0, :, LANES * pr:LANES * (pr + 1)]
        qm = jnp.where(low if hh == 0 else jnp.logical_not(low), qp, jnp.zeros_like(qp))
        kl = 0 if shared_kv else LANES * pr
        return _dot_nt(qm, k_ref[0, pl.ds(start, win), kl:kl + LANES])

    def sink(pr, hh):
        return sink_ref[g * heads_per_group + 2 * pr + hh]

    def stage_scores(head):
        s = scores(*head) + bias_ref[var, 2 * head[0] + head[1]]
        m = jnp.max(s, axis=1, keepdims=True)
        return s, (jnp.maximum(m, sink(*head)) if use_sinks else m)

    def stage_exp(head, sm):
        s, m = sm
        e = jnp.exp(s - m)
        l = jnp.sum(e, axis=1, keepdims=True)
        return e.astype(BF16), (l + jnp.exp(sink(*head) - m) if use_sinks else l)

    outs = {}

    def stage_pv(head, pl_):
        p, l = pl_
        pr, hh = head
        kl = 0 if shared_kv else LANES * pr
        outs[hh] = _dot(p, v_ref[0, pl.ds(start, win), kl:kl + LANES]) / l
        if hh == 1:
            o_ref[0, :, LANES * pr:LANES * (pr + 1)] = jnp.where(low, outs[0], outs[1]).astype(BF16)

    _staggered(heads, stage_scores, stage_exp, stage_pv)


def _band(q, k, v, bias, sinks, *, tq, left, npairs, shared_kv, name):
    b, s, wq = q.shape
    win = bias.shape[-1]
    wblk = npairs * LANES
    groups = wq // wblk
    hpg = 2 * npairs
    use_sinks = sinks is not None
    kern = functools.partial(_band_kernel, tq=tq, win=win, left=left, npairs=npairs,
                             heads_per_group=hpg, use_sinks=use_sinks, shared_kv=shared_kv)
    kvw = LANES if shared_kv else wblk
    resident = lambda shape, imap: pl.BlockSpec(shape, imap, pipeline_mode=pl.Buffered(1))
    in_specs = [pl.BlockSpec((1, tq, wblk), lambda bi, g, i: (bi, i, g)),
                resident((1, s, kvw), lambda bi, g, i: (bi, 0, g)),
                resident((1, s, kvw), lambda bi, g, i: (bi, 0, g)),
                resident((bias.shape[0], hpg, tq, win), lambda bi, g, i: (0, g, 0, 0))]
    args = [q, k, v, bias]
    if use_sinks:
        in_specs = [pl.BlockSpec(memory_space=pltpu.SMEM)] + in_specs
        args = [sinks] + args
    return pl.pallas_call(
        kern,
        grid=(b, groups, s // tq),
        in_specs=in_specs,
        out_specs=pl.BlockSpec((1, tq, wblk), lambda bi, g, i: (bi, i, g)),
        out_shape=jax.ShapeDtypeStruct((b, s, wq), BF16),
        compiler_params=_params(("arbitrary", "arbitrary", "arbitrary")),
        name=name,
    )(*args)


def _out_ffn_kernel(*refs, n_attn, d_ff, final_norm):
    x_ref = refs[0]
    attn = refs[1:1 + 2 * n_attn]
    gffn_ref, win_ref, wout_ref, gfin_ref, o_ref = refs[1 + 2 * n_attn:]
    y = x_ref[...]
    for a in range(n_attn):
        y = y + _dot(attn[2 * a][...], attn[2 * a + 1][...])
    h = _rmsnorm(y, gffn_ref[...]).astype(BF16)
    gu = _dot(h, win_ref[...])
    gate = gu[:, :d_ff]
    up = gu[:, d_ff:]
    act = (gate * (1.0 / (1.0 + jnp.exp(-gate))) * up).astype(BF16)
    y = y + _dot(act, wout_ref[...])
    o_ref[...] = _rmsnorm(y, gfin_ref[...]) if final_norm else y


def _out_ffn(x, attn_pairs, g_ffn, w_in, w_out, g_fin, tm, final_norm):
    n, d = x.shape
    d_ff = w_out.shape[0]
    row = lambda w: pl.BlockSpec((tm, w), lambda i: (i, 0))
    args, specs = [x], [row(d)]
    for o, w in attn_pairs:
        args += [o, w]
        specs += [row(o.shape[1]), _const_spec(w.shape)]
    args += [g_ffn, w_in, w_out, g_fin]
    specs += [_const_spec((1, d)), _const_spec(w_in.shape), _const_spec(w_out.shape), _const_spec((1, d))]
    return pl.pallas_call(
        functools.partial(_out_ffn_kernel, n_attn=len(attn_pairs), d_ff=d_ff, final_norm=final_norm),
        grid=(n // tm,),
        in_specs=specs,
        out_specs=row(d),
        out_shape=jax.ShapeDtypeStruct((n, d), F32),
        compiler_params=_params(("arbitrary",)),
        name="out_ffn",
    )(*args)


def _odd_proj_kernel(x_ref, g_ref, w_ref, q_ref, k_ref, v_ref, kt_ref, vt_ref, *, tiles_per_batch, tail):
    t = pl.program_id(1)
    h = _rmsnorm(x_ref[0], g_ref[...]).astype(BF16)
    p = _dot(h, w_ref[...])
    wq = H_C * HEAD_DIM
    q_ref[0] = (p[:, :wq] * SCALE).astype(BF16)
    tm = p.shape[0]
    lane = lax.broadcasted_iota(jnp.int32, (tm, LANES), 1)
    low = lane < HEAD_DIM
    for src, dst in ((p[:, wq:wq + LANES], k_ref), (p[:, wq + LANES:wq + 2 * LANES], v_ref)):
        rolled = pltpu.roll(src, HEAD_DIM, 1)
        dst[0, :, :LANES] = jnp.where(low, src, rolled).astype(BF16)
        dst[0, :, LANES:] = jnp.where(low, rolled, src).astype(BF16)

    @pl.when(t == tiles_per_batch - 1)
    def _():
        kt_ref[0] = p[tm - tail:, wq:wq + LANES]
        vt_ref[0] = p[tm - tail:, wq + LANES:wq + 2 * LANES]


def _odd_proj(x, g, w, tm):
    b, s, d = x.shape
    nt = s // tm
    lc = min(C_LEFT_CHUNKS * CHUNK, s)
    assert lc <= tm
    wq = H_C * HEAD_DIM
    row = lambda w_: pl.BlockSpec((1, tm, w_), lambda bi, ti: (bi, ti, 0))
    tail = pl.BlockSpec((1, lc, LANES), lambda bi, ti: (bi, 0, 0))
    return pl.pallas_call(
        functools.partial(_odd_proj_kernel, tiles_per_batch=nt, tail=lc),
        grid=(b, nt),
        in_specs=[row(d), _const_spec((1, d)), _const_spec(w.shape)],
        out_specs=(row(wq), row(2 * LANES), row(2 * LANES), tail, tail),
        out_shape=(jax.ShapeDtypeStruct((b, s, wq), BF16),
                   jax.ShapeDtypeStruct((b, s, 2 * LANES), BF16),
                   jax.ShapeDtypeStruct((b, s, 2 * LANES), BF16),
                   jax.ShapeDtypeStruct((b, lc, LANES), F32),
                   jax.ShapeDtypeStruct((b, lc, LANES), F32)),
        compiler_params=_params(("arbitrary", "arbitrary")),
        name="odd_proj",
    )(x, g, w)


def _sample_proj_kernel(*refs, gate_col):
    if gate_col is None:
        x_ref, g_ref, w_ref, p_ref = refs
    else:
        x_ref, g_ref, w_ref, bf_ref, p_ref, logf_ref = refs
    h = _rmsnorm(x_ref[...], g_ref[...]).astype(BF16)
    p = _dot(h, w_ref[...])
    p_ref[...] = p
    if gate_col is not None:
        logf_ref[...] = _log_sigmoid(p[:, gate_col:gate_col + LANES] + bf_ref[...])


def _sample_proj(x, g, w, tm, b_f=None, gate_col=None):
    n, d = x.shape
    row = lambda w_: pl.BlockSpec((tm, w_), lambda i: (i, 0))
    in_specs = [row(d), _const_spec((1, d)), _const_spec(w.shape)]
    args = [x, g, w]
    out_specs = row(w.shape[1])
    out_shape = jax.ShapeDtypeStruct((n, w.shape[1]), F32)
    if gate_col is not None:
        in_specs.append(_const_spec((1, LANES)))
        args.append(b_f)
        out_specs = (out_specs, row(LANES))
        out_shape = (out_shape, jax.ShapeDtypeStruct((n, LANES), F32))
    return pl.pallas_call(
        functools.partial(_sample_proj_kernel, gate_col=gate_col),
        grid=(n // tm,),
        in_specs=in_specs,
        out_specs=out_specs,
        out_shape=out_shape,
        compiler_params=_params(("arbitrary",)),
        name="sample_proj",
    )(*args)


def _block_diag_q(q, nheads, scale):
    t, w = q.shape
    tiled = jnp.concatenate([q] * nheads, axis=0)
    r = lax.broadcasted_iota(jnp.int32, (nheads * t, w), 0)
    c = lax.broadcasted_iota(jnp.int32, (nheads * t, w), 1)
    return jnp.where(r // t == c // HEAD_DIM, tiled * scale, 0.0).astype(BF16)


def _block_diag_extract(o_all, nheads, t):
    w = o_all.shape[1]
    c = lax.broadcasted_iota(jnp.int32, (t, w), 1)
    out = jnp.zeros((t, w), F32)
    for h in range(nheads):
        out = jnp.where(c // HEAD_DIM == h, o_all[h * t:(h + 1) * t, :], out)
    return out


def _pad_rows(x, rows):
    return jnp.concatenate([x, jnp.zeros((rows - x.shape[0], x.shape[1]), x.dtype)], axis=0)


def _sample_even_kernel(qa_ref, kan_ref, van_ref, qb_ref, kbn_ref, vbn_ref, ck_ref, cv_ref, lft_ref,
                        cbk_ref, cbv_ref, e3_ref, biasb_ref, oa_ref, ob_ref,
                        bias_sc, m_sc, acc_sc, *, t, past, kc, nkc):
    c = pl.program_id(1)
    rows = H_A * t
    qbd = _block_diag_q(qa_ref[...], H_A, SCALE)

    @pl.when(c == 0)
    def _():
        cum = _cumsum_lanes(lft_ref[0])
        last = past + t - 1
        suffix = cum[:, last:last + 1] - cum
        hi, mid, lo = _split3(suffix)
        s3 = jnp.concatenate([hi, mid, lo, jnp.zeros((LANES - 3 * H_A, suffix.shape[1]), F32)], axis=0)
        s3 = s3.astype(BF16)
        for ch in range(nkc + 1):
            w = kc if ch < nkc else LANES
            bias_sc[ch, :, :w] = _dot(e3_ref[...], s3[:, ch * kc:ch * kc + w])
        m_sc[...] = jnp.full_like(m_sc, NEG_INF)
        acc_sc[...] = jnp.zeros_like(acc_sc)

    def update(s, v):
        m = m_sc[...]
        m_new = jnp.maximum(m, jnp.max(s, axis=1, keepdims=True))
        p = jnp.exp(s - m_new)
        alpha = jnp.exp(m - m_new)
        acc = acc_sc[...]
        lsum = alpha * acc[:, W_A:W_A + 1] + jnp.sum(p, axis=1, keepdims=True)
        o = alpha * acc[:, :W_A] + _dot(p.astype(BF16), v)
        acc_sc[:, :W_A] = o
        acc_sc[:, W_A:] = jnp.broadcast_to(lsum, (rows, LANES))
        m_sc[...] = m_new

    s = _dot_nt(qbd, ck_ref[0].astype(BF16)) + bias_sc[c]
    update(s, cv_ref[0].astype(BF16))

    @pl.when(c == nkc - 1)
    def _():
        kn = _pad_rows(kan_ref[...], LANES).astype(BF16)
        vn = _pad_rows(van_ref[...], LANES).astype(BF16)
        sn = _dot_nt(qbd, kn) + bias_sc[nkc, :, :LANES]
        r = lax.broadcasted_iota(jnp.int32, sn.shape, 0)
        col = lax.broadcasted_iota(jnp.int32, sn.shape, 1)
        sn = jnp.where(col <= r % t, sn, NEG_INF)
        update(sn, vn)
        acc = acc_sc[...]
        o_all = acc[:, :W_A] / acc[:, W_A:W_A + 1]
        oa_ref[...] = _block_diag_extract(o_all, H_A, t).astype(BF16)

        qbd_b = _block_diag_q(qb_ref[...], H_B, SCALE)
        kb = jnp.concatenate([cbk_ref[0], _pad_rows(kbn_ref[...], LANES)], axis=0).astype(BF16)
        vb = jnp.concatenate([cbv_ref[0], _pad_rows(vbn_ref[...], LANES)], axis=0).astype(BF16)
        sb = _dot_nt(qbd_b, kb) + biasb_ref[...]
        mb = jnp.max(sb, axis=1, keepdims=True)
        eb = jnp.exp(sb - mb)
        lb = jnp.sum(eb, axis=1, keepdims=True)
        ob_all = _dot(eb.astype(BF16), vb) / lb
        ob_ref[...] = _block_diag_extract(ob_all, H_B, t).astype(BF16)


def _sample_even(p, cache_k, cache_v, logf_t, cache_bk, cache_bv, e3, bias_b, *, t, kc):
    n = p.shape[0]
    nb = n // t
    past = cache_k.shape[1]
    nkc = past // kc
    lb = cache_bk.shape[1]
    pcol = lambda j: pl.BlockSpec((t, W_A), lambda bi, c: (bi, j))
    rows = H_A * t
    return pl.pallas_call(
        functools.partial(_sample_even_kernel, t=t, past=past, kc=kc, nkc=nkc),
        grid=(nb, nkc),
        in_specs=[pcol(0), pcol(1), pcol(2), pcol(3), pcol(4), pcol(5),
                  pl.BlockSpec((1, kc, W_A), lambda bi, c: (bi, c, 0)),
                  pl.BlockSpec((1, kc, W_A), lambda bi, c: (bi, c, 0)),
                  pl.BlockSpec((1, H_A, logf_t.shape[2]), lambda bi, c: (bi, 0, 0)),
                  pl.BlockSpec((1, lb, W_B), lambda bi, c: (bi, 0, 0)),
                  pl.BlockSpec((1, lb, W_B), lambda bi, c: (bi, 0, 0)),
                  _const_spec(e3.shape), _const_spec(bias_b.shape)],
        out_specs=(pl.BlockSpec((t, W_A), lambda bi, c: (bi, 0)),
                   pl.BlockSpec((t, W_B), lambda bi, c: (bi, 0))),
        out_shape=(jax.ShapeDtypeStruct((n, W_A), BF16), jax.ShapeDtypeStruct((n, W_B), BF16)),
        scratch_shapes=[pltpu.VMEM((nkc + 1, rows, kc), F32),
                        pltpu.VMEM((rows, 1), F32),
                        pltpu.VMEM((rows, W_A + LANES), F32)],
        compiler_params=_params(("arbitrary", "arbitrary")),
        name="sample_even",
    )(p, p, p, p, p, p, cache_k, cache_v, logf_t, cache_bk, cache_bv, e3, bias_b)


def _sample_odd_kernel(q_ref, kn_ref, vn_ref, ck_ref, cv_ref, x_ref, bias_ref, sink_ref, o_ref, *, t):
    qbd = _block_diag_q(q_ref[...], H_C, SCALE)
    kall = jnp.concatenate([ck_ref[0], _pad_rows(kn_ref[...], LANES)], axis=0).astype(BF16)
    vall = jnp.concatenate([cv_ref[0], _pad_rows(vn_ref[...], LANES)], axis=0).astype(BF16)
    kexp = _dot(kall, x_ref[...]).astype(BF16)
    vexp = _dot(vall, x_ref[...]).astype(BF16)
    s = _dot_nt(qbd, kexp) + bias_ref[...]
    sk = sink_ref[...]
    m = jnp.maximum(jnp.max(s, axis=1, keepdims=True), sk)
    e = jnp.exp(s - m)
    l = jnp.sum(e, axis=1, keepdims=True) + jnp.exp(sk - m)
    o_all = _dot(e.astype(BF16), vexp) / l
    o_ref[...] = _block_diag_extract(o_all, H_C, t).astype(BF16)


def _sample_odd(p, cache_k, cache_v, xexp, bias, sink_col, *, t):
    n = p.shape[0]
    nb = n // t
    wq = H_C * HEAD_DIM
    lc = cache_k.shape[1]
    return pl.pallas_call(
        functools.partial(_sample_odd_kernel, t=t),
        grid=(nb,),
        in_specs=[pl.BlockSpec((t, wq), lambda bi: (bi, 0)),
                  pl.BlockSpec((t, LANES), lambda bi: (bi, wq // LANES)),
                  pl.BlockSpec((t, LANES), lambda bi: (bi, wq // LANES + 1)),
                  pl.BlockSpec((1, lc, LANES), lambda bi: (bi, 0, 0)),
                  pl.BlockSpec((1, lc, LANES), lambda bi: (bi, 0, 0)),
                  _const_spec(xexp.shape), _const_spec(bias.shape), _const_spec(sink_col.shape)],
        out_specs=pl.BlockSpec((t, wq), lambda bi: (bi, 0)),
        out_shape=jax.ShapeDtypeStruct((n, wq), BF16),
        compiler_params=_params(("arbitrary",)),
        name="sample_odd",
    )(p, p, p, cache_k, cache_v, xexp, bias, sink_col)


def _t5_bucket(rel_mem):
    nb = T5_BUCKETS // 2
    max_exact = nb // 2
    n = jnp.abs(rel_mem)
    large = max_exact + (jnp.log(jnp.maximum(n, 1).astype(F32) / max_exact)
                         / math.log(T5_MAX_DIST / max_exact) * (nb - max_exact)).astype(jnp.int32)
    large = jnp.minimum(large, nb - 1)
    return jnp.where(rel_mem > 0, nb, 0) + jnp.where(n < max_exact, n, large)


def _chunk_valid(tq, win, offsets, left_chunks):
    off = np.asarray(offsets)[:, None, None] // CHUNK
    d = off + (np.arange(tq) // CHUNK)[None, :, None] - (np.arange(win) // CHUNK)[None, None, :]
    return (d >= 0) & (d <= left_chunks)


def _bias_b_of_rel(table, rel):
    idx = np.clip(rel, -B_REL_CLIP, B_REL_CLIP) + B_REL_CLIP
    return table.astype(F32)[idx].T


def _bias_c_of_rel(table, rel):
    return table.astype(F32)[_t5_bucket(-jnp.asarray(rel, jnp.int32))].T


def _toeplitz(bias_of_rel, tq, ncols, left):
    n = tq + ncols - 1
    f_rev = bias_of_rel(tq - 1 + left - np.arange(n))
    flat = jnp.tile(jnp.pad(f_rev, ((0, 0), (0, 1))), (1, tq))[:, :tq * n]
    return flat.reshape(f_rev.shape[0], tq, n)[:, :, tq - 1:tq - 1 + ncols]


def _band_bias_tiles(bias_of_rel, tq, left, offsets, left_chunks):
    win = tq + left
    wide = _toeplitz(bias_of_rel, tq, win + left, left)
    tiles = jnp.stack([wide[:, :, left - o:left - o + win] for o in offsets], axis=0)
    valid = _chunk_valid(tq, win, offsets, left_chunks)
    return jnp.where(valid[:, None], tiles, NEG_INF)


def _sample_bias(bias_of_rel, t, cache_len):
    ncols = cache_len + LANES
    tiles = _toeplitz(bias_of_rel, t, ncols, cache_len)
    tiles = jnp.where((np.arange(ncols) < cache_len + t)[None, None, :], tiles, NEG_INF)
    return tiles.reshape(-1, ncols)


def _placement():
    pq = np.zeros((LANES, H_A * LANES), np.float32)
    pk = np.zeros((LANES, H_A * LANES), np.float32)
    for h in range(H_A):
        for j in range(3):
            pq[8 * j + h, LANES * h + HEAD_DIM + j] = 1.0
            pq[24, LANES * h + HEAD_DIM + 3 + j] = 1.0
            pk[24, LANES * h + HEAD_DIM + j] = 1.0
            pk[8 * j + h, LANES * h + HEAD_DIM + 3 + j] = -1.0
    return jnp.asarray(pq, BF16), jnp.asarray(pk, BF16)


def kernel(x_prompt, x_sample, cache_a_k, cache_a_v, cache_a_logf, cache_b_k, cache_b_v, cache_c_k, cache_c_v,
           norm_mix, norm_ffn, norm_final, w_in_even, b_forget, rel_bias_b, w_out_even, w_in_odd, sinks_c,
           w_out_odd, t5_bias, w_ffn_in, w_ffn_out):
    b, s, d = x_prompt.shape
    nb, t, _ = x_sample.shape
    past = cache_a_k.shape[2]
    n_p, n_s = b * s, nb * t

    w_even = w_in_even[0]
    w_main = w_even[:, :3 * W_A + 3 * W_B].astype(BF16)
    wf = w_even[:, 3 * W_A + 3 * W_B:]
    w_f = jnp.concatenate([wf, wf, wf, jnp.zeros((d, LANES - 3 * H_A), F32)], axis=1).astype(BF16)
    bf = b_forget[0].astype(F32)
    b_f = jnp.concatenate([bf, bf, bf, jnp.zeros((LANES - 3 * H_A,), F32)])[None, :]
    pq, pk = _placement()
    w_oe = w_out_even[0].astype(BF16)
    w_oo = w_out_odd[0].astype(BF16)
    w_odd = w_in_odd[0].astype(BF16)
    w_fi = w_ffn_in.astype(BF16)
    w_fo = w_ffn_out.astype(BF16)
    g_mix = norm_mix.astype(F32)[:, None, :]
    g_ffn = norm_ffn.astype(F32)[:, None, :]
    g_fin = norm_final.astype(F32)[None, :]

    bias_b_of = functools.partial(_bias_b_of_rel, rel_bias_b[0])
    bias_c_of = functools.partial(_bias_c_of_rel, t5_bias)
    tq_b, left_b = 256, B_LEFT_CHUNKS * CHUNK
    offs_b = [min(v * tq_b, left_b) for v in range(left_b // tq_b + 1)]
    bias_b = _band_bias_tiles(bias_b_of, tq_b, left_b, offs_b, B_LEFT_CHUNKS)
    tq_c, left_c = 256, C_LEFT_CHUNKS * CHUNK
    offs_c = [0, left_c]
    bias_c = _band_bias_tiles(bias_c_of, tq_c, left_c, offs_c, C_LEFT_CHUNKS)

    xp = x_prompt
    qaug, kaug, vaug, ka, va, logf, qb, kb, vb, kbt, vbt = _even_proj(
        xp, g_mix[0], w_main, w_f, b_f, pq, pk, tm=512)
    oa = _fox(qaug, kaug, vaug, blk=512, nsub=2)
    ob = _band(qb, kb, vb, bias_b, None, tq=tq_b, left=left_b, npairs=H_B // 2, shared_kv=False,
               name="band_b")
    xp1 = _out_ffn(xp.reshape(n_p, d),
                   [(oa.reshape(n_p, W_A), w_oe[:W_A]), (ob.reshape(n_p, W_B), w_oe[W_A:])],
                   g_ffn[0], w_fi[0], w_fo[0], g_fin, tm=256, final_norm=False)

    qc, kcd, vcd, kct, vct = _odd_proj(xp1.reshape(b, s, d), g_mix[1], w_odd, tm=512)
    oc = _band(qc, kcd, vcd, bias_c, sinks_c[0].astype(F32), tq=tq_c, left=left_c, npairs=G_C // 2,
               shared_kv=True, name="band_c")
    y_prompt = _out_ffn(xp1, [(oc.reshape(n_p, H_C * HEAD_DIM), w_oo)],
                        g_ffn[1], w_fi[1], w_fo[1], g_fin, tm=256, final_norm=True)

    xs = x_sample.reshape(n_s, d)
    fcol = 3 * W_A + 3 * W_B
    w_even_s = jnp.concatenate([w_main, w_f], axis=1)
    ps, logf_sp = _sample_proj(xs, g_mix[0], w_even_s, tm=256, b_f=b_f, gate_col=fcol)
    logf_s = logf_sp[:, :H_A]
    kpad = LANES * -(-(past + t) // LANES)
    logf_all = jnp.concatenate([cache_a_logf[0].astype(F32), logf_s.reshape(nb, t, H_A)], axis=1)
    logf_t = jnp.pad(jnp.swapaxes(logf_all, 1, 2), ((0, 0), (0, 0), (0, kpad - past - t)))
    lbs = cache_b_k.shape[2]
    bias_sb = _sample_bias(bias_b_of, t, lbs)
    e3 = np.zeros((H_A * t, LANES), np.float32)
    for j in range(3):
        e3[np.arange(H_A * t), 8 * j + np.arange(H_A * t) // t] = 1.0
    oa_s, ob_s = _sample_even(ps, cache_a_k[0].reshape(nb, past, W_A), cache_a_v[0].reshape(nb, past, W_A),
                              logf_t, cache_b_k[0].reshape(nb, lbs, W_B), cache_b_v[0].reshape(nb, lbs, W_B),
                              jnp.asarray(e3, BF16), bias_sb, t=t, kc=2048)
    xs1 = _out_ffn(xs, [(oa_s, w_oe[:W_A]), (ob_s, w_oe[W_A:])], g_ffn[0], w_fi[0], w_fo[0], g_fin,
                   tm=256, final_norm=False)

    ps2 = _sample_proj(xs1, g_mix[1], w_odd, tm=256)
    lcs = cache_c_k.shape[2]
    bias_sc = _sample_bias(bias_c_of, t, lcs)
    sink_col = jnp.repeat(sinks_c[0].astype(F32), t)[:, None]
    lane_head = np.arange(H_C * HEAD_DIM) // HEAD_DIM
    src_lane = (lane_head // G_C) * HEAD_DIM + np.arange(H_C * HEAD_DIM) % HEAD_DIM
    xexp = jnp.asarray(np.arange(LANES)[:, None] == src_lane[None, :], BF16)
    oc_s = _sample_odd(ps2, cache_c_k[0].reshape(nb, lcs, LANES), cache_c_v[0].reshape(nb, lcs, LANES),
                       xexp, bias_sc, sink_col, t=t)
    y_sample = _out_ffn(xs1, [(oc_s, w_oo)], g_ffn[1], w_fi[1], w_fo[1], g_fin, tm=256, final_norm=True)

    wq = H_C * HEAD_DIM
    hd = lambda a, lead, h: a.reshape((1,) + lead + (h, HEAD_DIM))
    return (
        y_prompt.reshape(b, s, d), y_sample.reshape(nb, t, d),
        hd(ka, (b, s), H_A), hd(va, (b, s), H_A), logf[None],
        hd(kbt, (b, kbt.shape[1]), H_B), hd(vbt, (b, vbt.shape[1]), H_B),
        hd(kct, (b, kct.shape[1]), HKV_C), hd(vct, (b, vct.shape[1]), HKV_C),
        hd(ps[:, W_A:2 * W_A], (nb, t), H_A), hd(ps[:, 2 * W_A:3 * W_A], (nb, t), H_A),
        logf_s.reshape(1, nb, t, H_A),
        hd(ps[:, 3 * W_A + W_B:3 * W_A + 2 * W_B], (nb, t), H_B),
        hd(ps[:, 3 * W_A + 2 * W_B:3 * W_A + 3 * W_B], (nb, t), H_B),
        hd(ps2[:, wq:wq + LANES], (nb, t), HKV_C), hd(ps2[:, wq + LANES:wq + 2 * LANES], (nb, t), HKV_C),
    )
```

```python
import functools
import math

import jax
import jax.numpy as jnp
import numpy as np
from jax import lax
from jax.experimental import pallas as pl
from jax.experimental.pallas import tpu as pltpu

D_MODEL = 1024
HEAD_DIM = 64
CHUNK = 64
H_A = 8
H_B = 8
B_LEFT_CHUNKS = 8
B_REL_CLIP = 128
H_C = 16
HKV_C = 2
G_C = H_C // HKV_C
WINDOW = 128
C_LEFT_CHUNKS = WINDOW // CHUNK
T5_BUCKETS = 32
T5_MAX_DIST = 128
EPS = 1e-6
W_A = H_A * HEAD_DIM
W_B = H_B * HEAD_DIM
SCALE = HEAD_DIM ** -0.5

LANES = 128
VMEM_LIMIT = 60 * 1024 * 1024

F32 = jnp.float32
BF16 = jnp.bfloat16
NEG_INF = float("-inf")

_NT = (((1,), (1,)), ((), ()))


def _dot(a, b):
    return jnp.dot(a, b, preferred_element_type=F32)


def _dot_nt(a, b):
    return lax.dot_general(a, b, _NT, preferred_element_type=F32)


def _rmsnorm(x, g):
    ms = jnp.mean(x * x, axis=-1, keepdims=True)
    return x * lax.rsqrt(ms + EPS) * g


def _log_sigmoid(x):
    return jnp.minimum(x, 0.0) - jnp.log1p(jnp.exp(-jnp.abs(x)))


def _split3(x):
    hi = x.astype(BF16).astype(F32)
    r1 = x - hi
    mid = r1.astype(BF16).astype(F32)
    lo = (r1 - mid).astype(BF16).astype(F32)
    return hi, mid, lo


def _cumsum_rows(x):
    n = x.shape[0]
    row = lax.broadcasted_iota(jnp.int32, x.shape, 0)
    s = 1
    while s < n:
        x = x + jnp.where(row >= s, pltpu.roll(x, s, 0), 0.0)
        s *= 2
    return x


def _cumsum_lanes(x):
    n = x.shape[1]
    col = lax.broadcasted_iota(jnp.int32, x.shape, 1)
    s = 1
    while s < n:
        x = x + jnp.where(col >= s, pltpu.roll(x, s, 1), 0.0)
        s *= 2
    return x


def _staggered(items, stage_a, stage_b, stage_c):
    n = len(items)
    a_out, b_out = {}, {}
    for step in range(n + 2):
        if step < n:
            a_out[step] = stage_a(items[step])
        if 0 <= step - 1 < n:
            b_out[step - 1] = stage_b(items[step - 1], a_out.pop(step - 1))
        if 0 <= step - 2 < n:
            stage_c(items[step - 2], b_out.pop(step - 2))


def _const_spec(shape):
    nd = len(shape)
    return pl.BlockSpec(shape, lambda *_: (0,) * nd, pipeline_mode=pl.Buffered(1))


def _params(sem):
    return pltpu.CompilerParams(dimension_semantics=sem, vmem_limit_bytes=VMEM_LIMIT)


def _even_proj_kernel(x_ref, g_ref, w_ref, bf_ref, pq_ref, pk_ref,
                      qaug_ref, kaug_ref, vaug_ref, ka_ref, va_ref, logf_ref,
                      qb_ref, kb_ref, vb_ref, kbt_ref, vbt_ref, carry_ref):
    @pl.when(pl.program_id(1) == 0)
    def _():
        carry_ref[...] = jnp.zeros_like(carry_ref)

    h = _rmsnorm(x_ref[0], g_ref[...]).astype(BF16)
    tm = h.shape[0]
    lane = lax.broadcasted_iota(jnp.int32, (tm, LANES), 1)
    low = lane < HEAD_DIM
    first = W_B + LANES

    def chunk(n):
        return _dot(h, w_ref[:, first + W_A * n:first + W_A * (n + 1)])

    pf = _dot(h, w_ref[:, :first])
    qb_ref[0] = (pf[:, :W_B] * SCALE).astype(BF16)
    fa = pf[:, W_B:]
    pc = chunk(3)
    kb_ref[0] = pc.astype(BF16)
    kbt_ref[0] = pc
    pc = chunk(4)
    vb_ref[0] = pc.astype(BF16)
    vbt_ref[0] = pc
    logf = _log_sigmoid(fa + bf_ref[...])
    logf_ref[0] = logf[:, :H_A]
    c = _cumsum_rows(logf) + carry_ref[...]
    carry_ref[...] = c[tm - 1:tm, :]
    hi, mid, lo = _split3(c)
    a3 = jnp.where(lane < 8, hi, jnp.where(lane < 16, mid, jnp.where(lane < 24, lo,
                   jnp.where(lane == 24, 1.0, 0.0)))).astype(BF16)
    augq = _dot(a3, pq_ref[...])
    augk = _dot(a3, pk_ref[...])
    vone = jnp.where(lane == HEAD_DIM, 1.0, 0.0)

    def per_head(dst_ref, pc, spare):
        for j in range(H_A // 2):
            pair = pc[:, LANES * j:LANES * (j + 1)]
            for hh, val in enumerate((pair, pltpu.roll(pair, HEAD_DIM, 1))):
                o = slice(LANES * (2 * j + hh), LANES * (2 * j + hh + 1))
                dst_ref[0, :, o] = jnp.where(low, val, vone if spare is None else spare[:, o]).astype(BF16)

    pc = chunk(0)
    per_head(qaug_ref, pc * SCALE, augq)
    pc = chunk(1)
    ka_ref[0] = pc
    per_head(kaug_ref, pc, augk)
    pc = chunk(2)
    va_ref[0] = pc
    per_head(vaug_ref, pc, None)


def _even_proj(x, g, w_all, b_f, pq, pk, tm):
    b, s, d = x.shape
    nt = s // tm
    lb = min(B_LEFT_CHUNKS * CHUNK, s)
    assert lb == tm, "band-state tail must be exactly one row tile"
    row = lambda w: pl.BlockSpec((1, tm, w), lambda bi, ti: (bi, ti, 0))
    tail = pl.BlockSpec((1, lb, W_B), lambda bi, ti: (bi, 0, 0))
    outs = (
        jax.ShapeDtypeStruct((b, s, H_A * LANES), BF16),
        jax.ShapeDtypeStruct((b, s, H_A * LANES), BF16),
        jax.ShapeDtypeStruct((b, s, H_A * LANES), BF16),
        jax.ShapeDtypeStruct((b, s, W_A), F32),
        jax.ShapeDtypeStruct((b, s, W_A), F32),
        jax.ShapeDtypeStruct((b, s, H_A), F32),
        jax.ShapeDtypeStruct((b, s, W_B), BF16),
        jax.ShapeDtypeStruct((b, s, W_B), BF16),
        jax.ShapeDtypeStruct((b, s, W_B), BF16),
        jax.ShapeDtypeStruct((b, lb, W_B), F32),
        jax.ShapeDtypeStruct((b, lb, W_B), F32),
    )
    return pl.pallas_call(
        _even_proj_kernel,
        grid=(b, nt),
        in_specs=[row(d), _const_spec((1, d)), _const_spec(w_all.shape),
                  _const_spec((1, LANES)), _const_spec(pq.shape), _const_spec(pk.shape)],
        out_specs=(row(H_A * LANES), row(H_A * LANES), row(H_A * LANES), row(W_A), row(W_A), row(H_A),
                   row(W_B), row(W_B), row(W_B), tail, tail),
        out_shape=outs,
        scratch_shapes=[pltpu.VMEM((1, LANES), F32)],
        compiler_params=_params(("arbitrary", "arbitrary")),
        name="even_proj",
    )(x, g, w_all, b_f, pq, pk)


def _fox_kernel(q_ref, k_ref, v_ref, o_ref, m_sc, acc_sc, *, blk, nsub):
    i = pl.program_id(2)
    nh = 2
    chains = [(sub, hh) for sub in range(nsub) for hh in range(nh)]
    qs = {(sub, hh): q_ref[0, sub * blk:(sub + 1) * blk, LANES * hh:LANES * (hh + 1)] for sub, hh in chains}
    r = lax.broadcasted_iota(jnp.int32, (blk, blk), 0)
    c = lax.broadcasted_iota(jnp.int32, (blk, blk), 1)
    causal = c <= r
    m_sc[...] = jnp.full_like(m_sc, NEG_INF)
    acc_sc[...] = jnp.zeros_like(acc_sc)

    def blocks(specs):
        items = [(pl.multiple_of(j * blk, blk), ch, modes[ch[0]])
                 for j, modes in specs for ch in chains if modes[ch[0]] is not None]

        def stage_scores(item):
            start, ch, mode = item
            k = k_ref[0, pl.ds(start, blk), LANES * ch[1]:LANES * (ch[1] + 1)]
            s = _dot_nt(qs[ch], k)
            if mode == "masked":
                s = jnp.where(causal, s, NEG_INF)
            m = m_sc[chains.index(ch)]
            m_new = jnp.maximum(m, jnp.max(s, axis=1, keepdims=True))
            m_sc[chains.index(ch)] = m_new
            return s, m, m_new

        def stage_exp(item, sm):
            s, m, m_new = sm
            p = jnp.concatenate([jnp.exp(s[:, LANES * cb:LANES * (cb + 1)] - m_new)
                                 for cb in range(blk // LANES)], axis=1)
            return p.astype(BF16), jnp.exp(m - m_new)

        def stage_pv(item, pa):
            start, ch, _ = item
            p, alpha = pa
            v = v_ref[0, pl.ds(start, blk), LANES * ch[1]:LANES * (ch[1] + 1)]
            n = chains.index(ch)
            acc_sc[n] = alpha * acc_sc[n] + _dot(p, v)

        _staggered(items, stage_scores, stage_exp, stage_pv)

    @pl.loop(0, i)
    def _(jj):
        blocks([(nsub * jj + d, ["full"] * nsub) for d in range(nsub)])

    blocks([(nsub * i + d, [None if sub < d else ("masked" if sub == d else "full") for sub in range(nsub)])
            for d in range(nsub)])

    lane = lax.broadcasted_iota(jnp.int32, (blk, LANES), 1)
    for sub in range(nsub):
        a0 = acc_sc[chains.index((sub, 0))]
        a1 = acc_sc[chains.index((sub, 1))]
        o0 = a0 / a0[:, HEAD_DIM:HEAD_DIM + 1]
        o1 = a1 / a1[:, HEAD_DIM:HEAD_DIM + 1]
        o_ref[0, sub * blk:(sub + 1) * blk, :] = jnp.where(
            lane < HEAD_DIM, o0, pltpu.roll(o1, HEAD_DIM, 1)).astype(BF16)


def _fox(qaug, kaug, vaug, blk, nsub):
    b, s, _ = qaug.shape
    pairs = H_A // 2
    tq = blk * nsub
    return pl.pallas_call(
        functools.partial(_fox_kernel, blk=blk, nsub=nsub),
        grid=(b, pairs, s // tq),
        in_specs=[pl.BlockSpec((1, tq, 2 * LANES), lambda bi, hp, i: (bi, i, hp)),
                  pl.BlockSpec((1, s, 2 * LANES), lambda bi, hp, i: (bi, 0, hp)),
                  pl.BlockSpec((1, s, 2 * LANES), lambda bi, hp, i: (bi, 0, hp))],
        out_specs=pl.BlockSpec((1, tq, LANES), lambda bi, hp, i: (bi, i, hp)),
        out_shape=jax.ShapeDtypeStruct((b, s, W_A), BF16),
        scratch_shapes=[pltpu.VMEM((2 * nsub, blk, LANES), F32), pltpu.VMEM((2 * nsub, blk, LANES), F32)],
        compiler_params=_params(("arbitrary", "arbitrary", "arbitrary")),
        name="fox",
    )(qaug, kaug, vaug)


def _band_kernel(*refs, tq, win, left, npairs, heads_per_group, use_sinks, shared_kv):
    if use_sinks:
        sink_ref, q_ref, k_ref, v_ref, bias_ref, o_ref = refs
    else:
        q_ref, k_ref, v_ref, bias_ref, o_ref = refs
    g = pl.program_id(1)
    i = pl.program_id(2)
    nvar = bias_ref.shape[0]
    var = jnp.minimum(i, nvar - 1)
    start = pl.multiple_of(jnp.maximum(i * tq - left, 0), LANES)
    lane = lax.broadcasted_iota(jnp.int32, (tq, LANES), 1)
    low = lane < HEAD_DIM
    heads = [(pr, hh) for pr in range(npairs) for hh in range(2)]

    def scores(pr, hh):
        qp = q_ref[0, :, LANES * pr:LANES * (pr + 1)]
        qm = jnp.where(low if hh == 0 else jnp.logical_not(low), qp, jnp.zeros_like(qp))
        kl = 0 if shared_kv else LANES * pr
        return _dot_nt(qm, k_ref[0, pl.ds(start, win), kl:kl + LANES])

    def sink(pr, hh):
        return sink_ref[g * heads_per_group + 2 * pr + hh]

    def stage_scores(head):
        s = scores(*head) + bias_ref[var, 2 * head[0] + head[1]]
        m = jnp.max(s, axis=1, keepdims=True)
        return s, (jnp.maximum(m, sink(*head)) if use_sinks else m)

    def stage_exp(head, sm):
        s, m = sm
        e = jnp.exp(s - m)
        l = jnp.sum(e, axis=1, keepdims=True)
        return e.astype(BF16), (l + jnp.exp(sink(*head) - m) if use_sinks else l)

    outs = {}

    def stage_pv(head, pl_):
        p, l = pl_
        pr, hh = head
        kl = 0 if shared_kv else LANES * pr
        outs[hh] = _dot(p, v_ref[0, pl.ds(start, win), kl:kl + LANES]) / l
        if hh == 1:
            o_ref[0, :, LANES * pr:LANES * (pr + 1)] = jnp.where(low, outs[0], outs[1]).astype(BF16)

    _staggered(heads, stage_scores, stage_exp, stage_pv)


def _band(q, k, v, bias, sinks, *, tq, left, npairs, shared_kv, name):
    b, s, wq = q.shape
    win = bias.shape[-1]
    wblk = npairs * LANES
    groups = wq // wblk
    hpg = 2 * npairs
    use_sinks = sinks is not None
    kern = functools.partial(_band_kernel, tq=tq, win=win, left=left, npairs=npairs,
                             heads_per_group=hpg, use_sinks=use_sinks, shared_kv=shared_kv)
    kvw = LANES if shared_kv else wblk
    resident = lambda shape, imap: pl.BlockSpec(shape, imap, pipeline_mode=pl.Buffered(1))
    in_specs = [pl.BlockSpec((1, tq, wblk), lambda bi, g, i: (bi, i, g)),
                resident((1, s, kvw), lambda bi, g, i: (bi, 0, g)),
                resident((1, s, kvw), lambda bi, g, i: (bi, 0, g)),
                resident((bias.shape[0], hpg, tq, win), lambda bi, g, i: (0, g, 0, 0))]
    args = [q, k, v, bias]
    if use_sinks:
        in_specs = [pl.BlockSpec(memory_space=pltpu.SMEM)] + in_specs
        args = [sinks] + args
    return pl.pallas_call(
        kern,
        grid=(b, groups, s // tq),
        in_specs=in_specs,
        out_specs=pl.BlockSpec((1, tq, wblk), lambda bi, g, i: (bi, i, g)),
        out_shape=jax.ShapeDtypeStruct((b, s, wq), BF16),
        compiler_params=_params(("arbitrary", "arbitrary", "arbitrary")),
        name=name,
    )(*args)


def _out_ffn_kernel(*refs, n_attn, d_ff, final_norm):
    x_ref = refs[0]
    attn = refs[1:1 + 2 * n_attn]
    gffn_ref, win_ref, wout_ref, gfin_ref, o_ref = refs[1 + 2 * n_attn:]
    y = x_ref[...]
    for a in range(n_attn):
        y = y + _dot(attn[2 * a][...], attn[2 * a + 1][...])
    h = _rmsnorm(y, gffn_ref[...]).astype(BF16)
    gu = _dot(h, win_ref[...])
    gate = gu[:, :d_ff]
    up = gu[:, d_ff:]
    act = (gate * (1.0 / (1.0 + jnp.exp(-gate))) * up).astype(BF16)
    y = y + _dot(act, wout_ref[...])
    o_ref[...] = _rmsnorm(y, gfin_ref[...]) if final_norm else y


def _out_ffn(x, attn_pairs, g_ffn, w_in, w_out, g_fin, tm, layer, final_norm):
    n, d = x.shape
    d_ff = w_out.shape[1]
    row = lambda w: pl.BlockSpec((tm, w), lambda i: (i, 0))
    slab = lambda w: pl.BlockSpec((None,) + w.shape[1:], lambda i: (layer, 0, 0), pipeline_mode=pl.Buffered(1))
    args, specs = [x], [row(d)]
    for o, w in attn_pairs:
        args += [o, w]
        specs += [row(o.shape[1]), _const_spec(w.shape)]
    args += [g_ffn, w_in, w_out, g_fin]
    specs += [_const_spec((1, d)), slab(w_in), slab(w_out), _const_spec((1, d))]
    return pl.pallas_call(
        functools.partial(_out_ffn_kernel, n_attn=len(attn_pairs), d_ff=d_ff, final_norm=final_norm),
        grid=(n // tm,),
        in_specs=specs,
        out_specs=row(d),
        out_shape=jax.ShapeDtypeStruct((n, d), F32),
        compiler_params=_params(("arbitrary",)),
        name="out_ffn",
    )(*args)


def _odd_proj_kernel(x_ref, g_ref, w_ref, q_ref, k_ref, v_ref, kt_ref, vt_ref, *, tiles_per_batch, tail):
    t = pl.program_id(1)
    h = _rmsnorm(x_ref[0], g_ref[...]).astype(BF16)
    p = _dot(h, w_ref[...])
    wq = H_C * HEAD_DIM
    q_ref[0] = (p[:, :wq] * SCALE).astype(BF16)
    tm = p.shape[0]
    lane = lax.broadcasted_iota(jnp.int32, (tm, LANES), 1)
    low = lane < HEAD_DIM
    for src, dst in ((p[:, wq:wq + LANES], k_ref), (p[:, wq + LANES:wq + 2 * LANES], v_ref)):
        rolled = pltpu.roll(src, HEAD_DIM, 1)
        dst[0, :, :LANES] = jnp.where(low, src, rolled).astype(BF16)
        dst[0, :, LANES:] = jnp.where(low, rolled, src).astype(BF16)

    @pl.when(t == tiles_per_batch - 1)
    def _():
        kt_ref[0] = p[tm - tail:, wq:wq + LANES]
        vt_ref[0] = p[tm - tail:, wq + LANES:wq + 2 * LANES]


def _odd_proj(x, g, w, tm):
    b, s, d = x.shape
    nt = s // tm
    lc = min(C_LEFT_CHUNKS * CHUNK, s)
    assert lc <= tm
    wq = H_C * HEAD_DIM
    row = lambda w_: pl.BlockSpec((1, tm, w_), lambda bi, ti: (bi, ti, 0))
    tail = pl.BlockSpec((1, lc, LANES), lambda bi, ti: (bi, 0, 0))
    return pl.pallas_call(
        functools.partial(_odd_proj_kernel, tiles_per_batch=nt, tail=lc),
        grid=(b, nt),
        in_specs=[row(d), _const_spec((1, d)), _const_spec(w.shape)],
        out_specs=(row(wq), row(2 * LANES), row(2 * LANES), tail, tail),
        out_shape=(jax.ShapeDtypeStruct((b, s, wq), BF16),
                   jax.ShapeDtypeStruct((b, s, 2 * LANES), BF16),
                   jax.ShapeDtypeStruct((b, s, 2 * LANES), BF16),
                   jax.ShapeDtypeStruct((b, lc, LANES), F32),
                   jax.ShapeDtypeStruct((b, lc, LANES), F32)),
        compiler_params=_params(("arbitrary", "arbitrary")),
        name="odd_proj",
    )(x, g, w)


def _sample_proj_kernel(*refs, gated):
    if gated:
        x_ref, g_ref, w_ref, wf_ref, bf_ref, p_ref, logf_ref = refs
    else:
        x_ref, g_ref, w_ref, p_ref = refs
    h = _rmsnorm(x_ref[...], g_ref[...]).astype(BF16)
    p_ref[...] = _dot(h, w_ref[...])
    if gated:
        logf_ref[...] = _log_sigmoid(_dot(h, wf_ref[...]) + bf_ref[...])


def _sample_proj(x, g, w, tm, w_f=None, b_f=None):
    n, d = x.shape
    row = lambda w_: pl.BlockSpec((tm, w_), lambda i: (i, 0))
    in_specs = [row(d), _const_spec((1, d)), _const_spec(w.shape)]
    args = [x, g, w]
    out_specs = row(w.shape[1])
    out_shape = jax.ShapeDtypeStruct((n, w.shape[1]), F32)
    if w_f is not None:
        in_specs += [_const_spec(w_f.shape), _const_spec((1, LANES))]
        args += [w_f, b_f]
        out_specs = (out_specs, row(LANES))
        out_shape = (out_shape, jax.ShapeDtypeStruct((n, LANES), F32))
    return pl.pallas_call(
        functools.partial(_sample_proj_kernel, gated=w_f is not None),
        grid=(n // tm,),
        in_specs=in_specs,
        out_specs=out_specs,
        out_shape=out_shape,
        compiler_params=_params(("arbitrary",)),
        name="sample_proj",
    )(*args)


def _block_diag_q(q, nheads, scale):
    t, w = q.shape
    tiled = jnp.concatenate([q] * nheads, axis=0)
    r = lax.broadcasted_iota(jnp.int32, (nheads * t, w), 0)
    c = lax.broadcasted_iota(jnp.int32, (nheads * t, w), 1)
    return jnp.where(r // t == c // HEAD_DIM, tiled * scale, 0.0).astype(BF16)


def _block_diag_extract(o_all, nheads, t):
    w = o_all.shape[1]
    c = lax.broadcasted_iota(jnp.int32, (t, w), 1)
    out = jnp.zeros((t, w), F32)
    for h in range(nheads):
        out = jnp.where(c // HEAD_DIM == h, o_all[h * t:(h + 1) * t, :], out)
    return out


def _dense_heads(ref, nheads, nkeys):
    pairs = [jnp.concatenate([ref[0, pl.ds(hd, nkeys, stride=nheads), :] for hd in (2 * j, 2 * j + 1)], axis=1)
             for j in range(nheads // 2)]
    return jnp.concatenate(pairs, axis=1)


def _pad_rows(x, rows):
    return jnp.concatenate([x, jnp.zeros((rows - x.shape[0], x.shape[1]), x.dtype)], axis=0)


def _sample_even_kernel(qa_ref, kan_ref, van_ref, qb_ref, kbn_ref, vbn_ref, ck_ref, cv_ref, lft_ref,
                        cbk_ref, cbv_ref, e3_ref, biasb_ref, oa_ref, ob_ref,
                        bias_sc, m_sc, acc_sc, *, t, past, kc, nkc):
    c = pl.program_id(1)
    rows = H_A * t
    qbd = _block_diag_q(qa_ref[...], H_A, SCALE)

    @pl.when(c == 0)
    def _():
        cum = _cumsum_lanes(lft_ref[0])
        last = past + t - 1
        suffix = cum[:, last:last + 1] - cum
        hi, mid, lo = _split3(suffix)
        s3 = jnp.concatenate([hi, mid, lo, jnp.zeros((LANES - 3 * H_A, suffix.shape[1]), F32)], axis=0)
        s3 = s3.astype(BF16)
        for ch in range(nkc + 1):
            w = kc if ch < nkc else LANES
            bias_sc[ch, :, :w] = _dot(e3_ref[...], s3[:, ch * kc:ch * kc + w])
        m_sc[...] = jnp.full_like(m_sc, NEG_INF)
        acc_sc[...] = jnp.zeros_like(acc_sc)

    def update(s, v):
        m = m_sc[...]
        m_new = jnp.maximum(m, jnp.max(s, axis=1, keepdims=True))
        p = jnp.exp(s - m_new)
        alpha = jnp.exp(m - m_new)
        acc = acc_sc[...]
        lsum = alpha * acc[:, W_A:W_A + 1] + jnp.sum(p, axis=1, keepdims=True)
        o = alpha * acc[:, :W_A] + _dot(p.astype(BF16), v)
        acc_sc[:, :W_A] = o
        acc_sc[:, W_A:] = jnp.broadcast_to(lsum, (rows, LANES))
        m_sc[...] = m_new

    s = _dot_nt(qbd, _dense_heads(ck_ref, H_A, kc).astype(BF16)) + bias_sc[c]
    update(s, _dense_heads(cv_ref, H_A, kc).astype(BF16))

    @pl.when(c == nkc - 1)
    def _():
        kn = _pad_rows(kan_ref[...], LANES).astype(BF16)
        vn = _pad_rows(van_ref[...], LANES).astype(BF16)
        sn = _dot_nt(qbd, kn) + bias_sc[nkc, :, :LANES]
        r = lax.broadcasted_iota(jnp.int32, sn.shape, 0)
        col = lax.broadcasted_iota(jnp.int32, sn.shape, 1)
        sn = jnp.where(col <= r % t, sn, NEG_INF)
        update(sn, vn)
        acc = acc_sc[...]
        o_all = acc[:, :W_A] / acc[:, W_A:W_A + 1]
        oa_ref[...] = _block_diag_extract(o_all, H_A, t).astype(BF16)

        qbd_b = _block_diag_q(qb_ref[...], H_B, SCALE)
        nkeys = cbk_ref.shape[1] // H_B
        kb = jnp.concatenate([_dense_heads(cbk_ref, H_B, nkeys), _pad_rows(kbn_ref[...], LANES)],
                             axis=0).astype(BF16)
        vb = jnp.concatenate([_dense_heads(cbv_ref, H_B, nkeys), _pad_rows(vbn_ref[...], LANES)],
                             axis=0).astype(BF16)
        sb = _dot_nt(qbd_b, kb) + biasb_ref[...]
        mb = jnp.max(sb, axis=1, keepdims=True)
        eb = jnp.exp(sb - mb)
        lb = jnp.sum(eb, axis=1, keepdims=True)
        ob_all = _dot(eb.astype(BF16), vb) / lb
        ob_ref[...] = _block_diag_extract(ob_all, H_B, t).astype(BF16)


def _sample_even(p, cache_k, cache_v, logf_t, cache_bk, cache_bv, e3, bias_b, *, t, kc):
    n = p.shape[0]
    nb = n // t
    past = cache_k.shape[1] // H_A
    nkc = past // kc
    lbh = cache_bk.shape[1]
    pcol = lambda j: pl.BlockSpec((t, W_A), lambda bi, c: (bi, j))
    rows = H_A * t
    return pl.pallas_call(
        functools.partial(_sample_even_kernel, t=t, past=past, kc=kc, nkc=nkc),
        grid=(nb, nkc),
        in_specs=[pcol(0), pcol(1), pcol(2), pcol(3), pcol(4), pcol(5),
                  pl.BlockSpec((1, kc * H_A, HEAD_DIM), lambda bi, c: (bi, c, 0)),
                  pl.BlockSpec((1, kc * H_A, HEAD_DIM), lambda bi, c: (bi, c, 0)),
                  pl.BlockSpec((1, H_A, logf_t.shape[2]), lambda bi, c: (bi, 0, 0)),
                  pl.BlockSpec((1, lbh, HEAD_DIM), lambda bi, c: (bi, 0, 0)),
                  pl.BlockSpec((1, lbh, HEAD_DIM), lambda bi, c: (bi, 0, 0)),
                  _const_spec(e3.shape), _const_spec(bias_b.shape)],
        out_specs=(pl.BlockSpec((t, W_A), lambda bi, c: (bi, 0)),
                   pl.BlockSpec((t, W_B), lambda bi, c: (bi, 0))),
        out_shape=(jax.ShapeDtypeStruct((n, W_A), BF16), jax.ShapeDtypeStruct((n, W_B), BF16)),
        scratch_shapes=[pltpu.VMEM((nkc + 1, rows, kc), F32),
                        pltpu.VMEM((rows, 1), F32),
                        pltpu.VMEM((rows, W_A + LANES), F32)],
        compiler_params=_params(("arbitrary", "arbitrary")),
        name="sample_even",
    )(p, p, p, p, p, p, cache_k, cache_v, logf_t, cache_bk, cache_bv, e3, bias_b)


def _sample_odd_kernel(q_ref, kn_ref, vn_ref, ck_ref, cv_ref, x_ref, bias_ref, sink_ref, o_ref, *, t):
    qbd = _block_diag_q(q_ref[...], H_C, SCALE)
    kall = jnp.concatenate([ck_ref[0], _pad_rows(kn_ref[...], LANES)], axis=0).astype(BF16)
    vall = jnp.concatenate([cv_ref[0], _pad_rows(vn_ref[...], LANES)], axis=0).astype(BF16)
    kexp = _dot(kall, x_ref[...]).astype(BF16)
    vexp = _dot(vall, x_ref[...]).astype(BF16)
    s = _dot_nt(qbd, kexp) + bias_ref[...]
    sk = sink_ref[...]
    m = jnp.maximum(jnp.max(s, axis=1, keepdims=True), sk)
    e = jnp.exp(s - m)
    l = jnp.sum(e, axis=1, keepdims=True) + jnp.exp(sk - m)
    o_all = _dot(e.astype(BF16), vexp) / l
    o_ref[...] = _block_diag_extract(o_all, H_C, t).astype(BF16)


def _sample_odd(p, cache_k, cache_v, xexp, bias, sink_col, *, t):
    n = p.shape[0]
    nb = n // t
    wq = H_C * HEAD_DIM
    lc = cache_k.shape[1]
    return pl.pallas_call(
        functools.partial(_sample_odd_kernel, t=t),
        grid=(nb,),
        in_specs=[pl.BlockSpec((t, wq), lambda bi: (bi, 0)),
                  pl.BlockSpec((t, LANES), lambda bi: (bi, wq // LANES)),
                  pl.BlockSpec((t, LANES), lambda bi: (bi, wq // LANES + 1)),
                  pl.BlockSpec((1, lc, LANES), lambda bi: (bi, 0, 0)),
                  pl.BlockSpec((1, lc, LANES), lambda bi: (bi, 0, 0)),
                  _const_spec(xexp.shape), _const_spec(bias.shape), _const_spec(sink_col.shape)],
        out_specs=pl.BlockSpec((t, wq), lambda bi: (bi, 0)),
        out_shape=jax.ShapeDtypeStruct((n, wq), BF16),
        compiler_params=_params(("arbitrary",)),
        name="sample_odd",
    )(p, p, p, cache_k, cache_v, xexp, bias, sink_col)


def _t5_bucket(rel_mem):
    nb = T5_BUCKETS // 2
    max_exact = nb // 2
    n = jnp.abs(rel_mem)
    large = max_exact + (jnp.log(jnp.maximum(n, 1).astype(F32) / max_exact)
                         / math.log(T5_MAX_DIST / max_exact) * (nb - max_exact)).astype(jnp.int32)
    large = jnp.minimum(large, nb - 1)
    return jnp.where(rel_mem > 0, nb, 0) + jnp.where(n < max_exact, n, large)


def _chunk_valid(tq, win, offsets, left_chunks):
    off = np.asarray(offsets)[:, None, None] // CHUNK
    d = off + (np.arange(tq) // CHUNK)[None, :, None] - (np.arange(win) // CHUNK)[None, None, :]
    return (d >= 0) & (d <= left_chunks)


def _bias_b_of_rel(table, rel):
    idx = np.clip(rel, -B_REL_CLIP, B_REL_CLIP) + B_REL_CLIP
    return table.astype(F32)[idx].T


def _bias_c_of_rel(table, rel):
    return table.astype(F32)[_t5_bucket(-jnp.asarray(rel, jnp.int32))].T


def _toeplitz(bias_of_rel, tq, ncols, left):
    n = tq + ncols - 1
    f_rev = bias_of_rel(tq - 1 + left - np.arange(n))
    flat = jnp.tile(jnp.pad(f_rev, ((0, 0), (0, 1))), (1, tq))[:, :tq * n]
    return flat.reshape(f_rev.shape[0], tq, n)[:, :, tq - 1:tq - 1 + ncols]


def _band_bias_tiles(bias_of_rel, tq, left, offsets, left_chunks):
    win = tq + left
    wide = _toeplitz(bias_of_rel, tq, win + left, left)
    tiles = jnp.stack([wide[:, :, left - o:left - o + win] for o in offsets], axis=0)
    valid = _chunk_valid(tq, win, offsets, left_chunks)
    return jnp.where(valid[:, None], tiles, NEG_INF)


def _sample_bias(bias_of_rel, t, cache_len):
    ncols = cache_len + LANES
    tiles = _toeplitz(bias_of_rel, t, ncols, cache_len)
    tiles = jnp.where((np.arange(ncols) < cache_len + t)[None, None, :], tiles, NEG_INF)
    return tiles.reshape(-1, ncols)


def _placement():
    pq = np.zeros((LANES, H_A * LANES), np.float32)
    pk = np.zeros((LANES, H_A * LANES), np.float32)
    for h in range(H_A):
        for j in range(3):
            pq[8 * j + h, LANES * h + HEAD_DIM + j] = 1.0
            pq[24, LANES * h + HEAD_DIM + 3 + j] = 1.0
            pk[24, LANES * h + HEAD_DIM + j] = 1.0
            pk[8 * j + h, LANES * h + HEAD_DIM + 3 + j] = -1.0
    return jnp.asarray(pq, BF16), jnp.asarray(pk, BF16)


def kernel(x_prompt, x_sample, cache_a_k, cache_a_v, cache_a_logf, cache_b_k, cache_b_v, cache_c_k, cache_c_v,
           norm_mix, norm_ffn, norm_final, w_in_even, b_forget, rel_bias_b, w_out_even, w_in_odd, sinks_c,
           w_out_odd, t5_bias, w_ffn_in, w_ffn_out):
    b, s, d = x_prompt.shape
    nb, t, _ = x_sample.shape
    past = cache_a_k.shape[2]
    n_p, n_s = b * s, nb * t

    w_even = w_in_even[0]
    w_main = w_even[:, :3 * W_A + 3 * W_B].astype(BF16)
    wf = w_even[:, 3 * W_A + 3 * W_B:]
    w_f = jnp.concatenate([wf, wf, wf, jnp.zeros((d, LANES - 3 * H_A), F32)], axis=1).astype(BF16)
    bf = b_forget[0].astype(F32)
    b_f = jnp.concatenate([bf, bf, bf, jnp.zeros((LANES - 3 * H_A,), F32)])[None, :]
    pq, pk = _placement()
    w_oe = w_out_even[0].astype(BF16)
    w_oo = w_out_odd[0].astype(BF16)
    w_odd = w_in_odd[0].astype(BF16)
    w_fi = w_ffn_in.astype(BF16)
    w_fo = w_ffn_out.astype(BF16)
    g_mix = norm_mix.astype(F32)[:, None, :]
    g_ffn = norm_ffn.astype(F32)[:, None, :]
    g_fin = norm_final.astype(F32)[None, :]

    bias_b_of = functools.partial(_bias_b_of_rel, rel_bias_b[0])
    bias_c_of = functools.partial(_bias_c_of_rel, t5_bias)
    tq_b, left_b = 256, B_LEFT_CHUNKS * CHUNK
    offs_b = [min(v * tq_b, left_b) for v in range(left_b // tq_b + 1)]
    bias_b = _band_bias_tiles(bias_b_of, tq_b, left_b, offs_b, B_LEFT_CHUNKS)
    tq_c, left_c = 256, C_LEFT_CHUNKS * CHUNK
    offs_c = [0, left_c]
    bias_c = _band_bias_tiles(bias_c_of, tq_c, left_c, offs_c, C_LEFT_CHUNKS)

    xp = x_prompt
    w_all = jnp.concatenate([w_main[:, 3 * W_A:3 * W_A + W_B], w_f, w_main[:, :3 * W_A],
                             w_main[:, 3 * W_A + W_B:]], axis=1)
    qaug, kaug, vaug, ka, va, logf, qb, kb, vb, kbt, vbt = _even_proj(
        xp, g_mix[0], w_all, b_f, pq, pk, tm=512)
    oa = _fox(qaug, kaug, vaug, blk=512, nsub=2)
    ob = _band(qb, kb, vb, bias_b, None, tq=tq_b, left=left_b, npairs=H_B // 2, shared_kv=False,
               name="band_b")
    xp1 = _out_ffn(xp.reshape(n_p, d),
                   [(oa.reshape(n_p, W_A), w_oe[:W_A]), (ob.reshape(n_p, W_B), w_oe[W_A:])],
                   g_ffn[0], w_fi, w_fo, g_fin, tm=256, layer=0, final_norm=False)

    qc, kcd, vcd, kct, vct = _odd_proj(xp1.reshape(b, s, d), g_mix[1], w_odd, tm=512)
    oc = _band(qc, kcd, vcd, bias_c, sinks_c[0].astype(F32), tq=tq_c, left=left_c, npairs=G_C // 2,
               shared_kv=True, name="band_c")
    y_prompt = _out_ffn(xp1, [(oc.reshape(n_p, H_C * HEAD_DIM), w_oo)],
                        g_ffn[1], w_fi, w_fo, g_fin, tm=256, layer=1, final_norm=True)

    xs = x_sample.reshape(n_s, d)
    ps, logf_sp = _sample_proj(xs, g_mix[0], w_main, tm=256, w_f=w_f, b_f=b_f)
    logf_s = logf_sp[:, :H_A]
    kpad = LANES * -(-(past + t) // LANES)
    logf_all = jnp.concatenate([cache_a_logf[0].astype(F32), logf_s.reshape(nb, t, H_A)], axis=1)
    logf_t = jnp.pad(jnp.swapaxes(logf_all, 1, 2), ((0, 0), (0, 0), (0, kpad - past - t)))
    lbs = cache_b_k.shape[2]
    bias_sb = _sample_bias(bias_b_of, t, lbs)
    e3 = np.zeros((H_A * t, LANES), np.float32)
    for j in range(3):
        e3[np.arange(H_A * t), 8 * j + np.arange(H_A * t) // t] = 1.0
    heads_as_rows = lambda cache: cache[0].reshape(nb, -1, HEAD_DIM)
    oa_s, ob_s = _sample_even(ps, heads_as_rows(cache_a_k), heads_as_rows(cache_a_v), logf_t,
                              heads_as_rows(cache_b_k), heads_as_rows(cache_b_v),
                              jnp.asarray(e3, BF16), bias_sb, t=t, kc=1024)
    xs1 = _out_ffn(xs, [(oa_s, w_oe[:W_A]), (ob_s, w_oe[W_A:])], g_ffn[0], w_fi, w_fo, g_fin,
                   tm=256, layer=0, final_norm=False)

    ps2 = _sample_proj(xs1, g_mix[1], w_odd, tm=256)
    lcs = cache_c_k.shape[2]
    bias_sc = _sample_bias(bias_c_of, t, lcs)
    sink_col = jnp.repeat(sinks_c[0].astype(F32), t)[:, None]
    lane_head = np.arange(H_C * HEAD_DIM) // HEAD_DIM
    src_lane = (lane_head // G_C) * HEAD_DIM + np.arange(H_C * HEAD_DIM) % HEAD_DIM
    xexp = jnp.asarray(np.arange(LANES)[:, None] == src_lane[None, :], BF16)
    oc_s = _sample_odd(ps2, cache_c_k[0].reshape(nb, lcs, LANES), cache_c_v[0].reshape(nb, lcs, LANES),
                       xexp, bias_sc, sink_col, t=t)
    y_sample = _out_ffn(xs1, [(oc_s, w_oo)], g_ffn[1], w_fi, w_fo, g_fin, tm=256, layer=1, final_norm=True)

    wq = H_C * HEAD_DIM
    hd = lambda a, lead, h: a.reshape((1,) + lead + (h, HEAD_DIM))
    return (
        y_prompt.reshape(b, s, d), y_sample.reshape(nb, t, d),
        hd(ka, (b, s), H_A), hd(va, (b, s), H_A), logf[None],
        hd(kbt, (b, kbt.shape[1]), H_B), hd(vbt, (b, vbt.shape[1]), H_B),
        hd(kct, (b, kct.shape[1]), HKV_C), hd(vct, (b, vct.shape[1]), HKV_C),
        hd(ps[:, W_A:2 * W_A], (nb, t), H_A), hd(ps[:, 2 * W_A:3 * W_A], (nb, t), H_A),
        logf_s.reshape(1, nb, t, H_A),
        hd(ps[:, 3 * W_A + W_B:3 * W_A + 2 * W_B], (nb, t), H_B),
        hd(ps[:, 3 * W_A + 2 * W_B:3 * W_A + 3 * W_B], (nb, t), H_B),
        hd(ps2[:, wq:wq + LANES], (nb, t), HKV_C), hd(ps2[:, wq + LANES:wq + 2 * LANES], (nb, t), HKV_C),
    )
```

```python
import functools
import math

import jax
import jax.numpy as jnp
import numpy as np
from jax import lax
from jax.experimental import pallas as pl
from jax.experimental.pallas import tpu as pltpu

D_MODEL = 1024
HEAD_DIM = 64
CHUNK = 64
H_A = 8
H_B = 8
B_LEFT_CHUNKS = 8
B_REL_CLIP = 128
H_C = 16
HKV_C = 2
G_C = H_C // HKV_C
WINDOW = 128
C_LEFT_CHUNKS = WINDOW // CHUNK
T5_BUCKETS = 32
T5_MAX_DIST = 128
EPS = 1e-6
W_A = H_A * HEAD_DIM
W_B = H_B * HEAD_DIM
SCALE = HEAD_DIM ** -0.5

LANES = 128
VMEM_LIMIT = 60 * 1024 * 1024

F32 = jnp.float32
BF16 = jnp.bfloat16
NEG_INF = float("-inf")

_NT = (((1,), (1,)), ((), ()))


def _dot(a, b):
    return jnp.dot(a, b, preferred_element_type=F32)


def _dot_nt(a, b):
    return lax.dot_general(a, b, _NT, preferred_element_type=F32)


def _rmsnorm(x, g):
    ms = jnp.mean(x * x, axis=-1, keepdims=True)
    return x * lax.rsqrt(ms + EPS) * g


def _log_sigmoid(x):
    return jnp.minimum(x, 0.0) - jnp.log1p(jnp.exp(-jnp.abs(x)))


def _split3(x):
    hi = x.astype(BF16).astype(F32)
    r1 = x - hi
    mid = r1.astype(BF16).astype(F32)
    lo = (r1 - mid).astype(BF16).astype(F32)
    return hi, mid, lo


def _cumsum_rows(x):
    n = x.shape[0]
    row = lax.broadcasted_iota(jnp.int32, x.shape, 0)
    s = 1
    while s < n:
        x = x + jnp.where(row >= s, pltpu.roll(x, s, 0), 0.0)
        s *= 2
    return x


def _cumsum_lanes(x):
    n = x.shape[1]
    col = lax.broadcasted_iota(jnp.int32, x.shape, 1)
    s = 1
    while s < n:
        x = x + jnp.where(col >= s, pltpu.roll(x, s, 1), 0.0)
        s *= 2
    return x


def _staggered(items, stage_a, stage_b, stage_c):
    n = len(items)
    a_out, b_out = {}, {}
    for step in range(n + 2):
        if step < n:
            a_out[step] = stage_a(items[step])
        if 0 <= step - 1 < n:
            b_out[step - 1] = stage_b(items[step - 1], a_out.pop(step - 1))
        if 0 <= step - 2 < n:
            stage_c(items[step - 2], b_out.pop(step - 2))


def _const_spec(shape):
    nd = len(shape)
    return pl.BlockSpec(shape, lambda *_: (0,) * nd, pipeline_mode=pl.Buffered(1))


def _params(sem):
    return pltpu.CompilerParams(dimension_semantics=sem, vmem_limit_bytes=VMEM_LIMIT)


def _even_proj_kernel(x_ref, g_ref, w_ref, bf_ref, pq_ref, pk_ref,
                      qaug_ref, kaug_ref, vaug_ref, ka_ref, va_ref, logf_ref,
                      qb_ref, kb_ref, vb_ref, kbt_ref, vbt_ref, carry_ref):
    @pl.when(pl.program_id(1) == 0)
    def _():
        carry_ref[...] = jnp.zeros_like(carry_ref)

    h = _rmsnorm(x_ref[0], g_ref[...]).astype(BF16)
    tm = h.shape[0]
    lane = lax.broadcasted_iota(jnp.int32, (tm, LANES), 1)
    low = lane < HEAD_DIM
    first = W_B + LANES

    def chunk(n):
        return _dot(h, w_ref[:, first + W_A * n:first + W_A * (n + 1)])

    pf = _dot(h, w_ref[:, :first])
    qb_ref[0] = (pf[:, :W_B] * SCALE).astype(BF16)
    fa = pf[:, W_B:]
    pc = chunk(3)
    kb_ref[0] = pc.astype(BF16)
    kbt_ref[0] = pc
    pc = chunk(4)
    vb_ref[0] = pc.astype(BF16)
    vbt_ref[0] = pc
    logf = _log_sigmoid(fa + bf_ref[...])
    logf_ref[0] = logf[:, :H_A]
    c = _cumsum_rows(logf) + carry_ref[...]
    carry_ref[...] = c[tm - 1:tm, :]
    hi, mid, lo = _split3(c)
    a3 = jnp.where(lane < 8, hi, jnp.where(lane < 16, mid, jnp.where(lane < 24, lo,
                   jnp.where(lane == 24, 1.0, 0.0)))).astype(BF16)
    augq = _dot(a3, pq_ref[...])
    augk = _dot(a3, pk_ref[...])
    vone = jnp.where(lane == HEAD_DIM, 1.0, 0.0)

    def per_head(dst_ref, pc, spare):
        for j in range(H_A // 2):
            pair = pc[:, LANES * j:LANES * (j + 1)]
            for hh, val in enumerate((pair, pltpu.roll(pair, HEAD_DIM, 1))):
                o = slice(LANES * (2 * j + hh), LANES * (2 * j + hh + 1))
                dst_ref[0, :, o] = jnp.where(low, val, vone if spare is None else spare[:, o]).astype(BF16)

    pc = chunk(0)
    per_head(qaug_ref, pc * SCALE, augq)
    pc = chunk(1)
    ka_ref[0] = pc
    per_head(kaug_ref, pc, augk)
    pc = chunk(2)
    va_ref[0] = pc
    per_head(vaug_ref, pc, None)


def _even_proj(x, g, w_all, b_f, pq, pk, tm):
    b, s, d = x.shape
    nt = s // tm
    lb = min(B_LEFT_CHUNKS * CHUNK, s)
    assert lb == tm, "band-state tail must be exactly one row tile"
    row = lambda w: pl.BlockSpec((1, tm, w), lambda bi, ti: (bi, ti, 0))
    tail = pl.BlockSpec((1, lb, W_B), lambda bi, ti: (bi, 0, 0))
    outs = (
        jax.ShapeDtypeStruct((b, s, H_A * LANES), BF16),
        jax.ShapeDtypeStruct((b, s, H_A * LANES), BF16),
        jax.ShapeDtypeStruct((b, s, H_A * LANES), BF16),
        jax.ShapeDtypeStruct((b, s, W_A), F32),
        jax.ShapeDtypeStruct((b, s, W_A), F32),
        jax.ShapeDtypeStruct((b, s, H_A), F32),
        jax.ShapeDtypeStruct((b, s, W_B), BF16),
        jax.ShapeDtypeStruct((b, s, W_B), BF16),
        jax.ShapeDtypeStruct((b, s, W_B), BF16),
        jax.ShapeDtypeStruct((b, lb, W_B), F32),
        jax.ShapeDtypeStruct((b, lb, W_B), F32),
    )
    return pl.pallas_call(
        _even_proj_kernel,
        grid=(b, nt),
        in_specs=[row(d), _const_spec((1, d)), _const_spec(w_all.shape),
                  _const_spec((1, LANES)), _const_spec(pq.shape), _const_spec(pk.shape)],
        out_specs=(row(H_A * LANES), row(H_A * LANES), row(H_A * LANES), row(W_A), row(W_A), row(H_A),
                   row(W_B), row(W_B), row(W_B), tail, tail),
        out_shape=outs,
        scratch_shapes=[pltpu.VMEM((1, LANES), F32)],
        compiler_params=_params(("arbitrary", "arbitrary")),
        name="even_proj",
    )(x, g, w_all, b_f, pq, pk)


def _fox_kernel(q_ref, k_ref, v_ref, o_ref, m_sc, acc_sc, *, blk, nsub):
    i = pl.program_id(2)
    nh = 2
    chains = [(sub, hh) for sub in range(nsub) for hh in range(nh)]
    qs = {(sub, hh): q_ref[0, sub * blk:(sub + 1) * blk, LANES * hh:LANES * (hh + 1)] for sub, hh in chains}
    r = lax.broadcasted_iota(jnp.int32, (blk, blk), 0)
    c = lax.broadcasted_iota(jnp.int32, (blk, blk), 1)
    causal = c <= r
    m_sc[...] = jnp.full_like(m_sc, NEG_INF)
    acc_sc[...] = jnp.zeros_like(acc_sc)

    def blocks(specs):
        items = [(pl.multiple_of(j * blk, blk), ch, modes[ch[0]])
                 for j, modes in specs for ch in chains if modes[ch[0]] is not None]

        def stage_scores(item):
            start, ch, mode = item
            k = k_ref[0, pl.ds(start, blk), LANES * ch[1]:LANES * (ch[1] + 1)]
            s = _dot_nt(qs[ch], k)
            if mode == "masked":
                s = jnp.where(causal, s, NEG_INF)
            m = m_sc[chains.index(ch)]
            m_new = jnp.maximum(m, jnp.max(s, axis=1, keepdims=True))
            m_sc[chains.index(ch)] = m_new
            return s, m, m_new

        def stage_exp(item, sm):
            s, m, m_new = sm
            p = jnp.concatenate([jnp.exp(s[:, LANES * cb:LANES * (cb + 1)] - m_new)
                                 for cb in range(blk // LANES)], axis=1)
            return p.astype(BF16), jnp.exp(m - m_new)

        def stage_pv(item, pa):
            start, ch, _ = item
            p, alpha = pa
            v = v_ref[0, pl.ds(start, blk), LANES * ch[1]:LANES * (ch[1] + 1)]
            n = chains.index(ch)
            acc_sc[n] = alpha * acc_sc[n] + _dot(p, v)

        _staggered(items, stage_scores, stage_exp, stage_pv)

    @pl.loop(0, i)
    def _(jj):
        blocks([(nsub * jj + d, ["full"] * nsub) for d in range(nsub)])

    blocks([(nsub * i + d, [None if sub < d else ("masked" if sub == d else "full") for sub in range(nsub)])
            for d in range(nsub)])

    lane = lax.broadcasted_iota(jnp.int32, (blk, LANES), 1)
    for sub in range(nsub):
        a0 = acc_sc[chains.index((sub, 0))]
        a1 = acc_sc[chains.index((sub, 1))]
        o0 = a0 / a0[:, HEAD_DIM:HEAD_DIM + 1]
        o1 = a1 / a1[:, HEAD_DIM:HEAD_DIM + 1]
        o_ref[0, sub * blk:(sub + 1) * blk, :] = jnp.where(
            lane < HEAD_DIM, o0, pltpu.roll(o1, HEAD_DIM, 1)).astype(BF16)


def _fox(qaug, kaug, vaug, blk, nsub):
    b, s, _ = qaug.shape
    pairs = H_A // 2
    tq = blk * nsub
    return pl.pallas_call(
        functools.partial(_fox_kernel, blk=blk, nsub=nsub),
        grid=(b, pairs, s // tq),
        in_specs=[pl.BlockSpec((1, tq, 2 * LANES), lambda bi, hp, i: (bi, i, hp)),
                  pl.BlockSpec((1, s, 2 * LANES), lambda bi, hp, i: (bi, 0, hp)),
                  pl.BlockSpec((1, s, 2 * LANES), lambda bi, hp, i: (bi, 0, hp))],
        out_specs=pl.BlockSpec((1, tq, LANES), lambda bi, hp, i: (bi, i, hp)),
        out_shape=jax.ShapeDtypeStruct((b, s, W_A), BF16),
        scratch_shapes=[pltpu.VMEM((2 * nsub, blk, LANES), F32), pltpu.VMEM((2 * nsub, blk, LANES), F32)],
        compiler_params=_params(("arbitrary", "arbitrary", "arbitrary")),
        name="fox",
    )(qaug, kaug, vaug)


def _band_kernel(*refs, tq, win, left, npairs, heads_per_group, use_sinks, shared_kv):
    if use_sinks:
        sink_ref, q_ref, k_ref, v_ref, bias_ref, o_ref = refs
    else:
        q_ref, k_ref, v_ref, bias_ref, o_ref = refs
    g = pl.program_id(1)
    i = pl.program_id(2)
    nvar = bias_ref.shape[0]
    var = jnp.minimum(i, nvar - 1)
    start = pl.multiple_of(jnp.maximum(i * tq - left, 0), LANES)
    lane = lax.broadcasted_iota(jnp.int32, (tq, LANES), 1)
    low = lane < HEAD_DIM
    heads = [(pr, hh) for pr in range(npairs) for hh in range(2)]

    def scores(pr, hh):
        qp = q_ref[0, :, LANES * pr:LANES * (pr + 1)]
        qm = jnp.where(low if hh == 0 else jnp.logical_not(low), qp, jnp.zeros_like(qp))
        kl = 0 if shared_kv else LANES * pr
        return _dot_nt(qm, k_ref[0, pl.ds(start, win), kl:kl + LANES])

    def sink(pr, hh):
        return sink_ref[g * heads_per_group + 2 * pr + hh]

    def stage_scores(head):
        s = scores(*head) + bias_ref[var, 2 * head[0] + head[1]]
        m = jnp.max(s, axis=1, keepdims=True)
        return s, (jnp.maximum(m, sink(*head)) if use_sinks else m)

    def stage_exp(head, sm):
        s, m = sm
        e = jnp.exp(s - m)
        l = jnp.sum(e, axis=1, keepdims=True)
        return e.astype(BF16), (l + jnp.exp(sink(*head) - m) if use_sinks else l)

    outs = {}

    def stage_pv(head, pl_):
        p, l = pl_
        pr, hh = head
        kl = 0 if shared_kv else LANES * pr
        outs[hh] = _dot(p, v_ref[0, pl.ds(start, win), kl:kl + LANES]) / l
        if hh == 1:
            o_ref[0, :, LANES * pr:LANES * (pr + 1)] = jnp.where(low, outs[0], outs[1]).astype(BF16)

    _staggered(heads, stage_scores, stage_exp, stage_pv)


def _band(q, k, v, bias, sinks, *, tq, left, npairs, shared_kv, name):
    b, s, wq = q.shape
    win = bias.shape[-1]
    wblk = npairs * LANES
    groups = wq // wblk
    hpg = 2 * npairs
    use_sinks = sinks is not None
    kern = functools.partial(_band_kernel, tq=tq, win=win, left=left, npairs=npairs,
                             heads_per_group=hpg, use_sinks=use_sinks, shared_kv=shared_kv)
    kvw = LANES if shared_kv else wblk
    resident = lambda shape, imap: pl.BlockSpec(shape, imap, pipeline_mode=pl.Buffered(1))
    in_specs = [pl.BlockSpec((1, tq, wblk), lambda bi, g, i: (bi, i, g)),
                resident((1, s, kvw), lambda bi, g, i: (bi, 0, g)),
                resident((1, s, kvw), lambda bi, g, i: (bi, 0, g)),
                resident((bias.shape[0], hpg, tq, win), lambda bi, g, i: (0, g, 0, 0))]
    args = [q, k, v, bias]
    if use_sinks:
        in_specs = [pl.BlockSpec(memory_space=pltpu.SMEM)] + in_specs
        args = [sinks] + args
    return pl.pallas_call(
        kern,
        grid=(b, groups, s // tq),
        in_specs=in_specs,
        out_specs=pl.BlockSpec((1, tq, wblk), lambda bi, g, i: (bi, i, g)),
        out_shape=jax.ShapeDtypeStruct((b, s, wq), BF16),
        compiler_params=_params(("arbitrary", "arbitrary", "arbitrary")),
        name=name,
    )(*args)


def _out_ffn_kernel(*refs, n_attn, d_ff, final_norm):
    x_ref = refs[0]
    attn = refs[1:1 + 2 * n_attn]
    gffn_ref, win_ref, wout_ref, gfin_ref, o_ref = refs[1 + 2 * n_attn:]
    y = x_ref[...]
    for a in range(n_attn):
        y = y + _dot(attn[2 * a][...], attn[2 * a + 1][...])
    h = _rmsnorm(y, gffn_ref[...]).astype(BF16)
    gu = _dot(h, win_ref[...])
    gate = gu[:, :d_ff]
    up = gu[:, d_ff:]
    act = (gate * (1.0 / (1.0 + jnp.exp(-gate))) * up).astype(BF16)
    y = y + _dot(act, wout_ref[...])
    o_ref[...] = _rmsnorm(y, gfin_ref[...]) if final_norm else y


def _out_ffn(x, attn_pairs, g_ffn, w_in, w_out, g_fin, tm, layer, final_norm):
    n, d = x.shape
    d_ff = w_out.shape[1]
    row = lambda w: pl.BlockSpec((tm, w), lambda i: (i, 0))
    slab = lambda w: pl.BlockSpec((None,) + w.shape[1:], lambda i: (layer, 0, 0), pipeline_mode=pl.Buffered(1))
    args, specs = [x], [row(d)]
    for o, w in attn_pairs:
        args += [o, w]
        specs += [row(o.shape[1]), _const_spec(w.shape)]
    args += [g_ffn, w_in, w_out, g_fin]
    specs += [_const_spec((1, d)), slab(w_in), slab(w_out), _const_spec((1, d))]
    return pl.pallas_call(
        functools.partial(_out_ffn_kernel, n_attn=len(attn_pairs), d_ff=d_ff, final_norm=final_norm),
        grid=(n // tm,),
        in_specs=specs,
        out_specs=row(d),
        out_shape=jax.ShapeDtypeStruct((n, d), F32),
        compiler_params=_params(("arbitrary",)),
        name="out_ffn",
    )(*args)


def _odd_proj_kernel(x_ref, g_ref, w_ref, q_ref, k_ref, v_ref, kt_ref, vt_ref, *, tiles_per_batch, tail):
    t = pl.program_id(1)
    h = _rmsnorm(x_ref[0], g_ref[...]).astype(BF16)
    p = _dot(h, w_ref[...])
    wq = H_C * HEAD_DIM
    q_ref[0] = (p[:, :wq] * SCALE).astype(BF16)
    tm = p.shape[0]
    lane = lax.broadcasted_iota(jnp.int32, (tm, LANES), 1)
    low = lane < HEAD_DIM
    for src, dst in ((p[:, wq:wq + LANES], k_ref), (p[:, wq + LANES:wq + 2 * LANES], v_ref)):
        rolled = pltpu.roll(src, HEAD_DIM, 1)
        dst[0, :, :LANES] = jnp.where(low, src, rolled).astype(BF16)
        dst[0, :, LANES:] = jnp.where(low, rolled, src).astype(BF16)

    @pl.when(t == tiles_per_batch - 1)
    def _():
        kt_ref[0] = p[tm - tail:, wq:wq + LANES]
        vt_ref[0] = p[tm - tail:, wq + LANES:wq + 2 * LANES]


def _odd_proj(x, g, w, tm):
    b, s, d = x.shape
    nt = s // tm
    lc = min(C_LEFT_CHUNKS * CHUNK, s)
    assert lc <= tm
    wq = H_C * HEAD_DIM
    row = lambda w_: pl.BlockSpec((1, tm, w_), lambda bi, ti: (bi, ti, 0))
    tail = pl.BlockSpec((1, lc, LANES), lambda bi, ti: (bi, 0, 0))
    return pl.pallas_call(
        functools.partial(_odd_proj_kernel, tiles_per_batch=nt, tail=lc),
        grid=(b, nt),
        in_specs=[row(d), _const_spec((1, d)), _const_spec(w.shape)],
        out_specs=(row(wq), row(2 * LANES), row(2 * LANES), tail, tail),
        out_shape=(jax.ShapeDtypeStruct((b, s, wq), BF16),
                   jax.ShapeDtypeStruct((b, s, 2 * LANES), BF16),
                   jax.ShapeDtypeStruct((b, s, 2 * LANES), BF16),
                   jax.ShapeDtypeStruct((b, lc, LANES), F32),
                   jax.ShapeDtypeStruct((b, lc, LANES), F32)),
        compiler_params=_params(("arbitrary", "arbitrary")),
        name="odd_proj",
    )(x, g, w)


def _sample_proj_kernel(*refs, gated):
    if gated:
        x_ref, g_ref, w_ref, wf_ref, bf_ref, p_ref, logf_ref = refs
    else:
        x_ref, g_ref, w_ref, p_ref = refs
    h = _rmsnorm(x_ref[...], g_ref[...]).astype(BF16)
    p_ref[...] = _dot(h, w_ref[...])
    if gated:
        logf_ref[...] = _log_sigmoid(_dot(h, wf_ref[...]) + bf_ref[...])


def _sample_proj(x, g, w, tm, w_f=None, b_f=None):
    n, d = x.shape
    row = lambda w_: pl.BlockSpec((tm, w_), lambda i: (i, 0))
    in_specs = [row(d), _const_spec((1, d)), _const_spec(w.shape)]
    args = [x, g, w]
    out_specs = row(w.shape[1])
    out_shape = jax.ShapeDtypeStruct((n, w.shape[1]), F32)
    if w_f is not None:
        in_specs += [_const_spec(w_f.shape), _const_spec((1, LANES))]
        args += [w_f, b_f]
        out_specs = (out_specs, row(LANES))
        out_shape = (out_shape, jax.ShapeDtypeStruct((n, LANES), F32))
    return pl.pallas_call(
        functools.partial(_sample_proj_kernel, gated=w_f is not None),
        grid=(n // tm,),
        in_specs=in_specs,
        out_specs=out_specs,
        out_shape=out_shape,
        compiler_params=_params(("arbitrary",)),
        name="sample_proj",
    )(*args)


def _block_diag_q(q, nheads, scale):
    t, w = q.shape
    tiled = jnp.concatenate([q] * nheads, axis=0)
    r = lax.broadcasted_iota(jnp.int32, (nheads * t, w), 0)
    c = lax.broadcasted_iota(jnp.int32, (nheads * t, w), 1)
    return jnp.where(r // t == c // HEAD_DIM, tiled * scale, 0.0).astype(BF16)


def _block_diag_extract(o_all, nheads, t):
    w = o_all.shape[1]
    c = lax.broadcasted_iota(jnp.int32, (t, w), 1)
    out = jnp.zeros((t, w), F32)
    for h in range(nheads):
        out = jnp.where(c // HEAD_DIM == h, o_all[h * t:(h + 1) * t, :], out)
    return out


def _heads_major(ref):
    _, nh, hd, nk = ref.shape
    return ref[0].reshape(nh * hd, nk).astype(BF16)


def _pad_rows(x, rows):
    return jnp.concatenate([x, jnp.zeros((rows - x.shape[0], x.shape[1]), x.dtype)], axis=0)


def _sample_even_kernel(qa_ref, kan_ref, van_ref, qb_ref, kbn_ref, vbn_ref, ck_ref, cv_ref, lft_ref,
                        cbk_ref, cbv_ref, e3_ref, biasb_ref, oa_ref, ob_ref,
                        bias_sc, m_sc, acc_sc, *, t, past, kc, nkc):
    c = pl.program_id(1)
    rows = H_A * t
    qbd = _block_diag_q(qa_ref[...], H_A, SCALE)

    @pl.when(c == 0)
    def _():
        cum = _cumsum_lanes(lft_ref[0])
        last = past + t - 1
        suffix = cum[:, last:last + 1] - cum
        hi, mid, lo = _split3(suffix)
        s3 = jnp.concatenate([hi, mid, lo, jnp.zeros((LANES - 3 * H_A, suffix.shape[1]), F32)], axis=0)
        s3 = s3.astype(BF16)
        for ch in range(nkc + 1):
            w = kc if ch < nkc else LANES
            bias_sc[ch, :, :w] = _dot(e3_ref[...], s3[:, ch * kc:ch * kc + w])
        m_sc[...] = jnp.full_like(m_sc, NEG_INF)
        acc_sc[...] = jnp.zeros_like(acc_sc)

    def update(s, pv):
        m = m_sc[...]
        m_new = jnp.maximum(m, jnp.max(s, axis=1, keepdims=True))
        p = jnp.exp(s - m_new)
        alpha = jnp.exp(m - m_new)
        acc = acc_sc[...]
        lsum = alpha * acc[:, W_A:W_A + 1] + jnp.sum(p, axis=1, keepdims=True)
        o = alpha * acc[:, :W_A] + pv(p.astype(BF16))
        acc_sc[:, :W_A] = o
        acc_sc[:, W_A:] = jnp.broadcast_to(lsum, (rows, LANES))
        m_sc[...] = m_new

    s = _dot(qbd, _heads_major(ck_ref)) + bias_sc[c]
    vt = _heads_major(cv_ref)
    update(s, lambda p: _dot_nt(p, vt))

    @pl.when(c == nkc - 1)
    def _():
        kn = _pad_rows(kan_ref[...], LANES).astype(BF16)
        vn = _pad_rows(van_ref[...], LANES).astype(BF16)
        sn = _dot_nt(qbd, kn) + bias_sc[nkc, :, :LANES]
        r = lax.broadcasted_iota(jnp.int32, sn.shape, 0)
        col = lax.broadcasted_iota(jnp.int32, sn.shape, 1)
        sn = jnp.where(col <= r % t, sn, NEG_INF)
        update(sn, lambda p: _dot(p, vn))
        acc = acc_sc[...]
        o_all = acc[:, :W_A] / acc[:, W_A:W_A + 1]
        oa_ref[...] = _block_diag_extract(o_all, H_A, t).astype(BF16)

        qbd_b = _block_diag_q(qb_ref[...], H_B, SCALE)
        lbk = cbk_ref.shape[3]
        kbn = _pad_rows(kbn_ref[...], LANES).astype(BF16)
        vbn = _pad_rows(vbn_ref[...], LANES).astype(BF16)
        sb = jnp.concatenate([_dot(qbd_b, _heads_major(cbk_ref)), _dot_nt(qbd_b, kbn)], axis=1) + biasb_ref[...]
        mb = jnp.max(sb, axis=1, keepdims=True)
        eb = jnp.exp(sb - mb)
        lb = jnp.sum(eb, axis=1, keepdims=True)
        eb = eb.astype(BF16)
        ob_all = (_dot_nt(eb[:, :lbk], _heads_major(cbv_ref)) + _dot(eb[:, lbk:], vbn)) / lb
        ob_ref[...] = _block_diag_extract(ob_all, H_B, t).astype(BF16)


def _sample_even(p, cache_k, cache_v, logf_t, cache_bk, cache_bv, e3, bias_b, *, t, kc):
    n = p.shape[0]
    nb = n // t
    past = cache_k.shape[3]
    nkc = past // kc
    lbk = cache_bk.shape[3]
    pcol = lambda j: pl.BlockSpec((t, W_A), lambda bi, c: (bi, j))
    rows = H_A * t
    return pl.pallas_call(
        functools.partial(_sample_even_kernel, t=t, past=past, kc=kc, nkc=nkc),
        grid=(nb, nkc),
        in_specs=[pcol(0), pcol(1), pcol(2), pcol(3), pcol(4), pcol(5),
                  pl.BlockSpec((1, H_A, HEAD_DIM, kc), lambda bi, c: (bi, 0, 0, c)),
                  pl.BlockSpec((1, H_A, HEAD_DIM, kc), lambda bi, c: (bi, 0, 0, c)),
                  pl.BlockSpec((1, H_A, logf_t.shape[2]), lambda bi, c: (bi, 0, 0)),
                  pl.BlockSpec((1, H_B, HEAD_DIM, lbk), lambda bi, c: (bi, 0, 0, 0)),
                  pl.BlockSpec((1, H_B, HEAD_DIM, lbk), lambda bi, c: (bi, 0, 0, 0)),
                  _const_spec(e3.shape), _const_spec(bias_b.shape)],
        out_specs=(pl.BlockSpec((t, W_A), lambda bi, c: (bi, 0)),
                   pl.BlockSpec((t, W_B), lambda bi, c: (bi, 0))),
        out_shape=(jax.ShapeDtypeStruct((n, W_A), BF16), jax.ShapeDtypeStruct((n, W_B), BF16)),
        scratch_shapes=[pltpu.VMEM((nkc + 1, rows, kc), F32),
                        pltpu.VMEM((rows, 1), F32),
                        pltpu.VMEM((rows, W_A + LANES), F32)],
        compiler_params=_params(("arbitrary", "arbitrary")),
        name="sample_even",
    )(p, p, p, p, p, p, cache_k, cache_v, logf_t, cache_bk, cache_bv, e3, bias_b)


def _sample_odd_kernel(q_ref, kn_ref, vn_ref, ck_ref, cv_ref, x_ref, bias_ref, sink_ref, o_ref, *, t):
    qbd = _block_diag_q(q_ref[...], H_C, SCALE)
    kall = jnp.concatenate([ck_ref[0], _pad_rows(kn_ref[...], LANES)], axis=0).astype(BF16)
    vall = jnp.concatenate([cv_ref[0], _pad_rows(vn_ref[...], LANES)], axis=0).astype(BF16)
    kexp = _dot(kall, x_ref[...]).astype(BF16)
    vexp = _dot(vall, x_ref[...]).astype(BF16)
    s = _dot_nt(qbd, kexp) + bias_ref[...]
    sk = sink_ref[...]
    m = jnp.maximum(jnp.max(s, axis=1, keepdims=True), sk)
    e = jnp.exp(s - m)
    l = jnp.sum(e, axis=1, keepdims=True) + jnp.exp(sk - m)
    o_all = _dot(e.astype(BF16), vexp) / l
    o_ref[...] = _block_diag_extract(o_all, H_C, t).astype(BF16)


def _sample_odd(p, cache_k, cache_v, xexp, bias, sink_col, *, t):
    n = p.shape[0]
    nb = n // t
    wq = H_C * HEAD_DIM
    lc = cache_k.shape[1]
    return pl.pallas_call(
        functools.partial(_sample_odd_kernel, t=t),
        grid=(nb,),
        in_specs=[pl.BlockSpec((t, wq), lambda bi: (bi, 0)),
                  pl.BlockSpec((t, LANES), lambda bi: (bi, wq // LANES)),
                  pl.BlockSpec((t, LANES), lambda bi: (bi, wq // LANES + 1)),
                  pl.BlockSpec((1, lc, LANES), lambda bi: (bi, 0, 0)),
                  pl.BlockSpec((1, lc, LANES), lambda bi: (bi, 0, 0)),
                  _const_spec(xexp.shape), _const_spec(bias.shape), _const_spec(sink_col.shape)],
        out_specs=pl.BlockSpec((t, wq), lambda bi: (bi, 0)),
        out_shape=jax.ShapeDtypeStruct((n, wq), BF16),
        compiler_params=_params(("arbitrary",)),
        name="sample_odd",
    )(p, p, p, cache_k, cache_v, xexp, bias, sink_col)


def _t5_bucket(rel_mem):
    nb = T5_BUCKETS // 2
    max_exact = nb // 2
    n = jnp.abs(rel_mem)
    large = max_exact + (jnp.log(jnp.maximum(n, 1).astype(F32) / max_exact)
                         / math.log(T5_MAX_DIST / max_exact) * (nb - max_exact)).astype(jnp.int32)
    large = jnp.minimum(large, nb - 1)
    return jnp.where(rel_mem > 0, nb, 0) + jnp.where(n < max_exact, n, large)


def _chunk_valid(tq, win, offsets, left_chunks):
    off = np.asarray(offsets)[:, None, None] // CHUNK
    d = off + (np.arange(tq) // CHUNK)[None, :, None] - (np.arange(win) // CHUNK)[None, None, :]
    return (d >= 0) & (d <= left_chunks)


def _bias_b_of_rel(table, rel):
    idx = np.clip(rel, -B_REL_CLIP, B_REL_CLIP) + B_REL_CLIP
    return table.astype(F32)[idx].T


def _bias_c_of_rel(table, rel):
    return table.astype(F32)[_t5_bucket(-jnp.asarray(rel, jnp.int32))].T


def _toeplitz(bias_of_rel, tq, ncols, left):
    n = tq + ncols - 1
    f_rev = bias_of_rel(tq - 1 + left - np.arange(n))
    flat = jnp.tile(jnp.pad(f_rev, ((0, 0), (0, 1))), (1, tq))[:, :tq * n]
    return flat.reshape(f_rev.shape[0], tq, n)[:, :, tq - 1:tq - 1 + ncols]


def _band_bias_tiles(bias_of_rel, tq, left, offsets, left_chunks):
    win = tq + left
    wide = _toeplitz(bias_of_rel, tq, win + left, left)
    tiles = jnp.stack([wide[:, :, left - o:left - o + win] for o in offsets], axis=0)
    valid = _chunk_valid(tq, win, offsets, left_chunks)
    return jnp.where(valid[:, None], tiles, NEG_INF)


def _sample_bias(bias_of_rel, t, cache_len):
    ncols = cache_len + LANES
    tiles = _toeplitz(bias_of_rel, t, ncols, cache_len)
    tiles = jnp.where((np.arange(ncols) < cache_len + t)[None, None, :], tiles, NEG_INF)
    return tiles.reshape(-1, ncols)


def _placement():
    pq = np.zeros((LANES, H_A * LANES), np.float32)
    pk = np.zeros((LANES, H_A * LANES), np.float32)
    for h in range(H_A):
        for j in range(3):
            pq[8 * j + h, LANES * h + HEAD_DIM + j] = 1.0
            pq[24, LANES * h + HEAD_DIM + 3 + j] = 1.0
            pk[24, LANES * h + HEAD_DIM + j] = 1.0
            pk[8 * j + h, LANES * h + HEAD_DIM + 3 + j] = -1.0
    return jnp.asarray(pq, BF16), jnp.asarray(pk, BF16)


def kernel(x_prompt, x_sample, cache_a_k, cache_a_v, cache_a_logf, cache_b_k, cache_b_v, cache_c_k, cache_c_v,
           norm_mix, norm_ffn, norm_final, w_in_even, b_forget, rel_bias_b, w_out_even, w_in_odd, sinks_c,
           w_out_odd, t5_bias, w_ffn_in, w_ffn_out):
    b, s, d = x_prompt.shape
    nb, t, _ = x_sample.shape
    past = cache_a_k.shape[2]
    n_p, n_s = b * s, nb * t

    w_even = w_in_even[0]
    w_main = w_even[:, :3 * W_A + 3 * W_B].astype(BF16)
    wf = w_even[:, 3 * W_A + 3 * W_B:]
    w_f = jnp.concatenate([wf, wf, wf, jnp.zeros((d, LANES - 3 * H_A), F32)], axis=1).astype(BF16)
    bf = b_forget[0].astype(F32)
    b_f = jnp.concatenate([bf, bf, bf, jnp.zeros((LANES - 3 * H_A,), F32)])[None, :]
    pq, pk = _placement()
    w_oe = w_out_even[0].astype(BF16)
    w_oo = w_out_odd[0].astype(BF16)
    w_odd = w_in_odd[0].astype(BF16)
    w_fi = w_ffn_in.astype(BF16)
    w_fo = w_ffn_out.astype(BF16)
    g_mix = norm_mix.astype(F32)[:, None, :]
    g_ffn = norm_ffn.astype(F32)[:, None, :]
    g_fin = norm_final.astype(F32)[None, :]

    bias_b_of = functools.partial(_bias_b_of_rel, rel_bias_b[0])
    bias_c_of = functools.partial(_bias_c_of_rel, t5_bias)
    tq_b, left_b = 256, B_LEFT_CHUNKS * CHUNK
    offs_b = [min(v * tq_b, left_b) for v in range(left_b // tq_b + 1)]
    bias_b = _band_bias_tiles(bias_b_of, tq_b, left_b, offs_b, B_LEFT_CHUNKS)
    tq_c, left_c = 256, C_LEFT_CHUNKS * CHUNK
    offs_c = [0, left_c]
    bias_c = _band_bias_tiles(bias_c_of, tq_c, left_c, offs_c, C_LEFT_CHUNKS)

    xp = x_prompt
    w_all = jnp.concatenate([w_main[:, 3 * W_A:3 * W_A + W_B], w_f, w_main[:, :3 * W_A],
                             w_main[:, 3 * W_A + W_B:]], axis=1)
    qaug, kaug, vaug, ka, va, logf, qb, kb, vb, kbt, vbt = _even_proj(
        xp, g_mix[0], w_all, b_f, pq, pk, tm=512)
    oa = _fox(qaug, kaug, vaug, blk=512, nsub=2)
    ob = _band(qb, kb, vb, bias_b, None, tq=tq_b, left=left_b, npairs=H_B // 2, shared_kv=False,
               name="band_b")
    xp1 = _out_ffn(xp.reshape(n_p, d),
                   [(oa.reshape(n_p, W_A), w_oe[:W_A]), (ob.reshape(n_p, W_B), w_oe[W_A:])],
                   g_ffn[0], w_fi, w_fo, g_fin, tm=256, layer=0, final_norm=False)

    qc, kcd, vcd, kct, vct = _odd_proj(xp1.reshape(b, s, d), g_mix[1], w_odd, tm=512)
    oc = _band(qc, kcd, vcd, bias_c, sinks_c[0].astype(F32), tq=tq_c, left=left_c, npairs=G_C // 2,
               shared_kv=True, name="band_c")
    y_prompt = _out_ffn(xp1, [(oc.reshape(n_p, H_C * HEAD_DIM), w_oo)],
                        g_ffn[1], w_fi, w_fo, g_fin, tm=256, layer=1, final_norm=True)

    xs = x_sample.reshape(n_s, d)
    ps, logf_sp = _sample_proj(xs, g_mix[0], w_main, tm=256, w_f=w_f, b_f=b_f)
    logf_s = logf_sp[:, :H_A]
    kpad = LANES * -(-(past + t) // LANES)
    logf_all = jnp.concatenate([cache_a_logf[0].astype(F32), logf_s.reshape(nb, t, H_A)], axis=1)
    logf_t = jnp.pad(jnp.swapaxes(logf_all, 1, 2), ((0, 0), (0, 0), (0, kpad - past - t)))
    lbs = cache_b_k.shape[2]
    bias_sb = _sample_bias(bias_b_of, t, lbs)
    e3 = np.zeros((H_A * t, LANES), np.float32)
    for j in range(3):
        e3[np.arange(H_A * t), 8 * j + np.arange(H_A * t) // t] = 1.0
    keys_minor = lambda cache: jnp.transpose(cache[0], (0, 2, 3, 1))
    oa_s, ob_s = _sample_even(ps, keys_minor(cache_a_k), keys_minor(cache_a_v), logf_t,
                              keys_minor(cache_b_k), keys_minor(cache_b_v),
                              jnp.asarray(e3, BF16), bias_sb, t=t, kc=2048)
    xs1 = _out_ffn(xs, [(oa_s, w_oe[:W_A]), (ob_s, w_oe[W_A:])], g_ffn[0], w_fi, w_fo, g_fin,
                   tm=256, layer=0, final_norm=False)

    ps2 = _sample_proj(xs1, g_mix[1], w_odd, tm=256)
    lcs = cache_c_k.shape[2]
    bias_sc = _sample_bias(bias_c_of, t, lcs)
    sink_col = jnp.repeat(sinks_c[0].astype(F32), t)[:, None]
    lane_head = np.arange(H_C * HEAD_DIM) // HEAD_DIM
    src_lane = (lane_head // G_C) * HEAD_DIM + np.arange(H_C * HEAD_DIM) % HEAD_DIM
    xexp = jnp.asarray(np.arange(LANES)[:, None] == src_lane[None, :], BF16)
    oc_s = _sample_odd(ps2, cache_c_k[0].reshape(nb, lcs, LANES), cache_c_v[0].reshape(nb, lcs, LANES),
                       xexp, bias_sc, sink_col, t=t)
    y_sample = _out_ffn(xs1, [(oc_s, w_oo)], g_ffn[1], w_fi, w_fo, g_fin, tm=256, layer=1, final_norm=True)

    wq = H_C * HEAD_DIM
    hd = lambda a, lead, h: a.reshape((1,) + lead + (h, HEAD_DIM))
    return (
        y_prompt.reshape(b, s, d), y_sample.reshape(nb, t, d),
        hd(ka, (b, s), H_A), hd(va, (b, s), H_A), logf[None],
        hd(kbt, (b, kbt.shape[1]), H_B), hd(vbt, (b, vbt.shape[1]), H_B),
        hd(kct, (b, kct.shape[1]), HKV_C), hd(vct, (b, vct.shape[1]), HKV_C),
        hd(ps[:, W_A:2 * W_A], (nb, t), H_A), hd(ps[:, 2 * W_A:3 * W_A], (nb, t), H_A),
        logf_s.reshape(1, nb, t, H_A),
        hd(ps[:, 3 * W_A + W_B:3 * W_A + 2 * W_B], (nb, t), H_B),
        hd(ps[:, 3 * W_A + 2 * W_B:3 * W_A + 3 * W_B], (nb, t), H_B),
        hd(ps2[:, wq:wq + LANES], (nb, t), HKV_C), hd(ps2[:, wq + LANES:wq + 2 * LANES], (nb, t), HKV_C),
    )
```

```python
import functools
import math

import jax
import jax.numpy as jnp
import numpy as np
from jax import lax
from jax.experimental import pallas as pl
from jax.experimental.pallas import tpu as pltpu

D_MODEL = 1024
HEAD_DIM = 64
CHUNK = 64
H_A = 8
H_B = 8
B_LEFT_CHUNKS = 8
B_REL_CLIP = 128
H_C = 16
HKV_C = 2
G_C = H_C // HKV_C
WINDOW = 128
C_LEFT_CHUNKS = WINDOW // CHUNK
T5_BUCKETS = 32
T5_MAX_DIST = 128
EPS = 1e-6
W_A = H_A * HEAD_DIM
W_B = H_B * HEAD_DIM
SCALE = HEAD_DIM ** -0.5
LOG2E = math.log2(math.e)

LANES = 128
VMEM_LIMIT = 60 * 1024 * 1024

F32 = jnp.float32
BF16 = jnp.bfloat16
NEG_INF = float("-inf")

_NT = (((1,), (1,)), ((), ()))


def _dot(a, b):
    return jnp.dot(a, b, preferred_element_type=F32)


def _dot_nt(a, b):
    return lax.dot_general(a, b, _NT, preferred_element_type=F32)


def _rmsnorm(x, g):
    ms = jnp.mean(x * x, axis=-1, keepdims=True)
    return x * lax.rsqrt(ms + EPS) * g


def _log_sigmoid(x):
    return jnp.minimum(x, 0.0) - jnp.log1p(jnp.exp(-jnp.abs(x)))


def _split3(x):
    hi = x.astype(BF16).astype(F32)
    r1 = x - hi
    mid = r1.astype(BF16).astype(F32)
    lo = (r1 - mid).astype(BF16).astype(F32)
    return hi, mid, lo


def _cumsum_rows(x):
    n = x.shape[0]
    row = lax.broadcasted_iota(jnp.int32, x.shape, 0)
    s = 1
    while s < n:
        x = x + jnp.where(row >= s, pltpu.roll(x, s, 0), 0.0)
        s *= 2
    return x


def _cumsum_lanes(x):
    n = x.shape[1]
    col = lax.broadcasted_iota(jnp.int32, x.shape, 1)
    s = 1
    while s < n:
        x = x + jnp.where(col >= s, pltpu.roll(x, s, 1), 0.0)
        s *= 2
    return x


def _staggered(items, stage_a, stage_b, stage_c):
    n = len(items)
    a_out, b_out = {}, {}
    for step in range(n + 2):
        if step < n:
            a_out[step] = stage_a(items[step])
        if 0 <= step - 1 < n:
            b_out[step - 1] = stage_b(items[step - 1], a_out.pop(step - 1))
        if 0 <= step - 2 < n:
            stage_c(items[step - 2], b_out.pop(step - 2))


def _const_spec(shape):
    nd = len(shape)
    return pl.BlockSpec(shape, lambda *_: (0,) * nd, pipeline_mode=pl.Buffered(1))


def _params(sem):
    return pltpu.CompilerParams(dimension_semantics=sem, vmem_limit_bytes=VMEM_LIMIT)


def _even_proj_kernel(x_ref, g_ref, w_ref, bf_ref, pq_ref, pk_ref,
                      qaug_ref, kaug_ref, vaug_ref, ka_ref, va_ref, logf_ref,
                      qb_ref, kb_ref, vb_ref, kbt_ref, vbt_ref, carry_ref):
    @pl.when(pl.program_id(1) == 0)
    def _():
        carry_ref[...] = jnp.zeros_like(carry_ref)

    h = _rmsnorm(x_ref[0], g_ref[...]).astype(BF16)
    tm = h.shape[0]
    lane = lax.broadcasted_iota(jnp.int32, (tm, LANES), 1)
    low = lane < HEAD_DIM
    first = W_B + LANES

    def chunk(n):
        return _dot(h, w_ref[:, first + W_A * n:first + W_A * (n + 1)])

    pf = _dot(h, w_ref[:, :first])
    qb_ref[0] = (pf[:, :W_B] * (SCALE * LOG2E)).astype(BF16)
    fa = pf[:, W_B:]
    pc = chunk(3)
    kb_ref[0] = pc.astype(BF16)
    kbt_ref[0] = pc
    pc = chunk(4)
    vb_ref[0] = pc.astype(BF16)
    vbt_ref[0] = pc
    logf = _log_sigmoid(fa + bf_ref[...])
    logf_ref[0] = logf[:, :H_A]
    c = _cumsum_rows(logf) + carry_ref[...]
    carry_ref[...] = c[tm - 1:tm, :]
    hi, mid, lo = _split3(c * LOG2E)
    a3 = jnp.where(lane < 8, hi, jnp.where(lane < 16, mid, jnp.where(lane < 24, lo,
                   jnp.where(lane == 24, 1.0, 0.0)))).astype(BF16)
    augq = _dot(a3, pq_ref[...])
    augk = _dot(a3, pk_ref[...])
    vone = jnp.where(lane == HEAD_DIM, 1.0, 0.0)

    def per_head(dst_ref, pc, spare):
        for j in range(H_A // 2):
            pair = pc[:, LANES * j:LANES * (j + 1)]
            for hh, val in enumerate((pair, pltpu.roll(pair, HEAD_DIM, 1))):
                o = slice(LANES * (2 * j + hh), LANES * (2 * j + hh + 1))
                if spare is None:
                    dst_ref[0, o, :] = jnp.where(low, val, vone).T.astype(BF16)
                else:
                    dst_ref[0, :, o] = jnp.where(low, val, spare[:, o]).astype(BF16)

    pc = chunk(0)
    per_head(qaug_ref, pc * (SCALE * LOG2E), augq)
    pc = chunk(1)
    ka_ref[0] = pc
    per_head(kaug_ref, pc, augk)
    pc = chunk(2)
    va_ref[0] = pc
    per_head(vaug_ref, pc, None)


def _even_proj(x, g, w_all, b_f, pq, pk, tm):
    b, s, d = x.shape
    nt = s // tm
    lb = min(B_LEFT_CHUNKS * CHUNK, s)
    assert lb == tm, "band-state tail must be exactly one row tile"
    row = lambda w: pl.BlockSpec((1, tm, w), lambda bi, ti: (bi, ti, 0))
    tail = pl.BlockSpec((1, lb, W_B), lambda bi, ti: (bi, 0, 0))
    outs = (
        jax.ShapeDtypeStruct((b, s, H_A * LANES), BF16),
        jax.ShapeDtypeStruct((b, s, H_A * LANES), BF16),
        jax.ShapeDtypeStruct((b, H_A * LANES, s), BF16),
        jax.ShapeDtypeStruct((b, s, W_A), F32),
        jax.ShapeDtypeStruct((b, s, W_A), F32),
        jax.ShapeDtypeStruct((b, s, H_A), F32),
        jax.ShapeDtypeStruct((b, s, W_B), BF16),
        jax.ShapeDtypeStruct((b, s, W_B), BF16),
        jax.ShapeDtypeStruct((b, s, W_B), BF16),
        jax.ShapeDtypeStruct((b, lb, W_B), F32),
        jax.ShapeDtypeStruct((b, lb, W_B), F32),
    )
    return pl.pallas_call(
        _even_proj_kernel,
        grid=(b, nt),
        in_specs=[row(d), _const_spec((1, d)), _const_spec(w_all.shape),
                  _const_spec((1, LANES)), _const_spec(pq.shape), _const_spec(pk.shape)],
        out_specs=(row(H_A * LANES), row(H_A * LANES),
                   pl.BlockSpec((1, H_A * LANES, tm), lambda bi, ti: (bi, 0, ti)), row(W_A), row(W_A), row(H_A),
                   row(W_B), row(W_B), row(W_B), tail, tail),
        out_shape=outs,
        scratch_shapes=[pltpu.VMEM((1, LANES), F32)],
        compiler_params=_params(("arbitrary", "arbitrary")),
        name="even_proj",
    )(x, g, w_all, b_f, pq, pk)


SUBLANES = 8


def _fox_kernel(q_ref, k_ref, vt_ref, o_ref, m_sc, acc_sc, *, blk, nsub):
    i = pl.program_id(2)
    nh = 2
    groups = blk // SUBLANES
    chains = [(sub, hh) for sub in range(nsub) for hh in range(nh)]
    qs = {(sub, hh): q_ref[0, sub * blk:(sub + 1) * blk, LANES * hh:LANES * (hh + 1)] for sub, hh in chains}
    key = lax.broadcasted_iota(jnp.int32, (blk, blk), 0)
    qry = lax.broadcasted_iota(jnp.int32, (blk, blk), 1)
    causal = key <= qry
    m_sc[...] = jnp.full_like(m_sc, NEG_INF)
    acc_sc[...] = jnp.zeros_like(acc_sc)

    def blocks(specs):
        items = [(pl.multiple_of(j * blk, blk), ch, modes[ch[0]])
                 for j, modes in specs for ch in chains if modes[ch[0]] is not None]

        def stage_scores(item):
            start, ch, mode = item
            k = k_ref[0, pl.ds(start, blk), LANES * ch[1]:LANES * (ch[1] + 1)]
            s = _dot_nt(k, qs[ch])
            if mode == "masked":
                s = jnp.where(causal, s, NEG_INF)
            s = s.reshape(groups, SUBLANES, blk)
            part = jnp.max(s, axis=0)
            for shift in (4, 2, 1):
                part = jnp.maximum(part, pltpu.roll(part, shift, 0))
            m = m_sc[chains.index(ch)]
            m_new = jnp.maximum(m, part)
            m_sc[chains.index(ch)] = m_new
            return s, m, m_new

        def stage_exp(item, sm):
            s, m, m_new = sm
            p = jnp.exp2(s - m_new[None]).reshape(blk, blk)
            return p.astype(BF16), jnp.exp2(m - m_new)

        def stage_pv(item, pa):
            start, ch, _ = item
            p, alpha = pa
            vt = vt_ref[0, LANES * ch[1]:LANES * (ch[1] + 1), pl.ds(start, blk)]
            n = chains.index(ch)
            acc = acc_sc[n].reshape(LANES // SUBLANES, SUBLANES, blk) * alpha[None]
            acc_sc[n] = acc.reshape(LANES, blk) + _dot(vt, p)

        _staggered(items, stage_scores, stage_exp, stage_pv)

    @pl.loop(0, i)
    def _(jj):
        blocks([(nsub * jj + d, ["full"] * nsub) for d in range(nsub)])

    blocks([(nsub * i + d, [None if sub < d else ("masked" if sub == d else "full") for sub in range(nsub)])
            for d in range(nsub)])

    for sub in range(nsub):
        halves = []
        for hh in range(nh):
            acc = acc_sc[chains.index((sub, hh))]
            halves.append(acc[:HEAD_DIM] / acc[HEAD_DIM:HEAD_DIM + 1])
        o_ref[0, sub * blk:(sub + 1) * blk, :] = jnp.concatenate(halves, axis=0).T.astype(BF16)


def _fox(qaug, kaug, vt, blk, nsub):
    b, s, _ = qaug.shape
    pairs = H_A // 2
    tq = blk * nsub
    return pl.pallas_call(
        functools.partial(_fox_kernel, blk=blk, nsub=nsub),
        grid=(b, pairs, s // tq),
        in_specs=[pl.BlockSpec((1, tq, 2 * LANES), lambda bi, hp, i: (bi, i, hp)),
                  pl.BlockSpec((1, s, 2 * LANES), lambda bi, hp, i: (bi, 0, hp)),
                  pl.BlockSpec((1, 2 * LANES, s), lambda bi, hp, i: (bi, hp, 0))],
        out_specs=pl.BlockSpec((1, tq, LANES), lambda bi, hp, i: (bi, i, hp)),
        out_shape=jax.ShapeDtypeStruct((b, s, W_A), BF16),
        scratch_shapes=[pltpu.VMEM((2 * nsub, SUBLANES, blk), F32), pltpu.VMEM((2 * nsub, LANES, blk), F32)],
        compiler_params=_params(("arbitrary", "arbitrary", "arbitrary")),
        name="fox",
    )(qaug, kaug, vt)


def _band_kernel(*refs, tq, win, left, npairs, heads_per_group, use_sinks, shared_kv):
    mxu_rowsum = not use_sinks
    if use_sinks:
        sink_ref, q_ref, k_ref, v_ref, bias_ref, o_ref = refs
    else:
        q_ref, k_ref, v_ref, bias_ref, o_ref = refs
    g = pl.program_id(1)
    i = pl.program_id(2)
    nvar = bias_ref.shape[0]
    var = jnp.minimum(i, nvar - 1)
    start = pl.multiple_of(jnp.maximum(i * tq - left, 0), LANES)
    lane = lax.broadcasted_iota(jnp.int32, (tq, LANES), 1)
    low = lane < HEAD_DIM
    heads = [(pr, hh) for pr in range(npairs) for hh in range(2)]

    def scores(pr, hh):
        qp = q_ref[0, :, LANES * pr:LANES * (pr + 1)]
        qm = jnp.where(low if hh == 0 else jnp.logical_not(low), qp, jnp.zeros_like(qp))
        kl = 0 if shared_kv else LANES * pr
        return _dot_nt(qm, k_ref[0, pl.ds(start, win), kl:kl + LANES])

    def sink(pr, hh):
        return sink_ref[g * heads_per_group + 2 * pr + hh]

    def stage_scores(head):
        s = scores(*head) + bias_ref[var, 2 * head[0] + head[1]]
        m = jnp.max(s, axis=1, keepdims=True)
        return s, (jnp.maximum(m, sink(*head)) if use_sinks else m)

    def stage_exp(head, sm):
        s, m = sm
        e = jnp.exp2(s - m)
        l = None if mxu_rowsum else jnp.sum(e, axis=1, keepdims=True)
        if use_sinks:
            l = l + jnp.exp2(sink(*head) - m)
        return e.astype(BF16), l

    outs = {}

    def stage_pv(head, pe):
        p, l = pe
        pr, hh = head
        kl = 0 if shared_kv else LANES * pr
        v = v_ref[0, pl.ds(start, win), kl:kl + LANES]
        if mxu_rowsum:
            ov = _dot(p, jnp.concatenate([v, jnp.ones((win, LANES), BF16)], axis=1))
            outs[hh] = ov[:, :LANES] / ov[:, LANES:]
        else:
            outs[hh] = _dot(p, v) / l
        if hh == 1:
            o_ref[0, :, LANES * pr:LANES * (pr + 1)] = jnp.where(low, outs[0], outs[1]).astype(BF16)

    _staggered(heads, stage_scores, stage_exp, stage_pv)


def _band(q, k, v, bias, sinks, *, tq, left, npairs, shared_kv, name):
    b, s, wq = q.shape
    win = bias.shape[-1]
    wblk = npairs * LANES
    groups = wq // wblk
    hpg = 2 * npairs
    use_sinks = sinks is not None
    kern = functools.partial(_band_kernel, tq=tq, win=win, left=left, npairs=npairs,
                             heads_per_group=hpg, use_sinks=use_sinks, shared_kv=shared_kv)
    kvw = LANES if shared_kv else wblk
    resident = lambda shape, imap: pl.BlockSpec(shape, imap, pipeline_mode=pl.Buffered(1))
    in_specs = [pl.BlockSpec((1, tq, wblk), lambda bi, g, i: (bi, i, g)),
                resident((1, s, kvw), lambda bi, g, i: (bi, 0, g)),
                resident((1, s, kvw), lambda bi, g, i: (bi, 0, g)),
                resident((bias.shape[0], hpg, tq, win), lambda bi, g, i: (0, g, 0, 0))]
    args = [q, k, v, bias]
    if use_sinks:
        in_specs = [pl.BlockSpec(memory_space=pltpu.SMEM)] + in_specs
        args = [sinks] + args
    return pl.pallas_call(
        kern,
        grid=(b, groups, s // tq),
        in_specs=in_specs,
        out_specs=pl.BlockSpec((1, tq, wblk), lambda bi, g, i: (bi, i, g)),
        out_shape=jax.ShapeDtypeStruct((b, s, wq), BF16),
        compiler_params=_params(("arbitrary", "arbitrary", "arbitrary")),
        name=name,
    )(*args)


def _out_ffn_kernel(*refs, n_attn, d_ff, final_norm):
    x_ref = refs[0]
    attn = refs[1:1 + 2 * n_attn]
    gffn_ref, win_ref, wout_ref, gfin_ref, o_ref = refs[1 + 2 * n_attn:]
    y = x_ref[...]
    for a in range(n_attn):
        y = y + _dot(attn[2 * a][...], attn[2 * a + 1][...])
    h = _rmsnorm(y, gffn_ref[...]).astype(BF16)
    gu = _dot(h, win_ref[...])
    gate = gu[:, :d_ff]
    up = gu[:, d_ff:]
    act = (gate * (1.0 / (1.0 + jnp.exp(-gate))) * up).astype(BF16)
    y = y + _dot(act, wout_ref[...])
    o_ref[...] = _rmsnorm(y, gfin_ref[...]) if final_norm else y


def _out_ffn(x, attn_pairs, g_ffn, w_in, w_out, g_fin, tm, layer, final_norm):
    n, d = x.shape
    d_ff = w_out.shape[1]
    row = lambda w: pl.BlockSpec((tm, w), lambda i: (i, 0))
    slab = lambda w: pl.BlockSpec((None,) + w.shape[1:], lambda i: (layer, 0, 0), pipeline_mode=pl.Buffered(1))
    args, specs = [x], [row(d)]
    for o, w in attn_pairs:
        args += [o, w]
        specs += [row(o.shape[1]), _const_spec(w.shape)]
    args += [g_ffn, w_in, w_out, g_fin]
    specs += [_const_spec((1, d)), slab(w_in), slab(w_out), _const_spec((1, d))]
    return pl.pallas_call(
        functools.partial(_out_ffn_kernel, n_attn=len(attn_pairs), d_ff=d_ff, final_norm=final_norm),
        grid=(n // tm,),
        in_specs=specs,
        out_specs=row(d),
        out_shape=jax.ShapeDtypeStruct((n, d), F32),
        compiler_params=_params(("arbitrary",)),
        name="out_ffn",
    )(*args)


def _odd_proj_kernel(x_ref, g_ref, w_ref, q_ref, k_ref, v_ref, kt_ref, vt_ref, *, tiles_per_batch, tail):
    t = pl.program_id(1)
    h = _rmsnorm(x_ref[0], g_ref[...]).astype(BF16)
    p = _dot(h, w_ref[...])
    wq = H_C * HEAD_DIM
    q_ref[0] = (p[:, :wq] * (SCALE * LOG2E)).astype(BF16)
    tm = p.shape[0]
    lane = lax.broadcasted_iota(jnp.int32, (tm, LANES), 1)
    low = lane < HEAD_DIM
    for src, dst in ((p[:, wq:wq + LANES], k_ref), (p[:, wq + LANES:wq + 2 * LANES], v_ref)):
        rolled = pltpu.roll(src, HEAD_DIM, 1)
        dst[0, :, :LANES] = jnp.where(low, src, rolled).astype(BF16)
        dst[0, :, LANES:] = jnp.where(low, rolled, src).astype(BF16)

    @pl.when(t == tiles_per_batch - 1)
    def _():
        kt_ref[0] = p[tm - tail:, wq:wq + LANES]
        vt_ref[0] = p[tm - tail:, wq + LANES:wq + 2 * LANES]


def _odd_proj(x, g, w, tm):
    b, s, d = x.shape
    nt = s // tm
    lc = min(C_LEFT_CHUNKS * CHUNK, s)
    assert lc <= tm
    wq = H_C * HEAD_DIM
    row = lambda w_: pl.BlockSpec((1, tm, w_), lambda bi, ti: (bi, ti, 0))
    tail = pl.BlockSpec((1, lc, LANES), lambda bi, ti: (bi, 0, 0))
    return pl.pallas_call(
        functools.partial(_odd_proj_kernel, tiles_per_batch=nt, tail=lc),
        grid=(b, nt),
        in_specs=[row(d), _const_spec((1, d)), _const_spec(w.shape)],
        out_specs=(row(wq), row(2 * LANES), row(2 * LANES), tail, tail),
        out_shape=(jax.ShapeDtypeStruct((b, s, wq), BF16),
                   jax.ShapeDtypeStruct((b, s, 2 * LANES), BF16),
                   jax.ShapeDtypeStruct((b, s, 2 * LANES), BF16),
                   jax.ShapeDtypeStruct((b, lc, LANES), F32),
                   jax.ShapeDtypeStruct((b, lc, LANES), F32)),
        compiler_params=_params(("arbitrary", "arbitrary")),
        name="odd_proj",
    )(x, g, w)


def _sample_proj_kernel(*refs, gated):
    if gated:
        x_ref, g_ref, w_ref, wf_ref, bf_ref, p_ref, logf_ref = refs
    else:
        x_ref, g_ref, w_ref, p_ref = refs
    h = _rmsnorm(x_ref[...], g_ref[...]).astype(BF16)
    p_ref[...] = _dot(h, w_ref[...])
    if gated:
        logf_ref[...] = _log_sigmoid(_dot(h, wf_ref[...]) + bf_ref[...])


def _sample_proj(x, g, w, tm, w_f=None, b_f=None):
    n, d = x.shape
    row = lambda w_: pl.BlockSpec((tm, w_), lambda i: (i, 0))
    in_specs = [row(d), _const_spec((1, d)), _const_spec(w.shape)]
    args = [x, g, w]
    out_specs = row(w.shape[1])
    out_shape = jax.ShapeDtypeStruct((n, w.shape[1]), F32)
    if w_f is not None:
        in_specs += [_const_spec(w_f.shape), _const_spec((1, LANES))]
        args += [w_f, b_f]
        out_specs = (out_specs, row(LANES))
        out_shape = (out_shape, jax.ShapeDtypeStruct((n, LANES), F32))
    return pl.pallas_call(
        functools.partial(_sample_proj_kernel, gated=w_f is not None),
        grid=(n // tm,),
        in_specs=in_specs,
        out_specs=out_specs,
        out_shape=out_shape,
        compiler_params=_params(("arbitrary",)),
        name="sample_proj",
    )(*args)


def _block_diag_q(q, nheads, scale):
    t, w = q.shape
    tiled = jnp.concatenate([q] * nheads, axis=0)
    r = lax.broadcasted_iota(jnp.int32, (nheads * t, w), 0)
    c = lax.broadcasted_iota(jnp.int32, (nheads * t, w), 1)
    return jnp.where(r // t == c // HEAD_DIM, tiled * scale, 0.0).astype(BF16)


def _block_diag_extract(o_all, nheads, t):
    w = o_all.shape[1]
    c = lax.broadcasted_iota(jnp.int32, (t, w), 1)
    out = jnp.zeros((t, w), F32)
    for h in range(nheads):
        out = jnp.where(c // HEAD_DIM == h, o_all[h * t:(h + 1) * t, :], out)
    return out


def _heads_major(ref):
    _, nh, hd, nk = ref.shape
    return ref[0].reshape(nh * hd, nk).astype(BF16)


def _pad_rows(x, rows):
    return jnp.concatenate([x, jnp.zeros((rows - x.shape[0], x.shape[1]), x.dtype)], axis=0)


def _sample_even_kernel(qa_ref, kan_ref, van_ref, qb_ref, kbn_ref, vbn_ref, ck_ref, cv_ref, lft_ref,
                        cbk_ref, cbv_ref, e3_ref, biasb_ref, oa_ref, ob_ref,
                        bias_sc, m_sc, acc_sc, *, t, past, kc, nkc):
    c = pl.program_id(1)
    rows = H_A * t
    qbd = _block_diag_q(qa_ref[...], H_A, SCALE)

    @pl.when(c == 0)
    def _():
        cum = _cumsum_lanes(lft_ref[0])
        last = past + t - 1
        suffix = cum[:, last:last + 1] - cum
        hi, mid, lo = _split3(suffix)
        s3 = jnp.concatenate([hi, mid, lo, jnp.zeros((LANES - 3 * H_A, suffix.shape[1]), F32)], axis=0)
        s3 = s3.astype(BF16)
        for ch in range(nkc + 1):
            w = kc if ch < nkc else LANES
            bias_sc[ch, :, :w] = _dot(e3_ref[...], s3[:, ch * kc:ch * kc + w])
        m_sc[...] = jnp.full_like(m_sc, NEG_INF)
        acc_sc[...] = jnp.zeros_like(acc_sc)

    def update(s, pv):
        m = m_sc[...]
        m_new = jnp.maximum(m, jnp.max(s, axis=1, keepdims=True))
        p = jnp.exp(s - m_new)
        alpha = jnp.exp(m - m_new)
        acc = acc_sc[...]
        lsum = alpha * acc[:, W_A:W_A + 1] + jnp.sum(p, axis=1, keepdims=True)
        o = alpha * acc[:, :W_A] + pv(p.astype(BF16))
        acc_sc[:, :W_A] = o
        acc_sc[:, W_A:] = jnp.broadcast_to(lsum, (rows, LANES))
        m_sc[...] = m_new

    s = _dot(qbd, _heads_major(ck_ref)) + bias_sc[c]
    vt = _heads_major(cv_ref)
    update(s, lambda p: _dot_nt(p, vt))

    @pl.when(c == nkc - 1)
    def _():
        kn = _pad_rows(kan_ref[...], LANES).astype(BF16)
        vn = _pad_rows(van_ref[...], LANES).astype(BF16)
        sn = _dot_nt(qbd, kn) + bias_sc[nkc, :, :LANES]
        r = lax.broadcasted_iota(jnp.int32, sn.shape, 0)
        col = lax.broadcasted_iota(jnp.int32, sn.shape, 1)
        sn = jnp.where(col <= r % t, sn, NEG_INF)
        update(sn, lambda p: _dot(p, vn))
        acc = acc_sc[...]
        o_all = acc[:, :W_A] / acc[:, W_A:W_A + 1]
        oa_ref[...] = _block_diag_extract(o_all, H_A, t).astype(BF16)

        qbd_b = _block_diag_q(qb_ref[...], H_B, SCALE)
        lbk = cbk_ref.shape[3]
        kbn = _pad_rows(kbn_ref[...], LANES).astype(BF16)
        vbn = _pad_rows(vbn_ref[...], LANES).astype(BF16)
        sb = jnp.concatenate([_dot(qbd_b, _heads_major(cbk_ref)), _dot_nt(qbd_b, kbn)], axis=1) + biasb_ref[...]
        mb = jnp.max(sb, axis=1, keepdims=True)
        eb = jnp.exp(sb - mb)
        lb = jnp.sum(eb, axis=1, keepdims=True)
        eb = eb.astype(BF16)
        ob_all = (_dot_nt(eb[:, :lbk], _heads_major(cbv_ref)) + _dot(eb[:, lbk:], vbn)) / lb
        ob_ref[...] = _block_diag_extract(ob_all, H_B, t).astype(BF16)


def _sample_even(p, cache_k, cache_v, logf_t, cache_bk, cache_bv, e3, bias_b, *, t, kc):
    n = p.shape[0]
    nb = n // t
    past = cache_k.shape[3]
    nkc = past // kc
    lbk = cache_bk.shape[3]
    pcol = lambda j: pl.BlockSpec((t, W_A), lambda bi, c: (bi, j))
    rows = H_A * t
    return pl.pallas_call(
        functools.partial(_sample_even_kernel, t=t, past=past, kc=kc, nkc=nkc),
        grid=(nb, nkc),
        in_specs=[pcol(0), pcol(1), pcol(2), pcol(3), pcol(4), pcol(5),
                  pl.BlockSpec((1, H_A, HEAD_DIM, kc), lambda bi, c: (bi, 0, 0, c)),
                  pl.BlockSpec((1, H_A, HEAD_DIM, kc), lambda bi, c: (bi, 0, 0, c)),
                  pl.BlockSpec((1, H_A, logf_t.shape[2]), lambda bi, c: (bi, 0, 0)),
                  pl.BlockSpec((1, H_B, HEAD_DIM, lbk), lambda bi, c: (bi, 0, 0, 0)),
                  pl.BlockSpec((1, H_B, HEAD_DIM, lbk), lambda bi, c: (bi, 0, 0, 0)),
                  _const_spec(e3.shape), _const_spec(bias_b.shape)],
        out_specs=(pl.BlockSpec((t, W_A), lambda bi, c: (bi, 0)),
                   pl.BlockSpec((t, W_B), lambda bi, c: (bi, 0))),
        out_shape=(jax.ShapeDtypeStruct((n, W_A), BF16), jax.ShapeDtypeStruct((n, W_B), BF16)),
        scratch_shapes=[pltpu.VMEM((nkc + 1, rows, kc), F32),
                        pltpu.VMEM((rows, 1), F32),
                        pltpu.VMEM((rows, W_A + LANES), F32)],
        compiler_params=_params(("arbitrary", "arbitrary")),
        name="sample_even",
    )(p, p, p, p, p, p, cache_k, cache_v, logf_t, cache_bk, cache_bv, e3, bias_b)


def _sample_odd_kernel(q_ref, kn_ref, vn_ref, ck_ref, cv_ref, x_ref, bias_ref, sink_ref, o_ref, *, t):
    qbd = _block_diag_q(q_ref[...], H_C, SCALE)
    kall = jnp.concatenate([ck_ref[0], _pad_rows(kn_ref[...], LANES)], axis=0).astype(BF16)
    vall = jnp.concatenate([cv_ref[0], _pad_rows(vn_ref[...], LANES)], axis=0).astype(BF16)
    kexp = _dot(kall, x_ref[...]).astype(BF16)
    vexp = _dot(vall, x_ref[...]).astype(BF16)
    s = _dot_nt(qbd, kexp) + bias_ref[...]
    sk = sink_ref[...]
    m = jnp.maximum(jnp.max(s, axis=1, keepdims=True), sk)
    e = jnp.exp(s - m)
    l = jnp.sum(e, axis=1, keepdims=True) + jnp.exp(sk - m)
    o_all = _dot(e.astype(BF16), vexp) / l
    o_ref[...] = _block_diag_extract(o_all, H_C, t).astype(BF16)


def _sample_odd(p, cache_k, cache_v, xexp, bias, sink_col, *, t):
    n = p.shape[0]
    nb = n // t
    wq = H_C * HEAD_DIM
    lc = cache_k.shape[1]
    return pl.pallas_call(
        functools.partial(_sample_odd_kernel, t=t),
        grid=(nb,),
        in_specs=[pl.BlockSpec((t, wq), lambda bi: (bi, 0)),
                  pl.BlockSpec((t, LANES), lambda bi: (bi, wq // LANES)),
                  pl.BlockSpec((t, LANES), lambda bi: (bi, wq // LANES + 1)),
                  pl.BlockSpec((1, lc, LANES), lambda bi: (bi, 0, 0)),
                  pl.BlockSpec((1, lc, LANES), lambda bi: (bi, 0, 0)),
                  _const_spec(xexp.shape), _const_spec(bias.shape), _const_spec(sink_col.shape)],
        out_specs=pl.BlockSpec((t, wq), lambda bi: (bi, 0)),
        out_shape=jax.ShapeDtypeStruct((n, wq), BF16),
        compiler_params=_params(("arbitrary",)),
        name="sample_odd",
    )(p, p, p, cache_k, cache_v, xexp, bias, sink_col)


def _t5_bucket(rel_mem):
    nb = T5_BUCKETS // 2
    max_exact = nb // 2
    n = jnp.abs(rel_mem)
    large = max_exact + (jnp.log(jnp.maximum(n, 1).astype(F32) / max_exact)
                         / math.log(T5_MAX_DIST / max_exact) * (nb - max_exact)).astype(jnp.int32)
    large = jnp.minimum(large, nb - 1)
    return jnp.where(rel_mem > 0, nb, 0) + jnp.where(n < max_exact, n, large)


def _chunk_valid(tq, win, offsets, left_chunks):
    off = np.asarray(offsets)[:, None, None] // CHUNK
    d = off + (np.arange(tq) // CHUNK)[None, :, None] - (np.arange(win) // CHUNK)[None, None, :]
    return (d >= 0) & (d <= left_chunks)


def _bias_b_of_rel(table, rel):
    idx = np.clip(rel, -B_REL_CLIP, B_REL_CLIP) + B_REL_CLIP
    return table.astype(F32)[idx].T


def _bias_c_of_rel(table, rel):
    return table.astype(F32)[_t5_bucket(-jnp.asarray(rel, jnp.int32))].T


def _toeplitz(bias_of_rel, tq, ncols, left):
    n = tq + ncols - 1
    f_rev = bias_of_rel(tq - 1 + left - np.arange(n))
    flat = jnp.tile(jnp.pad(f_rev, ((0, 0), (0, 1))), (1, tq))[:, :tq * n]
    return flat.reshape(f_rev.shape[0], tq, n)[:, :, tq - 1:tq - 1 + ncols]


def _band_bias_tiles(bias_of_rel, tq, left, offsets, left_chunks):
    win = tq + left
    wide = _toeplitz(lambda rel: bias_of_rel(rel) * LOG2E, tq, win + left, left)
    tiles = jnp.stack([wide[:, :, left - o:left - o + win] for o in offsets], axis=0)
    valid = _chunk_valid(tq, win, offsets, left_chunks)
    return jnp.where(valid[:, None], tiles, NEG_INF)


def _sample_bias(bias_of_rel, t, cache_len):
    ncols = cache_len + LANES
    tiles = _toeplitz(bias_of_rel, t, ncols, cache_len)
    tiles = jnp.where((np.arange(ncols) < cache_len + t)[None, None, :], tiles, NEG_INF)
    return tiles.reshape(-1, ncols)


def _placement():
    pq = np.zeros((LANES, H_A * LANES), np.float32)
    pk = np.zeros((LANES, H_A * LANES), np.float32)
    for h in range(H_A):
        for j in range(3):
            pq[8 * j + h, LANES * h + HEAD_DIM + j] = 1.0
            pq[24, LANES * h + HEAD_DIM + 3 + j] = 1.0
            pk[24, LANES * h + HEAD_DIM + j] = 1.0
            pk[8 * j + h, LANES * h + HEAD_DIM + 3 + j] = -1.0
    return jnp.asarray(pq, BF16), jnp.asarray(pk, BF16)


def kernel(x_prompt, x_sample, cache_a_k, cache_a_v, cache_a_logf, cache_b_k, cache_b_v, cache_c_k, cache_c_v,
           norm_mix, norm_ffn, norm_final, w_in_even, b_forget, rel_bias_b, w_out_even, w_in_odd, sinks_c,
           w_out_odd, t5_bias, w_ffn_in, w_ffn_out):
    b, s, d = x_prompt.shape
    nb, t, _ = x_sample.shape
    past = cache_a_k.shape[2]
    n_p, n_s = b * s, nb * t

    w_even = w_in_even[0]
    w_main = w_even[:, :3 * W_A + 3 * W_B].astype(BF16)
    wf = w_even[:, 3 * W_A + 3 * W_B:]
    w_f = jnp.concatenate([wf, wf, wf, jnp.zeros((d, LANES - 3 * H_A), F32)], axis=1).astype(BF16)
    bf = b_forget[0].astype(F32)
    b_f = jnp.concatenate([bf, bf, bf, jnp.zeros((LANES - 3 * H_A,), F32)])[None, :]
    pq, pk = _placement()
    w_oe = w_out_even[0].astype(BF16)
    w_oo = w_out_odd[0].astype(BF16)
    w_odd = w_in_odd[0].astype(BF16)
    w_fi = w_ffn_in.astype(BF16)
    w_fo = w_ffn_out.astype(BF16)
    g_mix = norm_mix.astype(F32)[:, None, :]
    g_ffn = norm_ffn.astype(F32)[:, None, :]
    g_fin = norm_final.astype(F32)[None, :]

    bias_b_of = functools.partial(_bias_b_of_rel, rel_bias_b[0])
    bias_c_of = functools.partial(_bias_c_of_rel, t5_bias)
    tq_b, left_b = 256, B_LEFT_CHUNKS * CHUNK
    offs_b = [min(v * tq_b, left_b) for v in range(left_b // tq_b + 1)]
    bias_b = _band_bias_tiles(bias_b_of, tq_b, left_b, offs_b, B_LEFT_CHUNKS)
    tq_c, left_c = 256, C_LEFT_CHUNKS * CHUNK
    offs_c = [0, left_c]
    bias_c = _band_bias_tiles(bias_c_of, tq_c, left_c, offs_c, C_LEFT_CHUNKS)

    xp = x_prompt
    w_all = jnp.concatenate([w_main[:, 3 * W_A:3 * W_A + W_B], w_f, w_main[:, :3 * W_A],
                             w_main[:, 3 * W_A + W_B:]], axis=1)
    qaug, kaug, vaug, ka, va, logf, qb, kb, vb, kbt, vbt = _even_proj(
        xp, g_mix[0], w_all, b_f, pq, pk, tm=512)
    oa = _fox(qaug, kaug, vaug, blk=512, nsub=2)
    ob = _band(qb, kb, vb, bias_b, None, tq=tq_b, left=left_b, npairs=H_B // 2, shared_kv=False,
               name="band_b")
    xp1 = _out_ffn(xp.reshape(n_p, d),
                   [(oa.reshape(n_p, W_A), w_oe[:W_A]), (ob.reshape(n_p, W_B), w_oe[W_A:])],
                   g_ffn[0], w_fi, w_fo, g_fin, tm=512, layer=0, final_norm=False)

    qc, kcd, vcd, kct, vct = _odd_proj(xp1.reshape(b, s, d), g_mix[1], w_odd, tm=512)
    oc = _band(qc, kcd, vcd, bias_c, sinks_c[0].astype(F32) * LOG2E, tq=tq_c, left=left_c, npairs=G_C // 2,
               shared_kv=True, name="band_c")
    y_prompt = _out_ffn(xp1, [(oc.reshape(n_p, H_C * HEAD_DIM), w_oo)],
                        g_ffn[1], w_fi, w_fo, g_fin, tm=512, layer=1, final_norm=True)

    xs = x_sample.reshape(n_s, d)
    ps, logf_sp = _sample_proj(xs, g_mix[0], w_main, tm=256, w_f=w_f, b_f=b_f)
    logf_s = logf_sp[:, :H_A]
    kpad = LANES * -(-(past + t) // LANES)
    logf_all = jnp.concatenate([cache_a_logf[0].astype(F32), logf_s.reshape(nb, t, H_A)], axis=1)
    logf_t = jnp.pad(jnp.swapaxes(logf_all, 1, 2), ((0, 0), (0, 0), (0, kpad - past - t)))
    lbs = cache_b_k.shape[2]
    bias_sb = _sample_bias(bias_b_of, t, lbs)
    e3 = np.zeros((H_A * t, LANES), np.float32)
    for j in range(3):
        e3[np.arange(H_A * t), 8 * j + np.arange(H_A * t) // t] = 1.0
    keys_minor = lambda cache: jnp.transpose(cache[0], (0, 2, 3, 1))
    oa_s, ob_s = _sample_even(ps, keys_minor(cache_a_k), keys_minor(cache_a_v), logf_t,
                              keys_minor(cache_b_k), keys_minor(cache_b_v),
                              jnp.asarray(e3, BF16), bias_sb, t=t, kc=2048)
    xs1 = _out_ffn(xs, [(oa_s, w_oe[:W_A]), (ob_s, w_oe[W_A:])], g_ffn[0], w_fi, w_fo, g_fin,
                   tm=256, layer=0, final_norm=False)

    ps2 = _sample_proj(xs1, g_mix[1], w_odd, tm=256)
    lcs = cache_c_k.shape[2]
    bias_sc = _sample_bias(bias_c_of, t, lcs)
    sink_col = jnp.repeat(sinks_c[0].astype(F32), t)[:, None]
    lane_head = np.arange(H_C * HEAD_DIM) // HEAD_DIM
    src_lane = (lane_head // G_C) * HEAD_DIM + np.arange(H_C * HEAD_DIM) % HEAD_DIM
    xexp = jnp.asarray(np.arange(LANES)[:, None] == src_lane[None, :], BF16)
    oc_s = _sample_odd(ps2, cache_c_k[0].reshape(nb, lcs, LANES), cache_c_v[0].reshape(nb, lcs, LANES),
                       xexp, bias_sc, sink_col, t=t)
    y_sample = _out_ffn(xs1, [(oc_s, w_oo)], g_ffn[1], w_fi, w_fo, g_fin, tm=256, layer=1, final_norm=True)

    wq = H_C * HEAD_DIM
    hd = lambda a, lead, h: a.reshape((1,) + lead + (h, HEAD_DIM))
    return (
        y_prompt.reshape(b, s, d), y_sample.reshape(nb, t, d),
        hd(ka, (b, s), H_A), hd(va, (b, s), H_A), logf[None],
        hd(kbt, (b, kbt.shape[1]), H_B), hd(vbt, (b, vbt.shape[1]), H_B),
        hd(kct, (b, kct.shape[1]), HKV_C), hd(vct, (b, vct.shape[1]), HKV_C),
        hd(ps[:, W_A:2 * W_A], (nb, t), H_A), hd(ps[:, 2 * W_A:3 * W_A], (nb, t), H_A),
        logf_s.reshape(1, nb, t, H_A),
        hd(ps[:, 3 * W_A + W_B:3 * W_A + 2 * W_B], (nb, t), H_B),
        hd(ps[:, 3 * W_A + 2 * W_B:3 * W_A + 3 * W_B], (nb, t), H_B),
        hd(ps2[:, wq:wq + LANES], (nb, t), HKV_C), hd(ps2[:, wq + LANES:wq + 2 * LANES], (nb, t), HKV_C),
    )
```

```python
import functools
import math

import jax
import jax.numpy as jnp
import numpy as np
from jax import lax
from jax.experimental import pallas as pl
from jax.experimental.pallas import tpu as pltpu

D_MODEL = 1024
HEAD_DIM = 64
CHUNK = 64
H_A = 8
H_B = 8
B_LEFT_CHUNKS = 8
B_REL_CLIP = 128
H_C = 16
HKV_C = 2
G_C = H_C // HKV_C
WINDOW = 128
C_LEFT_CHUNKS = WINDOW // CHUNK
T5_BUCKETS = 32
T5_MAX_DIST = 128
EPS = 1e-6
W_A = H_A * HEAD_DIM
W_B = H_B * HEAD_DIM
SCALE = HEAD_DIM ** -0.5
LOG2E = math.log2(math.e)

LANES = 128
VMEM_LIMIT = 60 * 1024 * 1024

F32 = jnp.float32
BF16 = jnp.bfloat16
NEG_INF = float("-inf")

_NT = (((1,), (1,)), ((), ()))


def _dot(a, b):
    return jnp.dot(a, b, preferred_element_type=F32)


def _dot_nt(a, b):
    return lax.dot_general(a, b, _NT, preferred_element_type=F32)


def _rmsnorm(x, g):
    ms = jnp.mean(x * x, axis=-1, keepdims=True)
    return x * lax.rsqrt(ms + EPS) * g


def _log_sigmoid(x):
    return jnp.minimum(x, 0.0) - jnp.log1p(jnp.exp(-jnp.abs(x)))


def _split3(x):
    hi = x.astype(BF16).astype(F32)
    r1 = x - hi
    mid = r1.astype(BF16).astype(F32)
    lo = (r1 - mid).astype(BF16).astype(F32)
    return hi, mid, lo


def _cumsum_rows(x):
    n = x.shape[0]
    row = lax.broadcasted_iota(jnp.int32, x.shape, 0)
    s = 1
    while s < n:
        x = x + jnp.where(row >= s, pltpu.roll(x, s, 0), 0.0)
        s *= 2
    return x


def _cumsum_lanes(x):
    n = x.shape[1]
    col = lax.broadcasted_iota(jnp.int32, x.shape, 1)
    s = 1
    while s < n:
        x = x + jnp.where(col >= s, pltpu.roll(x, s, 1), 0.0)
        s *= 2
    return x


def _staggered(items, stage_a, stage_b, stage_c):
    n = len(items)
    a_out, b_out = {}, {}
    for step in range(n + 2):
        if step < n:
            a_out[step] = stage_a(items[step])
        if 0 <= step - 1 < n:
            b_out[step - 1] = stage_b(items[step - 1], a_out.pop(step - 1))
        if 0 <= step - 2 < n:
            stage_c(items[step - 2], b_out.pop(step - 2))


def _const_spec(shape):
    nd = len(shape)
    return pl.BlockSpec(shape, lambda *_: (0,) * nd, pipeline_mode=pl.Buffered(1))


def _params(sem):
    return pltpu.CompilerParams(dimension_semantics=sem, vmem_limit_bytes=VMEM_LIMIT)


def _even_proj_kernel(x_ref, g_ref, w_ref, bf_ref, pq_ref, pk_ref,
                      qaug_ref, kaug_ref, vaug_ref, ka_ref, va_ref, logf_ref,
                      qb_ref, kb_ref, vb_ref, kbt_ref, vbt_ref, carry_ref):
    @pl.when(pl.program_id(1) == 0)
    def _():
        carry_ref[...] = jnp.zeros_like(carry_ref)

    h = _rmsnorm(x_ref[0], g_ref[...]).astype(BF16)
    tm = h.shape[0]
    lane = lax.broadcasted_iota(jnp.int32, (tm, LANES), 1)
    low = lane < HEAD_DIM
    first = W_B + LANES

    def chunk(n):
        return _dot(h, w_ref[:, first + W_A * n:first + W_A * (n + 1)])

    pf = _dot(h, w_ref[:, :first])
    qb_ref[0] = (pf[:, :W_B] * (SCALE * LOG2E)).astype(BF16)
    fa = pf[:, W_B:]
    pc = chunk(3)
    kb_ref[0] = pc.astype(BF16)
    kbt_ref[0] = pc
    pc = chunk(4)
    vb_ref[0] = pc.astype(BF16)
    vbt_ref[0] = pc
    logf = _log_sigmoid(fa + bf_ref[...])
    logf_ref[0] = logf[:, :H_A]
    c = _cumsum_rows(logf) + carry_ref[...]
    carry_ref[...] = c[tm - 1:tm, :]
    hi, mid, lo = _split3(c * LOG2E)
    a3 = jnp.where(lane < 8, hi, jnp.where(lane < 16, mid, jnp.where(lane < 24, lo,
                   jnp.where(lane == 24, 1.0, 0.0)))).astype(BF16)
    augq = _dot(a3, pq_ref[...])
    augk = _dot(a3, pk_ref[...])
    vone = jnp.where(lane == HEAD_DIM, 1.0, 0.0)

    def per_head(dst_ref, pc, spare):
        for j in range(H_A // 2):
            pair = pc[:, LANES * j:LANES * (j + 1)]
            for hh, val in enumerate((pair, pltpu.roll(pair, HEAD_DIM, 1))):
                o = slice(LANES * (2 * j + hh), LANES * (2 * j + hh + 1))
                dst_ref[0, :, o] = jnp.where(low, val, vone if spare is None else spare[:, o]).astype(BF16)

    pc = chunk(0)
    per_head(qaug_ref, pc * (SCALE * LOG2E), augq)
    pc = chunk(1)
    ka_ref[0] = pc
    per_head(kaug_ref, pc, augk)
    pc = chunk(2)
    va_ref[0] = pc
    per_head(vaug_ref, pc, None)


def _even_proj(x, g, w_all, b_f, pq, pk, tm):
    b, s, d = x.shape
    nt = s // tm
    lb = min(B_LEFT_CHUNKS * CHUNK, s)
    assert lb == tm, "band-state tail must be exactly one row tile"
    row = lambda w: pl.BlockSpec((1, tm, w), lambda bi, ti: (bi, ti, 0))
    tail = pl.BlockSpec((1, lb, W_B), lambda bi, ti: (bi, 0, 0))
    outs = (
        jax.ShapeDtypeStruct((b, s, H_A * LANES), BF16),
        jax.ShapeDtypeStruct((b, s, H_A * LANES), BF16),
        jax.ShapeDtypeStruct((b, s, H_A * LANES), BF16),
        jax.ShapeDtypeStruct((b, s, W_A), F32),
        jax.ShapeDtypeStruct((b, s, W_A), F32),
        jax.ShapeDtypeStruct((b, s, H_A), F32),
        jax.ShapeDtypeStruct((b, s, W_B), BF16),
        jax.ShapeDtypeStruct((b, s, W_B), BF16),
        jax.ShapeDtypeStruct((b, s, W_B), BF16),
        jax.ShapeDtypeStruct((b, lb, W_B), F32),
        jax.ShapeDtypeStruct((b, lb, W_B), F32),
    )
    return pl.pallas_call(
        _even_proj_kernel,
        grid=(b, nt),
        in_specs=[row(d), _const_spec((1, d)), _const_spec(w_all.shape),
                  _const_spec((1, LANES)), _const_spec(pq.shape), _const_spec(pk.shape)],
        out_specs=(row(H_A * LANES), row(H_A * LANES), row(H_A * LANES), row(W_A), row(W_A), row(H_A),
                   row(W_B), row(W_B), row(W_B), tail, tail),
        out_shape=outs,
        scratch_shapes=[pltpu.VMEM((1, LANES), F32)],
        compiler_params=_params(("arbitrary", "arbitrary")),
        name="even_proj",
    )(x, g, w_all, b_f, pq, pk)


def _fox_kernel(q_ref, k_ref, v_ref, o_ref, m_sc, acc_sc, *, blk, nsub):
    i = pl.program_id(2)
    nh = 2
    chains = [(sub, hh) for sub in range(nsub) for hh in range(nh)]
    qs = {(sub, hh): q_ref[0, sub * blk:(sub + 1) * blk, LANES * hh:LANES * (hh + 1)] for sub, hh in chains}
    r = lax.broadcasted_iota(jnp.int32, (blk, blk), 0)
    c = lax.broadcasted_iota(jnp.int32, (blk, blk), 1)
    causal = c <= r
    m_sc[...] = jnp.full_like(m_sc, NEG_INF)
    acc_sc[...] = jnp.zeros_like(acc_sc)

    def blocks(specs):
        items = [(pl.multiple_of(j * blk, blk), ch, modes[ch[0]])
                 for j, modes in specs for ch in chains if modes[ch[0]] is not None]

        def stage_scores(item):
            start, ch, mode = item
            k = k_ref[0, pl.ds(start, blk), LANES * ch[1]:LANES * (ch[1] + 1)]
            s = _dot_nt(qs[ch], k)
            if mode == "masked":
                s = jnp.where(causal, s, NEG_INF)
            m = m_sc[chains.index(ch)]
            m_new = jnp.maximum(m, jnp.max(s, axis=1, keepdims=True))
            m_sc[chains.index(ch)] = m_new
            return s, m, m_new

        def stage_exp(item, sm):
            s, m, m_new = sm
            p = jnp.concatenate([jnp.exp2(s[:, LANES * cb:LANES * (cb + 1)] - m_new)
                                 for cb in range(blk // LANES)], axis=1)
            return p.astype(BF16), jnp.exp2(m - m_new)

        def stage_pv(item, pa):
            start, ch, _ = item
            p, alpha = pa
            v = v_ref[0, pl.ds(start, blk), LANES * ch[1]:LANES * (ch[1] + 1)]
            n = chains.index(ch)
            acc_sc[n] = alpha * acc_sc[n] + _dot(p, v)

        _staggered(items, stage_scores, stage_exp, stage_pv)

    @pl.loop(0, i)
    def _(jj):
        blocks([(nsub * jj + d, ["full"] * nsub) for d in range(nsub)])

    blocks([(nsub * i + d, [None if sub < d else ("masked" if sub == d else "full") for sub in range(nsub)])
            for d in range(nsub)])

    lane = lax.broadcasted_iota(jnp.int32, (blk, LANES), 1)
    for sub in range(nsub):
        a0 = acc_sc[chains.index((sub, 0))]
        a1 = acc_sc[chains.index((sub, 1))]
        o0 = a0 / a0[:, HEAD_DIM:HEAD_DIM + 1]
        o1 = a1 / a1[:, HEAD_DIM:HEAD_DIM + 1]
        o_ref[0, sub * blk:(sub + 1) * blk, :] = jnp.where(
            lane < HEAD_DIM, o0, pltpu.roll(o1, HEAD_DIM, 1)).astype(BF16)


def _fox(qaug, kaug, vaug, blk, nsub):
    b, s, _ = qaug.shape
    pairs = H_A // 2
    tq = blk * nsub
    return pl.pallas_call(
        functools.partial(_fox_kernel, blk=blk, nsub=nsub),
        grid=(b, pairs, s // tq),
        in_specs=[pl.BlockSpec((1, tq, 2 * LANES), lambda bi, hp, i: (bi, i, hp)),
                  pl.BlockSpec((1, s, 2 * LANES), lambda bi, hp, i: (bi, 0, hp)),
                  pl.BlockSpec((1, s, 2 * LANES), lambda bi, hp, i: (bi, 0, hp))],
        out_specs=pl.BlockSpec((1, tq, LANES), lambda bi, hp, i: (bi, i, hp)),
        out_shape=jax.ShapeDtypeStruct((b, s, W_A), BF16),
        scratch_shapes=[pltpu.VMEM((2 * nsub, blk, LANES), F32), pltpu.VMEM((2 * nsub, blk, LANES), F32)],
        compiler_params=_params(("arbitrary", "arbitrary", "arbitrary")),
        name="fox",
    )(qaug, kaug, vaug)


def _band_kernel(*refs, tq, win, left, npairs, heads_per_group, use_sinks, shared_kv):
    mxu_rowsum = not use_sinks
    if use_sinks:
        sink_ref, q_ref, k_ref, v_ref, bias_ref, o_ref = refs
    else:
        q_ref, k_ref, v_ref, bias_ref, o_ref = refs
    g = pl.program_id(1)
    i = pl.program_id(2)
    nvar = bias_ref.shape[0]
    var = jnp.minimum(i, nvar - 1)
    start = pl.multiple_of(jnp.maximum(i * tq - left, 0), LANES)
    lane = lax.broadcasted_iota(jnp.int32, (tq, LANES), 1)
    low = lane < HEAD_DIM
    heads = [(pr, hh) for pr in range(npairs) for hh in range(2)]

    def scores(pr, hh):
        qp = q_ref[0, :, LANES * pr:LANES * (pr + 1)]
        qm = jnp.where(low if hh == 0 else jnp.logical_not(low), qp, jnp.zeros_like(qp))
        kl = 0 if shared_kv else LANES * pr
        return _dot_nt(qm, k_ref[0, pl.ds(start, win), kl:kl + LANES])

    def sink(pr, hh):
        return sink_ref[g * heads_per_group + 2 * pr + hh]

    def stage_scores(head):
        s = scores(*head) + bias_ref[var, 2 * head[0] + head[1]]
        m = jnp.max(s, axis=1, keepdims=True)
        return s, (jnp.maximum(m, sink(*head)) if use_sinks else m)

    def stage_exp(head, sm):
        s, m = sm
        e = jnp.exp2(s - m)
        l = None if mxu_rowsum else jnp.sum(e, axis=1, keepdims=True)
        if use_sinks:
            l = l + jnp.exp2(sink(*head) - m)
        return e.astype(BF16), l

    outs = {}

    def stage_pv(head, pe):
        p, l = pe
        pr, hh = head
        kl = 0 if shared_kv else LANES * pr
        v = v_ref[0, pl.ds(start, win), kl:kl + LANES]
        if mxu_rowsum:
            ov = _dot(p, jnp.concatenate([v, jnp.ones((win, LANES), BF16)], axis=1))
            outs[hh] = ov[:, :LANES] / ov[:, LANES:]
        else:
            outs[hh] = _dot(p, v) / l
        if hh == 1:
            o_ref[0, :, LANES * pr:LANES * (pr + 1)] = jnp.where(low, outs[0], outs[1]).astype(BF16)

    _staggered(heads, stage_scores, stage_exp, stage_pv)


def _band(q, k, v, bias, sinks, *, tq, left, npairs, shared_kv, name):
    b, s, wq = q.shape
    win = bias.shape[-1]
    wblk = npairs * LANES
    groups = wq // wblk
    hpg = 2 * npairs
    use_sinks = sinks is not None
    kern = functools.partial(_band_kernel, tq=tq, win=win, left=left, npairs=npairs,
                             heads_per_group=hpg, use_sinks=use_sinks, shared_kv=shared_kv)
    kvw = LANES if shared_kv else wblk
    resident = lambda shape, imap: pl.BlockSpec(shape, imap, pipeline_mode=pl.Buffered(1))
    in_specs = [pl.BlockSpec((1, tq, wblk), lambda bi, g, i: (bi, i, g)),
                resident((1, s, kvw), lambda bi, g, i: (bi, 0, g)),
                resident((1, s, kvw), lambda bi, g, i: (bi, 0, g)),
                resident((bias.shape[0], hpg, tq, win), lambda bi, g, i: (0, g, 0, 0))]
    args = [q, k, v, bias]
    if use_sinks:
        in_specs = [pl.BlockSpec(memory_space=pltpu.SMEM)] + in_specs
        args = [sinks] + args
    return pl.pallas_call(
        kern,
        grid=(b, groups, s // tq),
        in_specs=in_specs,
        out_specs=pl.BlockSpec((1, tq, wblk), lambda bi, g, i: (bi, i, g)),
        out_shape=jax.ShapeDtypeStruct((b, s, wq), BF16),
        compiler_params=_params(("arbitrary", "arbitrary", "arbitrary")),
        name=name,
    )(*args)


def _out_ffn_kernel(*refs, n_attn, d_ff, final_norm):
    x_ref = refs[0]
    attn = refs[1:1 + 2 * n_attn]
    gffn_ref, win_ref, wout_ref, gfin_ref, o_ref = refs[1 + 2 * n_attn:]
    y = x_ref[...]
    for a in range(n_attn):
        y = y + _dot(attn[2 * a][...], attn[2 * a + 1][...])
    h = _rmsnorm(y, gffn_ref[...]).astype(BF16)
    gu = _dot(h, win_ref[...])
    gate = gu[:, :d_ff]
    up = gu[:, d_ff:]
    act = (gate * (1.0 / (1.0 + jnp.exp(-gate))) * up).astype(BF16)
    y = y + _dot(act, wout_ref[...])
    o_ref[...] = _rmsnorm(y, gfin_ref[...]) if final_norm else y


def _out_ffn(x, attn_pairs, g_ffn, w_in, w_out, g_fin, tm, layer, final_norm):
    n, d = x.shape
    d_ff = w_out.shape[1]
    row = lambda w: pl.BlockSpec((tm, w), lambda i: (i, 0))
    slab = lambda w: pl.BlockSpec((None,) + w.shape[1:], lambda i: (layer, 0, 0), pipeline_mode=pl.Buffered(1))
    args, specs = [x], [row(d)]
    for o, w in attn_pairs:
        args += [o, w]
        specs += [row(o.shape[1]), _const_spec(w.shape)]
    args += [g_ffn, w_in, w_out, g_fin]
    specs += [_const_spec((1, d)), slab(w_in), slab(w_out), _const_spec((1, d))]
    return pl.pallas_call(
        functools.partial(_out_ffn_kernel, n_attn=len(attn_pairs), d_ff=d_ff, final_norm=final_norm),
        grid=(n // tm,),
        in_specs=specs,
        out_specs=row(d),
        out_shape=jax.ShapeDtypeStruct((n, d), F32),
        compiler_params=_params(("arbitrary",)),
        name="out_ffn",
    )(*args)


def _odd_proj_kernel(x_ref, g_ref, w_ref, q_ref, k_ref, v_ref, kt_ref, vt_ref, *, tiles_per_batch, tail):
    t = pl.program_id(1)
    h = _rmsnorm(x_ref[0], g_ref[...]).astype(BF16)
    p = _dot(h, w_ref[...])
    wq = H_C * HEAD_DIM
    q_ref[0] = (p[:, :wq] * (SCALE * LOG2E)).astype(BF16)
    tm = p.shape[0]
    lane = lax.broadcasted_iota(jnp.int32, (tm, LANES), 1)
    low = lane < HEAD_DIM
    for src, dst in ((p[:, wq:wq + LANES], k_ref), (p[:, wq + LANES:wq + 2 * LANES], v_ref)):
        rolled = pltpu.roll(src, HEAD_DIM, 1)
        dst[0, :, :LANES] = jnp.where(low, src, rolled).astype(BF16)
        dst[0, :, LANES:] = jnp.where(low, rolled, src).astype(BF16)

    @pl.when(t == tiles_per_batch - 1)
    def _():
        kt_ref[0] = p[tm - tail:, wq:wq + LANES]
        vt_ref[0] = p[tm - tail:, wq + LANES:wq + 2 * LANES]


def _odd_proj(x, g, w, tm):
    b, s, d = x.shape
    nt = s // tm
    lc = min(C_LEFT_CHUNKS * CHUNK, s)
    assert lc <= tm
    wq = H_C * HEAD_DIM
    row = lambda w_: pl.BlockSpec((1, tm, w_), lambda bi, ti: (bi, ti, 0))
    tail = pl.BlockSpec((1, lc, LANES), lambda bi, ti: (bi, 0, 0))
    return pl.pallas_call(
        functools.partial(_odd_proj_kernel, tiles_per_batch=nt, tail=lc),
        grid=(b, nt),
        in_specs=[row(d), _const_spec((1, d)), _const_spec(w.shape)],
        out_specs=(row(wq), row(2 * LANES), row(2 * LANES), tail, tail),
        out_shape=(jax.ShapeDtypeStruct((b, s, wq), BF16),
                   jax.ShapeDtypeStruct((b, s, 2 * LANES), BF16),
                   jax.ShapeDtypeStruct((b, s, 2 * LANES), BF16),
                   jax.ShapeDtypeStruct((b, lc, LANES), F32),
                   jax.ShapeDtypeStruct((b, lc, LANES), F32)),
        compiler_params=_params(("arbitrary", "arbitrary")),
        name="odd_proj",
    )(x, g, w)


def _sample_proj_kernel(*refs, gated):
    if gated:
        x_ref, g_ref, w_ref, wf_ref, bf_ref, p_ref, logf_ref = refs
    else:
        x_ref, g_ref, w_ref, p_ref = refs
    h = _rmsnorm(x_ref[...], g_ref[...]).astype(BF16)
    p_ref[...] = _dot(h, w_ref[...])
    if gated:
        logf_ref[...] = _log_sigmoid(_dot(h, wf_ref[...]) + bf_ref[...])


def _sample_proj(x, g, w, tm, w_f=None, b_f=None):
    n, d = x.shape
    row = lambda w_: pl.BlockSpec((tm, w_), lambda i: (i, 0))
    in_specs = [row(d), _const_spec((1, d)), _const_spec(w.shape)]
    args = [x, g, w]
    out_specs = row(w.shape[1])
    out_shape = jax.ShapeDtypeStruct((n, w.shape[1]), F32)
    if w_f is not None:
        in_specs += [_const_spec(w_f.shape), _const_spec((1, LANES))]
        args += [w_f, b_f]
        out_specs = (out_specs, row(LANES))
        out_shape = (out_shape, jax.ShapeDtypeStruct((n, LANES), F32))
    return pl.pallas_call(
        functools.partial(_sample_proj_kernel, gated=w_f is not None),
        grid=(n // tm,),
        in_specs=in_specs,
        out_specs=out_specs,
        out_shape=out_shape,
        compiler_params=_params(("arbitrary",)),
        name="sample_proj",
    )(*args)


def _block_diag_q(q, nheads, scale):
    t, w = q.shape
    tiled = jnp.concatenate([q] * nheads, axis=0)
    r = lax.broadcasted_iota(jnp.int32, (nheads * t, w), 0)
    c = lax.broadcasted_iota(jnp.int32, (nheads * t, w), 1)
    return jnp.where(r // t == c // HEAD_DIM, tiled * scale, 0.0).astype(BF16)


def _block_diag_extract(o_all, nheads, t):
    w = o_all.shape[1]
    c = lax.broadcasted_iota(jnp.int32, (t, w), 1)
    out = jnp.zeros((t, w), F32)
    for h in range(nheads):
        out = jnp.where(c // HEAD_DIM == h, o_all[h * t:(h + 1) * t, :], out)
    return out


def _heads_major(ref):
    _, nh, hd, nk = ref.shape
    return ref[0].reshape(nh * hd, nk).astype(BF16)


def _pad_rows(x, rows):
    return jnp.concatenate([x, jnp.zeros((rows - x.shape[0], x.shape[1]), x.dtype)], axis=0)


def _sample_even_kernel(qa_ref, kan_ref, van_ref, qb_ref, kbn_ref, vbn_ref, ck_ref, cv_ref, lft_ref,
                        cbk_ref, cbv_ref, e3_ref, biasb_ref, oa_ref, ob_ref,
                        bias_sc, m_sc, acc_sc, *, t, past, kc, nkc):
    c = pl.program_id(1)
    rows = H_A * t
    qbd = _block_diag_q(qa_ref[...], H_A, SCALE)

    @pl.when(c == 0)
    def _():
        cum = _cumsum_lanes(lft_ref[0])
        last = past + t - 1
        suffix = cum[:, last:last + 1] - cum
        hi, mid, lo = _split3(suffix)
        s3 = jnp.concatenate([hi, mid, lo, jnp.zeros((LANES - 3 * H_A, suffix.shape[1]), F32)], axis=0)
        s3 = s3.astype(BF16)
        for ch in range(nkc + 1):
            w = kc if ch < nkc else LANES
            bias_sc[ch, :, :w] = _dot(e3_ref[...], s3[:, ch * kc:ch * kc + w])
        m_sc[...] = jnp.full_like(m_sc, NEG_INF)
        acc_sc[...] = jnp.zeros_like(acc_sc)

    def update(s, pv):
        m = m_sc[...]
        m_new = jnp.maximum(m, jnp.max(s, axis=1, keepdims=True))
        p = jnp.exp(s - m_new)
        alpha = jnp.exp(m - m_new)
        acc = acc_sc[...]
        lsum = alpha * acc[:, W_A:W_A + 1] + jnp.sum(p, axis=1, keepdims=True)
        o = alpha * acc[:, :W_A] + pv(p.astype(BF16))
        acc_sc[:, :W_A] = o
        acc_sc[:, W_A:] = jnp.broadcast_to(lsum, (rows, LANES))
        m_sc[...] = m_new

    s = _dot(qbd, _heads_major(ck_ref)) + bias_sc[c]
    vt = _heads_major(cv_ref)
    update(s, lambda p: _dot_nt(p, vt))

    @pl.when(c == nkc - 1)
    def _():
        kn = _pad_rows(kan_ref[...], LANES).astype(BF16)
        vn = _pad_rows(van_ref[...], LANES).astype(BF16)
        sn = _dot_nt(qbd, kn) + bias_sc[nkc, :, :LANES]
        r = lax.broadcasted_iota(jnp.int32, sn.shape, 0)
        col = lax.broadcasted_iota(jnp.int32, sn.shape, 1)
        sn = jnp.where(col <= r % t, sn, NEG_INF)
        update(sn, lambda p: _dot(p, vn))
        acc = acc_sc[...]
        o_all = acc[:, :W_A] / acc[:, W_A:W_A + 1]
        oa_ref[...] = _block_diag_extract(o_all, H_A, t).astype(BF16)

        qbd_b = _block_diag_q(qb_ref[...], H_B, SCALE)
        lbk = cbk_ref.shape[3]
        kbn = _pad_rows(kbn_ref[...], LANES).astype(BF16)
        vbn = _pad_rows(vbn_ref[...], LANES).astype(BF16)
        sb = jnp.concatenate([_dot(qbd_b, _heads_major(cbk_ref)), _dot_nt(qbd_b, kbn)], axis=1) + biasb_ref[...]
        mb = jnp.max(sb, axis=1, keepdims=True)
        eb = jnp.exp(sb - mb)
        lb = jnp.sum(eb, axis=1, keepdims=True)
        eb = eb.astype(BF16)
        ob_all = (_dot_nt(eb[:, :lbk], _heads_major(cbv_ref)) + _dot(eb[:, lbk:], vbn)) / lb
        ob_ref[...] = _block_diag_extract(ob_all, H_B, t).astype(BF16)


def _sample_even(p, cache_k, cache_v, logf_t, cache_bk, cache_bv, e3, bias_b, *, t, kc):
    n = p.shape[0]
    nb = n // t
    past = cache_k.shape[3]
    nkc = past // kc
    lbk = cache_bk.shape[3]
    pcol = lambda j: pl.BlockSpec((t, W_A), lambda bi, c: (bi, j))
    rows = H_A * t
    return pl.pallas_call(
        functools.partial(_sample_even_kernel, t=t, past=past, kc=kc, nkc=nkc),
        grid=(nb, nkc),
        in_specs=[pcol(0), pcol(1), pcol(2), pcol(3), pcol(4), pcol(5),
                  pl.BlockSpec((1, H_A, HEAD_DIM, kc), lambda bi, c: (bi, 0, 0, c)),
                  pl.BlockSpec((1, H_A, HEAD_DIM, kc), lambda bi, c: (bi, 0, 0, c)),
                  pl.BlockSpec((1, H_A, logf_t.shape[2]), lambda bi, c: (bi, 0, 0)),
                  pl.BlockSpec((1, H_B, HEAD_DIM, lbk), lambda bi, c: (bi, 0, 0, 0)),
                  pl.BlockSpec((1, H_B, HEAD_DIM, lbk), lambda bi, c: (bi, 0, 0, 0)),
                  _const_spec(e3.shape), _const_spec(bias_b.shape)],
        out_specs=(pl.BlockSpec((t, W_A), lambda bi, c: (bi, 0)),
                   pl.BlockSpec((t, W_B), lambda bi, c: (bi, 0))),
        out_shape=(jax.ShapeDtypeStruct((n, W_A), BF16), jax.ShapeDtypeStruct((n, W_B), BF16)),
        scratch_shapes=[pltpu.VMEM((nkc + 1, rows, kc), F32),
                        pltpu.VMEM((rows, 1), F32),
                        pltpu.VMEM((rows, W_A + LANES), F32)],
        compiler_params=_params(("arbitrary", "arbitrary")),
        name="sample_even",
    )(p, p, p, p, p, p, cache_k, cache_v, logf_t, cache_bk, cache_bv, e3, bias_b)


def _sample_odd_kernel(q_ref, kn_ref, vn_ref, ck_ref, cv_ref, x_ref, bias_ref, sink_ref, o_ref, *, t):
    qbd = _block_diag_q(q_ref[...], H_C, SCALE)
    kall = jnp.concatenate([ck_ref[0], _pad_rows(kn_ref[...], LANES)], axis=0).astype(BF16)
    vall = jnp.concatenate([cv_ref[0], _pad_rows(vn_ref[...], LANES)], axis=0).astype(BF16)
    kexp = _dot(kall, x_ref[...]).astype(BF16)
    vexp = _dot(vall, x_ref[...]).astype(BF16)
    s = _dot_nt(qbd, kexp) + bias_ref[...]
    sk = sink_ref[...]
    m = jnp.maximum(jnp.max(s, axis=1, keepdims=True), sk)
    e = jnp.exp(s - m)
    l = jnp.sum(e, axis=1, keepdims=True) + jnp.exp(sk - m)
    o_all = _dot(e.astype(BF16), vexp) / l
    o_ref[...] = _block_diag_extract(o_all, H_C, t).astype(BF16)


def _sample_odd(p, cache_k, cache_v, xexp, bias, sink_col, *, t):
    n = p.shape[0]
    nb = n // t
    wq = H_C * HEAD_DIM
    lc = cache_k.shape[1]
    return pl.pallas_call(
        functools.partial(_sample_odd_kernel, t=t),
        grid=(nb,),
        in_specs=[pl.BlockSpec((t, wq), lambda bi: (bi, 0)),
                  pl.BlockSpec((t, LANES), lambda bi: (bi, wq // LANES)),
                  pl.BlockSpec((t, LANES), lambda bi: (bi, wq // LANES + 1)),
                  pl.BlockSpec((1, lc, LANES), lambda bi: (bi, 0, 0)),
                  pl.BlockSpec((1, lc, LANES), lambda bi: (bi, 0, 0)),
                  _const_spec(xexp.shape), _const_spec(bias.shape), _const_spec(sink_col.shape)],
        out_specs=pl.BlockSpec((t, wq), lambda bi: (bi, 0)),
        out_shape=jax.ShapeDtypeStruct((n, wq), BF16),
        compiler_params=_params(("arbitrary",)),
        name="sample_odd",
    )(p, p, p, cache_k, cache_v, xexp, bias, sink_col)


def _t5_bucket(rel_mem):
    nb = T5_BUCKETS // 2
    max_exact = nb // 2
    n = jnp.abs(rel_mem)
    large = max_exact + (jnp.log(jnp.maximum(n, 1).astype(F32) / max_exact)
                         / math.log(T5_MAX_DIST / max_exact) * (nb - max_exact)).astype(jnp.int32)
    large = jnp.minimum(large, nb - 1)
    return jnp.where(rel_mem > 0, nb, 0) + jnp.where(n < max_exact, n, large)


def _bias_b_of_rel(table, rel):
    idx = np.clip(rel, -B_REL_CLIP, B_REL_CLIP) + B_REL_CLIP
    return table.astype(F32)[idx].T


def _bias_c_of_rel(table, rel):
    return table.astype(F32)[_t5_bucket(-jnp.asarray(rel, jnp.int32))].T


def _bias_tiles_kernel(f_ref, o_ref, *, tq, left, offsets, left_chunks, valid_cols):
    n = f_ref.shape[2]
    win = o_ref.shape[3]
    rows = jnp.broadcast_to(f_ref[0], (tq, n))
    wide = pltpu.roll(rows, n - (tq - 1), 1, stride=1, stride_axis=0)
    q = lax.broadcasted_iota(jnp.int32, (tq, win), 0)
    k = lax.broadcasted_iota(jnp.int32, (tq, win), 1)
    shift = CHUNK.bit_length() - 1
    for v, off in enumerate(offsets):
        if valid_cols is None:
            d = off // CHUNK + jnp.right_shift(q, shift) - jnp.right_shift(k, shift)
            ok = (d >= 0) & (d <= left_chunks)
        else:
            ok = k < valid_cols
        o_ref[v, 0] = jnp.where(ok, wide[:, left - off:left - off + win], NEG_INF)


def _bias_tiles(bias_of_rel, *, tq, left, win, offsets, left_chunks=None, valid_cols=None):
    assert CHUNK & (CHUNK - 1) == 0 and all((left - o) % LANES == 0 for o in offsets)
    ncols = win + left - min(offsets)
    n = tq + ncols - 1
    n_pad = -(-n // LANES) * LANES
    f_rev = bias_of_rel(tq - 1 + left - np.arange(n))
    nheads = f_rev.shape[0]
    f_rev = jnp.pad(f_rev, ((0, 0), (0, n_pad - n)))[:, None, :]
    return pl.pallas_call(
        functools.partial(_bias_tiles_kernel, tq=tq, left=left, offsets=tuple(offsets),
                          left_chunks=left_chunks, valid_cols=valid_cols),
        grid=(nheads,),
        in_specs=[pl.BlockSpec((1, 1, n_pad), lambda h: (h, 0, 0))],
        out_specs=pl.BlockSpec((len(offsets), 1, tq, win), lambda h: (0, h, 0, 0)),
        out_shape=jax.ShapeDtypeStruct((len(offsets), nheads, tq, win), F32),
        compiler_params=_params(("arbitrary",)),
        name="bias_tiles",
    )(f_rev)


def _sample_bias(bias_of_rel, t, cache_len):
    ncols = cache_len + LANES
    tiles = _bias_tiles(bias_of_rel, tq=t, left=cache_len, win=ncols, offsets=[cache_len],
                        valid_cols=cache_len + t)
    return tiles.reshape(-1, ncols)


def _placement():
    pq = np.zeros((LANES, H_A * LANES), np.float32)
    pk = np.zeros((LANES, H_A * LANES), np.float32)
    for h in range(H_A):
        for j in range(3):
            pq[8 * j + h, LANES * h + HEAD_DIM + j] = 1.0
            pq[24, LANES * h + HEAD_DIM + 3 + j] = 1.0
            pk[24, LANES * h + HEAD_DIM + j] = 1.0
            pk[8 * j + h, LANES * h + HEAD_DIM + 3 + j] = -1.0
    return jnp.asarray(pq, BF16), jnp.asarray(pk, BF16)


def kernel(x_prompt, x_sample, cache_a_k, cache_a_v, cache_a_logf, cache_b_k, cache_b_v, cache_c_k, cache_c_v,
           norm_mix, norm_ffn, norm_final, w_in_even, b_forget, rel_bias_b, w_out_even, w_in_odd, sinks_c,
           w_out_odd, t5_bias, w_ffn_in, w_ffn_out):
    b, s, d = x_prompt.shape
    nb, t, _ = x_sample.shape
    past = cache_a_k.shape[2]
    n_p, n_s = b * s, nb * t

    w_even = w_in_even[0]
    w_main = w_even[:, :3 * W_A + 3 * W_B].astype(BF16)
    wf = w_even[:, 3 * W_A + 3 * W_B:]
    w_f = jnp.concatenate([wf, wf, wf, jnp.zeros((d, LANES - 3 * H_A), F32)], axis=1).astype(BF16)
    bf = b_forget[0].astype(F32)
    b_f = jnp.concatenate([bf, bf, bf, jnp.zeros((LANES - 3 * H_A,), F32)])[None, :]
    pq, pk = _placement()
    w_oe = w_out_even[0].astype(BF16)
    w_oo = w_out_odd[0].astype(BF16)
    w_odd = w_in_odd[0].astype(BF16)
    w_fi = w_ffn_in.astype(BF16)
    w_fo = w_ffn_out.astype(BF16)
    g_mix = norm_mix.astype(F32)[:, None, :]
    g_ffn = norm_ffn.astype(F32)[:, None, :]
    g_fin = norm_final.astype(F32)[None, :]

    bias_b_of = functools.partial(_bias_b_of_rel, rel_bias_b[0])
    bias_c_of = functools.partial(_bias_c_of_rel, t5_bias)
    tq_b, left_b = 256, B_LEFT_CHUNKS * CHUNK
    offs_b = [min(v * tq_b, left_b) for v in range(left_b // tq_b + 1)]
    log2_domain = lambda bias_of: (lambda rel: bias_of(rel) * LOG2E)
    bias_b = _bias_tiles(log2_domain(bias_b_of), tq=tq_b, left=left_b, win=tq_b + left_b, offsets=offs_b,
                         left_chunks=B_LEFT_CHUNKS)
    tq_c, left_c = 256, C_LEFT_CHUNKS * CHUNK
    offs_c = [0, left_c]
    bias_c = _bias_tiles(log2_domain(bias_c_of), tq=tq_c, left=left_c, win=tq_c + left_c, offsets=offs_c,
                         left_chunks=C_LEFT_CHUNKS)

    xp = x_prompt
    w_all = jnp.concatenate([w_main[:, 3 * W_A:3 * W_A + W_B], w_f, w_main[:, :3 * W_A],
                             w_main[:, 3 * W_A + W_B:]], axis=1)
    qaug, kaug, vaug, ka, va, logf, qb, kb, vb, kbt, vbt = _even_proj(
        xp, g_mix[0], w_all, b_f, pq, pk, tm=512)
    oa = _fox(qaug, kaug, vaug, blk=512, nsub=2)
    ob = _band(qb, kb, vb, bias_b, None, tq=tq_b, left=left_b, npairs=H_B // 2, shared_kv=False,
               name="band_b")
    xp1 = _out_ffn(xp.reshape(n_p, d),
                   [(oa.reshape(n_p, W_A), w_oe[:W_A]), (ob.reshape(n_p, W_B), w_oe[W_A:])],
                   g_ffn[0], w_fi, w_fo, g_fin, tm=512, layer=0, final_norm=False)

    qc, kcd, vcd, kct, vct = _odd_proj(xp1.reshape(b, s, d), g_mix[1], w_odd, tm=512)
    oc = _band(qc, kcd, vcd, bias_c, sinks_c[0].astype(F32) * LOG2E, tq=tq_c, left=left_c, npairs=G_C // 2,
               shared_kv=True, name="band_c")
    y_prompt = _out_ffn(xp1, [(oc.reshape(n_p, H_C * HEAD_DIM), w_oo)],
                        g_ffn[1], w_fi, w_fo, g_fin, tm=512, layer=1, final_norm=True)

    xs = x_sample.reshape(n_s, d)
    ps, logf_sp = _sample_proj(xs, g_mix[0], w_main, tm=256, w_f=w_f, b_f=b_f)
    logf_s = logf_sp[:, :H_A]
    kpad = LANES * -(-(past + t) // LANES)
    logf_all = jnp.concatenate([cache_a_logf[0].astype(F32), logf_s.reshape(nb, t, H_A)], axis=1)
    logf_t = jnp.pad(jnp.swapaxes(logf_all, 1, 2), ((0, 0), (0, 0), (0, kpad - past - t)))
    lbs = cache_b_k.shape[2]
    bias_sb = _sample_bias(bias_b_of, t, lbs)
    e3 = np.zeros((H_A * t, LANES), np.float32)
    for j in range(3):
        e3[np.arange(H_A * t), 8 * j + np.arange(H_A * t) // t] = 1.0
    keys_minor = lambda cache: jnp.transpose(cache[0], (0, 2, 3, 1))
    oa_s, ob_s = _sample_even(ps, keys_minor(cache_a_k), keys_minor(cache_a_v), logf_t,
                              keys_minor(cache_b_k), keys_minor(cache_b_v),
                              jnp.asarray(e3, BF16), bias_sb, t=t, kc=2048)
    xs1 = _out_ffn(xs, [(oa_s, w_oe[:W_A]), (ob_s, w_oe[W_A:])], g_ffn[0], w_fi, w_fo, g_fin,
                   tm=256, layer=0, final_norm=False)

    ps2 = _sample_proj(xs1, g_mix[1], w_odd, tm=256)
    lcs = cache_c_k.shape[2]
    bias_sc = _sample_bias(bias_c_of, t, lcs)
    sink_col = jnp.repeat(sinks_c[0].astype(F32), t)[:, None]
    lane_head = np.arange(H_C * HEAD_DIM) // HEAD_DIM
    src_lane = (lane_head // G_C) * HEAD_DIM + np.arange(H_C * HEAD_DIM) % HEAD_DIM
    xexp = jnp.asarray(np.arange(LANES)[:, None] == src_lane[None, :], BF16)
    oc_s = _sample_odd(ps2, cache_c_k[0].reshape(nb, lcs, LANES), cache_c_v[0].reshape(nb, lcs, LANES),
                       xexp, bias_sc, sink_col, t=t)
    y_sample = _out_ffn(xs1, [(oc_s, w_oo)], g_ffn[1], w_fi, w_fo, g_fin, tm=256, layer=1, final_norm=True)

    wq = H_C * HEAD_DIM
    hd = lambda a, lead, h: a.reshape((1,) + lead + (h, HEAD_DIM))
    return (
        y_prompt.reshape(b, s, d), y_sample.reshape(nb, t, d),
        hd(ka, (b, s), H_A), hd(va, (b, s), H_A), logf[None],
        hd(kbt, (b, kbt.shape[1]), H_B), hd(vbt, (b, vbt.shape[1]), H_B),
        hd(kct, (b, kct.shape[1]), HKV_C), hd(vct, (b, vct.shape[1]), HKV_C),
        hd(ps[:, W_A:2 * W_A], (nb, t), H_A), hd(ps[:, 2 * W_A:3 * W_A], (nb, t), H_A),
        logf_s.reshape(1, nb, t, H_A),
        hd(ps[:, 3 * W_A + W_B:3 * W_A + 2 * W_B], (nb, t), H_B),
        hd(ps[:, 3 * W_A + 2 * W_B:3 * W_A + 3 * W_B], (nb, t), H_B),
        hd(ps2[:, wq:wq + LANES], (nb, t), HKV_C), hd(ps2[:, wq + LANES:wq + 2 * LANES], (nb, t), HKV_C),
    )
```

```python
import functools
import math

import jax
import jax.numpy as jnp
import numpy as np
from jax import lax
from jax.experimental import pallas as pl
from jax.experimental.pallas import tpu as pltpu

D_MODEL = 1024
HEAD_DIM = 64
CHUNK = 64
H_A = 8
H_B = 8
B_LEFT_CHUNKS = 8
B_REL_CLIP = 128
H_C = 16
HKV_C = 2
G_C = H_C // HKV_C
WINDOW = 128
C_LEFT_CHUNKS = WINDOW // CHUNK
T5_BUCKETS = 32
T5_MAX_DIST = 128
EPS = 1e-6
W_A = H_A * HEAD_DIM
W_B = H_B * HEAD_DIM
SCALE = HEAD_DIM ** -0.5
LOG2E = math.log2(math.e)

LANES = 128
VMEM_LIMIT = 60 * 1024 * 1024

F32 = jnp.float32
BF16 = jnp.bfloat16
NEG_INF = float("-inf")

_NT = (((1,), (1,)), ((), ()))


def _dot(a, b):
    return jnp.dot(a, b, preferred_element_type=F32)


def _dot_nt(a, b):
    return lax.dot_general(a, b, _NT, preferred_element_type=F32)


def _rmsnorm(x, g):
    ms = jnp.mean(x * x, axis=-1, keepdims=True)
    return x * lax.rsqrt(ms + EPS) * g


def _log_sigmoid(x):
    return jnp.minimum(x, 0.0) - jnp.log1p(jnp.exp(-jnp.abs(x)))


def _split3(x):
    hi = x.astype(BF16).astype(F32)
    r1 = x - hi
    mid = r1.astype(BF16).astype(F32)
    lo = (r1 - mid).astype(BF16).astype(F32)
    return hi, mid, lo


def _cumsum_rows(x):
    n = x.shape[0]
    row = lax.broadcasted_iota(jnp.int32, x.shape, 0)
    s = 1
    while s < n:
        x = x + jnp.where(row >= s, pltpu.roll(x, s, 0), 0.0)
        s *= 2
    return x


def _cumsum_lanes(x):
    n = x.shape[1]
    col = lax.broadcasted_iota(jnp.int32, x.shape, 1)
    s = 1
    while s < n:
        x = x + jnp.where(col >= s, pltpu.roll(x, s, 1), 0.0)
        s *= 2
    return x


def _staggered(items, stage_a, stage_b, stage_c):
    n = len(items)
    a_out, b_out = {}, {}
    for step in range(n + 2):
        if step < n:
            a_out[step] = stage_a(items[step])
        if 0 <= step - 1 < n:
            b_out[step - 1] = stage_b(items[step - 1], a_out.pop(step - 1))
        if 0 <= step - 2 < n:
            stage_c(items[step - 2], b_out.pop(step - 2))


def _const_spec(shape):
    nd = len(shape)
    return pl.BlockSpec(shape, lambda *_: (0,) * nd, pipeline_mode=pl.Buffered(1))


def _params(sem):
    return pltpu.CompilerParams(dimension_semantics=sem, vmem_limit_bytes=VMEM_LIMIT)


def _even_proj_kernel(x_ref, g_ref, w_ref, bf_ref, pq_ref, pk_ref,
                      qaug_ref, kaug_ref, vaug_ref, ka_ref, va_ref, logf_ref,
                      qb_ref, kb_ref, vb_ref, kbt_ref, vbt_ref, carry_ref):
    @pl.when(pl.program_id(1) == 0)
    def _():
        carry_ref[...] = jnp.zeros_like(carry_ref)

    h = _rmsnorm(x_ref[0], g_ref[...]).astype(BF16)
    tm = h.shape[0]
    lane = lax.broadcasted_iota(jnp.int32, (tm, LANES), 1)
    low = lane < HEAD_DIM
    first = W_B + LANES

    def chunk(n):
        return _dot(h, w_ref[:, first + W_A * n:first + W_A * (n + 1)])

    pf = _dot(h, w_ref[:, :first])
    qb_ref[0] = (pf[:, :W_B] * (SCALE * LOG2E)).astype(BF16)
    fa = pf[:, W_B:]
    pc = chunk(3)
    kb_ref[0] = pc.astype(BF16)
    kbt_ref[0] = pc
    pc = chunk(4)
    vb_ref[0] = pc.astype(BF16)
    vbt_ref[0] = pc
    logf = _log_sigmoid(fa + bf_ref[...])
    logf_ref[0] = logf[:, :H_A]
    c = _cumsum_rows(logf) + carry_ref[...]
    carry_ref[...] = c[tm - 1:tm, :]
    hi, mid, lo = _split3(c * LOG2E)
    a3 = jnp.where(lane < 8, hi, jnp.where(lane < 16, mid, jnp.where(lane < 24, lo,
                   jnp.where(lane == 24, 1.0, 0.0)))).astype(BF16)
    augq = _dot(a3, pq_ref[...])
    augk = _dot(a3, pk_ref[...])
    vone = jnp.where(lane == HEAD_DIM, 1.0, 0.0)

    def per_head(dst_ref, pc, spare):
        for j in range(H_A // 2):
            pair = pc[:, LANES * j:LANES * (j + 1)]
            for hh, val in enumerate((pair, pltpu.roll(pair, HEAD_DIM, 1))):
                o = slice(LANES * (2 * j + hh), LANES * (2 * j + hh + 1))
                dst_ref[0, :, o] = jnp.where(low, val, vone if spare is None else spare[:, o]).astype(BF16)

    pc = chunk(0)
    per_head(qaug_ref, pc * (SCALE * LOG2E), augq)
    pc = chunk(1)
    ka_ref[0] = pc
    per_head(kaug_ref, pc, augk)
    pc = chunk(2)
    va_ref[0] = pc
    per_head(vaug_ref, pc, None)


def _even_proj(x, g, w_all, b_f, pq, pk, tm):
    b, s, d = x.shape
    nt = s // tm
    lb = min(B_LEFT_CHUNKS * CHUNK, s)
    assert lb == tm, "band-state tail must be exactly one row tile"
    row = lambda w: pl.BlockSpec((1, tm, w), lambda bi, ti: (bi, ti, 0))
    tail = pl.BlockSpec((1, lb, W_B), lambda bi, ti: (bi, 0, 0))
    outs = (
        jax.ShapeDtypeStruct((b, s, H_A * LANES), BF16),
        jax.ShapeDtypeStruct((b, s, H_A * LANES), BF16),
        jax.ShapeDtypeStruct((b, s, H_A * LANES), BF16),
        jax.ShapeDtypeStruct((b, s, W_A), F32),
        jax.ShapeDtypeStruct((b, s, W_A), F32),
        jax.ShapeDtypeStruct((b, s, H_A), F32),
        jax.ShapeDtypeStruct((b, s, W_B), BF16),
        jax.ShapeDtypeStruct((b, s, W_B), BF16),
        jax.ShapeDtypeStruct((b, s, W_B), BF16),
        jax.ShapeDtypeStruct((b, lb, W_B), F32),
        jax.ShapeDtypeStruct((b, lb, W_B), F32),
    )
    return pl.pallas_call(
        _even_proj_kernel,
        grid=(b, nt),
        in_specs=[row(d), _const_spec((1, d)), _const_spec(w_all.shape),
                  _const_spec((1, LANES)), _const_spec(pq.shape), _const_spec(pk.shape)],
        out_specs=(row(H_A * LANES), row(H_A * LANES), row(H_A * LANES), row(W_A), row(W_A), row(H_A),
                   row(W_B), row(W_B), row(W_B), tail, tail),
        out_shape=outs,
        scratch_shapes=[pltpu.VMEM((1, LANES), F32)],
        compiler_params=_params(("arbitrary", "arbitrary")),
        name="even_proj",
    )(x, g, w_all, b_f, pq, pk)


def _fox_kernel(q_ref, k_ref, v_ref, o_ref, m_sc, acc_sc, *, blk, nsub):
    i = pl.program_id(2)
    nh = 2
    chains = [(sub, hh) for sub in range(nsub) for hh in range(nh)]
    qs = {(sub, hh): q_ref[0, sub * blk:(sub + 1) * blk, LANES * hh:LANES * (hh + 1)] for sub, hh in chains}
    r = lax.broadcasted_iota(jnp.int32, (blk, blk), 0)
    c = lax.broadcasted_iota(jnp.int32, (blk, blk), 1)
    causal = c <= r
    m_sc[...] = jnp.full_like(m_sc, NEG_INF)
    acc_sc[...] = jnp.zeros_like(acc_sc)

    def blocks(specs):
        items = [(pl.multiple_of(j * blk, blk), ch, modes[ch[0]])
                 for j, modes in specs for ch in chains if modes[ch[0]] is not None]

        def stage_scores(item):
            start, ch, mode = item
            k = k_ref[0, pl.ds(start, blk), LANES * ch[1]:LANES * (ch[1] + 1)]
            s = _dot_nt(qs[ch], k)
            if mode == "masked":
                s = jnp.where(causal, s, NEG_INF)
            m = m_sc[chains.index(ch)]
            m_new = jnp.maximum(m, jnp.max(s, axis=1, keepdims=True))
            m_sc[chains.index(ch)] = m_new
            return s, m, m_new

        def stage_exp(item, sm):
            s, m, m_new = sm
            p = jnp.concatenate([jnp.exp2(s[:, LANES * cb:LANES * (cb + 1)] - m_new)
                                 for cb in range(blk // LANES)], axis=1)
            return p.astype(BF16), jnp.exp2(m - m_new)

        def stage_pv(item, pa):
            start, ch, _ = item
            p, alpha = pa
            v = v_ref[0, pl.ds(start, blk), LANES * ch[1]:LANES * (ch[1] + 1)]
            n = chains.index(ch)
            acc_sc[n] = alpha * acc_sc[n] + _dot(p, v)

        _staggered(items, stage_scores, stage_exp, stage_pv)

    @pl.loop(0, i)
    def _(jj):
        blocks([(nsub * jj + d, ["full"] * nsub) for d in range(nsub)])

    blocks([(nsub * i + d, [None if sub < d else ("masked" if sub == d else "full") for sub in range(nsub)])
            for d in range(nsub)])

    lane = lax.broadcasted_iota(jnp.int32, (blk, LANES), 1)
    for sub in range(nsub):
        a0 = acc_sc[chains.index((sub, 0))]
        a1 = acc_sc[chains.index((sub, 1))]
        o0 = a0 / a0[:, HEAD_DIM:HEAD_DIM + 1]
        o1 = a1 / a1[:, HEAD_DIM:HEAD_DIM + 1]
        o_ref[0, sub * blk:(sub + 1) * blk, :] = jnp.where(
            lane < HEAD_DIM, o0, pltpu.roll(o1, HEAD_DIM, 1)).astype(BF16)


def _fox(qaug, kaug, vaug, blk, nsub):
    b, s, _ = qaug.shape
    pairs = H_A // 2
    tq = blk * nsub
    return pl.pallas_call(
        functools.partial(_fox_kernel, blk=blk, nsub=nsub),
        grid=(b, pairs, s // tq),
        in_specs=[pl.BlockSpec((1, tq, 2 * LANES), lambda bi, hp, i: (bi, i, hp)),
                  pl.BlockSpec((1, s, 2 * LANES), lambda bi, hp, i: (bi, 0, hp)),
                  pl.BlockSpec((1, s, 2 * LANES), lambda bi, hp, i: (bi, 0, hp))],
        out_specs=pl.BlockSpec((1, tq, LANES), lambda bi, hp, i: (bi, i, hp)),
        out_shape=jax.ShapeDtypeStruct((b, s, W_A), BF16),
        scratch_shapes=[pltpu.VMEM((2 * nsub, blk, LANES), F32), pltpu.VMEM((2 * nsub, blk, LANES), F32)],
        compiler_params=_params(("arbitrary", "arbitrary", "arbitrary")),
        name="fox",
    )(qaug, kaug, vaug)


def _band_kernel(*refs, tq, left, npairs, heads_per_group, use_sinks, shared_kv):
    mxu_rowsum = not use_sinks
    if use_sinks:
        sink_ref, q_ref, k_ref, v_ref, bias_ref, o_ref = refs
    else:
        q_ref, k_ref, v_ref, bias_ref, o_ref = refs
    g = pl.program_id(1)
    i = pl.program_id(2)
    nvar, _, tsub, win = bias_ref.shape
    nsq = tq // tsub
    lane = lax.broadcasted_iota(jnp.int32, (tsub, LANES), 1)
    low = lane < HEAD_DIM
    items = [(sq, pr, hh) for sq in range(nsq) for pr in range(npairs) for hh in range(2)]

    def window(sq):
        gsub = i * nsq + sq
        return pl.multiple_of(jnp.maximum(gsub * tsub - left, 0), LANES), jnp.minimum(gsub, nvar - 1)

    def kv_lanes(pr):
        kl = 0 if shared_kv else LANES * pr
        return slice(kl, kl + LANES)

    def sink(pr, hh):
        return sink_ref[g * heads_per_group + 2 * pr + hh]

    def stage_scores(item):
        sq, pr, hh = item
        start, var = window(sq)
        qp = q_ref[0, sq * tsub:(sq + 1) * tsub, LANES * pr:LANES * (pr + 1)]
        qm = jnp.where(low if hh == 0 else jnp.logical_not(low), qp, jnp.zeros_like(qp))
        s = _dot_nt(qm, k_ref[0, pl.ds(start, win), kv_lanes(pr)]) + bias_ref[var, 2 * pr + hh]
        m = jnp.max(s, axis=1, keepdims=True)
        return s, (jnp.maximum(m, sink(pr, hh)) if use_sinks else m)

    def stage_exp(item, sm):
        s, m = sm
        e = jnp.exp2(s - m)
        l = None if mxu_rowsum else jnp.sum(e, axis=1, keepdims=True)
        if use_sinks:
            l = l + jnp.exp2(sink(item[1], item[2]) - m)
        return e.astype(BF16), l

    outs = {}

    def stage_pv(item, pe):
        p, l = pe
        sq, pr, hh = item
        v = v_ref[0, pl.ds(window(sq)[0], win), kv_lanes(pr)]
        if mxu_rowsum:
            ov = _dot(p, jnp.concatenate([v, jnp.ones((win, LANES), BF16)], axis=1))
            outs[hh] = ov[:, :LANES] / ov[:, LANES:]
        else:
            outs[hh] = _dot(p, v) / l
        if hh == 1:
            o_ref[0, sq * tsub:(sq + 1) * tsub, LANES * pr:LANES * (pr + 1)] = jnp.where(
                low, outs[0], outs[1]).astype(BF16)

    _staggered(items, stage_scores, stage_exp, stage_pv)


def _band(q, k, v, bias, sinks, *, tq, left, npairs, shared_kv, name):
    b, s, wq = q.shape
    tsub, win = bias.shape[-2:]
    wblk = npairs * LANES
    groups = wq // wblk
    hpg = 2 * npairs
    use_sinks = sinks is not None
    kern = functools.partial(_band_kernel, tq=tq, left=left, npairs=npairs,
                             heads_per_group=hpg, use_sinks=use_sinks, shared_kv=shared_kv)
    kvw = LANES if shared_kv else wblk
    resident = lambda shape, imap: pl.BlockSpec(shape, imap, pipeline_mode=pl.Buffered(1))
    in_specs = [pl.BlockSpec((1, tq, wblk), lambda bi, g, i: (bi, i, g)),
                resident((1, s, kvw), lambda bi, g, i: (bi, 0, g)),
                resident((1, s, kvw), lambda bi, g, i: (bi, 0, g)),
                resident((bias.shape[0], hpg, tsub, win), lambda bi, g, i: (0, g, 0, 0))]
    args = [q, k, v, bias]
    if use_sinks:
        in_specs = [pl.BlockSpec(memory_space=pltpu.SMEM)] + in_specs
        args = [sinks] + args
    return pl.pallas_call(
        kern,
        grid=(b, groups, s // tq),
        in_specs=in_specs,
        out_specs=pl.BlockSpec((1, tq, wblk), lambda bi, g, i: (bi, i, g)),
        out_shape=jax.ShapeDtypeStruct((b, s, wq), BF16),
        compiler_params=_params(("arbitrary", "arbitrary", "arbitrary")),
        name=name,
    )(*args)


def _out_ffn_kernel(*refs, n_attn, d_ff, final_norm):
    x_ref = refs[0]
    attn = refs[1:1 + 2 * n_attn]
    gffn_ref, win_ref, wout_ref, gfin_ref, o_ref = refs[1 + 2 * n_attn:]
    y = x_ref[...]
    for a in range(n_attn):
        y = y + _dot(attn[2 * a][...], attn[2 * a + 1][...])
    h = _rmsnorm(y, gffn_ref[...]).astype(BF16)
    gu = _dot(h, win_ref[...])
    gate = gu[:, :d_ff]
    up = gu[:, d_ff:]
    act = (gate * (1.0 / (1.0 + jnp.exp(-gate))) * up).astype(BF16)
    y = y + _dot(act, wout_ref[...])
    o_ref[...] = _rmsnorm(y, gfin_ref[...]) if final_norm else y


def _out_ffn(x, attn_pairs, g_ffn, w_in, w_out, g_fin, tm, layer, final_norm):
    n, d = x.shape
    d_ff = w_out.shape[1]
    row = lambda w: pl.BlockSpec((tm, w), lambda i: (i, 0))
    slab = lambda w: pl.BlockSpec((None,) + w.shape[1:], lambda i: (layer, 0, 0), pipeline_mode=pl.Buffered(1))
    args, specs = [x], [row(d)]
    for o, w in attn_pairs:
        args += [o, w]
        specs += [row(o.shape[1]), _const_spec(w.shape)]
    args += [g_ffn, w_in, w_out, g_fin]
    specs += [_const_spec((1, d)), slab(w_in), slab(w_out), _const_spec((1, d))]
    return pl.pallas_call(
        functools.partial(_out_ffn_kernel, n_attn=len(attn_pairs), d_ff=d_ff, final_norm=final_norm),
        grid=(n // tm,),
        in_specs=specs,
        out_specs=row(d),
        out_shape=jax.ShapeDtypeStruct((n, d), F32),
        compiler_params=_params(("arbitrary",)),
        name="out_ffn",
    )(*args)


def _odd_proj_kernel(x_ref, g_ref, w_ref, q_ref, k_ref, v_ref, kt_ref, vt_ref, *, tiles_per_batch, tail):
    t = pl.program_id(1)
    h = _rmsnorm(x_ref[0], g_ref[...]).astype(BF16)
    p = _dot(h, w_ref[...])
    wq = H_C * HEAD_DIM
    q_ref[0] = (p[:, :wq] * (SCALE * LOG2E)).astype(BF16)
    tm = p.shape[0]
    lane = lax.broadcasted_iota(jnp.int32, (tm, LANES), 1)
    low = lane < HEAD_DIM
    for src, dst in ((p[:, wq:wq + LANES], k_ref), (p[:, wq + LANES:wq + 2 * LANES], v_ref)):
        rolled = pltpu.roll(src, HEAD_DIM, 1)
        dst[0, :, :LANES] = jnp.where(low, src, rolled).astype(BF16)
        dst[0, :, LANES:] = jnp.where(low, rolled, src).astype(BF16)

    @pl.when(t == tiles_per_batch - 1)
    def _():
        kt_ref[0] = p[tm - tail:, wq:wq + LANES]
        vt_ref[0] = p[tm - tail:, wq + LANES:wq + 2 * LANES]


def _odd_proj(x, g, w, tm):
    b, s, d = x.shape
    nt = s // tm
    lc = min(C_LEFT_CHUNKS * CHUNK, s)
    assert lc <= tm
    wq = H_C * HEAD_DIM
    row = lambda w_: pl.BlockSpec((1, tm, w_), lambda bi, ti: (bi, ti, 0))
    tail = pl.BlockSpec((1, lc, LANES), lambda bi, ti: (bi, 0, 0))
    return pl.pallas_call(
        functools.partial(_odd_proj_kernel, tiles_per_batch=nt, tail=lc),
        grid=(b, nt),
        in_specs=[row(d), _const_spec((1, d)), _const_spec(w.shape)],
        out_specs=(row(wq), row(2 * LANES), row(2 * LANES), tail, tail),
        out_shape=(jax.ShapeDtypeStruct((b, s, wq), BF16),
                   jax.ShapeDtypeStruct((b, s, 2 * LANES), BF16),
                   jax.ShapeDtypeStruct((b, s, 2 * LANES), BF16),
                   jax.ShapeDtypeStruct((b, lc, LANES), F32),
                   jax.ShapeDtypeStruct((b, lc, LANES), F32)),
        compiler_params=_params(("arbitrary", "arbitrary")),
        name="odd_proj",
    )(x, g, w)


def _sample_proj_kernel(*refs, gated):
    if gated:
        x_ref, g_ref, w_ref, wf_ref, bf_ref, p_ref, logf_ref = refs
    else:
        x_ref, g_ref, w_ref, p_ref = refs
    h = _rmsnorm(x_ref[...], g_ref[...]).astype(BF16)
    p_ref[...] = _dot(h, w_ref[...])
    if gated:
        logf_ref[...] = _log_sigmoid(_dot(h, wf_ref[...]) + bf_ref[...])


def _sample_proj(x, g, w, tm, w_f=None, b_f=None):
    n, d = x.shape
    row = lambda w_: pl.BlockSpec((tm, w_), lambda i: (i, 0))
    in_specs = [row(d), _const_spec((1, d)), _const_spec(w.shape)]
    args = [x, g, w]
    out_specs = row(w.shape[1])
    out_shape = jax.ShapeDtypeStruct((n, w.shape[1]), F32)
    if w_f is not None:
        in_specs += [_const_spec(w_f.shape), _const_spec((1, LANES))]
        args += [w_f, b_f]
        out_specs = (out_specs, row(LANES))
        out_shape = (out_shape, jax.ShapeDtypeStruct((n, LANES), F32))
    return pl.pallas_call(
        functools.partial(_sample_proj_kernel, gated=w_f is not None),
        grid=(n // tm,),
        in_specs=in_specs,
        out_specs=out_specs,
        out_shape=out_shape,
        compiler_params=_params(("arbitrary",)),
        name="sample_proj",
    )(*args)


def _block_diag_q(q, nheads, scale):
    t, w = q.shape
    tiled = jnp.concatenate([q] * nheads, axis=0)
    r = lax.broadcasted_iota(jnp.int32, (nheads * t, w), 0)
    c = lax.broadcasted_iota(jnp.int32, (nheads * t, w), 1)
    return jnp.where(r // t == c // HEAD_DIM, tiled * scale, 0.0).astype(BF16)


def _block_diag_extract(o_all, nheads, t):
    w = o_all.shape[1]
    c = lax.broadcasted_iota(jnp.int32, (t, w), 1)
    out = jnp.zeros((t, w), F32)
    for h in range(nheads):
        out = jnp.where(c // HEAD_DIM == h, o_all[h * t:(h + 1) * t, :], out)
    return out


def _heads_major(ref):
    _, nh, hd, nk = ref.shape
    return ref[0].reshape(nh * hd, nk).astype(BF16)


def _pad_rows(x, rows):
    return jnp.concatenate([x, jnp.zeros((rows - x.shape[0], x.shape[1]), x.dtype)], axis=0)


def _sample_even_kernel(qa_ref, kan_ref, van_ref, qb_ref, kbn_ref, vbn_ref, ck_ref, cv_ref, lft_ref,
                        cbk_ref, cbv_ref, e3_ref, biasb_ref, oa_ref, ob_ref,
                        bias_sc, m_sc, acc_sc, *, t, past, kc, nkc):
    c = pl.program_id(1)
    rows = H_A * t
    qbd = _block_diag_q(qa_ref[...], H_A, SCALE)

    @pl.when(c == 0)
    def _():
        cum = _cumsum_lanes(lft_ref[0])
        last = past + t - 1
        suffix = cum[:, last:last + 1] - cum
        hi, mid, lo = _split3(suffix)
        s3 = jnp.concatenate([hi, mid, lo, jnp.zeros((LANES - 3 * H_A, suffix.shape[1]), F32)], axis=0)
        s3 = s3.astype(BF16)
        for ch in range(nkc + 1):
            w = kc if ch < nkc else LANES
            bias_sc[ch, :, :w] = _dot(e3_ref[...], s3[:, ch * kc:ch * kc + w])
        m_sc[...] = jnp.full_like(m_sc, NEG_INF)
        acc_sc[...] = jnp.zeros_like(acc_sc)

    def update(s, pv):
        m = m_sc[...]
        m_new = jnp.maximum(m, jnp.max(s, axis=1, keepdims=True))
        p = jnp.exp(s - m_new)
        alpha = jnp.exp(m - m_new)
        acc = acc_sc[...]
        lsum = alpha * acc[:, W_A:W_A + 1] + jnp.sum(p, axis=1, keepdims=True)
        o = alpha * acc[:, :W_A] + pv(p.astype(BF16))
        acc_sc[:, :W_A] = o
        acc_sc[:, W_A:] = jnp.broadcast_to(lsum, (rows, LANES))
        m_sc[...] = m_new

    s = _dot(qbd, _heads_major(ck_ref)) + bias_sc[c]
    vt = _heads_major(cv_ref)
    update(s, lambda p: _dot_nt(p, vt))

    @pl.when(c == nkc - 1)
    def _():
        kn = _pad_rows(kan_ref[...], LANES).astype(BF16)
        vn = _pad_rows(van_ref[...], LANES).astype(BF16)
        sn = _dot_nt(qbd, kn) + bias_sc[nkc, :, :LANES]
        r = lax.broadcasted_iota(jnp.int32, sn.shape, 0)
        col = lax.broadcasted_iota(jnp.int32, sn.shape, 1)
        sn = jnp.where(col <= r % t, sn, NEG_INF)
        update(sn, lambda p: _dot(p, vn))
        acc = acc_sc[...]
        o_all = acc[:, :W_A] / acc[:, W_A:W_A + 1]
        oa_ref[...] = _block_diag_extract(o_all, H_A, t).astype(BF16)

        qbd_b = _block_diag_q(qb_ref[...], H_B, SCALE)
        lbk = cbk_ref.shape[3]
        kbn = _pad_rows(kbn_ref[...], LANES).astype(BF16)
        vbn = _pad_rows(vbn_ref[...], LANES).astype(BF16)
        sb = jnp.concatenate([_dot(qbd_b, _heads_major(cbk_ref)), _dot_nt(qbd_b, kbn)], axis=1) + biasb_ref[...]
        mb = jnp.max(sb, axis=1, keepdims=True)
        eb = jnp.exp(sb - mb)
        lb = jnp.sum(eb, axis=1, keepdims=True)
        eb = eb.astype(BF16)
        ob_all = (_dot_nt(eb[:, :lbk], _heads_major(cbv_ref)) + _dot(eb[:, lbk:], vbn)) / lb
        ob_ref[...] = _block_diag_extract(ob_all, H_B, t).astype(BF16)


def _sample_even(p, cache_k, cache_v, logf_t, cache_bk, cache_bv, e3, bias_b, *, t, kc):
    n = p.shape[0]
    nb = n // t
    past = cache_k.shape[3]
    nkc = past // kc
    lbk = cache_bk.shape[3]
    pcol = lambda j: pl.BlockSpec((t, W_A), lambda bi, c: (bi, j))
    rows = H_A * t
    return pl.pallas_call(
        functools.partial(_sample_even_kernel, t=t, past=past, kc=kc, nkc=nkc),
        grid=(nb, nkc),
        in_specs=[pcol(0), pcol(1), pcol(2), pcol(3), pcol(4), pcol(5),
                  pl.BlockSpec((1, H_A, HEAD_DIM, kc), lambda bi, c: (bi, 0, 0, c)),
                  pl.BlockSpec((1, H_A, HEAD_DIM, kc), lambda bi, c: (bi, 0, 0, c)),
                  pl.BlockSpec((1, H_A, logf_t.shape[2]), lambda bi, c: (bi, 0, 0)),
                  pl.BlockSpec((1, H_B, HEAD_DIM, lbk), lambda bi, c: (bi, 0, 0, 0)),
                  pl.BlockSpec((1, H_B, HEAD_DIM, lbk), lambda bi, c: (bi, 0, 0, 0)),
                  _const_spec(e3.shape), _const_spec(bias_b.shape)],
        out_specs=(pl.BlockSpec((t, W_A), lambda bi, c: (bi, 0)),
                   pl.BlockSpec((t, W_B), lambda bi, c: (bi, 0))),
        out_shape=(jax.ShapeDtypeStruct((n, W_A), BF16), jax.ShapeDtypeStruct((n, W_B), BF16)),
        scratch_shapes=[pltpu.VMEM((nkc + 1, rows, kc), F32),
                        pltpu.VMEM((rows, 1), F32),
                        pltpu.VMEM((rows, W_A + LANES), F32)],
        compiler_params=_params(("arbitrary", "arbitrary")),
        name="sample_even",
    )(p, p, p, p, p, p, cache_k, cache_v, logf_t, cache_bk, cache_bv, e3, bias_b)


def _sample_odd_kernel(q_ref, kn_ref, vn_ref, ck_ref, cv_ref, x_ref, bias_ref, sink_ref, o_ref, *, t):
    qbd = _block_diag_q(q_ref[...], H_C, SCALE)
    kall = jnp.concatenate([ck_ref[0], _pad_rows(kn_ref[...], LANES)], axis=0).astype(BF16)
    vall = jnp.concatenate([cv_ref[0], _pad_rows(vn_ref[...], LANES)], axis=0).astype(BF16)
    kexp = _dot(kall, x_ref[...]).astype(BF16)
    vexp = _dot(vall, x_ref[...]).astype(BF16)
    s = _dot_nt(qbd, kexp) + bias_ref[...]
    sk = sink_ref[...]
    m = jnp.maximum(jnp.max(s, axis=1, keepdims=True), sk)
    e = jnp.exp(s - m)
    l = jnp.sum(e, axis=1, keepdims=True) + jnp.exp(sk - m)
    o_all = _dot(e.astype(BF16), vexp) / l
    o_ref[...] = _block_diag_extract(o_all, H_C, t).astype(BF16)


def _sample_odd(p, cache_k, cache_v, xexp, bias, sink_col, *, t):
    n = p.shape[0]
    nb = n // t
    wq = H_C * HEAD_DIM
    lc = cache_k.shape[1]
    return pl.pallas_call(
        functools.partial(_sample_odd_kernel, t=t),
        grid=(nb,),
        in_specs=[pl.BlockSpec((t, wq), lambda bi: (bi, 0)),
                  pl.BlockSpec((t, LANES), lambda bi: (bi, wq // LANES)),
                  pl.BlockSpec((t, LANES), lambda bi: (bi, wq // LANES + 1)),
                  pl.BlockSpec((1, lc, LANES), lambda bi: (bi, 0, 0)),
                  pl.BlockSpec((1, lc, LANES), lambda bi: (bi, 0, 0)),
                  _const_spec(xexp.shape), _const_spec(bias.shape), _const_spec(sink_col.shape)],
        out_specs=pl.BlockSpec((t, wq), lambda bi: (bi, 0)),
        out_shape=jax.ShapeDtypeStruct((n, wq), BF16),
        compiler_params=_params(("arbitrary",)),
        name="sample_odd",
    )(p, p, p, cache_k, cache_v, xexp, bias, sink_col)


def _t5_bucket(rel_mem):
    nb = T5_BUCKETS // 2
    max_exact = nb // 2
    n = jnp.abs(rel_mem)
    large = max_exact + (jnp.log(jnp.maximum(n, 1).astype(F32) / max_exact)
                         / math.log(T5_MAX_DIST / max_exact) * (nb - max_exact)).astype(jnp.int32)
    large = jnp.minimum(large, nb - 1)
    return jnp.where(rel_mem > 0, nb, 0) + jnp.where(n < max_exact, n, large)


def _bias_b_of_rel(table, rel):
    idx = np.clip(rel, -B_REL_CLIP, B_REL_CLIP) + B_REL_CLIP
    return table.astype(F32)[idx].T


def _bias_c_of_rel(table, rel):
    return table.astype(F32)[_t5_bucket(-jnp.asarray(rel, jnp.int32))].T


def _bias_tiles_kernel(f_ref, o_ref, *, tq, left, offsets, left_chunks, valid_cols):
    n = f_ref.shape[2]
    win = o_ref.shape[3]
    rows = jnp.broadcast_to(f_ref[0], (tq, n))
    wide = pltpu.roll(rows, n - (tq - 1), 1, stride=1, stride_axis=0)
    q = lax.broadcasted_iota(jnp.int32, (tq, win), 0)
    k = lax.broadcasted_iota(jnp.int32, (tq, win), 1)
    shift = CHUNK.bit_length() - 1
    for v, off in enumerate(offsets):
        if valid_cols is None:
            d = off // CHUNK + jnp.right_shift(q, shift) - jnp.right_shift(k, shift)
            ok = (d >= 0) & (d <= left_chunks)
        else:
            ok = k < valid_cols
        o_ref[v, 0] = jnp.where(ok, wide[:, left - off:left - off + win], NEG_INF)


def _bias_tiles(bias_of_rel, *, tq, left, win, offsets, left_chunks=None, valid_cols=None):
    assert CHUNK & (CHUNK - 1) == 0 and all((left - o) % LANES == 0 for o in offsets)
    ncols = win + left - min(offsets)
    n = tq + ncols - 1
    n_pad = -(-n // LANES) * LANES
    f_rev = bias_of_rel(tq - 1 + left - np.arange(n))
    nheads = f_rev.shape[0]
    f_rev = jnp.pad(f_rev, ((0, 0), (0, n_pad - n)))[:, None, :]
    return pl.pallas_call(
        functools.partial(_bias_tiles_kernel, tq=tq, left=left, offsets=tuple(offsets),
                          left_chunks=left_chunks, valid_cols=valid_cols),
        grid=(nheads,),
        in_specs=[pl.BlockSpec((1, 1, n_pad), lambda h: (h, 0, 0))],
        out_specs=pl.BlockSpec((len(offsets), 1, tq, win), lambda h: (0, h, 0, 0)),
        out_shape=jax.ShapeDtypeStruct((len(offsets), nheads, tq, win), F32),
        compiler_params=_params(("arbitrary",)),
        name="bias_tiles",
    )(f_rev)


def _sample_bias(bias_of_rel, t, cache_len):
    ncols = cache_len + LANES
    tiles = _bias_tiles(bias_of_rel, tq=t, left=cache_len, win=ncols, offsets=[cache_len],
                        valid_cols=cache_len + t)
    return tiles.reshape(-1, ncols)


def _placement():
    pq = np.zeros((LANES, H_A * LANES), np.float32)
    pk = np.zeros((LANES, H_A * LANES), np.float32)
    for h in range(H_A):
        for j in range(3):
            pq[8 * j + h, LANES * h + HEAD_DIM + j] = 1.0
            pq[24, LANES * h + HEAD_DIM + 3 + j] = 1.0
            pk[24, LANES * h + HEAD_DIM + j] = 1.0
            pk[8 * j + h, LANES * h + HEAD_DIM + 3 + j] = -1.0
    return jnp.asarray(pq, BF16), jnp.asarray(pk, BF16)


def kernel(x_prompt, x_sample, cache_a_k, cache_a_v, cache_a_logf, cache_b_k, cache_b_v, cache_c_k, cache_c_v,
           norm_mix, norm_ffn, norm_final, w_in_even, b_forget, rel_bias_b, w_out_even, w_in_odd, sinks_c,
           w_out_odd, t5_bias, w_ffn_in, w_ffn_out):
    b, s, d = x_prompt.shape
    nb, t, _ = x_sample.shape
    past = cache_a_k.shape[2]
    n_p, n_s = b * s, nb * t

    w_even = w_in_even[0]
    w_main = w_even[:, :3 * W_A + 3 * W_B].astype(BF16)
    wf = w_even[:, 3 * W_A + 3 * W_B:]
    w_f = jnp.concatenate([wf, wf, wf, jnp.zeros((d, LANES - 3 * H_A), F32)], axis=1).astype(BF16)
    bf = b_forget[0].astype(F32)
    b_f = jnp.concatenate([bf, bf, bf, jnp.zeros((LANES - 3 * H_A,), F32)])[None, :]
    pq, pk = _placement()
    w_oe = w_out_even[0].astype(BF16)
    w_oo = w_out_odd[0].astype(BF16)
    w_odd = w_in_odd[0].astype(BF16)
    w_fi = w_ffn_in.astype(BF16)
    w_fo = w_ffn_out.astype(BF16)
    g_mix = norm_mix.astype(F32)[:, None, :]
    g_ffn = norm_ffn.astype(F32)[:, None, :]
    g_fin = norm_final.astype(F32)[None, :]

    bias_b_of = functools.partial(_bias_b_of_rel, rel_bias_b[0])
    bias_c_of = functools.partial(_bias_c_of_rel, t5_bias)
    log2_domain = lambda bias_of: (lambda rel: bias_of(rel) * LOG2E)
    variant_offsets = lambda tsub, left: [min(v * tsub, left) for v in range(left // tsub + 1)]
    tq_b, tsub_b, left_b = 256, 128, B_LEFT_CHUNKS * CHUNK
    bias_b = _bias_tiles(log2_domain(bias_b_of), tq=tsub_b, left=left_b, win=tsub_b + left_b,
                         offsets=variant_offsets(tsub_b, left_b), left_chunks=B_LEFT_CHUNKS)
    tq_c, tsub_c, left_c = 256, 128, C_LEFT_CHUNKS * CHUNK
    bias_c = _bias_tiles(log2_domain(bias_c_of), tq=tsub_c, left=left_c, win=tsub_c + left_c,
                         offsets=variant_offsets(tsub_c, left_c), left_chunks=C_LEFT_CHUNKS)

    xp = x_prompt
    w_all = jnp.concatenate([w_main[:, 3 * W_A:3 * W_A + W_B], w_f, w_main[:, :3 * W_A],
                             w_main[:, 3 * W_A + W_B:]], axis=1)
    qaug, kaug, vaug, ka, va, logf, qb, kb, vb, kbt, vbt = _even_proj(
        xp, g_mix[0], w_all, b_f, pq, pk, tm=512)
    oa = _fox(qaug, kaug, vaug, blk=512, nsub=2)
    ob = _band(qb, kb, vb, bias_b, None, tq=tq_b, left=left_b, npairs=H_B // 2, shared_kv=False,
               name="band_b")
    xp1 = _out_ffn(xp.reshape(n_p, d),
                   [(oa.reshape(n_p, W_A), w_oe[:W_A]), (ob.reshape(n_p, W_B), w_oe[W_A:])],
                   g_ffn[0], w_fi, w_fo, g_fin, tm=512, layer=0, final_norm=False)

    qc, kcd, vcd, kct, vct = _odd_proj(xp1.reshape(b, s, d), g_mix[1], w_odd, tm=512)
    oc = _band(qc, kcd, vcd, bias_c, sinks_c[0].astype(F32) * LOG2E, tq=tq_c, left=left_c, npairs=G_C // 2,
               shared_kv=True, name="band_c")
    y_prompt = _out_ffn(xp1, [(oc.reshape(n_p, H_C * HEAD_DIM), w_oo)],
                        g_ffn[1], w_fi, w_fo, g_fin, tm=512, layer=1, final_norm=True)

    xs = x_sample.reshape(n_s, d)
    ps, logf_sp = _sample_proj(xs, g_mix[0], w_main, tm=256, w_f=w_f, b_f=b_f)
    logf_s = logf_sp[:, :H_A]
    kpad = LANES * -(-(past + t) // LANES)
    logf_all = jnp.concatenate([cache_a_logf[0].astype(F32), logf_s.reshape(nb, t, H_A)], axis=1)
    logf_t = jnp.pad(jnp.swapaxes(logf_all, 1, 2), ((0, 0), (0, 0), (0, kpad - past - t)))
    lbs = cache_b_k.shape[2]
    bias_sb = _sample_bias(bias_b_of, t, lbs)
    e3 = np.zeros((H_A * t, LANES), np.float32)
    for j in range(3):
        e3[np.arange(H_A * t), 8 * j + np.arange(H_A * t) // t] = 1.0
    keys_minor = lambda cache: jnp.transpose(cache[0], (0, 2, 3, 1))
    oa_s, ob_s = _sample_even(ps, keys_minor(cache_a_k), keys_minor(cache_a_v), logf_t,
                              keys_minor(cache_b_k), keys_minor(cache_b_v),
                              jnp.asarray(e3, BF16), bias_sb, t=t, kc=2048)
    xs1 = _out_ffn(xs, [(oa_s, w_oe[:W_A]), (ob_s, w_oe[W_A:])], g_ffn[0], w_fi, w_fo, g_fin,
                   tm=256, layer=0, final_norm=False)

    ps2 = _sample_proj(xs1, g_mix[1], w_odd, tm=256)
    lcs = cache_c_k.shape[2]
    bias_sc = _sample_bias(bias_c_of, t, lcs)
    sink_col = jnp.repeat(sinks_c[0].astype(F32), t)[:, None]
    lane_head = np.arange(H_C * HEAD_DIM) // HEAD_DIM
    src_lane = (lane_head // G_C) * HEAD_DIM + np.arange(H_C * HEAD_DIM) % HEAD_DIM
    xexp = jnp.asarray(np.arange(LANES)[:, None] == src_lane[None, :], BF16)
    oc_s = _sample_odd(ps2, cache_c_k[0].reshape(nb, lcs, LANES), cache_c_v[0].reshape(nb, lcs, LANES),
                       xexp, bias_sc, sink_col, t=t)
    y_sample = _out_ffn(xs1, [(oc_s, w_oo)], g_ffn[1], w_fi, w_fo, g_fin, tm=256, layer=1, final_norm=True)

    wq = H_C * HEAD_DIM
    hd = lambda a, lead, h: a.reshape((1,) + lead + (h, HEAD_DIM))
    return (
        y_prompt.reshape(b, s, d), y_sample.reshape(nb, t, d),
        hd(ka, (b, s), H_A), hd(va, (b, s), H_A), logf[None],
        hd(kbt, (b, kbt.shape[1]), H_B), hd(vbt, (b, vbt.shape[1]), H_B),
        hd(kct, (b, kct.shape[1]), HKV_C), hd(vct, (b, vct.shape[1]), HKV_C),
        hd(ps[:, W_A:2 * W_A], (nb, t), H_A), hd(ps[:, 2 * W_A:3 * W_A], (nb, t), H_A),
        logf_s.reshape(1, nb, t, H_A),
        hd(ps[:, 3 * W_A + W_B:3 * W_A + 2 * W_B], (nb, t), H_B),
        hd(ps[:, 3 * W_A + 2 * W_B:3 * W_A + 3 * W_B], (nb, t), H_B),
        hd(ps2[:, wq:wq + LANES], (nb, t), HKV_C), hd(ps2[:, wq + LANES:wq + 2 * LANES], (nb, t), HKV_C),
    )
```

```python
import functools
import math

import jax
import jax.numpy as jnp
import numpy as np
from jax import lax
from jax.experimental import pallas as pl
from jax.experimental.pallas import tpu as pltpu

D_MODEL = 1024
HEAD_DIM = 64
CHUNK = 64
H_A = 8
H_B = 8
B_LEFT_CHUNKS = 8
B_REL_CLIP = 128
H_C = 16
HKV_C = 2
G_C = H_C // HKV_C
WINDOW = 128
C_LEFT_CHUNKS = WINDOW // CHUNK
T5_BUCKETS = 32
T5_MAX_DIST = 128
EPS = 1e-6
W_A = H_A * HEAD_DIM
W_B = H_B * HEAD_DIM
SCALE = HEAD_DIM ** -0.5
LOG2E = math.log2(math.e)

LANES = 128
VMEM_LIMIT = 60 * 1024 * 1024

F32 = jnp.float32
BF16 = jnp.bfloat16
NEG_INF = float("-inf")

_NT = (((1,), (1,)), ((), ()))


def _dot(a, b):
    return jnp.dot(a, b, preferred_element_type=F32)


def _dot_nt(a, b):
    return lax.dot_general(a, b, _NT, preferred_element_type=F32)


def _rmsnorm(x, g):
    ms = jnp.mean(x * x, axis=-1, keepdims=True)
    return x * lax.rsqrt(ms + EPS) * g


def _log_sigmoid(x):
    return jnp.minimum(x, 0.0) - jnp.log1p(jnp.exp(-jnp.abs(x)))


def _split3(x):
    hi = x.astype(BF16).astype(F32)
    r1 = x - hi
    mid = r1.astype(BF16).astype(F32)
    lo = (r1 - mid).astype(BF16).astype(F32)
    return hi, mid, lo


def _cumsum_rows(x):
    n = x.shape[0]
    row = lax.broadcasted_iota(jnp.int32, x.shape, 0)
    s = 1
    while s < n:
        x = x + jnp.where(row >= s, pltpu.roll(x, s, 0), 0.0)
        s *= 2
    return x


def _cumsum_lanes(x):
    n = x.shape[1]
    col = lax.broadcasted_iota(jnp.int32, x.shape, 1)
    s = 1
    while s < n:
        x = x + jnp.where(col >= s, pltpu.roll(x, s, 1), 0.0)
        s *= 2
    return x


def _staggered(items, stage_a, stage_b, stage_c):
    n = len(items)
    a_out, b_out = {}, {}
    for step in range(n + 2):
        if step < n:
            a_out[step] = stage_a(items[step])
        if 0 <= step - 1 < n:
            b_out[step - 1] = stage_b(items[step - 1], a_out.pop(step - 1))
        if 0 <= step - 2 < n:
            stage_c(items[step - 2], b_out.pop(step - 2))


def _const_spec(shape):
    nd = len(shape)
    return pl.BlockSpec(shape, lambda *_: (0,) * nd, pipeline_mode=pl.Buffered(1))


def _params(sem):
    return pltpu.CompilerParams(dimension_semantics=sem, vmem_limit_bytes=VMEM_LIMIT)


def _even_proj_kernel(x_ref, g_ref, w_ref, bf_ref, pq_ref, pk_ref,
                      qaug_ref, kaug_ref, vaug_ref, ka_ref, va_ref, logf_ref,
                      qb_ref, kb_ref, vb_ref, kbt_ref, vbt_ref, carry_ref):
    @pl.when(pl.program_id(1) == 0)
    def _():
        carry_ref[...] = jnp.zeros_like(carry_ref)

    h = _rmsnorm(x_ref[0], g_ref[...]).astype(BF16)
    tm = h.shape[0]
    lane = lax.broadcasted_iota(jnp.int32, (tm, LANES), 1)
    low = lane < HEAD_DIM
    first = W_B + LANES

    def chunk(n):
        return _dot(h, w_ref[:, first + W_A * n:first + W_A * (n + 1)])

    pf = _dot(h, w_ref[:, :first])
    qb_ref[0] = (pf[:, :W_B] * (SCALE * LOG2E)).astype(BF16)
    fa = pf[:, W_B:]
    pc = chunk(3)
    kb_ref[0] = pc.astype(BF16)
    kbt_ref[0] = pc
    pc = chunk(4)
    vb_ref[0] = pc.astype(BF16)
    vbt_ref[0] = pc
    logf = _log_sigmoid(fa + bf_ref[...])
    logf_ref[0] = logf[:, :H_A]
    c = _cumsum_rows(logf) + carry_ref[...]
    carry_ref[...] = c[tm - 1:tm, :]
    hi, mid, lo = _split3(c * LOG2E)
    a3 = jnp.where(lane < 8, hi, jnp.where(lane < 16, mid, jnp.where(lane < 24, lo,
                   jnp.where(lane == 24, 1.0, 0.0)))).astype(BF16)
    augq = _dot(a3, pq_ref[...])
    augk = _dot(a3, pk_ref[...])
    vone = jnp.where(lane == HEAD_DIM, 1.0, 0.0)

    def per_head(dst_ref, pc, spare):
        for j in range(H_A // 2):
            pair = pc[:, LANES * j:LANES * (j + 1)]
            for hh, val in enumerate((pair, pltpu.roll(pair, HEAD_DIM, 1))):
                o = slice(LANES * (2 * j + hh), LANES * (2 * j + hh + 1))
                dst_ref[0, :, o] = jnp.where(low, val, vone if spare is None else spare[:, o]).astype(BF16)

    pc = chunk(0)
    per_head(qaug_ref, pc * (SCALE * LOG2E), augq)
    pc = chunk(1)
    ka_ref[0] = pc
    per_head(kaug_ref, pc, augk)
    pc = chunk(2)
    va_ref[0] = pc
    per_head(vaug_ref, pc, None)


def _even_proj(x, g, w_all, b_f, pq, pk, tm):
    b, s, d = x.shape
    nt = s // tm
    lb = min(B_LEFT_CHUNKS * CHUNK, s)
    assert lb == tm, "band-state tail must be exactly one row tile"
    row = lambda w: pl.BlockSpec((1, tm, w), lambda bi, ti: (bi, ti, 0))
    tail = pl.BlockSpec((1, lb, W_B), lambda bi, ti: (bi, 0, 0))
    outs = (
        jax.ShapeDtypeStruct((b, s, H_A * LANES), BF16),
        jax.ShapeDtypeStruct((b, s, H_A * LANES), BF16),
        jax.ShapeDtypeStruct((b, s, H_A * LANES), BF16),
        jax.ShapeDtypeStruct((b, s, W_A), F32),
        jax.ShapeDtypeStruct((b, s, W_A), F32),
        jax.ShapeDtypeStruct((b, s, H_A), F32),
        jax.ShapeDtypeStruct((b, s, W_B), BF16),
        jax.ShapeDtypeStruct((b, s, W_B), BF16),
        jax.ShapeDtypeStruct((b, s, W_B), BF16),
        jax.ShapeDtypeStruct((b, lb, W_B), F32),
        jax.ShapeDtypeStruct((b, lb, W_B), F32),
    )
    return pl.pallas_call(
        _even_proj_kernel,
        grid=(b, nt),
        in_specs=[row(d), _const_spec((1, d)), _const_spec(w_all.shape),
                  _const_spec((1, LANES)), _const_spec(pq.shape), _const_spec(pk.shape)],
        out_specs=(row(H_A * LANES), row(H_A * LANES), row(H_A * LANES), row(W_A), row(W_A), row(H_A),
                   row(W_B), row(W_B), row(W_B), tail, tail),
        out_shape=outs,
        scratch_shapes=[pltpu.VMEM((1, LANES), F32)],
        compiler_params=_params(("arbitrary", "arbitrary")),
        name="even_proj",
    )(x, g, w_all, b_f, pq, pk)


def _fox_kernel(q_ref, k_ref, v_ref, o_ref, m_sc, acc_sc, *, blk, nsub):
    i = pl.program_id(2)
    nh = 2
    chains = [(sub, hh) for sub in range(nsub) for hh in range(nh)]
    qs = {(sub, hh): q_ref[0, sub * blk:(sub + 1) * blk, LANES * hh:LANES * (hh + 1)] for sub, hh in chains}
    r = lax.broadcasted_iota(jnp.int32, (blk, blk), 0)
    c = lax.broadcasted_iota(jnp.int32, (blk, blk), 1)
    causal = c <= r
    m_sc[...] = jnp.full_like(m_sc, NEG_INF)
    acc_sc[...] = jnp.zeros_like(acc_sc)

    def blocks(specs):
        items = [(pl.multiple_of(j * blk, blk), ch, modes[ch[0]])
                 for j, modes in specs for ch in chains if modes[ch[0]] is not None]

        def stage_scores(item):
            start, ch, mode = item
            k = k_ref[0, pl.ds(start, blk), LANES * ch[1]:LANES * (ch[1] + 1)]
            s = _dot_nt(qs[ch], k)
            if mode == "masked":
                s = jnp.where(causal, s, NEG_INF)
            m = m_sc[chains.index(ch)]
            m_new = jnp.maximum(m, jnp.max(s, axis=1, keepdims=True))
            m_sc[chains.index(ch)] = m_new
            return s, m, m_new

        def stage_exp(item, sm):
            s, m, m_new = sm
            p = jnp.concatenate([jnp.exp2(s[:, LANES * cb:LANES * (cb + 1)] - m_new)
                                 for cb in range(blk // LANES)], axis=1)
            return p.astype(BF16), jnp.exp2(m - m_new)

        def stage_pv(item, pa):
            start, ch, _ = item
            p, alpha = pa
            v = v_ref[0, pl.ds(start, blk), LANES * ch[1]:LANES * (ch[1] + 1)]
            n = chains.index(ch)
            acc_sc[n] = alpha * acc_sc[n] + _dot(p, v)

        _staggered(items, stage_scores, stage_exp, stage_pv)

    @pl.loop(0, i)
    def _(jj):
        blocks([(nsub * jj + d, ["full"] * nsub) for d in range(nsub)])

    blocks([(nsub * i + d, [None if sub < d else ("masked" if sub == d else "full") for sub in range(nsub)])
            for d in range(nsub)])

    lane = lax.broadcasted_iota(jnp.int32, (blk, LANES), 1)
    for sub in range(nsub):
        a0 = acc_sc[chains.index((sub, 0))]
        a1 = acc_sc[chains.index((sub, 1))]
        o0 = a0 / a0[:, HEAD_DIM:HEAD_DIM + 1]
        o1 = a1 / a1[:, HEAD_DIM:HEAD_DIM + 1]
        o_ref[0, sub * blk:(sub + 1) * blk, :] = jnp.where(
            lane < HEAD_DIM, o0, pltpu.roll(o1, HEAD_DIM, 1)).astype(BF16)


def _fox(qaug, kaug, vaug, blk, nsub):
    b, s, _ = qaug.shape
    pairs = H_A // 2
    tq = blk * nsub
    return pl.pallas_call(
        functools.partial(_fox_kernel, blk=blk, nsub=nsub),
        grid=(b, pairs, s // tq),
        in_specs=[pl.BlockSpec((1, tq, 2 * LANES), lambda bi, hp, i: (bi, i, hp)),
                  pl.BlockSpec((1, s, 2 * LANES), lambda bi, hp, i: (bi, 0, hp)),
                  pl.BlockSpec((1, s, 2 * LANES), lambda bi, hp, i: (bi, 0, hp))],
        out_specs=pl.BlockSpec((1, tq, LANES), lambda bi, hp, i: (bi, i, hp)),
        out_shape=jax.ShapeDtypeStruct((b, s, W_A), BF16),
        scratch_shapes=[pltpu.VMEM((2 * nsub, blk, LANES), F32), pltpu.VMEM((2 * nsub, blk, LANES), F32)],
        compiler_params=_params(("arbitrary", "arbitrary", "arbitrary")),
        name="fox",
    )(qaug, kaug, vaug)


def _band_kernel(*refs, tq, left, npairs, heads_per_group, use_sinks, pairs_per_kv):
    mxu_rowsum = not use_sinks
    if use_sinks:
        sink_ref, q_ref, k_ref, v_ref, bias_ref, o_ref = refs
    else:
        q_ref, k_ref, v_ref, bias_ref, o_ref = refs
    g = pl.program_id(1)
    i = pl.program_id(2)
    nvar, _, tsub, win = bias_ref.shape
    nsq = tq // tsub
    lane = lax.broadcasted_iota(jnp.int32, (tsub, LANES), 1)
    low = lane < HEAD_DIM
    items = [(sq, pr, hh) for sq in range(nsq) for pr in range(npairs) for hh in range(2)]

    def window(sq):
        gsub = i * nsq + sq
        return pl.multiple_of(jnp.maximum(gsub * tsub - left, 0), LANES), jnp.minimum(gsub, nvar - 1)

    def kv_lanes(pr):
        kl = LANES * (pr // pairs_per_kv)
        return slice(kl, kl + LANES)

    def sink(pr, hh):
        return sink_ref[g * heads_per_group + 2 * pr + hh]

    def stage_scores(item):
        sq, pr, hh = item
        start, var = window(sq)
        qp = q_ref[0, sq * tsub:(sq + 1) * tsub, LANES * pr:LANES * (pr + 1)]
        qm = jnp.where(low if hh == 0 else jnp.logical_not(low), qp, jnp.zeros_like(qp))
        s = _dot_nt(qm, k_ref[0, pl.ds(start, win), kv_lanes(pr)]) + bias_ref[var, 2 * pr + hh]
        m = jnp.max(s, axis=1, keepdims=True)
        return s, (jnp.maximum(m, sink(pr, hh)) if use_sinks else m)

    def stage_exp(item, sm):
        s, m = sm
        e = jnp.exp2(s - m)
        l = None if mxu_rowsum else jnp.sum(e, axis=1, keepdims=True)
        if use_sinks:
            l = l + jnp.exp2(sink(item[1], item[2]) - m)
        return e.astype(BF16), l

    outs = {}

    def stage_pv(item, pe):
        p, l = pe
        sq, pr, hh = item
        v = v_ref[0, pl.ds(window(sq)[0], win), kv_lanes(pr)]
        if mxu_rowsum:
            ov = _dot(p, jnp.concatenate([v, jnp.ones((win, LANES), BF16)], axis=1))
            outs[hh] = ov[:, :LANES] / ov[:, LANES:]
        else:
            outs[hh] = _dot(p, v) / l
        if hh == 1:
            o_ref[0, sq * tsub:(sq + 1) * tsub, LANES * pr:LANES * (pr + 1)] = jnp.where(
                low, outs[0], outs[1]).astype(BF16)

    _staggered(items, stage_scores, stage_exp, stage_pv)


def _band(q, k, v, bias, sinks, *, tq, left, npairs, pairs_per_kv, name):
    b, s, wq = q.shape
    tsub, win = bias.shape[-2:]
    wblk = npairs * LANES
    groups = wq // wblk
    hpg = 2 * npairs
    use_sinks = sinks is not None
    kern = functools.partial(_band_kernel, tq=tq, left=left, npairs=npairs,
                             heads_per_group=hpg, use_sinks=use_sinks, pairs_per_kv=pairs_per_kv)
    kvw = wblk // pairs_per_kv
    resident = lambda shape, imap: pl.BlockSpec(shape, imap, pipeline_mode=pl.Buffered(1))
    in_specs = [pl.BlockSpec((1, tq, wblk), lambda bi, g, i: (bi, i, g)),
                resident((1, s, kvw), lambda bi, g, i: (bi, 0, g)),
                resident((1, s, kvw), lambda bi, g, i: (bi, 0, g)),
                resident((bias.shape[0], hpg, tsub, win), lambda bi, g, i: (0, g, 0, 0))]
    args = [q, k, v, bias]
    if use_sinks:
        in_specs = [pl.BlockSpec(memory_space=pltpu.SMEM)] + in_specs
        args = [sinks] + args
    return pl.pallas_call(
        kern,
        grid=(b, groups, s // tq),
        in_specs=in_specs,
        out_specs=pl.BlockSpec((1, tq, wblk), lambda bi, g, i: (bi, i, g)),
        out_shape=jax.ShapeDtypeStruct((b, s, wq), BF16),
        compiler_params=_params(("arbitrary", "arbitrary", "arbitrary")),
        name=name,
    )(*args)


def _out_ffn_kernel(*refs, n_attn, d_ff, final_norm):
    x_ref = refs[0]
    attn = refs[1:1 + 2 * n_attn]
    gffn_ref, win_ref, wout_ref, gfin_ref, o_ref = refs[1 + 2 * n_attn:]
    y = x_ref[...]
    for a in range(n_attn):
        y = y + _dot(attn[2 * a][...], attn[2 * a + 1][...])
    h = _rmsnorm(y, gffn_ref[...]).astype(BF16)
    gu = _dot(h, win_ref[...])
    gate = gu[:, :d_ff]
    up = gu[:, d_ff:]
    act = (gate * (1.0 / (1.0 + jnp.exp(-gate))) * up).astype(BF16)
    y = y + _dot(act, wout_ref[...])
    o_ref[...] = _rmsnorm(y, gfin_ref[...]) if final_norm else y


def _out_ffn(x, attn_pairs, g_ffn, w_in, w_out, g_fin, tm, layer, final_norm):
    n, d = x.shape
    d_ff = w_out.shape[1]
    row = lambda w: pl.BlockSpec((tm, w), lambda i: (i, 0))
    slab = lambda w: pl.BlockSpec((None,) + w.shape[1:], lambda i: (layer, 0, 0), pipeline_mode=pl.Buffered(1))
    args, specs = [x], [row(d)]
    for o, w in attn_pairs:
        args += [o, w]
        specs += [row(o.shape[1]), _const_spec(w.shape)]
    args += [g_ffn, w_in, w_out, g_fin]
    specs += [_const_spec((1, d)), slab(w_in), slab(w_out), _const_spec((1, d))]
    return pl.pallas_call(
        functools.partial(_out_ffn_kernel, n_attn=len(attn_pairs), d_ff=d_ff, final_norm=final_norm),
        grid=(n // tm,),
        in_specs=specs,
        out_specs=row(d),
        out_shape=jax.ShapeDtypeStruct((n, d), F32),
        compiler_params=_params(("arbitrary",)),
        name="out_ffn",
    )(*args)


def _odd_proj_kernel(x_ref, g_ref, w_ref, q_ref, k_ref, v_ref, kt_ref, vt_ref, *, tiles_per_batch, tail):
    t = pl.program_id(1)
    h = _rmsnorm(x_ref[0], g_ref[...]).astype(BF16)
    p = _dot(h, w_ref[...])
    wq = H_C * HEAD_DIM
    q_ref[0] = (p[:, :wq] * (SCALE * LOG2E)).astype(BF16)
    tm = p.shape[0]
    lane = lax.broadcasted_iota(jnp.int32, (tm, LANES), 1)
    low = lane < HEAD_DIM
    for src, dst in ((p[:, wq:wq + LANES], k_ref), (p[:, wq + LANES:wq + 2 * LANES], v_ref)):
        rolled = pltpu.roll(src, HEAD_DIM, 1)
        dst[0, :, :LANES] = jnp.where(low, src, rolled).astype(BF16)
        dst[0, :, LANES:] = jnp.where(low, rolled, src).astype(BF16)

    @pl.when(t == tiles_per_batch - 1)
    def _():
        kt_ref[0] = p[tm - tail:, wq:wq + LANES]
        vt_ref[0] = p[tm - tail:, wq + LANES:wq + 2 * LANES]


def _odd_proj(x, g, w, tm):
    b, s, d = x.shape
    nt = s // tm
    lc = min(C_LEFT_CHUNKS * CHUNK, s)
    assert lc <= tm
    wq = H_C * HEAD_DIM
    row = lambda w_: pl.BlockSpec((1, tm, w_), lambda bi, ti: (bi, ti, 0))
    tail = pl.BlockSpec((1, lc, LANES), lambda bi, ti: (bi, 0, 0))
    return pl.pallas_call(
        functools.partial(_odd_proj_kernel, tiles_per_batch=nt, tail=lc),
        grid=(b, nt),
        in_specs=[row(d), _const_spec((1, d)), _const_spec(w.shape)],
        out_specs=(row(wq), row(2 * LANES), row(2 * LANES), tail, tail),
        out_shape=(jax.ShapeDtypeStruct((b, s, wq), BF16),
                   jax.ShapeDtypeStruct((b, s, 2 * LANES), BF16),
                   jax.ShapeDtypeStruct((b, s, 2 * LANES), BF16),
                   jax.ShapeDtypeStruct((b, lc, LANES), F32),
                   jax.ShapeDtypeStruct((b, lc, LANES), F32)),
        compiler_params=_params(("arbitrary", "arbitrary")),
        name="odd_proj",
    )(x, g, w)


def _sample_proj_kernel(*refs, gated):
    if gated:
        x_ref, g_ref, w_ref, wf_ref, bf_ref, p_ref, logf_ref = refs
    else:
        x_ref, g_ref, w_ref, p_ref = refs
    h = _rmsnorm(x_ref[...], g_ref[...]).astype(BF16)
    p_ref[...] = _dot(h, w_ref[...])
    if gated:
        logf_ref[...] = _log_sigmoid(_dot(h, wf_ref[...]) + bf_ref[...])


def _sample_proj(x, g, w, tm, w_f=None, b_f=None):
    n, d = x.shape
    row = lambda w_: pl.BlockSpec((tm, w_), lambda i: (i, 0))
    in_specs = [row(d), _const_spec((1, d)), _const_spec(w.shape)]
    args = [x, g, w]
    out_specs = row(w.shape[1])
    out_shape = jax.ShapeDtypeStruct((n, w.shape[1]), F32)
    if w_f is not None:
        in_specs += [_const_spec(w_f.shape), _const_spec((1, LANES))]
        args += [w_f, b_f]
        out_specs = (out_specs, row(LANES))
        out_shape = (out_shape, jax.ShapeDtypeStruct((n, LANES), F32))
    return pl.pallas_call(
        functools.partial(_sample_proj_kernel, gated=w_f is not None),
        grid=(n // tm,),
        in_specs=in_specs,
        out_specs=out_specs,
        out_shape=out_shape,
        compiler_params=_params(("arbitrary",)),
        name="sample_proj",
    )(*args)


def _block_diag_q(q, nheads, scale):
    t, w = q.shape
    tiled = jnp.concatenate([q] * nheads, axis=0)
    r = lax.broadcasted_iota(jnp.int32, (nheads * t, w), 0)
    c = lax.broadcasted_iota(jnp.int32, (nheads * t, w), 1)
    return jnp.where(r // t == c // HEAD_DIM, tiled * scale, 0.0).astype(BF16)


def _block_diag_extract(o_all, nheads, t):
    w = o_all.shape[1]
    c = lax.broadcasted_iota(jnp.int32, (t, w), 1)
    out = jnp.zeros((t, w), F32)
    for h in range(nheads):
        out = jnp.where(c // HEAD_DIM == h, o_all[h * t:(h + 1) * t, :], out)
    return out


def _heads_major(ref):
    _, nh, hd, nk = ref.shape
    return ref[0].reshape(nh * hd, nk).astype(BF16)


def _pad_rows(x, rows):
    return jnp.concatenate([x, jnp.zeros((rows - x.shape[0], x.shape[1]), x.dtype)], axis=0)


def _sample_even_kernel(qa_ref, kan_ref, van_ref, qb_ref, kbn_ref, vbn_ref, ck_ref, cv_ref, lft_ref,
                        cbk_ref, cbv_ref, e3_ref, biasb_ref, oa_ref, ob_ref,
                        bias_sc, m_sc, acc_sc, *, t, past, kc, nkc):
    c = pl.program_id(1)
    rows = H_A * t
    qbd = _block_diag_q(qa_ref[...], H_A, SCALE)

    @pl.when(c == 0)
    def _():
        cum = _cumsum_lanes(lft_ref[0])
        last = past + t - 1
        suffix = cum[:, last:last + 1] - cum
        hi, mid, lo = _split3(suffix)
        s3 = jnp.concatenate([hi, mid, lo, jnp.zeros((LANES - 3 * H_A, suffix.shape[1]), F32)], axis=0)
        s3 = s3.astype(BF16)
        for ch in range(nkc + 1):
            w = kc if ch < nkc else LANES
            bias_sc[ch, :, :w] = _dot(e3_ref[...], s3[:, ch * kc:ch * kc + w])
        m_sc[...] = jnp.full_like(m_sc, NEG_INF)
        acc_sc[...] = jnp.zeros_like(acc_sc)

    def update(s, pv):
        m = m_sc[...]
        m_new = jnp.maximum(m, jnp.max(s, axis=1, keepdims=True))
        p = jnp.exp(s - m_new)
        alpha = jnp.exp(m - m_new)
        acc = acc_sc[...]
        lsum = alpha * acc[:, W_A:W_A + 1] + jnp.sum(p, axis=1, keepdims=True)
        o = alpha * acc[:, :W_A] + pv(p.astype(BF16))
        acc_sc[:, :W_A] = o
        acc_sc[:, W_A:] = jnp.broadcast_to(lsum, (rows, LANES))
        m_sc[...] = m_new

    s = _dot(qbd, _heads_major(ck_ref)) + bias_sc[c]
    vt = _heads_major(cv_ref)
    update(s, lambda p: _dot_nt(p, vt))

    @pl.when(c == nkc - 1)
    def _():
        kn = _pad_rows(kan_ref[...], LANES).astype(BF16)
        vn = _pad_rows(van_ref[...], LANES).astype(BF16)
        sn = _dot_nt(qbd, kn) + bias_sc[nkc, :, :LANES]
        r = lax.broadcasted_iota(jnp.int32, sn.shape, 0)
        col = lax.broadcasted_iota(jnp.int32, sn.shape, 1)
        sn = jnp.where(col <= r % t, sn, NEG_INF)
        update(sn, lambda p: _dot(p, vn))
        acc = acc_sc[...]
        o_all = acc[:, :W_A] / acc[:, W_A:W_A + 1]
        oa_ref[...] = _block_diag_extract(o_all, H_A, t).astype(BF16)

        qbd_b = _block_diag_q(qb_ref[...], H_B, SCALE)
        lbk = cbk_ref.shape[3]
        kbn = _pad_rows(kbn_ref[...], LANES).astype(BF16)
        vbn = _pad_rows(vbn_ref[...], LANES).astype(BF16)
        sb = jnp.concatenate([_dot(qbd_b, _heads_major(cbk_ref)), _dot_nt(qbd_b, kbn)], axis=1) + biasb_ref[...]
        mb = jnp.max(sb, axis=1, keepdims=True)
        eb = jnp.exp(sb - mb)
        lb = jnp.sum(eb, axis=1, keepdims=True)
        eb = eb.astype(BF16)
        ob_all = (_dot_nt(eb[:, :lbk], _heads_major(cbv_ref)) + _dot(eb[:, lbk:], vbn)) / lb
        ob_ref[...] = _block_diag_extract(ob_all, H_B, t).astype(BF16)


def _sample_even(p, cache_k, cache_v, logf_t, cache_bk, cache_bv, e3, bias_b, *, t, kc):
    n = p.shape[0]
    nb = n // t
    past = cache_k.shape[3]
    nkc = past // kc
    lbk = cache_bk.shape[3]
    pcol = lambda j: pl.BlockSpec((t, W_A), lambda bi, c: (bi, j))
    rows = H_A * t
    return pl.pallas_call(
        functools.partial(_sample_even_kernel, t=t, past=past, kc=kc, nkc=nkc),
        grid=(nb, nkc),
        in_specs=[pcol(0), pcol(1), pcol(2), pcol(3), pcol(4), pcol(5),
                  pl.BlockSpec((1, H_A, HEAD_DIM, kc), lambda bi, c: (bi, 0, 0, c)),
                  pl.BlockSpec((1, H_A, HEAD_DIM, kc), lambda bi, c: (bi, 0, 0, c)),
                  pl.BlockSpec((1, H_A, logf_t.shape[2]), lambda bi, c: (bi, 0, 0)),
                  pl.BlockSpec((1, H_B, HEAD_DIM, lbk), lambda bi, c: (bi, 0, 0, 0)),
                  pl.BlockSpec((1, H_B, HEAD_DIM, lbk), lambda bi, c: (bi, 0, 0, 0)),
                  _const_spec(e3.shape), _const_spec(bias_b.shape)],
        out_specs=(pl.BlockSpec((t, W_A), lambda bi, c: (bi, 0)),
                   pl.BlockSpec((t, W_B), lambda bi, c: (bi, 0))),
        out_shape=(jax.ShapeDtypeStruct((n, W_A), BF16), jax.ShapeDtypeStruct((n, W_B), BF16)),
        scratch_shapes=[pltpu.VMEM((nkc + 1, rows, kc), F32),
                        pltpu.VMEM((rows, 1), F32),
                        pltpu.VMEM((rows, W_A + LANES), F32)],
        compiler_params=_params(("arbitrary", "arbitrary")),
        name="sample_even",
    )(p, p, p, p, p, p, cache_k, cache_v, logf_t, cache_bk, cache_bv, e3, bias_b)


def _sample_odd_kernel(q_ref, kn_ref, vn_ref, ck_ref, cv_ref, x_ref, bias_ref, sink_ref, o_ref, *, t, nbs):
    sk = sink_ref[...]
    for bb in range(nbs):
        rows = slice(bb * t, (bb + 1) * t)
        qbd = _block_diag_q(q_ref[rows, :], H_C, SCALE)
        kall = jnp.concatenate([ck_ref[bb], _pad_rows(kn_ref[rows, :], LANES)], axis=0).astype(BF16)
        vall = jnp.concatenate([cv_ref[bb], _pad_rows(vn_ref[rows, :], LANES)], axis=0).astype(BF16)
        kexp = _dot(kall, x_ref[...]).astype(BF16)
        vexp = _dot(vall, x_ref[...]).astype(BF16)
        s = _dot_nt(qbd, kexp) + bias_ref[...]
        m = jnp.maximum(jnp.max(s, axis=1, keepdims=True), sk)
        e = jnp.exp(s - m)
        l = jnp.sum(e, axis=1, keepdims=True) + jnp.exp(sk - m)
        o_all = _dot(e.astype(BF16), vexp) / l
        o_ref[rows, :] = _block_diag_extract(o_all, H_C, t).astype(BF16)


def _sample_odd(p, cache_k, cache_v, xexp, bias, sink_col, *, t, nbs):
    n = p.shape[0]
    nb = n // t
    wq = H_C * HEAD_DIM
    lc = cache_k.shape[1]
    rows = t * nbs
    return pl.pallas_call(
        functools.partial(_sample_odd_kernel, t=t, nbs=nbs),
        grid=(nb // nbs,),
        in_specs=[pl.BlockSpec((rows, wq), lambda bi: (bi, 0)),
                  pl.BlockSpec((rows, LANES), lambda bi: (bi, wq // LANES)),
                  pl.BlockSpec((rows, LANES), lambda bi: (bi, wq // LANES + 1)),
                  pl.BlockSpec((nbs, lc, LANES), lambda bi: (bi, 0, 0)),
                  pl.BlockSpec((nbs, lc, LANES), lambda bi: (bi, 0, 0)),
                  _const_spec(xexp.shape), _const_spec(bias.shape), _const_spec(sink_col.shape)],
        out_specs=pl.BlockSpec((rows, wq), lambda bi: (bi, 0)),
        out_shape=jax.ShapeDtypeStruct((n, wq), BF16),
        compiler_params=_params(("arbitrary",)),
        name="sample_odd",
    )(p, p, p, cache_k, cache_v, xexp, bias, sink_col)


def _t5_bucket(rel_mem):
    nb = T5_BUCKETS // 2
    max_exact = nb // 2
    n = jnp.abs(rel_mem)
    large = max_exact + (jnp.log(jnp.maximum(n, 1).astype(F32) / max_exact)
                         / math.log(T5_MAX_DIST / max_exact) * (nb - max_exact)).astype(jnp.int32)
    large = jnp.minimum(large, nb - 1)
    return jnp.where(rel_mem > 0, nb, 0) + jnp.where(n < max_exact, n, large)


def _bias_b_of_rel(table, rel):
    idx = np.clip(rel, -B_REL_CLIP, B_REL_CLIP) + B_REL_CLIP
    return table.astype(F32)[idx].T


def _bias_c_of_rel(table, rel):
    return table.astype(F32)[_t5_bucket(-jnp.asarray(rel, jnp.int32))].T


def _bias_tiles_kernel(f_ref, o_ref, *, tq, left, offsets, left_chunks, valid_cols):
    n = f_ref.shape[2]
    win = o_ref.shape[3]
    rows = jnp.broadcast_to(f_ref[0], (tq, n))
    wide = pltpu.roll(rows, n - (tq - 1), 1, stride=1, stride_axis=0)
    q = lax.broadcasted_iota(jnp.int32, (tq, win), 0)
    k = lax.broadcasted_iota(jnp.int32, (tq, win), 1)
    shift = CHUNK.bit_length() - 1
    for v, off in enumerate(offsets):
        if valid_cols is None:
            d = off // CHUNK + jnp.right_shift(q, shift) - jnp.right_shift(k, shift)
            ok = (d >= 0) & (d <= left_chunks)
        else:
            ok = k < valid_cols
        o_ref[v, 0] = jnp.where(ok, wide[:, left - off:left - off + win], NEG_INF)


def _bias_tiles(bias_of_rel, *, tq, left, win, offsets, left_chunks=None, valid_cols=None):
    assert CHUNK & (CHUNK - 1) == 0 and all((left - o) % LANES == 0 for o in offsets)
    ncols = win + left - min(offsets)
    n = tq + ncols - 1
    n_pad = -(-n // LANES) * LANES
    f_rev = bias_of_rel(tq - 1 + left - np.arange(n))
    nheads = f_rev.shape[0]
    f_rev = jnp.pad(f_rev, ((0, 0), (0, n_pad - n)))[:, None, :]
    return pl.pallas_call(
        functools.partial(_bias_tiles_kernel, tq=tq, left=left, offsets=tuple(offsets),
                          left_chunks=left_chunks, valid_cols=valid_cols),
        grid=(nheads,),
        in_specs=[pl.BlockSpec((1, 1, n_pad), lambda h: (h, 0, 0))],
        out_specs=pl.BlockSpec((len(offsets), 1, tq, win), lambda h: (0, h, 0, 0)),
        out_shape=jax.ShapeDtypeStruct((len(offsets), nheads, tq, win), F32),
        compiler_params=_params(("arbitrary",)),
        name="bias_tiles",
    )(f_rev)


def _sample_bias(bias_of_rel, t, cache_len):
    ncols = cache_len + LANES
    tiles = _bias_tiles(bias_of_rel, tq=t, left=cache_len, win=ncols, offsets=[cache_len],
                        valid_cols=cache_len + t)
    return tiles.reshape(-1, ncols)


def _cast_kernel(x_ref, o_ref):
    o_ref[...] = x_ref[...].astype(o_ref.dtype)


def _cast_bf16(w, rows):
    nl, r, c = w.shape
    w2 = w.reshape(nl * r, c)
    out = pl.pallas_call(
        _cast_kernel,
        grid=(nl * r // rows,),
        in_specs=[pl.BlockSpec((rows, c), lambda i: (i, 0))],
        out_specs=pl.BlockSpec((rows, c), lambda i: (i, 0)),
        out_shape=jax.ShapeDtypeStruct(w2.shape, BF16),
        compiler_params=_params(("arbitrary",)),
        name="cast_bf16",
    )(w2)
    return out.reshape(nl, r, c)


def _placement():
    pq = np.zeros((LANES, H_A * LANES), np.float32)
    pk = np.zeros((LANES, H_A * LANES), np.float32)
    for h in range(H_A):
        for j in range(3):
            pq[8 * j + h, LANES * h + HEAD_DIM + j] = 1.0
            pq[24, LANES * h + HEAD_DIM + 3 + j] = 1.0
            pk[24, LANES * h + HEAD_DIM + j] = 1.0
            pk[8 * j + h, LANES * h + HEAD_DIM + 3 + j] = -1.0
    return jnp.asarray(pq, BF16), jnp.asarray(pk, BF16)


def kernel(x_prompt, x_sample, cache_a_k, cache_a_v, cache_a_logf, cache_b_k, cache_b_v, cache_c_k, cache_c_v,
           norm_mix, norm_ffn, norm_final, w_in_even, b_forget, rel_bias_b, w_out_even, w_in_odd, sinks_c,
           w_out_odd, t5_bias, w_ffn_in, w_ffn_out):
    b, s, d = x_prompt.shape
    nb, t, _ = x_sample.shape
    past = cache_a_k.shape[2]
    n_p, n_s = b * s, nb * t

    w_even = w_in_even[0]
    w_main = w_even[:, :3 * W_A + 3 * W_B].astype(BF16)
    wf = w_even[:, 3 * W_A + 3 * W_B:]
    w_f = jnp.concatenate([wf, wf, wf, jnp.zeros((d, LANES - 3 * H_A), F32)], axis=1).astype(BF16)
    bf = b_forget[0].astype(F32)
    b_f = jnp.concatenate([bf, bf, bf, jnp.zeros((LANES - 3 * H_A,), F32)])[None, :]
    pq, pk = _placement()
    w_oe = w_out_even[0].astype(BF16)
    w_oo = w_out_odd[0].astype(BF16)
    w_odd = w_in_odd[0].astype(BF16)
    w_fi = _cast_bf16(w_ffn_in, rows=256)
    w_fo = _cast_bf16(w_ffn_out, rows=704)
    g_mix = norm_mix.astype(F32)[:, None, :]
    g_ffn = norm_ffn.astype(F32)[:, None, :]
    g_fin = norm_final.astype(F32)[None, :]

    bias_b_of = functools.partial(_bias_b_of_rel, rel_bias_b[0])
    bias_c_of = functools.partial(_bias_c_of_rel, t5_bias)
    log2_domain = lambda bias_of: (lambda rel: bias_of(rel) * LOG2E)
    variant_offsets = lambda tsub, left: [min(v * tsub, left) for v in range(left // tsub + 1)]
    tq_b, tsub_b, left_b = 256, 256, B_LEFT_CHUNKS * CHUNK
    bias_b = _bias_tiles(log2_domain(bias_b_of), tq=tsub_b, left=left_b, win=tsub_b + left_b,
                         offsets=variant_offsets(tsub_b, left_b), left_chunks=B_LEFT_CHUNKS)
    tq_c, tsub_c, left_c = 256, 128, C_LEFT_CHUNKS * CHUNK
    bias_c = _bias_tiles(log2_domain(bias_c_of), tq=tsub_c, left=left_c, win=tsub_c + left_c,
                         offsets=variant_offsets(tsub_c, left_c), left_chunks=C_LEFT_CHUNKS)

    xp = x_prompt
    w_all = jnp.concatenate([w_main[:, 3 * W_A:3 * W_A + W_B], w_f, w_main[:, :3 * W_A],
                             w_main[:, 3 * W_A + W_B:]], axis=1)
    qaug, kaug, vaug, ka, va, logf, qb, kb, vb, kbt, vbt = _even_proj(
        xp, g_mix[0], w_all, b_f, pq, pk, tm=512)
    oa = _fox(qaug, kaug, vaug, blk=512, nsub=2)
    ob = _band(qb, kb, vb, bias_b, None, tq=tq_b, left=left_b, npairs=H_B // 2, pairs_per_kv=1,
               name="band_b")
    xp1 = _out_ffn(xp.reshape(n_p, d),
                   [(oa.reshape(n_p, W_A), w_oe[:W_A]), (ob.reshape(n_p, W_B), w_oe[W_A:])],
                   g_ffn[0], w_fi, w_fo, g_fin, tm=512, layer=0, final_norm=False)

    qc, kcd, vcd, kct, vct = _odd_proj(xp1.reshape(b, s, d), g_mix[1], w_odd, tm=512)
    oc = _band(qc, kcd, vcd, bias_c, sinks_c[0].astype(F32) * LOG2E, tq=tq_c, left=left_c, npairs=H_C // 2,
               pairs_per_kv=G_C // 2, name="band_c")
    y_prompt = _out_ffn(xp1, [(oc.reshape(n_p, H_C * HEAD_DIM), w_oo)],
                        g_ffn[1], w_fi, w_fo, g_fin, tm=512, layer=1, final_norm=True)

    xs = x_sample.reshape(n_s, d)
    ps, logf_sp = _sample_proj(xs, g_mix[0], w_main, tm=256, w_f=w_f, b_f=b_f)
    logf_s = logf_sp[:, :H_A]
    kpad = LANES * -(-(past + t) // LANES)
    logf_all = jnp.concatenate([cache_a_logf[0].astype(F32), logf_s.reshape(nb, t, H_A)], axis=1)
    logf_t = jnp.pad(jnp.swapaxes(logf_all, 1, 2), ((0, 0), (0, 0), (0, kpad - past - t)))
    lbs = cache_b_k.shape[2]
    bias_sb = _sample_bias(bias_b_of, t, lbs)
    e3 = np.zeros((H_A * t, LANES), np.float32)
    for j in range(3):
        e3[np.arange(H_A * t), 8 * j + np.arange(H_A * t) // t] = 1.0
    keys_minor = lambda cache: jnp.transpose(cache[0], (0, 2, 3, 1))
    oa_s, ob_s = _sample_even(ps, keys_minor(cache_a_k), keys_minor(cache_a_v), logf_t,
                              keys_minor(cache_b_k), keys_minor(cache_b_v),
                              jnp.asarray(e3, BF16), bias_sb, t=t, kc=2048)
    xs1 = _out_ffn(xs, [(oa_s, w_oe[:W_A]), (ob_s, w_oe[W_A:])], g_ffn[0], w_fi, w_fo, g_fin,
                   tm=256, layer=0, final_norm=False)

    ps2 = _sample_proj(xs1, g_mix[1], w_odd, tm=256)
    lcs = cache_c_k.shape[2]
    bias_sc = _sample_bias(bias_c_of, t, lcs)
    sink_col = jnp.repeat(sinks_c[0].astype(F32), t)[:, None]
    lane_head = np.arange(H_C * HEAD_DIM) // HEAD_DIM
    src_lane = (lane_head // G_C) * HEAD_DIM + np.arange(H_C * HEAD_DIM) % HEAD_DIM
    xexp = jnp.asarray(np.arange(LANES)[:, None] == src_lane[None, :], BF16)
    oc_s = _sample_odd(ps2, cache_c_k[0].reshape(nb, lcs, LANES), cache_c_v[0].reshape(nb, lcs, LANES),
                       xexp, bias_sc, sink_col, t=t, nbs=4)
    y_sample = _out_ffn(xs1, [(oc_s, w_oo)], g_ffn[1], w_fi, w_fo, g_fin, tm=256, layer=1, final_norm=True)

    wq = H_C * HEAD_DIM
    hd = lambda a, lead, h: a.reshape((1,) + lead + (h, HEAD_DIM))
    return (
        y_prompt.reshape(b, s, d), y_sample.reshape(nb, t, d),
        hd(ka, (b, s), H_A), hd(va, (b, s), H_A), logf[None],
        hd(kbt, (b, kbt.shape[1]), H_B), hd(vbt, (b, vbt.shape[1]), H_B),
        hd(kct, (b, kct.shape[1]), HKV_C), hd(vct, (b, vct.shape[1]), HKV_C),
        hd(ps[:, W_A:2 * W_A], (nb, t), H_A), hd(ps[:, 2 * W_A:3 * W_A], (nb, t), H_A),
        logf_s.reshape(1, nb, t, H_A),
        hd(ps[:, 3 * W_A + W_B:3 * W_A + 2 * W_B], (nb, t), H_B),
        hd(ps[:, 3 * W_A + 2 * W_B:3 * W_A + 3 * W_B], (nb, t), H_B),
        hd(ps2[:, wq:wq + LANES], (nb, t), HKV_C), hd(ps2[:, wq + LANES:wq + 2 * LANES], (nb, t), HKV_C),
    )
```

```python
import functools
import math

import jax
import jax.numpy as jnp
import numpy as np
from jax import lax
from jax.experimental import pallas as pl
from jax.experimental.pallas import tpu as pltpu

D_MODEL = 1024
HEAD_DIM = 64
CHUNK = 64
H_A = 8
H_B = 8
B_LEFT_CHUNKS = 8
B_REL_CLIP = 128
H_C = 16
HKV_C = 2
G_C = H_C // HKV_C
WINDOW = 128
C_LEFT_CHUNKS = WINDOW // CHUNK
T5_BUCKETS = 32
T5_MAX_DIST = 128
EPS = 1e-6
W_A = H_A * HEAD_DIM
W_B = H_B * HEAD_DIM
SCALE = HEAD_DIM ** -0.5
LOG2E = math.log2(math.e)

LANES = 128
VMEM_LIMIT = 60 * 1024 * 1024

F32 = jnp.float32
BF16 = jnp.bfloat16
NEG_INF = float("-inf")

_NT = (((1,), (1,)), ((), ()))


def _dot(a, b):
    return jnp.dot(a, b, preferred_element_type=F32)


def _dot_nt(a, b):
    return lax.dot_general(a, b, _NT, preferred_element_type=F32)


def _rmsnorm(x, g):
    ms = jnp.mean(x * x, axis=-1, keepdims=True)
    return x * lax.rsqrt(ms + EPS) * g


def _log_sigmoid(x):
    return jnp.minimum(x, 0.0) - jnp.log1p(jnp.exp(-jnp.abs(x)))


def _split3(x):
    hi = x.astype(BF16).astype(F32)
    r1 = x - hi
    mid = r1.astype(BF16).astype(F32)
    lo = (r1 - mid).astype(BF16).astype(F32)
    return hi, mid, lo


def _cumsum_rows(x):
    n = x.shape[0]
    row = lax.broadcasted_iota(jnp.int32, x.shape, 0)
    s = 1
    while s < n:
        x = x + jnp.where(row >= s, pltpu.roll(x, s, 0), 0.0)
        s *= 2
    return x


def _cumsum_lanes(x):
    n = x.shape[1]
    col = lax.broadcasted_iota(jnp.int32, x.shape, 1)
    s = 1
    while s < n:
        x = x + jnp.where(col >= s, pltpu.roll(x, s, 1), 0.0)
        s *= 2
    return x


def _staggered(items, stage_a, stage_b, stage_c):
    n = len(items)
    a_out, b_out = {}, {}
    for step in range(n + 2):
        if step < n:
            a_out[step] = stage_a(items[step])
        if 0 <= step - 1 < n:
            b_out[step - 1] = stage_b(items[step - 1], a_out.pop(step - 1))
        if 0 <= step - 2 < n:
            stage_c(items[step - 2], b_out.pop(step - 2))


def _const_spec(shape):
    nd = len(shape)
    return pl.BlockSpec(shape, lambda *_: (0,) * nd, pipeline_mode=pl.Buffered(1))


def _params(sem):
    return pltpu.CompilerParams(dimension_semantics=sem, vmem_limit_bytes=VMEM_LIMIT)


def _even_proj_kernel(x_ref, g_ref, w_ref, bf_ref, pq_ref, pk_ref,
                      qaug_ref, kaug_ref, vaug_ref, ka_ref, va_ref, logf_ref,
                      qb_ref, kb_ref, vb_ref, kbt_ref, vbt_ref, carry_ref):
    @pl.when(pl.program_id(1) == 0)
    def _():
        carry_ref[...] = jnp.zeros_like(carry_ref)

    h = _rmsnorm(x_ref[0], g_ref[...]).astype(BF16)
    tm = h.shape[0]
    lane = lax.broadcasted_iota(jnp.int32, (tm, LANES), 1)
    low = lane < HEAD_DIM
    first = W_B + LANES

    def chunk(n):
        return _dot(h, w_ref[:, first + W_A * n:first + W_A * (n + 1)])

    pf = _dot(h, w_ref[:, :first])
    qb_ref[0] = (pf[:, :W_B] * (SCALE * LOG2E)).astype(BF16)
    fa = pf[:, W_B:]
    pc = chunk(3)
    kb_ref[0] = pc.astype(BF16)
    kbt_ref[0] = pc
    pc = chunk(4)
    vb_ref[0] = pc.astype(BF16)
    vbt_ref[0] = pc
    logf = _log_sigmoid(fa + bf_ref[...])
    logf_ref[0] = logf[:, :H_A]
    c = _cumsum_rows(logf) + carry_ref[...]
    carry_ref[...] = c[tm - 1:tm, :]
    hi, mid, lo = _split3(c * LOG2E)
    a3 = jnp.where(lane < 8, hi, jnp.where(lane < 16, mid, jnp.where(lane < 24, lo,
                   jnp.where(lane == 24, 1.0, 0.0)))).astype(BF16)
    augq = _dot(a3, pq_ref[...])
    augk = _dot(a3, pk_ref[...])
    vone = jnp.where(lane == HEAD_DIM, 1.0, 0.0)

    def per_head(dst_ref, pc, spare):
        for j in range(H_A // 2):
            pair = pc[:, LANES * j:LANES * (j + 1)]
            for hh, val in enumerate((pair, pltpu.roll(pair, HEAD_DIM, 1))):
                o = slice(LANES * (2 * j + hh), LANES * (2 * j + hh + 1))
                dst_ref[0, :, o] = jnp.where(low, val, vone if spare is None else spare[:, o]).astype(BF16)

    pc = chunk(0)
    per_head(qaug_ref, pc * (SCALE * LOG2E), augq)
    pc = chunk(1)
    ka_ref[0] = pc
    per_head(kaug_ref, pc, augk)
    pc = chunk(2)
    va_ref[0] = pc
    per_head(vaug_ref, pc, None)


def _even_proj(x, g, w_all, b_f, pq, pk, tm):
    b, s, d = x.shape
    nt = s // tm
    lb = min(B_LEFT_CHUNKS * CHUNK, s)
    assert lb == tm, "band-state tail must be exactly one row tile"
    row = lambda w: pl.BlockSpec((1, tm, w), lambda bi, ti: (bi, ti, 0))
    tail = pl.BlockSpec((1, lb, W_B), lambda bi, ti: (bi, 0, 0))
    outs = (
        jax.ShapeDtypeStruct((b, s, H_A * LANES), BF16),
        jax.ShapeDtypeStruct((b, s, H_A * LANES), BF16),
        jax.ShapeDtypeStruct((b, s, H_A * LANES), BF16),
        jax.ShapeDtypeStruct((b, s, W_A), F32),
        jax.ShapeDtypeStruct((b, s, W_A), F32),
        jax.ShapeDtypeStruct((b, s, H_A), F32),
        jax.ShapeDtypeStruct((b, s, W_B), BF16),
        jax.ShapeDtypeStruct((b, s, W_B), BF16),
        jax.ShapeDtypeStruct((b, s, W_B), BF16),
        jax.ShapeDtypeStruct((b, lb, W_B), F32),
        jax.ShapeDtypeStruct((b, lb, W_B), F32),
    )
    return pl.pallas_call(
        _even_proj_kernel,
        grid=(b, nt),
        in_specs=[row(d), _const_spec((1, d)), _const_spec(w_all.shape),
                  _const_spec((1, LANES)), _const_spec(pq.shape), _const_spec(pk.shape)],
        out_specs=(row(H_A * LANES), row(H_A * LANES), row(H_A * LANES), row(W_A), row(W_A), row(H_A),
                   row(W_B), row(W_B), row(W_B), tail, tail),
        out_shape=outs,
        scratch_shapes=[pltpu.VMEM((1, LANES), F32)],
        compiler_params=_params(("arbitrary", "arbitrary")),
        name="even_proj",
    )(x, g, w_all, b_f, pq, pk)


def _fox_kernel(q_ref, k_ref, v_ref, o_ref, m_sc, acc_sc, *, blk, nsub):
    i = pl.program_id(2)
    nh = 2
    chains = [(sub, hh) for sub in range(nsub) for hh in range(nh)]
    qs = {(sub, hh): q_ref[0, sub * blk:(sub + 1) * blk, LANES * hh:LANES * (hh + 1)] for sub, hh in chains}
    r = lax.broadcasted_iota(jnp.int32, (blk, blk), 0)
    c = lax.broadcasted_iota(jnp.int32, (blk, blk), 1)
    causal = c <= r
    m_sc[...] = jnp.full_like(m_sc, NEG_INF)
    acc_sc[...] = jnp.zeros_like(acc_sc)

    def blocks(specs):
        items = [(pl.multiple_of(j * blk, blk), ch, modes[ch[0]])
                 for j, modes in specs for ch in chains if modes[ch[0]] is not None]

        def stage_scores(item):
            start, ch, mode = item
            k = k_ref[0, pl.ds(start, blk), LANES * ch[1]:LANES * (ch[1] + 1)]
            s = _dot_nt(qs[ch], k)
            if mode == "masked":
                s = jnp.where(causal, s, NEG_INF)
            m = m_sc[chains.index(ch)]
            m_new = jnp.maximum(m, jnp.max(s, axis=1, keepdims=True))
            m_sc[chains.index(ch)] = m_new
            return s, m, m_new

        def stage_exp(item, sm):
            s, m, m_new = sm
            p = jnp.concatenate([jnp.exp2(s[:, LANES * cb:LANES * (cb + 1)] - m_new)
                                 for cb in range(blk // LANES)], axis=1)
            return p.astype(BF16), jnp.exp2(m - m_new)

        def stage_pv(item, pa):
            start, ch, _ = item
            p, alpha = pa
            v = v_ref[0, pl.ds(start, blk), LANES * ch[1]:LANES * (ch[1] + 1)]
            n = chains.index(ch)
            acc_sc[n] = alpha * acc_sc[n] + _dot(p, v)

        _staggered(items, stage_scores, stage_exp, stage_pv)

    @pl.loop(0, i)
    def _(jj):
        blocks([(nsub * jj + d, ["full"] * nsub) for d in range(nsub)])

    blocks([(nsub * i + d, [None if sub < d else ("masked" if sub == d else "full") for sub in range(nsub)])
            for d in range(nsub)])

    lane = lax.broadcasted_iota(jnp.int32, (blk, LANES), 1)
    for sub in range(nsub):
        a0 = acc_sc[chains.index((sub, 0))]
        a1 = acc_sc[chains.index((sub, 1))]
        o0 = a0 / a0[:, HEAD_DIM:HEAD_DIM + 1]
        o1 = a1 / a1[:, HEAD_DIM:HEAD_DIM + 1]
        o_ref[0, sub * blk:(sub + 1) * blk, :] = jnp.where(
            lane < HEAD_DIM, o0, pltpu.roll(o1, HEAD_DIM, 1)).astype(BF16)


def _fox(qaug, kaug, vaug, blk, nsub):
    b, s, _ = qaug.shape
    pairs = H_A // 2
    tq = blk * nsub
    return pl.pallas_call(
        functools.partial(_fox_kernel, blk=blk, nsub=nsub),
        grid=(b, pairs, s // tq),
        in_specs=[pl.BlockSpec((1, tq, 2 * LANES), lambda bi, hp, i: (bi, i, hp)),
                  pl.BlockSpec((1, s, 2 * LANES), lambda bi, hp, i: (bi, 0, hp)),
                  pl.BlockSpec((1, s, 2 * LANES), lambda bi, hp, i: (bi, 0, hp))],
        out_specs=pl.BlockSpec((1, tq, LANES), lambda bi, hp, i: (bi, i, hp)),
        out_shape=jax.ShapeDtypeStruct((b, s, W_A), BF16),
        scratch_shapes=[pltpu.VMEM((2 * nsub, blk, LANES), F32), pltpu.VMEM((2 * nsub, blk, LANES), F32)],
        compiler_params=_params(("arbitrary", "arbitrary", "arbitrary")),
        name="fox",
    )(qaug, kaug, vaug)


def _band_kernel(*refs, tq, left, npairs, heads_per_group, use_sinks, pairs_per_kv):
    mxu_rowsum = not use_sinks
    if use_sinks:
        sink_ref, q_ref, k_ref, v_ref, bias_ref, o_ref = refs
    else:
        q_ref, k_ref, v_ref, bias_ref, o_ref = refs
    g = pl.program_id(1)
    i = pl.program_id(2)
    nvar, _, tsub, win = bias_ref.shape
    nsq = tq // tsub
    lane = lax.broadcasted_iota(jnp.int32, (tsub, LANES), 1)
    low = lane < HEAD_DIM
    items = [(sq, pr, hh) for sq in range(nsq) for pr in range(npairs) for hh in range(2)]

    def window(sq):
        gsub = i * nsq + sq
        return pl.multiple_of(jnp.maximum(gsub * tsub - left, 0), LANES), jnp.minimum(gsub, nvar - 1)

    def kv_lanes(pr):
        kl = LANES * (pr // pairs_per_kv)
        return slice(kl, kl + LANES)

    def sink(pr, hh):
        return sink_ref[g * heads_per_group + 2 * pr + hh]

    def stage_scores(item):
        sq, pr, hh = item
        start, var = window(sq)
        qp = q_ref[0, sq * tsub:(sq + 1) * tsub, LANES * pr:LANES * (pr + 1)]
        qm = jnp.where(low if hh == 0 else jnp.logical_not(low), qp, jnp.zeros_like(qp))
        s = _dot_nt(qm, k_ref[0, pl.ds(start, win), kv_lanes(pr)]) + bias_ref[var, 2 * pr + hh]
        m = jnp.max(s, axis=1, keepdims=True)
        return s, (jnp.maximum(m, sink(pr, hh)) if use_sinks else m)

    def stage_exp(item, sm):
        s, m = sm
        e = jnp.exp2(s - m)
        l = None if mxu_rowsum else jnp.sum(e, axis=1, keepdims=True)
        if use_sinks:
            l = l + jnp.exp2(sink(item[1], item[2]) - m)
        return e.astype(BF16), l

    outs = {}

    def stage_pv(item, pe):
        p, l = pe
        sq, pr, hh = item
        v = v_ref[0, pl.ds(window(sq)[0], win), kv_lanes(pr)]
        if mxu_rowsum:
            ov = _dot(p, jnp.concatenate([v, jnp.ones((win, LANES), BF16)], axis=1))
            outs[hh] = ov[:, :LANES] / ov[:, LANES:]
        else:
            outs[hh] = _dot(p, v) / l
        if hh == 1:
            o_ref[0, sq * tsub:(sq + 1) * tsub, LANES * pr:LANES * (pr + 1)] = jnp.where(
                low, outs[0], outs[1]).astype(BF16)

    _staggered(items, stage_scores, stage_exp, stage_pv)


def _band(q, k, v, bias, sinks, *, tq, left, npairs, pairs_per_kv, name):
    b, s, wq = q.shape
    tsub, win = bias.shape[-2:]
    wblk = npairs * LANES
    groups = wq // wblk
    hpg = 2 * npairs
    use_sinks = sinks is not None
    kern = functools.partial(_band_kernel, tq=tq, left=left, npairs=npairs,
                             heads_per_group=hpg, use_sinks=use_sinks, pairs_per_kv=pairs_per_kv)
    kvw = wblk // pairs_per_kv
    resident = lambda shape, imap: pl.BlockSpec(shape, imap, pipeline_mode=pl.Buffered(1))
    in_specs = [pl.BlockSpec((1, tq, wblk), lambda bi, g, i: (bi, i, g)),
                resident((1, s, kvw), lambda bi, g, i: (bi, 0, g)),
                resident((1, s, kvw), lambda bi, g, i: (bi, 0, g)),
                resident((bias.shape[0], hpg, tsub, win), lambda bi, g, i: (0, g, 0, 0))]
    args = [q, k, v, bias]
    if use_sinks:
        in_specs = [pl.BlockSpec(memory_space=pltpu.SMEM)] + in_specs
        args = [sinks] + args
    return pl.pallas_call(
        kern,
        grid=(b, groups, s // tq),
        in_specs=in_specs,
        out_specs=pl.BlockSpec((1, tq, wblk), lambda bi, g, i: (bi, i, g)),
        out_shape=jax.ShapeDtypeStruct((b, s, wq), BF16),
        compiler_params=_params(("arbitrary", "arbitrary", "arbitrary")),
        name=name,
    )(*args)


def _out_ffn_kernel(*refs, n_attn, d_ff, final_norm):
    x_ref = refs[0]
    attn = refs[1:1 + 2 * n_attn]
    gffn_ref, win_ref, wout_ref, gfin_ref, o_ref = refs[1 + 2 * n_attn:]
    y = x_ref[...]
    for a in range(n_attn):
        y = y + _dot(attn[2 * a][...], attn[2 * a + 1][...])
    h = _rmsnorm(y, gffn_ref[...]).astype(BF16)
    gu = _dot(h, win_ref[...])
    gate = gu[:, :d_ff]
    up = gu[:, d_ff:]
    act = (gate * (1.0 / (1.0 + jnp.exp(-gate))) * up).astype(BF16)
    y = y + _dot(act, wout_ref[...])
    o_ref[...] = _rmsnorm(y, gfin_ref[...]) if final_norm else y


def _out_ffn(x, attn_pairs, g_ffn, w_in, w_out, g_fin, tm, layer, final_norm):
    n, d = x.shape
    d_ff = w_out.shape[1]
    row = lambda w: pl.BlockSpec((tm, w), lambda i: (i, 0))
    slab = lambda w: pl.BlockSpec((None,) + w.shape[1:], lambda i: (layer, 0, 0), pipeline_mode=pl.Buffered(1))
    args, specs = [x], [row(d)]
    for o, w in attn_pairs:
        args += [o, w]
        specs += [row(o.shape[1]), _const_spec(w.shape)]
    args += [g_ffn, w_in, w_out, g_fin]
    specs += [_const_spec((1, d)), slab(w_in), slab(w_out), _const_spec((1, d))]
    return pl.pallas_call(
        functools.partial(_out_ffn_kernel, n_attn=len(attn_pairs), d_ff=d_ff, final_norm=final_norm),
        grid=(n // tm,),
        in_specs=specs,
        out_specs=row(d),
        out_shape=jax.ShapeDtypeStruct((n, d), F32),
        compiler_params=_params(("arbitrary",)),
        name="out_ffn",
    )(*args)


def _odd_proj_kernel(x_ref, g_ref, w_ref, q_ref, k_ref, v_ref, kt_ref, vt_ref, *, tiles_per_batch, tail):
    t = pl.program_id(1)
    h = _rmsnorm(x_ref[0], g_ref[...]).astype(BF16)
    p = _dot(h, w_ref[...])
    wq = H_C * HEAD_DIM
    q_ref[0] = (p[:, :wq] * (SCALE * LOG2E)).astype(BF16)
    tm = p.shape[0]
    lane = lax.broadcasted_iota(jnp.int32, (tm, LANES), 1)
    low = lane < HEAD_DIM
    for src, dst in ((p[:, wq:wq + LANES], k_ref), (p[:, wq + LANES:wq + 2 * LANES], v_ref)):
        rolled = pltpu.roll(src, HEAD_DIM, 1)
        dst[0, :, :LANES] = jnp.where(low, src, rolled).astype(BF16)
        dst[0, :, LANES:] = jnp.where(low, rolled, src).astype(BF16)

    @pl.when(t == tiles_per_batch - 1)
    def _():
        kt_ref[0] = p[tm - tail:, wq:wq + LANES]
        vt_ref[0] = p[tm - tail:, wq + LANES:wq + 2 * LANES]


def _odd_proj(x, g, w, tm):
    b, s, d = x.shape
    nt = s // tm
    lc = min(C_LEFT_CHUNKS * CHUNK, s)
    assert lc <= tm
    wq = H_C * HEAD_DIM
    row = lambda w_: pl.BlockSpec((1, tm, w_), lambda bi, ti: (bi, ti, 0))
    tail = pl.BlockSpec((1, lc, LANES), lambda bi, ti: (bi, 0, 0))
    return pl.pallas_call(
        functools.partial(_odd_proj_kernel, tiles_per_batch=nt, tail=lc),
        grid=(b, nt),
        in_specs=[row(d), _const_spec((1, d)), _const_spec(w.shape)],
        out_specs=(row(wq), row(2 * LANES), row(2 * LANES), tail, tail),
        out_shape=(jax.ShapeDtypeStruct((b, s, wq), BF16),
                   jax.ShapeDtypeStruct((b, s, 2 * LANES), BF16),
                   jax.ShapeDtypeStruct((b, s, 2 * LANES), BF16),
                   jax.ShapeDtypeStruct((b, lc, LANES), F32),
                   jax.ShapeDtypeStruct((b, lc, LANES), F32)),
        compiler_params=_params(("arbitrary", "arbitrary")),
        name="odd_proj",
    )(x, g, w)


def _sample_proj_kernel(*refs, gated):
    if gated:
        x_ref, g_ref, w_ref, wf_ref, bf_ref, p_ref, logf_ref = refs
    else:
        x_ref, g_ref, w_ref, p_ref = refs
    h = _rmsnorm(x_ref[...], g_ref[...]).astype(BF16)
    p_ref[...] = _dot(h, w_ref[...])
    if gated:
        logf_ref[...] = _log_sigmoid(_dot(h, wf_ref[...]) + bf_ref[...])


def _sample_proj(x, g, w, tm, w_f=None, b_f=None):
    n, d = x.shape
    row = lambda w_: pl.BlockSpec((tm, w_), lambda i: (i, 0))
    in_specs = [row(d), _const_spec((1, d)), _const_spec(w.shape)]
    args = [x, g, w]
    out_specs = row(w.shape[1])
    out_shape = jax.ShapeDtypeStruct((n, w.shape[1]), F32)
    if w_f is not None:
        in_specs += [_const_spec(w_f.shape), _const_spec((1, LANES))]
        args += [w_f, b_f]
        out_specs = (out_specs, row(LANES))
        out_shape = (out_shape, jax.ShapeDtypeStruct((n, LANES), F32))
    return pl.pallas_call(
        functools.partial(_sample_proj_kernel, gated=w_f is not None),
        grid=(n // tm,),
        in_specs=in_specs,
        out_specs=out_specs,
        out_shape=out_shape,
        compiler_params=_params(("arbitrary",)),
        name="sample_proj",
    )(*args)


def _block_diag_q(q, nheads, scale):
    t, w = q.shape
    tiled = jnp.concatenate([q] * nheads, axis=0)
    r = lax.broadcasted_iota(jnp.int32, (nheads * t, w), 0)
    c = lax.broadcasted_iota(jnp.int32, (nheads * t, w), 1)
    return jnp.where(r // t == c // HEAD_DIM, tiled * scale, 0.0).astype(BF16)


def _block_diag_extract(o_all, nheads, t):
    w = o_all.shape[1]
    c = lax.broadcasted_iota(jnp.int32, (t, w), 1)
    out = jnp.zeros((t, w), F32)
    for h in range(nheads):
        out = jnp.where(c // HEAD_DIM == h, o_all[h * t:(h + 1) * t, :], out)
    return out


def _heads_major(ref):
    _, nh, hd, nk = ref.shape
    return ref[0].reshape(nh * hd, nk).astype(BF16)


def _pad_rows(x, rows):
    return jnp.concatenate([x, jnp.zeros((rows - x.shape[0], x.shape[1]), x.dtype)], axis=0)


def _sample_even_kernel(qa_ref, kan_ref, van_ref, qb_ref, kbn_ref, vbn_ref, ck_ref, cv_ref, lft_ref,
                        cbk_ref, cbv_ref, e3_ref, biasb_ref, oa_ref, ob_ref,
                        bias_sc, m_sc, acc_sc, *, t, past, kc, nkc):
    c = pl.program_id(1)
    rows = H_A * t
    qbd = _block_diag_q(qa_ref[...], H_A, SCALE)

    @pl.when(c == 0)
    def _():
        cum = _cumsum_lanes(lft_ref[0])
        last = past + t - 1
        suffix = cum[:, last:last + 1] - cum
        hi, mid, lo = _split3(suffix)
        s3 = jnp.concatenate([hi, mid, lo, jnp.zeros((LANES - 3 * H_A, suffix.shape[1]), F32)], axis=0)
        s3 = s3.astype(BF16)
        for ch in range(nkc + 1):
            w = kc if ch < nkc else LANES
            bias_sc[ch, :, :w] = _dot(e3_ref[...], s3[:, ch * kc:ch * kc + w])
        m_sc[...] = jnp.full_like(m_sc, NEG_INF)
        acc_sc[...] = jnp.zeros_like(acc_sc)

    def update(s, pv):
        m = m_sc[...]
        m_new = jnp.maximum(m, jnp.max(s, axis=1, keepdims=True))
        p = jnp.exp(s - m_new)
        alpha = jnp.exp(m - m_new)
        acc = acc_sc[...]
        lsum = alpha * acc[:, W_A:W_A + 1] + jnp.sum(p, axis=1, keepdims=True)
        o = alpha * acc[:, :W_A] + pv(p.astype(BF16))
        acc_sc[:, :W_A] = o
        acc_sc[:, W_A:] = jnp.broadcast_to(lsum, (rows, LANES))
        m_sc[...] = m_new

    s = _dot(qbd, _heads_major(ck_ref)) + bias_sc[c]
    vt = _heads_major(cv_ref)
    update(s, lambda p: _dot_nt(p, vt))

    @pl.when(c == nkc - 1)
    def _():
        kn = _pad_rows(kan_ref[...], LANES).astype(BF16)
        vn = _pad_rows(van_ref[...], LANES).astype(BF16)
        sn = _dot_nt(qbd, kn) + bias_sc[nkc, :, :LANES]
        r = lax.broadcasted_iota(jnp.int32, sn.shape, 0)
        col = lax.broadcasted_iota(jnp.int32, sn.shape, 1)
        sn = jnp.where(col <= r % t, sn, NEG_INF)
        update(sn, lambda p: _dot(p, vn))
        acc = acc_sc[...]
        o_all = acc[:, :W_A] / acc[:, W_A:W_A + 1]
        oa_ref[...] = _block_diag_extract(o_all, H_A, t).astype(BF16)

        qbd_b = _block_diag_q(qb_ref[...], H_B, SCALE)
        lbk = cbk_ref.shape[3]
        kbn = _pad_rows(kbn_ref[...], LANES).astype(BF16)
        vbn = _pad_rows(vbn_ref[...], LANES).astype(BF16)
        sb = jnp.concatenate([_dot(qbd_b, _heads_major(cbk_ref)), _dot_nt(qbd_b, kbn)], axis=1) + biasb_ref[...]
        mb = jnp.max(sb, axis=1, keepdims=True)
        eb = jnp.exp(sb - mb)
        lb = jnp.sum(eb, axis=1, keepdims=True)
        eb = eb.astype(BF16)
        ob_all = (_dot_nt(eb[:, :lbk], _heads_major(cbv_ref)) + _dot(eb[:, lbk:], vbn)) / lb
        ob_ref[...] = _block_diag_extract(ob_all, H_B, t).astype(BF16)


def _sample_even(p, cache_k, cache_v, logf_t, cache_bk, cache_bv, e3, bias_b, *, t, kc):
    n = p.shape[0]
    nb = n // t
    past = cache_k.shape[3]
    nkc = past // kc
    lbk = cache_bk.shape[3]
    pcol = lambda j: pl.BlockSpec((t, W_A), lambda bi, c: (bi, j))
    rows = H_A * t
    return pl.pallas_call(
        functools.partial(_sample_even_kernel, t=t, past=past, kc=kc, nkc=nkc),
        grid=(nb, nkc),
        in_specs=[pcol(0), pcol(1), pcol(2), pcol(3), pcol(4), pcol(5),
                  pl.BlockSpec((1, H_A, HEAD_DIM, kc), lambda bi, c: (bi, 0, 0, c)),
                  pl.BlockSpec((1, H_A, HEAD_DIM, kc), lambda bi, c: (bi, 0, 0, c)),
                  pl.BlockSpec((1, H_A, logf_t.shape[2]), lambda bi, c: (bi, 0, 0)),
                  pl.BlockSpec((1, H_B, HEAD_DIM, lbk), lambda bi, c: (bi, 0, 0, 0)),
                  pl.BlockSpec((1, H_B, HEAD_DIM, lbk), lambda bi, c: (bi, 0, 0, 0)),
                  _const_spec(e3.shape), _const_spec(bias_b.shape)],
        out_specs=(pl.BlockSpec((t, W_A), lambda bi, c: (bi, 0)),
                   pl.BlockSpec((t, W_B), lambda bi, c: (bi, 0))),
        out_shape=(jax.ShapeDtypeStruct((n, W_A), BF16), jax.ShapeDtypeStruct((n, W_B), BF16)),
        scratch_shapes=[pltpu.VMEM((nkc + 1, rows, kc), F32),
                        pltpu.VMEM((rows, 1), F32),
                        pltpu.VMEM((rows, W_A + LANES), F32)],
        compiler_params=_params(("arbitrary", "arbitrary")),
        name="sample_even",
    )(p, p, p, p, p, p, cache_k, cache_v, logf_t, cache_bk, cache_bv, e3, bias_b)


def _sample_odd_kernel(q_ref, kn_ref, vn_ref, ck_ref, cv_ref, x_ref, bias_ref, sink_ref, o_ref, *, t, nbs):
    sk = sink_ref[...]
    for bb in range(nbs):
        rows = slice(bb * t, (bb + 1) * t)
        qbd = _block_diag_q(q_ref[rows, :], H_C, SCALE)
        kall = jnp.concatenate([ck_ref[bb], _pad_rows(kn_ref[rows, :], LANES)], axis=0).astype(BF16)
        vall = jnp.concatenate([cv_ref[bb], _pad_rows(vn_ref[rows, :], LANES)], axis=0).astype(BF16)
        kexp = _dot(kall, x_ref[...]).astype(BF16)
        vexp = _dot(vall, x_ref[...]).astype(BF16)
        s = _dot_nt(qbd, kexp) + bias_ref[...]
        m = jnp.maximum(jnp.max(s, axis=1, keepdims=True), sk)
        e = jnp.exp(s - m)
        l = jnp.sum(e, axis=1, keepdims=True) + jnp.exp(sk - m)
        o_all = _dot(e.astype(BF16), vexp) / l
        o_ref[rows, :] = _block_diag_extract(o_all, H_C, t).astype(BF16)


def _sample_odd(p, cache_k, cache_v, xexp, bias, sink_col, *, t, nbs):
    n = p.shape[0]
    nb = n // t
    wq = H_C * HEAD_DIM
    lc = cache_k.shape[1]
    rows = t * nbs
    return pl.pallas_call(
        functools.partial(_sample_odd_kernel, t=t, nbs=nbs),
        grid=(nb // nbs,),
        in_specs=[pl.BlockSpec((rows, wq), lambda bi: (bi, 0)),
                  pl.BlockSpec((rows, LANES), lambda bi: (bi, wq // LANES)),
                  pl.BlockSpec((rows, LANES), lambda bi: (bi, wq // LANES + 1)),
                  pl.BlockSpec((nbs, lc, LANES), lambda bi: (bi, 0, 0)),
                  pl.BlockSpec((nbs, lc, LANES), lambda bi: (bi, 0, 0)),
                  _const_spec(xexp.shape), _const_spec(bias.shape), _const_spec(sink_col.shape)],
        out_specs=pl.BlockSpec((rows, wq), lambda bi: (bi, 0)),
        out_shape=jax.ShapeDtypeStruct((n, wq), BF16),
        compiler_params=_params(("arbitrary",)),
        name="sample_odd",
    )(p, p, p, cache_k, cache_v, xexp, bias, sink_col)


def _t5_bucket(rel_mem):
    nb = T5_BUCKETS // 2
    max_exact = nb // 2
    n = jnp.abs(rel_mem)
    large = max_exact + (jnp.log(jnp.maximum(n, 1).astype(F32) / max_exact)
                         / math.log(T5_MAX_DIST / max_exact) * (nb - max_exact)).astype(jnp.int32)
    large = jnp.minimum(large, nb - 1)
    return jnp.where(rel_mem > 0, nb, 0) + jnp.where(n < max_exact, n, large)


def _bias_b_of_rel(table, rel):
    idx = np.clip(rel, -B_REL_CLIP, B_REL_CLIP) + B_REL_CLIP
    return table.astype(F32)[idx].T


def _bias_c_of_rel(table, rel):
    return table.astype(F32)[_t5_bucket(-jnp.asarray(rel, jnp.int32))].T


def _bias_tiles_kernel(f_ref, o_ref, *, tq, left, offsets, left_chunks, valid_cols):
    n = f_ref.shape[2]
    win = o_ref.shape[3]
    rows = jnp.broadcast_to(f_ref[0], (tq, n))
    wide = pltpu.roll(rows, n - (tq - 1), 1, stride=1, stride_axis=0)
    q = lax.broadcasted_iota(jnp.int32, (tq, win), 0)
    k = lax.broadcasted_iota(jnp.int32, (tq, win), 1)
    shift = CHUNK.bit_length() - 1
    for v, off in enumerate(offsets):
        if valid_cols is None:
            d = off // CHUNK + jnp.right_shift(q, shift) - jnp.right_shift(k, shift)
            ok = (d >= 0) & (d <= left_chunks)
        else:
            ok = k < valid_cols
        o_ref[v, 0] = jnp.where(ok, wide[:, left - off:left - off + win], NEG_INF)


def _bias_tiles(bias_of_rel, *, tq, left, win, offsets, left_chunks=None, valid_cols=None):
    assert CHUNK & (CHUNK - 1) == 0 and all((left - o) % LANES == 0 for o in offsets)
    ncols = win + left - min(offsets)
    n = tq + ncols - 1
    n_pad = -(-n // LANES) * LANES
    f_rev = bias_of_rel(tq - 1 + left - np.arange(n))
    nheads = f_rev.shape[0]
    f_rev = jnp.pad(f_rev, ((0, 0), (0, n_pad - n)))[:, None, :]
    return pl.pallas_call(
        functools.partial(_bias_tiles_kernel, tq=tq, left=left, offsets=tuple(offsets),
                          left_chunks=left_chunks, valid_cols=valid_cols),
        grid=(nheads,),
        in_specs=[pl.BlockSpec((1, 1, n_pad), lambda h: (h, 0, 0))],
        out_specs=pl.BlockSpec((len(offsets), 1, tq, win), lambda h: (0, h, 0, 0)),
        out_shape=jax.ShapeDtypeStruct((len(offsets), nheads, tq, win), F32),
        compiler_params=_params(("arbitrary",)),
        name="bias_tiles",
    )(f_rev)


def _sample_bias(bias_of_rel, t, cache_len):
    ncols = cache_len + LANES
    tiles = _bias_tiles(bias_of_rel, tq=t, left=cache_len, win=ncols, offsets=[cache_len],
                        valid_cols=cache_len + t)
    return tiles.reshape(-1, ncols)


def _cast_kernel(x_ref, o_ref):
    o_ref[...] = x_ref[...].astype(o_ref.dtype)


def _cast_bf16(w, rows):
    nl, r, c = w.shape
    w2 = w.reshape(nl * r, c)
    out = pl.pallas_call(
        _cast_kernel,
        grid=(nl * r // rows,),
        in_specs=[pl.BlockSpec((rows, c), lambda i: (i, 0))],
        out_specs=pl.BlockSpec((rows, c), lambda i: (i, 0)),
        out_shape=jax.ShapeDtypeStruct(w2.shape, BF16),
        compiler_params=_params(("arbitrary",)),
        name="cast_bf16",
    )(w2)
    return out.reshape(nl, r, c)


def _placement():
    pq = np.zeros((LANES, H_A * LANES), np.float32)
    pk = np.zeros((LANES, H_A * LANES), np.float32)
    for h in range(H_A):
        for j in range(3):
            pq[8 * j + h, LANES * h + HEAD_DIM + j] = 1.0
            pq[24, LANES * h + HEAD_DIM + 3 + j] = 1.0
            pk[24, LANES * h + HEAD_DIM + j] = 1.0
            pk[8 * j + h, LANES * h + HEAD_DIM + 3 + j] = -1.0
    return jnp.asarray(pq, BF16), jnp.asarray(pk, BF16)


def kernel(x_prompt, x_sample, cache_a_k, cache_a_v, cache_a_logf, cache_b_k, cache_b_v, cache_c_k, cache_c_v,
           norm_mix, norm_ffn, norm_final, w_in_even, b_forget, rel_bias_b, w_out_even, w_in_odd, sinks_c,
           w_out_odd, t5_bias, w_ffn_in, w_ffn_out):
    b, s, d = x_prompt.shape
    nb, t, _ = x_sample.shape
    past = cache_a_k.shape[2]
    n_p, n_s = b * s, nb * t

    w_even = w_in_even[0]
    w_main = w_even[:, :3 * W_A + 3 * W_B].astype(BF16)
    wf = w_even[:, 3 * W_A + 3 * W_B:]
    w_f = jnp.concatenate([wf, wf, wf, jnp.zeros((d, LANES - 3 * H_A), F32)], axis=1).astype(BF16)
    bf = b_forget[0].astype(F32)
    b_f = jnp.concatenate([bf, bf, bf, jnp.zeros((LANES - 3 * H_A,), F32)])[None, :]
    pq, pk = _placement()
    w_oe = w_out_even[0].astype(BF16)
    w_oo = w_out_odd[0].astype(BF16)
    w_odd = w_in_odd[0].astype(BF16)
    w_fi = _cast_bf16(w_ffn_in, rows=256)
    w_fo = _cast_bf16(w_ffn_out, rows=704)
    g_mix = norm_mix.astype(F32)[:, None, :]
    g_ffn = norm_ffn.astype(F32)[:, None, :]
    g_fin = norm_final.astype(F32)[None, :]

    bias_b_of = functools.partial(_bias_b_of_rel, rel_bias_b[0])
    bias_c_of = functools.partial(_bias_c_of_rel, t5_bias)
    log2_domain = lambda bias_of: (lambda rel: bias_of(rel) * LOG2E)
    variant_offsets = lambda tsub, left: [min(v * tsub, left) for v in range(left // tsub + 1)]
    tq_b, tsub_b, left_b = 512, 256, B_LEFT_CHUNKS * CHUNK
    bias_b = _bias_tiles(log2_domain(bias_b_of), tq=tsub_b, left=left_b, win=tsub_b + left_b,
                         offsets=variant_offsets(tsub_b, left_b), left_chunks=B_LEFT_CHUNKS)
    tq_c, tsub_c, left_c = 256, 128, C_LEFT_CHUNKS * CHUNK
    bias_c = _bias_tiles(log2_domain(bias_c_of), tq=tsub_c, left=left_c, win=tsub_c + left_c,
                         offsets=variant_offsets(tsub_c, left_c), left_chunks=C_LEFT_CHUNKS)

    xp = x_prompt
    w_all = jnp.concatenate([w_main[:, 3 * W_A:3 * W_A + W_B], w_f, w_main[:, :3 * W_A],
                             w_main[:, 3 * W_A + W_B:]], axis=1)
    qaug, kaug, vaug, ka, va, logf, qb, kb, vb, kbt, vbt = _even_proj(
        xp, g_mix[0], w_all, b_f, pq, pk, tm=512)
    oa = _fox(qaug, kaug, vaug, blk=512, nsub=2)
    ob = _band(qb, kb, vb, bias_b, None, tq=tq_b, left=left_b, npairs=H_B // 2, pairs_per_kv=1,
               name="band_b")
    xp1 = _out_ffn(xp.reshape(n_p, d),
                   [(oa.reshape(n_p, W_A), w_oe[:W_A]), (ob.reshape(n_p, W_B), w_oe[W_A:])],
                   g_ffn[0], w_fi, w_fo, g_fin, tm=512, layer=0, final_norm=False)

    qc, kcd, vcd, kct, vct = _odd_proj(xp1.reshape(b, s, d), g_mix[1], w_odd, tm=512)
    oc = _band(qc, kcd, vcd, bias_c, sinks_c[0].astype(F32) * LOG2E, tq=tq_c, left=left_c, npairs=H_C // 2,
               pairs_per_kv=G_C // 2, name="band_c")
    y_prompt = _out_ffn(xp1, [(oc.reshape(n_p, H_C * HEAD_DIM), w_oo)],
                        g_ffn[1], w_fi, w_fo, g_fin, tm=512, layer=1, final_norm=True)

    xs = x_sample.reshape(n_s, d)
    ps, logf_sp = _sample_proj(xs, g_mix[0], w_main, tm=256, w_f=w_f, b_f=b_f)
    logf_s = logf_sp[:, :H_A]
    kpad = LANES * -(-(past + t) // LANES)
    logf_all = jnp.concatenate([cache_a_logf[0].astype(F32), logf_s.reshape(nb, t, H_A)], axis=1)
    logf_t = jnp.pad(jnp.swapaxes(logf_all, 1, 2), ((0, 0), (0, 0), (0, kpad - past - t)))
    lbs = cache_b_k.shape[2]
    bias_sb = _sample_bias(bias_b_of, t, lbs)
    e3 = np.zeros((H_A * t, LANES), np.float32)
    for j in range(3):
        e3[np.arange(H_A * t), 8 * j + np.arange(H_A * t) // t] = 1.0
    keys_minor = lambda cache: jnp.transpose(cache[0], (0, 2, 3, 1))
    oa_s, ob_s = _sample_even(ps, keys_minor(cache_a_k), keys_minor(cache_a_v), logf_t,
                              keys_minor(cache_b_k), keys_minor(cache_b_v),
                              jnp.asarray(e3, BF16), bias_sb, t=t, kc=4096)
    xs1 = _out_ffn(xs, [(oa_s, w_oe[:W_A]), (ob_s, w_oe[W_A:])], g_ffn[0], w_fi, w_fo, g_fin,
                   tm=256, layer=0, final_norm=False)

    ps2 = _sample_proj(xs1, g_mix[1], w_odd, tm=256)
    lcs = cache_c_k.shape[2]
    bias_sc = _sample_bias(bias_c_of, t, lcs)
    sink_col = jnp.repeat(sinks_c[0].astype(F32), t)[:, None]
    lane_head = np.arange(H_C * HEAD_DIM) // HEAD_DIM
    src_lane = (lane_head // G_C) * HEAD_DIM + np.arange(H_C * HEAD_DIM) % HEAD_DIM
    xexp = jnp.asarray(np.arange(LANES)[:, None] == src_lane[None, :], BF16)
    oc_s = _sample_odd(ps2, cache_c_k[0].reshape(nb, lcs, LANES), cache_c_v[0].reshape(nb, lcs, LANES),
                       xexp, bias_sc, sink_col, t=t, nbs=4)
    y_sample = _out_ffn(xs1, [(oc_s, w_oo)], g_ffn[1], w_fi, w_fo, g_fin, tm=256, layer=1, final_norm=True)

    wq = H_C * HEAD_DIM
    hd = lambda a, lead, h: a.reshape((1,) + lead + (h, HEAD_DIM))
    return (
        y_prompt.reshape(b, s, d), y_sample.reshape(nb, t, d),
        hd(ka, (b, s), H_A), hd(va, (b, s), H_A), logf[None],
        hd(kbt, (b, kbt.shape[1]), H_B), hd(vbt, (b, vbt.shape[1]), H_B),
        hd(kct, (b, kct.shape[1]), HKV_C), hd(vct, (b, vct.shape[1]), HKV_C),
        hd(ps[:, W_A:2 * W_A], (nb, t), H_A), hd(ps[:, 2 * W_A:3 * W_A], (nb, t), H_A),
        logf_s.reshape(1, nb, t, H_A),
        hd(ps[:, 3 * W_A + W_B:3 * W_A + 2 * W_B], (nb, t), H_B),
        hd(ps[:, 3 * W_A + 2 * W_B:3 * W_A + 3 * W_B], (nb, t), H_B),
        hd(ps2[:, wq:wq + LANES], (nb, t), HKV_C), hd(ps2[:, wq + LANES:wq + 2 * LANES], (nb, t), HKV_C),
    )
```

```python
import functools
import math

import jax
import jax.numpy as jnp
import numpy as np
from jax import lax
from jax.experimental import pallas as pl
from jax.experimental.pallas import tpu as pltpu

D_MODEL = 1024
HEAD_DIM = 64
CHUNK = 64
H_A = 8
H_B = 8
B_LEFT_CHUNKS = 8
B_REL_CLIP = 128
H_C = 16
HKV_C = 2
G_C = H_C // HKV_C
WINDOW = 128
C_LEFT_CHUNKS = WINDOW // CHUNK
T5_BUCKETS = 32
T5_MAX_DIST = 128
EPS = 1e-6
W_A = H_A * HEAD_DIM
W_B = H_B * HEAD_DIM
SCALE = HEAD_DIM ** -0.5
LOG2E = math.log2(math.e)

LANES = 128
VMEM_LIMIT = 60 * 1024 * 1024

F32 = jnp.float32
BF16 = jnp.bfloat16
NEG_INF = float("-inf")

_NT = (((1,), (1,)), ((), ()))


def _dot(a, b):
    return jnp.dot(a, b, preferred_element_type=F32)


def _dot_nt(a, b):
    return lax.dot_general(a, b, _NT, preferred_element_type=F32)


def _rmsnorm(x, g):
    ms = jnp.mean(x * x, axis=-1, keepdims=True)
    return x * lax.rsqrt(ms + EPS) * g


def _log_sigmoid(x):
    return jnp.minimum(x, 0.0) - jnp.log1p(jnp.exp(-jnp.abs(x)))


def _split3(x):
    hi = x.astype(BF16).astype(F32)
    r1 = x - hi
    mid = r1.astype(BF16).astype(F32)
    lo = (r1 - mid).astype(BF16).astype(F32)
    return hi, mid, lo


def _cumsum_rows(x):
    n = x.shape[0]
    row = lax.broadcasted_iota(jnp.int32, x.shape, 0)
    s = 1
    while s < n:
        x = x + jnp.where(row >= s, pltpu.roll(x, s, 0), 0.0)
        s *= 2
    return x


def _cumsum_lanes(x):
    n = x.shape[1]
    col = lax.broadcasted_iota(jnp.int32, x.shape, 1)
    s = 1
    while s < n:
        x = x + jnp.where(col >= s, pltpu.roll(x, s, 1), 0.0)
        s *= 2
    return x


def _staggered(items, stage_a, stage_b, stage_c):
    n = len(items)
    a_out, b_out = {}, {}
    for step in range(n + 2):
        if step < n:
            a_out[step] = stage_a(items[step])
        if 0 <= step - 1 < n:
            b_out[step - 1] = stage_b(items[step - 1], a_out.pop(step - 1))
        if 0 <= step - 2 < n:
            stage_c(items[step - 2], b_out.pop(step - 2))


def _const_spec(shape):
    nd = len(shape)
    return pl.BlockSpec(shape, lambda *_: (0,) * nd, pipeline_mode=pl.Buffered(1))


def _params(sem):
    return pltpu.CompilerParams(dimension_semantics=sem, vmem_limit_bytes=VMEM_LIMIT)


def _even_proj_kernel(x_ref, g_ref, w_ref, bf_ref, pq_ref, pk_ref,
                      qaug_ref, kaug_ref, vaug_ref, ka_ref, va_ref, logf_ref,
                      qb_ref, kb_ref, vb_ref, kbt_ref, vbt_ref, carry_ref):
    @pl.when(pl.program_id(1) == 0)
    def _():
        carry_ref[...] = jnp.zeros_like(carry_ref)

    h = _rmsnorm(x_ref[0], g_ref[...]).astype(BF16)
    tm = h.shape[0]
    lane = lax.broadcasted_iota(jnp.int32, (tm, LANES), 1)
    low = lane < HEAD_DIM
    first = W_B + LANES

    def chunk(n):
        return _dot(h, w_ref[:, first + W_A * n:first + W_A * (n + 1)])

    pf = _dot(h, w_ref[:, :first])
    qb_ref[0] = (pf[:, :W_B] * (SCALE * LOG2E)).astype(BF16)
    fa = pf[:, W_B:]
    pc = chunk(3)
    kb_ref[0] = pc.astype(BF16)
    kbt_ref[0] = pc
    pc = chunk(4)
    vb_ref[0] = pc.astype(BF16)
    vbt_ref[0] = pc
    logf = _log_sigmoid(fa + bf_ref[...])
    logf_ref[0] = logf[:, :H_A]
    c = _cumsum_rows(logf) + carry_ref[...]
    carry_ref[...] = c[tm - 1:tm, :]
    hi, mid, lo = _split3(c * LOG2E)
    a3 = jnp.where(lane < 8, hi, jnp.where(lane < 16, mid, jnp.where(lane < 24, lo,
                   jnp.where(lane == 24, 1.0, 0.0)))).astype(BF16)
    augq = _dot(a3, pq_ref[...])
    augk = _dot(a3, pk_ref[...])
    vone = jnp.where(lane == HEAD_DIM, 1.0, 0.0)

    def per_head(dst_ref, pc, spare):
        for j in range(H_A // 2):
            pair = pc[:, LANES * j:LANES * (j + 1)]
            for hh, val in enumerate((pair, pltpu.roll(pair, HEAD_DIM, 1))):
                o = slice(LANES * (2 * j + hh), LANES * (2 * j + hh + 1))
                dst_ref[0, :, o] = jnp.where(low, val, vone if spare is None else spare[:, o]).astype(BF16)

    pc = chunk(0)
    per_head(qaug_ref, pc * (SCALE * LOG2E), augq)
    pc = chunk(1)
    ka_ref[0] = pc
    per_head(kaug_ref, pc, augk)
    pc = chunk(2)
    va_ref[0] = pc
    per_head(vaug_ref, pc, None)


def _even_proj(x, g, w_all, b_f, pq, pk, tm):
    b, s, d = x.shape
    nt = s // tm
    lb = min(B_LEFT_CHUNKS * CHUNK, s)
    assert lb == tm, "band-state tail must be exactly one row tile"
    row = lambda w: pl.BlockSpec((1, tm, w), lambda bi, ti: (bi, ti, 0))
    tail = pl.BlockSpec((1, lb, W_B), lambda bi, ti: (bi, 0, 0))
    outs = (
        jax.ShapeDtypeStruct((b, s, H_A * LANES), BF16),
        jax.ShapeDtypeStruct((b, s, H_A * LANES), BF16),
        jax.ShapeDtypeStruct((b, s, H_A * LANES), BF16),
        jax.ShapeDtypeStruct((b, s, W_A), F32),
        jax.ShapeDtypeStruct((b, s, W_A), F32),
        jax.ShapeDtypeStruct((b, s, H_A), F32),
        jax.ShapeDtypeStruct((b, s, W_B), BF16),
        jax.ShapeDtypeStruct((b, s, W_B), BF16),
        jax.ShapeDtypeStruct((b, s, W_B), BF16),
        jax.ShapeDtypeStruct((b, lb, W_B), F32),
        jax.ShapeDtypeStruct((b, lb, W_B), F32),
    )
    return pl.pallas_call(
        _even_proj_kernel,
        grid=(b, nt),
        in_specs=[row(d), _const_spec((1, d)), _const_spec(w_all.shape),
                  _const_spec((1, LANES)), _const_spec(pq.shape), _const_spec(pk.shape)],
        out_specs=(row(H_A * LANES), row(H_A * LANES), row(H_A * LANES), row(W_A), row(W_A), row(H_A),
                   row(W_B), row(W_B), row(W_B), tail, tail),
        out_shape=outs,
        scratch_shapes=[pltpu.VMEM((1, LANES), F32)],
        compiler_params=_params(("arbitrary", "arbitrary")),
        name="even_proj",
    )(x, g, w_all, b_f, pq, pk)


def _fox_kernel(q_ref, k_ref, v_ref, o_ref, m_sc, acc_sc, *, blk, nsub):
    i = pl.program_id(2)
    nh = 2
    chains = [(sub, hh) for sub in range(nsub) for hh in range(nh)]
    qs = {(sub, hh): q_ref[0, sub * blk:(sub + 1) * blk, LANES * hh:LANES * (hh + 1)] for sub, hh in chains}
    r = lax.broadcasted_iota(jnp.int32, (blk, blk), 0)
    c = lax.broadcasted_iota(jnp.int32, (blk, blk), 1)
    causal = c <= r
    m_sc[...] = jnp.full_like(m_sc, NEG_INF)
    acc_sc[...] = jnp.zeros_like(acc_sc)

    def blocks(specs):
        items = [(pl.multiple_of(j * blk, blk), ch, modes[ch[0]])
                 for j, modes in specs for ch in chains if modes[ch[0]] is not None]

        def stage_scores(item):
            start, ch, mode = item
            k = k_ref[0, pl.ds(start, blk), LANES * ch[1]:LANES * (ch[1] + 1)]
            s = _dot_nt(qs[ch], k)
            if mode == "masked":
                s = jnp.where(causal, s, NEG_INF)
            m = m_sc[chains.index(ch)]
            m_new = jnp.maximum(m, jnp.max(s, axis=1, keepdims=True))
            m_sc[chains.index(ch)] = m_new
            return s, m, m_new

        def stage_exp(item, sm):
            s, m, m_new = sm
            p = jnp.concatenate([jnp.exp2(s[:, LANES * cb:LANES * (cb + 1)] - m_new)
                                 for cb in range(blk // LANES)], axis=1)
            return p.astype(BF16), jnp.exp2(m - m_new)

        def stage_pv(item, pa):
            start, ch, _ = item
            p, alpha = pa
            v = v_ref[0, pl.ds(start, blk), LANES * ch[1]:LANES * (ch[1] + 1)]
            n = chains.index(ch)
            acc_sc[n] = alpha * acc_sc[n] + _dot(p, v)

        _staggered(items, stage_scores, stage_exp, stage_pv)

    @pl.loop(0, i)
    def _(jj):
        blocks([(nsub * jj + d, ["full"] * nsub) for d in range(nsub)])

    blocks([(nsub * i + d, [None if sub < d else ("masked" if sub == d else "full") for sub in range(nsub)])
            for d in range(nsub)])

    lane = lax.broadcasted_iota(jnp.int32, (blk, LANES), 1)
    for sub in range(nsub):
        a0 = acc_sc[chains.index((sub, 0))]
        a1 = acc_sc[chains.index((sub, 1))]
        o0 = a0 / a0[:, HEAD_DIM:HEAD_DIM + 1]
        o1 = a1 / a1[:, HEAD_DIM:HEAD_DIM + 1]
        o_ref[0, sub * blk:(sub + 1) * blk, :] = jnp.where(
            lane < HEAD_DIM, o0, pltpu.roll(o1, HEAD_DIM, 1)).astype(BF16)


def _fox(qaug, kaug, vaug, blk, nsub):
    b, s, _ = qaug.shape
    pairs = H_A // 2
    tq = blk * nsub
    return pl.pallas_call(
        functools.partial(_fox_kernel, blk=blk, nsub=nsub),
        grid=(b, pairs, s // tq),
        in_specs=[pl.BlockSpec((1, tq, 2 * LANES), lambda bi, hp, i: (bi, i, hp)),
                  pl.BlockSpec((1, s, 2 * LANES), lambda bi, hp, i: (bi, 0, hp)),
                  pl.BlockSpec((1, s, 2 * LANES), lambda bi, hp, i: (bi, 0, hp))],
        out_specs=pl.BlockSpec((1, tq, LANES), lambda bi, hp, i: (bi, i, hp)),
        out_shape=jax.ShapeDtypeStruct((b, s, W_A), BF16),
        scratch_shapes=[pltpu.VMEM((2 * nsub, blk, LANES), F32), pltpu.VMEM((2 * nsub, blk, LANES), F32)],
        compiler_params=_params(("arbitrary", "arbitrary", "arbitrary")),
        name="fox",
    )(qaug, kaug, vaug)


def _band_kernel(*refs, tq, left, npairs, heads_per_group, use_sinks, pairs_per_kv):
    mxu_rowsum = not use_sinks
    if use_sinks:
        sink_ref, q_ref, k_ref, v_ref, bias_ref, o_ref = refs
    else:
        q_ref, k_ref, v_ref, bias_ref, o_ref = refs
    g = pl.program_id(1)
    i = pl.program_id(2)
    nvar, _, tsub, win = bias_ref.shape
    nsq = tq // tsub
    lane = lax.broadcasted_iota(jnp.int32, (tsub, LANES), 1)
    low = lane < HEAD_DIM
    items = [(sq, pr, hh) for sq in range(nsq) for pr in range(npairs) for hh in range(2)]

    def window(sq):
        gsub = i * nsq + sq
        return pl.multiple_of(jnp.maximum(gsub * tsub - left, 0), LANES), jnp.minimum(gsub, nvar - 1)

    def kv_lanes(pr):
        kl = LANES * (pr // pairs_per_kv)
        return slice(kl, kl + LANES)

    def sink(pr, hh):
        return sink_ref[g * heads_per_group + 2 * pr + hh]

    def stage_scores(item):
        sq, pr, hh = item
        start, var = window(sq)
        qp = q_ref[0, sq * tsub:(sq + 1) * tsub, LANES * pr:LANES * (pr + 1)]
        qm = jnp.where(low if hh == 0 else jnp.logical_not(low), qp, jnp.zeros_like(qp))
        s = _dot_nt(qm, k_ref[0, pl.ds(start, win), kv_lanes(pr)]) + bias_ref[var, 2 * pr + hh]
        m = jnp.max(s, axis=1, keepdims=True)
        return s, (jnp.maximum(m, sink(pr, hh)) if use_sinks else m)

    def stage_exp(item, sm):
        s, m = sm
        e = jnp.exp2(s - m)
        l = None if mxu_rowsum else jnp.sum(e, axis=1, keepdims=True)
        if use_sinks:
            l = l + jnp.exp2(sink(item[1], item[2]) - m)
        return e.astype(BF16), l

    outs = {}

    def stage_pv(item, pe):
        p, l = pe
        sq, pr, hh = item
        v = v_ref[0, pl.ds(window(sq)[0], win), kv_lanes(pr)]
        if mxu_rowsum:
            ov = _dot(p, jnp.concatenate([v, jnp.ones((win, LANES), BF16)], axis=1))
            outs[hh] = ov[:, :LANES] / ov[:, LANES:]
        else:
            outs[hh] = _dot(p, v) / l
        if hh == 1:
            o_ref[0, sq * tsub:(sq + 1) * tsub, LANES * pr:LANES * (pr + 1)] = jnp.where(
                low, outs[0], outs[1]).astype(BF16)

    _staggered(items, stage_scores, stage_exp, stage_pv)


def _band(q, k, v, bias, sinks, *, tq, left, npairs, pairs_per_kv, name):
    b, s, wq = q.shape
    tsub, win = bias.shape[-2:]
    wblk = npairs * LANES
    groups = wq // wblk
    hpg = 2 * npairs
    use_sinks = sinks is not None
    kern = functools.partial(_band_kernel, tq=tq, left=left, npairs=npairs,
                             heads_per_group=hpg, use_sinks=use_sinks, pairs_per_kv=pairs_per_kv)
    kvw = wblk // pairs_per_kv
    resident = lambda shape, imap: pl.BlockSpec(shape, imap, pipeline_mode=pl.Buffered(1))
    in_specs = [pl.BlockSpec((1, tq, wblk), lambda bi, g, i: (bi, i, g)),
                resident((1, s, kvw), lambda bi, g, i: (bi, 0, g)),
                resident((1, s, kvw), lambda bi, g, i: (bi, 0, g)),
                resident((bias.shape[0], hpg, tsub, win), lambda bi, g, i: (0, g, 0, 0))]
    args = [q, k, v, bias]
    if use_sinks:
        in_specs = [pl.BlockSpec(memory_space=pltpu.SMEM)] + in_specs
        args = [sinks] + args
    return pl.pallas_call(
        kern,
        grid=(b, groups, s // tq),
        in_specs=in_specs,
        out_specs=pl.BlockSpec((1, tq, wblk), lambda bi, g, i: (bi, i, g)),
        out_shape=jax.ShapeDtypeStruct((b, s, wq), BF16),
        compiler_params=_params(("arbitrary", "arbitrary", "arbitrary")),
        name=name,
    )(*args)


def _swa_proj(y, g_ref, w_ref, q_ref, k_ref, v_ref, kt_ref, vt_ref, tail):
    h = _rmsnorm(y, g_ref[...]).astype(BF16)
    wq = H_C * HEAD_DIM
    tm = h.shape[0]
    lane = lax.broadcasted_iota(jnp.int32, (tm, LANES), 1)
    low = lane < HEAD_DIM
    kv = _dot(h, w_ref[:, wq:])
    for src, dst, tail_ref in ((kv[:, :LANES], k_ref, kt_ref), (kv[:, LANES:], v_ref, vt_ref)):
        rolled = pltpu.roll(src, HEAD_DIM, 1)
        dst[:, :LANES] = jnp.where(low, src, rolled).astype(BF16)
        dst[:, LANES:] = jnp.where(low, rolled, src).astype(BF16)
        tail_ref[0] = src[tm - tail:, :]
    half = wq // 2
    for c in range(2):
        q_ref[:, half * c:half * (c + 1)] = (
            _dot(h, w_ref[:, half * c:half * (c + 1)]) * (SCALE * LOG2E)).astype(BF16)


def _out_ffn_kernel(*refs, n_attn, d_ff, final_norm, swa_tail):
    x_ref = refs[0]
    attn = refs[1:1 + 2 * n_attn]
    rest = refs[1 + 2 * n_attn:]
    gffn_ref, win_ref, wout_ref, gfin_ref = rest[:4]
    if swa_tail is None:
        (o_ref,) = rest[4:]
    else:
        g2_ref, w2_ref, o_ref = rest[4:7]
    y = x_ref[...]
    for a in range(n_attn):
        y = y + _dot(attn[2 * a][...], attn[2 * a + 1][...])
    h = _rmsnorm(y, gffn_ref[...]).astype(BF16)
    gu = _dot(h, win_ref[...])
    gate = gu[:, :d_ff]
    up = gu[:, d_ff:]
    act = (gate * (1.0 / (1.0 + jnp.exp(-gate))) * up).astype(BF16)
    y = y + _dot(act, wout_ref[...])
    o_ref[...] = _rmsnorm(y, gfin_ref[...]) if final_norm else y
    if swa_tail is not None:
        _swa_proj(y, g2_ref, w2_ref, *rest[7:], swa_tail)


def _out_ffn(x, attn_pairs, g_ffn, w_in, w_out, g_fin, tm, layer, final_norm, swa=None):
    n, d = x.shape
    d_ff = w_out.shape[1]
    row = lambda w: pl.BlockSpec((tm, w), lambda i: (i, 0))
    slab = lambda w: pl.BlockSpec((None,) + w.shape[1:], lambda i: (layer, 0, 0), pipeline_mode=pl.Buffered(1))
    args, specs = [x], [row(d)]
    for o, w in attn_pairs:
        args += [o, w]
        specs += [row(o.shape[1]), _const_spec(w.shape)]
    args += [g_ffn, w_in, w_out, g_fin]
    specs += [_const_spec((1, d)), slab(w_in), slab(w_out), _const_spec((1, d))]
    out_specs, out_shape, swa_tail = row(d), jax.ShapeDtypeStruct((n, d), F32), None
    if swa is not None:
        g2, w2, seq, swa_tail = swa
        assert seq % tm == 0 and swa_tail <= tm
        args += [g2, w2]
        specs += [_const_spec((1, d)), _const_spec(w2.shape)]
        wq = H_C * HEAD_DIM
        tail = pl.BlockSpec((1, swa_tail, LANES), lambda i: (i // (seq // tm), 0, 0))
        out_specs = (out_specs, row(wq), row(2 * LANES), row(2 * LANES), tail, tail)
        out_shape = (out_shape, jax.ShapeDtypeStruct((n, wq), BF16),
                     jax.ShapeDtypeStruct((n, 2 * LANES), BF16), jax.ShapeDtypeStruct((n, 2 * LANES), BF16),
                     jax.ShapeDtypeStruct((n // seq, swa_tail, LANES), F32),
                     jax.ShapeDtypeStruct((n // seq, swa_tail, LANES), F32))
    return pl.pallas_call(
        functools.partial(_out_ffn_kernel, n_attn=len(attn_pairs), d_ff=d_ff, final_norm=final_norm,
                          swa_tail=swa_tail),
        grid=(n // tm,),
        in_specs=specs,
        out_specs=out_specs,
        out_shape=out_shape,
        compiler_params=_params(("arbitrary",)),
        name="out_ffn",
    )(*args)


def _sample_proj_kernel(*refs, gated):
    if gated:
        x_ref, g_ref, w_ref, wf_ref, bf_ref, p_ref, logf_ref = refs
    else:
        x_ref, g_ref, w_ref, p_ref = refs
    h = _rmsnorm(x_ref[...], g_ref[...]).astype(BF16)
    p_ref[...] = _dot(h, w_ref[...])
    if gated:
        logf_ref[...] = _log_sigmoid(_dot(h, wf_ref[...]) + bf_ref[...])


def _sample_proj(x, g, w, tm, w_f=None, b_f=None):
    n, d = x.shape
    row = lambda w_: pl.BlockSpec((tm, w_), lambda i: (i, 0))
    in_specs = [row(d), _const_spec((1, d)), _const_spec(w.shape)]
    args = [x, g, w]
    out_specs = row(w.shape[1])
    out_shape = jax.ShapeDtypeStruct((n, w.shape[1]), F32)
    if w_f is not None:
        in_specs += [_const_spec(w_f.shape), _const_spec((1, LANES))]
        args += [w_f, b_f]
        out_specs = (out_specs, row(LANES))
        out_shape = (out_shape, jax.ShapeDtypeStruct((n, LANES), F32))
    return pl.pallas_call(
        functools.partial(_sample_proj_kernel, gated=w_f is not None),
        grid=(n // tm,),
        in_specs=in_specs,
        out_specs=out_specs,
        out_shape=out_shape,
        compiler_params=_params(("arbitrary",)),
        name="sample_proj",
    )(*args)


def _block_diag_q(q, nheads, scale):
    t, w = q.shape
    tiled = jnp.concatenate([q] * nheads, axis=0)
    r = lax.broadcasted_iota(jnp.int32, (nheads * t, w), 0)
    c = lax.broadcasted_iota(jnp.int32, (nheads * t, w), 1)
    return jnp.where(r // t == c // HEAD_DIM, tiled * scale, 0.0).astype(BF16)


def _block_diag_extract(o_all, nheads, t):
    w = o_all.shape[1]
    c = lax.broadcasted_iota(jnp.int32, (t, w), 1)
    out = jnp.zeros((t, w), F32)
    for h in range(nheads):
        out = jnp.where(c // HEAD_DIM == h, o_all[h * t:(h + 1) * t, :], out)
    return out


def _heads_major(ref):
    _, nh, hd, nk = ref.shape
    return ref[0].reshape(nh * hd, nk).astype(BF16)


def _pad_rows(x, rows):
    return jnp.concatenate([x, jnp.zeros((rows - x.shape[0], x.shape[1]), x.dtype)], axis=0)


def _sample_even_kernel(qa_ref, kan_ref, van_ref, qb_ref, kbn_ref, vbn_ref, ck_ref, cv_ref, lft_ref,
                        cbk_ref, cbv_ref, e3_ref, biasb_ref, oa_ref, ob_ref,
                        bias_sc, m_sc, acc_sc, *, t, past, kc, nkc):
    c = pl.program_id(1)
    rows = H_A * t
    qbd = _block_diag_q(qa_ref[...], H_A, SCALE)

    @pl.when(c == 0)
    def _():
        cum = _cumsum_lanes(lft_ref[0])
        last = past + t - 1
        suffix = cum[:, last:last + 1] - cum
        hi, mid, lo = _split3(suffix)
        s3 = jnp.concatenate([hi, mid, lo, jnp.zeros((LANES - 3 * H_A, suffix.shape[1]), F32)], axis=0)
        s3 = s3.astype(BF16)
        for ch in range(nkc + 1):
            w = kc if ch < nkc else LANES
            bias_sc[ch, :, :w] = _dot(e3_ref[...], s3[:, ch * kc:ch * kc + w])
        m_sc[...] = jnp.full_like(m_sc, NEG_INF)
        acc_sc[...] = jnp.zeros_like(acc_sc)

    def update(s, pv):
        m = m_sc[...]
        m_new = jnp.maximum(m, jnp.max(s, axis=1, keepdims=True))
        p = jnp.exp(s - m_new)
        alpha = jnp.exp(m - m_new)
        acc = acc_sc[...]
        lsum = alpha * acc[:, W_A:W_A + 1] + jnp.sum(p, axis=1, keepdims=True)
        o = alpha * acc[:, :W_A] + pv(p.astype(BF16))
        acc_sc[:, :W_A] = o
        acc_sc[:, W_A:] = jnp.broadcast_to(lsum, (rows, LANES))
        m_sc[...] = m_new

    s = _dot(qbd, _heads_major(ck_ref)) + bias_sc[c]
    vt = _heads_major(cv_ref)
    update(s, lambda p: _dot_nt(p, vt))

    @pl.when(c == nkc - 1)
    def _():
        kn = _pad_rows(kan_ref[...], LANES).astype(BF16)
        vn = _pad_rows(van_ref[...], LANES).astype(BF16)
        sn = _dot_nt(qbd, kn) + bias_sc[nkc, :, :LANES]
        r = lax.broadcasted_iota(jnp.int32, sn.shape, 0)
        col = lax.broadcasted_iota(jnp.int32, sn.shape, 1)
        sn = jnp.where(col <= r % t, sn, NEG_INF)
        update(sn, lambda p: _dot(p, vn))
        acc = acc_sc[...]
        o_all = acc[:, :W_A] / acc[:, W_A:W_A + 1]
        oa_ref[...] = _block_diag_extract(o_all, H_A, t).astype(BF16)

        qbd_b = _block_diag_q(qb_ref[...], H_B, SCALE)
        lbk = cbk_ref.shape[3]
        kbn = _pad_rows(kbn_ref[...], LANES).astype(BF16)
        vbn = _pad_rows(vbn_ref[...], LANES).astype(BF16)
        sb = jnp.concatenate([_dot(qbd_b, _heads_major(cbk_ref)), _dot_nt(qbd_b, kbn)], axis=1) + biasb_ref[...]
        mb = jnp.max(sb, axis=1, keepdims=True)
        eb = jnp.exp(sb - mb)
        lb = jnp.sum(eb, axis=1, keepdims=True)
        eb = eb.astype(BF16)
        ob_all = (_dot_nt(eb[:, :lbk], _heads_major(cbv_ref)) + _dot(eb[:, lbk:], vbn)) / lb
        ob_ref[...] = _block_diag_extract(ob_all, H_B, t).astype(BF16)


def _sample_even(p, cache_k, cache_v, logf_t, cache_bk, cache_bv, e3, bias_b, *, t, kc):
    n = p.shape[0]
    nb = n // t
    past = cache_k.shape[3]
    nkc = past // kc
    lbk = cache_bk.shape[3]
    pcol = lambda j: pl.BlockSpec((t, W_A), lambda bi, c: (bi, j))
    rows = H_A * t
    return pl.pallas_call(
        functools.partial(_sample_even_kernel, t=t, past=past, kc=kc, nkc=nkc),
        grid=(nb, nkc),
        in_specs=[pcol(0), pcol(1), pcol(2), pcol(3), pcol(4), pcol(5),
                  pl.BlockSpec((1, H_A, HEAD_DIM, kc), lambda bi, c: (bi, 0, 0, c)),
                  pl.BlockSpec((1, H_A, HEAD_DIM, kc), lambda bi, c: (bi, 0, 0, c)),
                  pl.BlockSpec((1, H_A, logf_t.shape[2]), lambda bi, c: (bi, 0, 0)),
                  pl.BlockSpec((1, H_B, HEAD_DIM, lbk), lambda bi, c: (bi, 0, 0, 0)),
                  pl.BlockSpec((1, H_B, HEAD_DIM, lbk), lambda bi, c: (bi, 0, 0, 0)),
                  _const_spec(e3.shape), _const_spec(bias_b.shape)],
        out_specs=(pl.BlockSpec((t, W_A), lambda bi, c: (bi, 0)),
                   pl.BlockSpec((t, W_B), lambda bi, c: (bi, 0))),
        out_shape=(jax.ShapeDtypeStruct((n, W_A), BF16), jax.ShapeDtypeStruct((n, W_B), BF16)),
        scratch_shapes=[pltpu.VMEM((nkc + 1, rows, kc), F32),
                        pltpu.VMEM((rows, 1), F32),
                        pltpu.VMEM((rows, W_A + LANES), F32)],
        compiler_params=_params(("arbitrary", "arbitrary")),
        name="sample_even",
    )(p, p, p, p, p, p, cache_k, cache_v, logf_t, cache_bk, cache_bv, e3, bias_b)


def _sample_odd_kernel(q_ref, kn_ref, vn_ref, ck_ref, cv_ref, x_ref, bias_ref, sink_ref, o_ref, *, t, nbs):
    sk = sink_ref[...]
    for bb in range(nbs):
        rows = slice(bb * t, (bb + 1) * t)
        qbd = _block_diag_q(q_ref[rows, :], H_C, SCALE)
        kall = jnp.concatenate([ck_ref[bb], _pad_rows(kn_ref[rows, :], LANES)], axis=0).astype(BF16)
        vall = jnp.concatenate([cv_ref[bb], _pad_rows(vn_ref[rows, :], LANES)], axis=0).astype(BF16)
        kexp = _dot(kall, x_ref[...]).astype(BF16)
        vexp = _dot(vall, x_ref[...]).astype(BF16)
        s = _dot_nt(qbd, kexp) + bias_ref[...]
        m = jnp.maximum(jnp.max(s, axis=1, keepdims=True), sk)
        e = jnp.exp(s - m)
        l = jnp.sum(e, axis=1, keepdims=True) + jnp.exp(sk - m)
        o_all = _dot(e.astype(BF16), vexp) / l
        o_ref[rows, :] = _block_diag_extract(o_all, H_C, t).astype(BF16)


def _sample_odd(p, cache_k, cache_v, xexp, bias, sink_col, *, t, nbs):
    n = p.shape[0]
    nb = n // t
    wq = H_C * HEAD_DIM
    lc = cache_k.shape[1]
    rows = t * nbs
    return pl.pallas_call(
        functools.partial(_sample_odd_kernel, t=t, nbs=nbs),
        grid=(nb // nbs,),
        in_specs=[pl.BlockSpec((rows, wq), lambda bi: (bi, 0)),
                  pl.BlockSpec((rows, LANES), lambda bi: (bi, wq // LANES)),
                  pl.BlockSpec((rows, LANES), lambda bi: (bi, wq // LANES + 1)),
                  pl.BlockSpec((nbs, lc, LANES), lambda bi: (bi, 0, 0)),
                  pl.BlockSpec((nbs, lc, LANES), lambda bi: (bi, 0, 0)),
                  _const_spec(xexp.shape), _const_spec(bias.shape), _const_spec(sink_col.shape)],
        out_specs=pl.BlockSpec((rows, wq), lambda bi: (bi, 0)),
        out_shape=jax.ShapeDtypeStruct((n, wq), BF16),
        compiler_params=_params(("arbitrary",)),
        name="sample_odd",
    )(p, p, p, cache_k, cache_v, xexp, bias, sink_col)


def _t5_bucket(rel_mem):
    nb = T5_BUCKETS // 2
    max_exact = nb // 2
    n = jnp.abs(rel_mem)
    large = max_exact + (jnp.log(jnp.maximum(n, 1).astype(F32) / max_exact)
                         / math.log(T5_MAX_DIST / max_exact) * (nb - max_exact)).astype(jnp.int32)
    large = jnp.minimum(large, nb - 1)
    return jnp.where(rel_mem > 0, nb, 0) + jnp.where(n < max_exact, n, large)


def _bias_b_of_rel(table, rel):
    idx = np.clip(rel, -B_REL_CLIP, B_REL_CLIP) + B_REL_CLIP
    return table.astype(F32)[idx].T


def _bias_c_of_rel(table, rel):
    return table.astype(F32)[_t5_bucket(-jnp.asarray(rel, jnp.int32))].T


def _bias_tiles_kernel(f_ref, o_ref, *, tq, left, offsets, left_chunks, valid_cols):
    n = f_ref.shape[2]
    win = o_ref.shape[3]
    rows = jnp.broadcast_to(f_ref[0], (tq, n))
    wide = pltpu.roll(rows, n - (tq - 1), 1, stride=1, stride_axis=0)
    q = lax.broadcasted_iota(jnp.int32, (tq, win), 0)
    k = lax.broadcasted_iota(jnp.int32, (tq, win), 1)
    shift = CHUNK.bit_length() - 1
    for v, off in enumerate(offsets):
        if valid_cols is None:
            d = off // CHUNK + jnp.right_shift(q, shift) - jnp.right_shift(k, shift)
            ok = (d >= 0) & (d <= left_chunks)
        else:
            ok = k < valid_cols
        o_ref[v, 0] = jnp.where(ok, wide[:, left - off:left - off + win], NEG_INF)


def _bias_tiles(bias_of_rel, *, tq, left, win, offsets, left_chunks=None, valid_cols=None):
    assert CHUNK & (CHUNK - 1) == 0 and all((left - o) % LANES == 0 for o in offsets)
    ncols = win + left - min(offsets)
    n = tq + ncols - 1
    n_pad = -(-n // LANES) * LANES
    f_rev = bias_of_rel(tq - 1 + left - np.arange(n))
    nheads = f_rev.shape[0]
    f_rev = jnp.pad(f_rev, ((0, 0), (0, n_pad - n)))[:, None, :]
    return pl.pallas_call(
        functools.partial(_bias_tiles_kernel, tq=tq, left=left, offsets=tuple(offsets),
                          left_chunks=left_chunks, valid_cols=valid_cols),
        grid=(nheads,),
        in_specs=[pl.BlockSpec((1, 1, n_pad), lambda h: (h, 0, 0))],
        out_specs=pl.BlockSpec((len(offsets), 1, tq, win), lambda h: (0, h, 0, 0)),
        out_shape=jax.ShapeDtypeStruct((len(offsets), nheads, tq, win), F32),
        compiler_params=_params(("arbitrary",)),
        name="bias_tiles",
    )(f_rev)


def _sample_bias(bias_of_rel, t, cache_len):
    ncols = cache_len + LANES
    tiles = _bias_tiles(bias_of_rel, tq=t, left=cache_len, win=ncols, offsets=[cache_len],
                        valid_cols=cache_len + t)
    return tiles.reshape(-1, ncols)


def _cast_kernel(x_ref, o_ref):
    o_ref[...] = x_ref[...].astype(o_ref.dtype)


def _cast_bf16(w, rows):
    nl, r, c = w.shape
    w2 = w.reshape(nl * r, c)
    out = pl.pallas_call(
        _cast_kernel,
        grid=(nl * r // rows,),
        in_specs=[pl.BlockSpec((rows, c), lambda i: (i, 0))],
        out_specs=pl.BlockSpec((rows, c), lambda i: (i, 0)),
        out_shape=jax.ShapeDtypeStruct(w2.shape, BF16),
        compiler_params=_params(("arbitrary",)),
        name="cast_bf16",
    )(w2)
    return out.reshape(nl, r, c)


def _placement():
    pq = np.zeros((LANES, H_A * LANES), np.float32)
    pk = np.zeros((LANES, H_A * LANES), np.float32)
    for h in range(H_A):
        for j in range(3):
            pq[8 * j + h, LANES * h + HEAD_DIM + j] = 1.0
            pq[24, LANES * h + HEAD_DIM + 3 + j] = 1.0
            pk[24, LANES * h + HEAD_DIM + j] = 1.0
            pk[8 * j + h, LANES * h + HEAD_DIM + 3 + j] = -1.0
    return jnp.asarray(pq, BF16), jnp.asarray(pk, BF16)


def kernel(x_prompt, x_sample, cache_a_k, cache_a_v, cache_a_logf, cache_b_k, cache_b_v, cache_c_k, cache_c_v,
           norm_mix, norm_ffn, norm_final, w_in_even, b_forget, rel_bias_b, w_out_even, w_in_odd, sinks_c,
           w_out_odd, t5_bias, w_ffn_in, w_ffn_out):
    b, s, d = x_prompt.shape
    nb, t, _ = x_sample.shape
    past = cache_a_k.shape[2]
    n_p, n_s = b * s, nb * t

    w_even = w_in_even[0]
    w_main = w_even[:, :3 * W_A + 3 * W_B].astype(BF16)
    wf = w_even[:, 3 * W_A + 3 * W_B:]
    w_f = jnp.concatenate([wf, wf, wf, jnp.zeros((d, LANES - 3 * H_A), F32)], axis=1).astype(BF16)
    bf = b_forget[0].astype(F32)
    b_f = jnp.concatenate([bf, bf, bf, jnp.zeros((LANES - 3 * H_A,), F32)])[None, :]
    pq, pk = _placement()
    w_oe = w_out_even[0].astype(BF16)
    w_oo = w_out_odd[0].astype(BF16)
    w_odd = w_in_odd[0].astype(BF16)
    w_fi = _cast_bf16(w_ffn_in, rows=256)
    w_fo = _cast_bf16(w_ffn_out, rows=704)
    g_mix = norm_mix.astype(F32)[:, None, :]
    g_ffn = norm_ffn.astype(F32)[:, None, :]
    g_fin = norm_final.astype(F32)[None, :]

    bias_b_of = functools.partial(_bias_b_of_rel, rel_bias_b[0])
    bias_c_of = functools.partial(_bias_c_of_rel, t5_bias)
    log2_domain = lambda bias_of: (lambda rel: bias_of(rel) * LOG2E)
    variant_offsets = lambda tsub, left: [min(v * tsub, left) for v in range(left // tsub + 1)]
    tq_b, tsub_b, left_b = 512, 256, B_LEFT_CHUNKS * CHUNK
    bias_b = _bias_tiles(log2_domain(bias_b_of), tq=tsub_b, left=left_b, win=tsub_b + left_b,
                         offsets=variant_offsets(tsub_b, left_b), left_chunks=B_LEFT_CHUNKS)
    tq_c, tsub_c, left_c = 256, 128, C_LEFT_CHUNKS * CHUNK
    bias_c = _bias_tiles(log2_domain(bias_c_of), tq=tsub_c, left=left_c, win=tsub_c + left_c,
                         offsets=variant_offsets(tsub_c, left_c), left_chunks=C_LEFT_CHUNKS)

    xp = x_prompt
    w_all = jnp.concatenate([w_main[:, 3 * W_A:3 * W_A + W_B], w_f, w_main[:, :3 * W_A],
                             w_main[:, 3 * W_A + W_B:]], axis=1)
    qaug, kaug, vaug, ka, va, logf, qb, kb, vb, kbt, vbt = _even_proj(
        xp, g_mix[0], w_all, b_f, pq, pk, tm=512)
    oa = _fox(qaug, kaug, vaug, blk=512, nsub=2)
    ob = _band(qb, kb, vb, bias_b, None, tq=tq_b, left=left_b, npairs=H_B // 2, pairs_per_kv=1,
               name="band_b")
    lc = min(C_LEFT_CHUNKS * CHUNK, s)
    xp1, qc, kcd, vcd, kct, vct = _out_ffn(
        xp.reshape(n_p, d), [(oa.reshape(n_p, W_A), w_oe[:W_A]), (ob.reshape(n_p, W_B), w_oe[W_A:])],
        g_ffn[0], w_fi, w_fo, g_fin, tm=512, layer=0, final_norm=False, swa=(g_mix[1], w_odd, s, lc))
    qc, kcd, vcd = (a.reshape(b, s, -1) for a in (qc, kcd, vcd))
    oc = _band(qc, kcd, vcd, bias_c, sinks_c[0].astype(F32) * LOG2E, tq=tq_c, left=left_c, npairs=H_C // 2,
               pairs_per_kv=G_C // 2, name="band_c")
    y_prompt = _out_ffn(xp1, [(oc.reshape(n_p, H_C * HEAD_DIM), w_oo)],
                        g_ffn[1], w_fi, w_fo, g_fin, tm=512, layer=1, final_norm=True)

    xs = x_sample.reshape(n_s, d)
    ps, logf_sp = _sample_proj(xs, g_mix[0], w_main, tm=256, w_f=w_f, b_f=b_f)
    logf_s = logf_sp[:, :H_A]
    kpad = LANES * -(-(past + t) // LANES)
    logf_all = jnp.concatenate([cache_a_logf[0].astype(F32), logf_s.reshape(nb, t, H_A)], axis=1)
    logf_t = jnp.pad(jnp.swapaxes(logf_all, 1, 2), ((0, 0), (0, 0), (0, kpad - past - t)))
    lbs = cache_b_k.shape[2]
    bias_sb = _sample_bias(bias_b_of, t, lbs)
    e3 = np.zeros((H_A * t, LANES), np.float32)
    for j in range(3):
        e3[np.arange(H_A * t), 8 * j + np.arange(H_A * t) // t] = 1.0
    keys_minor = lambda cache: jnp.transpose(cache[0], (0, 2, 3, 1))
    oa_s, ob_s = _sample_even(ps, keys_minor(cache_a_k), keys_minor(cache_a_v), logf_t,
                              keys_minor(cache_b_k), keys_minor(cache_b_v),
                              jnp.asarray(e3, BF16), bias_sb, t=t, kc=4096)
    xs1 = _out_ffn(xs, [(oa_s, w_oe[:W_A]), (ob_s, w_oe[W_A:])], g_ffn[0], w_fi, w_fo, g_fin,
                   tm=256, layer=0, final_norm=False)

    ps2 = _sample_proj(xs1, g_mix[1], w_odd, tm=256)
    lcs = cache_c_k.shape[2]
    bias_sc = _sample_bias(bias_c_of, t, lcs)
    sink_col = jnp.repeat(sinks_c[0].astype(F32), t)[:, None]
    lane_head = np.arange(H_C * HEAD_DIM) // HEAD_DIM
    src_lane = (lane_head // G_C) * HEAD_DIM + np.arange(H_C * HEAD_DIM) % HEAD_DIM
    xexp = jnp.asarray(np.arange(LANES)[:, None] == src_lane[None, :], BF16)
    oc_s = _sample_odd(ps2, cache_c_k[0].reshape(nb, lcs, LANES), cache_c_v[0].reshape(nb, lcs, LANES),
                       xexp, bias_sc, sink_col, t=t, nbs=4)
    y_sample = _out_ffn(xs1, [(oc_s, w_oo)], g_ffn[1], w_fi, w_fo, g_fin, tm=256, layer=1, final_norm=True)

    wq = H_C * HEAD_DIM
    hd = lambda a, lead, h: a.reshape((1,) + lead + (h, HEAD_DIM))
    return (
        y_prompt.reshape(b, s, d), y_sample.reshape(nb, t, d),
        hd(ka, (b, s), H_A), hd(va, (b, s), H_A), logf[None],
        hd(kbt, (b, kbt.shape[1]), H_B), hd(vbt, (b, vbt.shape[1]), H_B),
        hd(kct, (b, kct.shape[1]), HKV_C), hd(vct, (b, vct.shape[1]), HKV_C),
        hd(ps[:, W_A:2 * W_A], (nb, t), H_A), hd(ps[:, 2 * W_A:3 * W_A], (nb, t), H_A),
        logf_s.reshape(1, nb, t, H_A),
        hd(ps[:, 3 * W_A + W_B:3 * W_A + 2 * W_B], (nb, t), H_B),
        hd(ps[:, 3 * W_A + 2 * W_B:3 * W_A + 3 * W_B], (nb, t), H_B),
        hd(ps2[:, wq:wq + LANES], (nb, t), HKV_C), hd(ps2[:, wq + LANES:wq + 2 * LANES], (nb, t), HKV_C),
    )
```

```python
import functools
import math

import jax
import jax.numpy as jnp
import numpy as np
from jax import lax
from jax.experimental import pallas as pl
from jax.experimental.pallas import tpu as pltpu

D_MODEL = 1024
HEAD_DIM = 64
CHUNK = 64
H_A = 8
H_B = 8
B_LEFT_CHUNKS = 8
B_REL_CLIP = 128
H_C = 16
HKV_C = 2
G_C = H_C // HKV_C
WINDOW = 128
C_LEFT_CHUNKS = WINDOW // CHUNK
T5_BUCKETS = 32
T5_MAX_DIST = 128
EPS = 1e-6
W_A = H_A * HEAD_DIM
W_B = H_B * HEAD_DIM
SCALE = HEAD_DIM ** -0.5
LOG2E = math.log2(math.e)

LANES = 128
VMEM_LIMIT = 60 * 1024 * 1024

F32 = jnp.float32
BF16 = jnp.bfloat16
NEG_INF = float("-inf")

_NT = (((1,), (1,)), ((), ()))


def _dot(a, b):
    return jnp.dot(a, b, preferred_element_type=F32)


def _dot_nt(a, b):
    return lax.dot_general(a, b, _NT, preferred_element_type=F32)


def _rmsnorm(x, g):
    ms = jnp.mean(x * x, axis=-1, keepdims=True)
    return x * lax.rsqrt(ms + EPS) * g


def _log_sigmoid(x):
    return jnp.minimum(x, 0.0) - jnp.log1p(jnp.exp(-jnp.abs(x)))


def _split3(x):
    hi = x.astype(BF16).astype(F32)
    r1 = x - hi
    mid = r1.astype(BF16).astype(F32)
    lo = (r1 - mid).astype(BF16).astype(F32)
    return hi, mid, lo


def _cumsum_rows(x):
    n = x.shape[0]
    row = lax.broadcasted_iota(jnp.int32, x.shape, 0)
    s = 1
    while s < n:
        x = x + jnp.where(row >= s, pltpu.roll(x, s, 0), 0.0)
        s *= 2
    return x


def _cumsum_lanes(x):
    n = x.shape[1]
    col = lax.broadcasted_iota(jnp.int32, x.shape, 1)
    s = 1
    while s < n:
        x = x + jnp.where(col >= s, pltpu.roll(x, s, 1), 0.0)
        s *= 2
    return x


def _staggered(items, stage_a, stage_b, stage_c):
    n = len(items)
    a_out, b_out = {}, {}
    for step in range(n + 2):
        if step < n:
            a_out[step] = stage_a(items[step])
        if 0 <= step - 1 < n:
            b_out[step - 1] = stage_b(items[step - 1], a_out.pop(step - 1))
        if 0 <= step - 2 < n:
            stage_c(items[step - 2], b_out.pop(step - 2))


def _const_spec(shape):
    nd = len(shape)
    return pl.BlockSpec(shape, lambda *_: (0,) * nd, pipeline_mode=pl.Buffered(1))


def _params(sem):
    return pltpu.CompilerParams(dimension_semantics=sem, vmem_limit_bytes=VMEM_LIMIT)


def _even_proj_kernel(x_ref, g_ref, w_ref, bf_ref, pq_ref, pk_ref,
                      qaug_ref, kaug_ref, vaug_ref, ka_ref, va_ref, logf_ref,
                      qb_ref, kb_ref, vb_ref, kbt_ref, vbt_ref, carry_ref):
    @pl.when(pl.program_id(1) == 0)
    def _():
        carry_ref[...] = jnp.zeros_like(carry_ref)

    h = _rmsnorm(x_ref[0], g_ref[...]).astype(BF16)
    tm = h.shape[0]
    lane = lax.broadcasted_iota(jnp.int32, (tm, LANES), 1)
    low = lane < HEAD_DIM
    first = W_B + LANES

    def chunk(n):
        return _dot(h, w_ref[:, first + W_A * n:first + W_A * (n + 1)])

    pf = _dot(h, w_ref[:, :first])
    qb_ref[0] = (pf[:, :W_B] * (SCALE * LOG2E)).astype(BF16)
    fa = pf[:, W_B:]
    pc = chunk(3)
    kb_ref[0] = pc.astype(BF16)
    kbt_ref[0] = pc
    pc = chunk(4)
    vb_ref[0] = pc.astype(BF16)
    vbt_ref[0] = pc
    logf = _log_sigmoid(fa + bf_ref[...])
    logf_ref[0] = logf[:, :H_A]
    c = _cumsum_rows(logf) + carry_ref[...]
    carry_ref[...] = c[tm - 1:tm, :]
    hi, mid, lo = _split3(c * LOG2E)
    a3 = jnp.where(lane < 8, hi, jnp.where(lane < 16, mid, jnp.where(lane < 24, lo,
                   jnp.where(lane == 24, 1.0, 0.0)))).astype(BF16)
    augq = _dot(a3, pq_ref[...])
    augk = _dot(a3, pk_ref[...])
    vone = jnp.where(lane == HEAD_DIM, 1.0, 0.0)

    def per_head(dst_ref, pc, spare):
        for j in range(H_A // 2):
            pair = pc[:, LANES * j:LANES * (j + 1)]
            for hh, val in enumerate((pair, pltpu.roll(pair, HEAD_DIM, 1))):
                o = slice(LANES * (2 * j + hh), LANES * (2 * j + hh + 1))
                dst_ref[0, :, o] = jnp.where(low, val, vone if spare is None else spare[:, o]).astype(BF16)

    pc = chunk(0)
    per_head(qaug_ref, pc * (SCALE * LOG2E), augq)
    pc = chunk(1)
    ka_ref[0] = pc
    per_head(kaug_ref, pc, augk)
    pc = chunk(2)
    va_ref[0] = pc
    per_head(vaug_ref, pc, None)


def _even_proj(x, g, w_all, b_f, pq, pk, tm):
    b, s, d = x.shape
    nt = s // tm
    lb = min(B_LEFT_CHUNKS * CHUNK, s)
    assert lb == tm, "band-state tail must be exactly one row tile"
    row = lambda w: pl.BlockSpec((1, tm, w), lambda bi, ti: (bi, ti, 0))
    tail = pl.BlockSpec((1, lb, W_B), lambda bi, ti: (bi, 0, 0))
    outs = (
        jax.ShapeDtypeStruct((b, s, H_A * LANES), BF16),
        jax.ShapeDtypeStruct((b, s, H_A * LANES), BF16),
        jax.ShapeDtypeStruct((b, s, H_A * LANES), BF16),
        jax.ShapeDtypeStruct((b, s, W_A), F32),
        jax.ShapeDtypeStruct((b, s, W_A), F32),
        jax.ShapeDtypeStruct((b, s, H_A), F32),
        jax.ShapeDtypeStruct((b, s, W_B), BF16),
        jax.ShapeDtypeStruct((b, s, W_B), BF16),
        jax.ShapeDtypeStruct((b, s, W_B), BF16),
        jax.ShapeDtypeStruct((b, lb, W_B), F32),
        jax.ShapeDtypeStruct((b, lb, W_B), F32),
    )
    return pl.pallas_call(
        _even_proj_kernel,
        grid=(b, nt),
        in_specs=[row(d), _const_spec((1, d)), _const_spec(w_all.shape),
                  _const_spec((1, LANES)), _const_spec(pq.shape), _const_spec(pk.shape)],
        out_specs=(row(H_A * LANES), row(H_A * LANES), row(H_A * LANES), row(W_A), row(W_A), row(H_A),
                   row(W_B), row(W_B), row(W_B), tail, tail),
        out_shape=outs,
        scratch_shapes=[pltpu.VMEM((1, LANES), F32)],
        compiler_params=_params(("arbitrary", "arbitrary")),
        name="even_proj",
    )(x, g, w_all, b_f, pq, pk)


def _fox_kernel(q_ref, k_ref, v_ref, o_ref, m_sc, acc_sc, *, blk, nsub):
    i = pl.program_id(2)
    nh = 2
    chains = [(sub, hh) for sub in range(nsub) for hh in range(nh)]
    qs = {(sub, hh): q_ref[0, sub * blk:(sub + 1) * blk, LANES * hh:LANES * (hh + 1)] for sub, hh in chains}
    r = lax.broadcasted_iota(jnp.int32, (blk, blk), 0)
    c = lax.broadcasted_iota(jnp.int32, (blk, blk), 1)
    causal = c <= r
    m_sc[...] = jnp.full_like(m_sc, NEG_INF)
    acc_sc[...] = jnp.zeros_like(acc_sc)

    def blocks(specs):
        items = [(pl.multiple_of(j * blk, blk), ch, modes[ch[0]])
                 for j, modes in specs for ch in chains if modes[ch[0]] is not None]

        def stage_scores(item):
            start, ch, mode = item
            k = k_ref[0, pl.ds(start, blk), LANES * ch[1]:LANES * (ch[1] + 1)]
            s = _dot_nt(qs[ch], k)
            if mode == "masked":
                s = jnp.where(causal, s, NEG_INF)
            m = m_sc[chains.index(ch)]
            m_new = jnp.maximum(m, jnp.max(s, axis=1, keepdims=True))
            m_sc[chains.index(ch)] = m_new
            return s, m, m_new

        def stage_exp(item, sm):
            s, m, m_new = sm
            p = jnp.concatenate([jnp.exp2(s[:, LANES * cb:LANES * (cb + 1)] - m_new)
                                 for cb in range(blk // LANES)], axis=1)
            return p.astype(BF16), jnp.exp2(m - m_new)

        def stage_pv(item, pa):
            start, ch, _ = item
            p, alpha = pa
            v = v_ref[0, pl.ds(start, blk), LANES * ch[1]:LANES * (ch[1] + 1)]
            n = chains.index(ch)
            acc_sc[n] = alpha * acc_sc[n] + _dot(p, v)

        _staggered(items, stage_scores, stage_exp, stage_pv)

    @pl.loop(0, i)
    def _(jj):
        blocks([(nsub * jj + d, ["full"] * nsub) for d in range(nsub)])

    blocks([(nsub * i + d, [None if sub < d else ("masked" if sub == d else "full") for sub in range(nsub)])
            for d in range(nsub)])

    lane = lax.broadcasted_iota(jnp.int32, (blk, LANES), 1)
    for sub in range(nsub):
        a0 = acc_sc[chains.index((sub, 0))]
        a1 = acc_sc[chains.index((sub, 1))]
        o0 = a0 / a0[:, HEAD_DIM:HEAD_DIM + 1]
        o1 = a1 / a1[:, HEAD_DIM:HEAD_DIM + 1]
        o_ref[0, sub * blk:(sub + 1) * blk, :] = jnp.where(
            lane < HEAD_DIM, o0, pltpu.roll(o1, HEAD_DIM, 1)).astype(BF16)


def _fox(qaug, kaug, vaug, blk, nsub):
    b, s, _ = qaug.shape
    pairs = H_A // 2
    tq = blk * nsub
    return pl.pallas_call(
        functools.partial(_fox_kernel, blk=blk, nsub=nsub),
        grid=(b, pairs, s // tq),
        in_specs=[pl.BlockSpec((1, tq, 2 * LANES), lambda bi, hp, i: (bi, i, hp)),
                  pl.BlockSpec((1, s, 2 * LANES), lambda bi, hp, i: (bi, 0, hp)),
                  pl.BlockSpec((1, s, 2 * LANES), lambda bi, hp, i: (bi, 0, hp))],
        out_specs=pl.BlockSpec((1, tq, LANES), lambda bi, hp, i: (bi, i, hp)),
        out_shape=jax.ShapeDtypeStruct((b, s, W_A), BF16),
        scratch_shapes=[pltpu.VMEM((2 * nsub, blk, LANES), F32), pltpu.VMEM((2 * nsub, blk, LANES), F32)],
        compiler_params=_params(("arbitrary", "arbitrary", "arbitrary")),
        name="fox",
    )(qaug, kaug, vaug)


def _band_kernel(*refs, tq, left, npairs, heads_per_group, use_sinks, pairs_per_kv):
    mxu_rowsum = not use_sinks
    if use_sinks:
        sink_ref, q_ref, k_ref, v_ref, bias_ref, o_ref = refs
    else:
        q_ref, k_ref, v_ref, bias_ref, o_ref = refs
    g = pl.program_id(1)
    i = pl.program_id(2)
    nvar, _, tsub, win = bias_ref.shape
    nsq = tq // tsub
    lane = lax.broadcasted_iota(jnp.int32, (tsub, LANES), 1)
    low = lane < HEAD_DIM
    items = [(sq, pr, hh) for sq in range(nsq) for pr in range(npairs) for hh in range(2)]

    def window(sq):
        gsub = i * nsq + sq
        return pl.multiple_of(jnp.maximum(gsub * tsub - left, 0), LANES), jnp.minimum(gsub, nvar - 1)

    def kv_lanes(pr):
        kl = LANES * (pr // pairs_per_kv)
        return slice(kl, kl + LANES)

    def sink(pr, hh):
        return sink_ref[g * heads_per_group + 2 * pr + hh]

    def stage_scores(item):
        sq, pr, hh = item
        start, var = window(sq)
        qp = q_ref[0, sq * tsub:(sq + 1) * tsub, LANES * pr:LANES * (pr + 1)]
        qm = jnp.where(low if hh == 0 else jnp.logical_not(low), qp, jnp.zeros_like(qp))
        s = _dot_nt(qm, k_ref[0, pl.ds(start, win), kv_lanes(pr)]) + bias_ref[var, 2 * pr + hh]
        m = jnp.max(s, axis=1, keepdims=True)
        return s, (jnp.maximum(m, sink(pr, hh)) if use_sinks else m)

    def stage_exp(item, sm):
        s, m = sm
        e = jnp.exp2(s - m)
        l = None if mxu_rowsum else jnp.sum(e, axis=1, keepdims=True)
        if use_sinks:
            l = l + jnp.exp2(sink(item[1], item[2]) - m)
        return e.astype(BF16), l

    outs = {}

    def stage_pv(item, pe):
        p, l = pe
        sq, pr, hh = item
        v = v_ref[0, pl.ds(window(sq)[0], win), kv_lanes(pr)]
        if mxu_rowsum:
            ov = _dot(p, jnp.concatenate([v, jnp.ones((win, LANES), BF16)], axis=1))
            outs[hh] = ov[:, :LANES] / ov[:, LANES:]
        else:
            outs[hh] = _dot(p, v) / l
        if hh == 1:
            o_ref[0, sq * tsub:(sq + 1) * tsub, LANES * pr:LANES * (pr + 1)] = jnp.where(
                low, outs[0], outs[1]).astype(BF16)

    _staggered(items, stage_scores, stage_exp, stage_pv)


def _band(q, k, v, bias, sinks, *, tq, left, npairs, pairs_per_kv, name):
    b, s, wq = q.shape
    tsub, win = bias.shape[-2:]
    wblk = npairs * LANES
    groups = wq // wblk
    hpg = 2 * npairs
    use_sinks = sinks is not None
    kern = functools.partial(_band_kernel, tq=tq, left=left, npairs=npairs,
                             heads_per_group=hpg, use_sinks=use_sinks, pairs_per_kv=pairs_per_kv)
    kvw = wblk // pairs_per_kv
    resident = lambda shape, imap: pl.BlockSpec(shape, imap, pipeline_mode=pl.Buffered(1))
    in_specs = [pl.BlockSpec((1, tq, wblk), lambda bi, g, i: (bi, i, g)),
                resident((1, s, kvw), lambda bi, g, i: (bi, 0, g)),
                resident((1, s, kvw), lambda bi, g, i: (bi, 0, g)),
                resident((bias.shape[0], hpg, tsub, win), lambda bi, g, i: (0, g, 0, 0))]
    args = [q, k, v, bias]
    if use_sinks:
        in_specs = [pl.BlockSpec(memory_space=pltpu.SMEM)] + in_specs
        args = [sinks] + args
    return pl.pallas_call(
        kern,
        grid=(b, groups, s // tq),
        in_specs=in_specs,
        out_specs=pl.BlockSpec((1, tq, wblk), lambda bi, g, i: (bi, i, g)),
        out_shape=jax.ShapeDtypeStruct((b, s, wq), BF16),
        compiler_params=_params(("arbitrary", "arbitrary", "arbitrary")),
        name=name,
    )(*args)


def _out_ffn_kernel(*refs, n_attn, d_ff, final_norm):
    x_ref = refs[0]
    attn = refs[1:1 + 2 * n_attn]
    gffn_ref, win_ref, wout_ref, gfin_ref, o_ref = refs[1 + 2 * n_attn:]
    y = x_ref[...]
    for a in range(n_attn):
        y = y + _dot(attn[2 * a][...], attn[2 * a + 1][...])
    h = _rmsnorm(y, gffn_ref[...]).astype(BF16)
    gu = _dot(h, win_ref[...])
    gate = gu[:, :d_ff]
    up = gu[:, d_ff:]
    act = (gate * (1.0 / (1.0 + jnp.exp(-gate))) * up).astype(BF16)
    y = y + _dot(act, wout_ref[...])
    o_ref[...] = _rmsnorm(y, gfin_ref[...]) if final_norm else y


def _out_ffn(x, attn_pairs, g_ffn, w_in, w_out, g_fin, tm, layer, final_norm):
    n, d = x.shape
    d_ff = w_out.shape[1]
    row = lambda w: pl.BlockSpec((tm, w), lambda i: (i, 0))
    slab = lambda w: pl.BlockSpec((None,) + w.shape[1:], lambda i: (layer, 0, 0), pipeline_mode=pl.Buffered(1))
    args, specs = [x], [row(d)]
    for o, w in attn_pairs:
        args += [o, w]
        specs += [row(o.shape[1]), _const_spec(w.shape)]
    args += [g_ffn, w_in, w_out, g_fin]
    specs += [_const_spec((1, d)), slab(w_in), slab(w_out), _const_spec((1, d))]
    return pl.pallas_call(
        functools.partial(_out_ffn_kernel, n_attn=len(attn_pairs), d_ff=d_ff, final_norm=final_norm),
        grid=(n // tm,),
        in_specs=specs,
        out_specs=row(d),
        out_shape=jax.ShapeDtypeStruct((n, d), F32),
        compiler_params=_params(("arbitrary",)),
        name="out_ffn",
    )(*args)


def _odd_proj_kernel(x_ref, g_ref, w_ref, q_ref, k_ref, v_ref, kt_ref, vt_ref, *, tail):
    h = _rmsnorm(x_ref[0], g_ref[...]).astype(BF16)
    wq = H_C * HEAD_DIM
    tm = h.shape[0]
    lane = lax.broadcasted_iota(jnp.int32, (tm, LANES), 1)
    low = lane < HEAD_DIM
    kv = _dot(h, w_ref[:, wq:])
    for src, dst, tail_ref in ((kv[:, :LANES], k_ref, kt_ref), (kv[:, LANES:], v_ref, vt_ref)):
        rolled = pltpu.roll(src, HEAD_DIM, 1)
        dst[0, :, :LANES] = jnp.where(low, src, rolled).astype(BF16)
        dst[0, :, LANES:] = jnp.where(low, rolled, src).astype(BF16)
        tail_ref[0] = src[tm - tail:, :]
    half = wq // 2
    for c in range(2):
        q_ref[0, :, half * c:half * (c + 1)] = (
            _dot(h, w_ref[:, half * c:half * (c + 1)]) * (SCALE * LOG2E)).astype(BF16)


def _odd_proj(x, g, w, tm):
    b, s, d = x.shape
    nt = s // tm
    lc = min(C_LEFT_CHUNKS * CHUNK, s)
    assert lc <= tm
    wq = H_C * HEAD_DIM
    row = lambda w_: pl.BlockSpec((1, tm, w_), lambda bi, ti: (bi, ti, 0))
    tail = pl.BlockSpec((1, lc, LANES), lambda bi, ti: (bi, 0, 0))
    return pl.pallas_call(
        functools.partial(_odd_proj_kernel, tail=lc),
        grid=(b, nt),
        in_specs=[row(d), _const_spec((1, d)), _const_spec(w.shape)],
        out_specs=(row(wq), row(2 * LANES), row(2 * LANES), tail, tail),
        out_shape=(jax.ShapeDtypeStruct((b, s, wq), BF16),
                   jax.ShapeDtypeStruct((b, s, 2 * LANES), BF16),
                   jax.ShapeDtypeStruct((b, s, 2 * LANES), BF16),
                   jax.ShapeDtypeStruct((b, lc, LANES), F32),
                   jax.ShapeDtypeStruct((b, lc, LANES), F32)),
        compiler_params=_params(("arbitrary", "arbitrary")),
        name="odd_proj",
    )(x, g, w)


def _sample_proj_kernel(*refs, gated):
    if gated:
        x_ref, g_ref, w_ref, wf_ref, bf_ref, p_ref, logf_ref = refs
    else:
        x_ref, g_ref, w_ref, p_ref = refs
    h = _rmsnorm(x_ref[...], g_ref[...]).astype(BF16)
    p_ref[...] = _dot(h, w_ref[...])
    if gated:
        logf_ref[...] = _log_sigmoid(_dot(h, wf_ref[...]) + bf_ref[...])


def _sample_proj(x, g, w, tm, w_f=None, b_f=None):
    n, d = x.shape
    row = lambda w_: pl.BlockSpec((tm, w_), lambda i: (i, 0))
    in_specs = [row(d), _const_spec((1, d)), _const_spec(w.shape)]
    args = [x, g, w]
    out_specs = row(w.shape[1])
    out_shape = jax.ShapeDtypeStruct((n, w.shape[1]), F32)
    if w_f is not None:
        in_specs += [_const_spec(w_f.shape), _const_spec((1, LANES))]
        args += [w_f, b_f]
        out_specs = (out_specs, row(LANES))
        out_shape = (out_shape, jax.ShapeDtypeStruct((n, LANES), F32))
    return pl.pallas_call(
        functools.partial(_sample_proj_kernel, gated=w_f is not None),
        grid=(n // tm,),
        in_specs=in_specs,
        out_specs=out_specs,
        out_shape=out_shape,
        compiler_params=_params(("arbitrary",)),
        name="sample_proj",
    )(*args)


def _block_diag_q(q, nheads, scale):
    t, w = q.shape
    tiled = jnp.concatenate([q] * nheads, axis=0)
    r = lax.broadcasted_iota(jnp.int32, (nheads * t, w), 0)
    c = lax.broadcasted_iota(jnp.int32, (nheads * t, w), 1)
    return jnp.where(r // t == c // HEAD_DIM, tiled * scale, 0.0).astype(BF16)


def _block_diag_extract(o_all, nheads, t):
    w = o_all.shape[1]
    c = lax.broadcasted_iota(jnp.int32, (t, w), 1)
    out = jnp.zeros((t, w), F32)
    for h in range(nheads):
        out = jnp.where(c // HEAD_DIM == h, o_all[h * t:(h + 1) * t, :], out)
    return out


def _heads_major(ref):
    _, nh, hd, nk = ref.shape
    return ref[0].reshape(nh * hd, nk).astype(BF16)


def _pad_rows(x, rows):
    return jnp.concatenate([x, jnp.zeros((rows - x.shape[0], x.shape[1]), x.dtype)], axis=0)


def _sample_even_kernel(qa_ref, kan_ref, van_ref, qb_ref, kbn_ref, vbn_ref, ck_ref, cv_ref, lft_ref,
                        cbk_ref, cbv_ref, e3_ref, biasb_ref, oa_ref, ob_ref,
                        bias_sc, m_sc, acc_sc, *, t, past, kc, nkc):
    c = pl.program_id(1)
    rows = H_A * t
    qbd = _block_diag_q(qa_ref[...], H_A, SCALE)

    @pl.when(c == 0)
    def _():
        cum = _cumsum_lanes(lft_ref[0])
        last = past + t - 1
        suffix = cum[:, last:last + 1] - cum
        hi, mid, lo = _split3(suffix)
        s3 = jnp.concatenate([hi, mid, lo, jnp.zeros((LANES - 3 * H_A, suffix.shape[1]), F32)], axis=0)
        s3 = s3.astype(BF16)
        for ch in range(nkc + 1):
            w = kc if ch < nkc else LANES
            bias_sc[ch, :, :w] = _dot(e3_ref[...], s3[:, ch * kc:ch * kc + w])
        m_sc[...] = jnp.full_like(m_sc, NEG_INF)
        acc_sc[...] = jnp.zeros_like(acc_sc)

    def update(s, pv):
        m = m_sc[...]
        m_new = jnp.maximum(m, jnp.max(s, axis=1, keepdims=True))
        p = jnp.exp(s - m_new)
        alpha = jnp.exp(m - m_new)
        acc = acc_sc[...]
        lsum = alpha * acc[:, W_A:W_A + 1] + jnp.sum(p, axis=1, keepdims=True)
        o = alpha * acc[:, :W_A] + pv(p.astype(BF16))
        acc_sc[:, :W_A] = o
        acc_sc[:, W_A:] = jnp.broadcast_to(lsum, (rows, LANES))
        m_sc[...] = m_new

    s = _dot(qbd, _heads_major(ck_ref)) + bias_sc[c]
    vt = _heads_major(cv_ref)
    update(s, lambda p: _dot_nt(p, vt))

    @pl.when(c == nkc - 1)
    def _():
        kn = _pad_rows(kan_ref[...], LANES).astype(BF16)
        vn = _pad_rows(van_ref[...], LANES).astype(BF16)
        sn = _dot_nt(qbd, kn) + bias_sc[nkc, :, :LANES]
        r = lax.broadcasted_iota(jnp.int32, sn.shape, 0)
        col = lax.broadcasted_iota(jnp.int32, sn.shape, 1)
        sn = jnp.where(col <= r % t, sn, NEG_INF)
        update(sn, lambda p: _dot(p, vn))
        acc = acc_sc[...]
        o_all = acc[:, :W_A] / acc[:, W_A:W_A + 1]
        oa_ref[...] = _block_diag_extract(o_all, H_A, t).astype(BF16)

        qbd_b = _block_diag_q(qb_ref[...], H_B, SCALE)
        lbk = cbk_ref.shape[3]
        kbn = _pad_rows(kbn_ref[...], LANES).astype(BF16)
        vbn = _pad_rows(vbn_ref[...], LANES).astype(BF16)
        sb = jnp.concatenate([_dot(qbd_b, _heads_major(cbk_ref)), _dot_nt(qbd_b, kbn)], axis=1) + biasb_ref[...]
        mb = jnp.max(sb, axis=1, keepdims=True)
        eb = jnp.exp(sb - mb)
        lb = jnp.sum(eb, axis=1, keepdims=True)
        eb = eb.astype(BF16)
        ob_all = (_dot_nt(eb[:, :lbk], _heads_major(cbv_ref)) + _dot(eb[:, lbk:], vbn)) / lb
        ob_ref[...] = _block_diag_extract(ob_all, H_B, t).astype(BF16)


def _sample_even(p, cache_k, cache_v, logf_t, cache_bk, cache_bv, e3, bias_b, *, t, kc):
    n = p.shape[0]
    nb = n // t
    past = cache_k.shape[3]
    nkc = past // kc
    lbk = cache_bk.shape[3]
    pcol = lambda j: pl.BlockSpec((t, W_A), lambda bi, c: (bi, j))
    rows = H_A * t
    return pl.pallas_call(
        functools.partial(_sample_even_kernel, t=t, past=past, kc=kc, nkc=nkc),
        grid=(nb, nkc),
        in_specs=[pcol(0), pcol(1), pcol(2), pcol(3), pcol(4), pcol(5),
                  pl.BlockSpec((1, H_A, HEAD_DIM, kc), lambda bi, c: (bi, 0, 0, c)),
                  pl.BlockSpec((1, H_A, HEAD_DIM, kc), lambda bi, c: (bi, 0, 0, c)),
                  pl.BlockSpec((1, H_A, logf_t.shape[2]), lambda bi, c: (bi, 0, 0)),
                  pl.BlockSpec((1, H_B, HEAD_DIM, lbk), lambda bi, c: (bi, 0, 0, 0)),
                  pl.BlockSpec((1, H_B, HEAD_DIM, lbk), lambda bi, c: (bi, 0, 0, 0)),
                  _const_spec(e3.shape), _const_spec(bias_b.shape)],
        out_specs=(pl.BlockSpec((t, W_A), lambda bi, c: (bi, 0)),
                   pl.BlockSpec((t, W_B), lambda bi, c: (bi, 0))),
        out_shape=(jax.ShapeDtypeStruct((n, W_A), BF16), jax.ShapeDtypeStruct((n, W_B), BF16)),
        scratch_shapes=[pltpu.VMEM((nkc + 1, rows, kc), F32),
                        pltpu.VMEM((rows, 1), F32),
                        pltpu.VMEM((rows, W_A + LANES), F32)],
        compiler_params=_params(("arbitrary", "arbitrary")),
        name="sample_even",
    )(p, p, p, p, p, p, cache_k, cache_v, logf_t, cache_bk, cache_bv, e3, bias_b)


def _sample_odd_kernel(q_ref, kn_ref, vn_ref, ck_ref, cv_ref, x_ref, bias_ref, sink_ref, o_ref, *, t, nbs):
    sk = sink_ref[...]
    for bb in range(nbs):
        rows = slice(bb * t, (bb + 1) * t)
        qbd = _block_diag_q(q_ref[rows, :], H_C, SCALE)
        kall = jnp.concatenate([ck_ref[bb], _pad_rows(kn_ref[rows, :], LANES)], axis=0).astype(BF16)
        vall = jnp.concatenate([cv_ref[bb], _pad_rows(vn_ref[rows, :], LANES)], axis=0).astype(BF16)
        kexp = _dot(kall, x_ref[...]).astype(BF16)
        vexp = _dot(vall, x_ref[...]).astype(BF16)
        s = _dot_nt(qbd, kexp) + bias_ref[...]
        m = jnp.maximum(jnp.max(s, axis=1, keepdims=True), sk)
        e = jnp.exp(s - m)
        l = jnp.sum(e, axis=1, keepdims=True) + jnp.exp(sk - m)
        o_all = _dot(e.astype(BF16), vexp) / l
        o_ref[rows, :] = _block_diag_extract(o_all, H_C, t).astype(BF16)


def _sample_odd(p, cache_k, cache_v, xexp, bias, sink_col, *, t, nbs):
    n = p.shape[0]
    nb = n // t
    wq = H_C * HEAD_DIM
    lc = cache_k.shape[1]
    rows = t * nbs
    return pl.pallas_call(
        functools.partial(_sample_odd_kernel, t=t, nbs=nbs),
        grid=(nb // nbs,),
        in_specs=[pl.BlockSpec((rows, wq), lambda bi: (bi, 0)),
                  pl.BlockSpec((rows, LANES), lambda bi: (bi, wq // LANES)),
                  pl.BlockSpec((rows, LANES), lambda bi: (bi, wq // LANES + 1)),
                  pl.BlockSpec((nbs, lc, LANES), lambda bi: (bi, 0, 0)),
                  pl.BlockSpec((nbs, lc, LANES), lambda bi: (bi, 0, 0)),
                  _const_spec(xexp.shape), _const_spec(bias.shape), _const_spec(sink_col.shape)],
        out_specs=pl.BlockSpec((rows, wq), lambda bi: (bi, 0)),
        out_shape=jax.ShapeDtypeStruct((n, wq), BF16),
        compiler_params=_params(("arbitrary",)),
        name="sample_odd",
    )(p, p, p, cache_k, cache_v, xexp, bias, sink_col)


def _t5_bucket(rel_mem):
    nb = T5_BUCKETS // 2
    max_exact = nb // 2
    n = jnp.abs(rel_mem)
    large = max_exact + (jnp.log(jnp.maximum(n, 1).astype(F32) / max_exact)
                         / math.log(T5_MAX_DIST / max_exact) * (nb - max_exact)).astype(jnp.int32)
    large = jnp.minimum(large, nb - 1)
    return jnp.where(rel_mem > 0, nb, 0) + jnp.where(n < max_exact, n, large)


def _bias_b_of_rel(table, rel):
    idx = np.clip(rel, -B_REL_CLIP, B_REL_CLIP) + B_REL_CLIP
    return table.astype(F32)[idx].T


def _bias_c_of_rel(table, rel):
    return table.astype(F32)[_t5_bucket(-jnp.asarray(rel, jnp.int32))].T


def _bias_tiles_kernel(f_ref, o_ref, *, tq, left, offsets, left_chunks, valid_cols):
    n = f_ref.shape[2]
    win = o_ref.shape[3]
    rows = jnp.broadcast_to(f_ref[0], (tq, n))
    wide = pltpu.roll(rows, n - (tq - 1), 1, stride=1, stride_axis=0)
    q = lax.broadcasted_iota(jnp.int32, (tq, win), 0)
    k = lax.broadcasted_iota(jnp.int32, (tq, win), 1)
    shift = CHUNK.bit_length() - 1
    for v, off in enumerate(offsets):
        if valid_cols is None:
            d = off // CHUNK + jnp.right_shift(q, shift) - jnp.right_shift(k, shift)
            ok = (d >= 0) & (d <= left_chunks)
        else:
            ok = k < valid_cols
        o_ref[v, 0] = jnp.where(ok, wide[:, left - off:left - off + win], NEG_INF)


def _bias_tiles(bias_of_rel, *, tq, left, win, offsets, left_chunks=None, valid_cols=None):
    assert CHUNK & (CHUNK - 1) == 0 and all((left - o) % LANES == 0 for o in offsets)
    ncols = win + left - min(offsets)
    n = tq + ncols - 1
    n_pad = -(-n // LANES) * LANES
    f_rev = bias_of_rel(tq - 1 + left - np.arange(n))
    nheads = f_rev.shape[0]
    f_rev = jnp.pad(f_rev, ((0, 0), (0, n_pad - n)))[:, None, :]
    return pl.pallas_call(
        functools.partial(_bias_tiles_kernel, tq=tq, left=left, offsets=tuple(offsets),
                          left_chunks=left_chunks, valid_cols=valid_cols),
        grid=(nheads,),
        in_specs=[pl.BlockSpec((1, 1, n_pad), lambda h: (h, 0, 0))],
        out_specs=pl.BlockSpec((len(offsets), 1, tq, win), lambda h: (0, h, 0, 0)),
        out_shape=jax.ShapeDtypeStruct((len(offsets), nheads, tq, win), F32),
        compiler_params=_params(("arbitrary",)),
        name="bias_tiles",
    )(f_rev)


def _sample_bias(bias_of_rel, t, cache_len):
    ncols = cache_len + LANES
    tiles = _bias_tiles(bias_of_rel, tq=t, left=cache_len, win=ncols, offsets=[cache_len],
                        valid_cols=cache_len + t)
    return tiles.reshape(-1, ncols)


def _cast_kernel(x_ref, o_ref):
    o_ref[...] = x_ref[...].astype(o_ref.dtype)


def _cast_bf16(w, rows):
    nl, r, c = w.shape
    w2 = w.reshape(nl * r, c)
    out = pl.pallas_call(
        _cast_kernel,
        grid=(nl * r // rows,),
        in_specs=[pl.BlockSpec((rows, c), lambda i: (i, 0))],
        out_specs=pl.BlockSpec((rows, c), lambda i: (i, 0)),
        out_shape=jax.ShapeDtypeStruct(w2.shape, BF16),
        compiler_params=_params(("arbitrary",)),
        name="cast_bf16",
    )(w2)
    return out.reshape(nl, r, c)


def _placement():
    pq = np.zeros((LANES, H_A * LANES), np.float32)
    pk = np.zeros((LANES, H_A * LANES), np.float32)
    for h in range(H_A):
        for j in range(3):
            pq[8 * j + h, LANES * h + HEAD_DIM + j] = 1.0
            pq[24, LANES * h + HEAD_DIM + 3 + j] = 1.0
            pk[24, LANES * h + HEAD_DIM + j] = 1.0
            pk[8 * j + h, LANES * h + HEAD_DIM + 3 + j] = -1.0
    return jnp.asarray(pq, BF16), jnp.asarray(pk, BF16)


def kernel(x_prompt, x_sample, cache_a_k, cache_a_v, cache_a_logf, cache_b_k, cache_b_v, cache_c_k, cache_c_v,
           norm_mix, norm_ffn, norm_final, w_in_even, b_forget, rel_bias_b, w_out_even, w_in_odd, sinks_c,
           w_out_odd, t5_bias, w_ffn_in, w_ffn_out):
    b, s, d = x_prompt.shape
    nb, t, _ = x_sample.shape
    past = cache_a_k.shape[2]
    n_p, n_s = b * s, nb * t

    w_even = w_in_even[0]
    w_main = w_even[:, :3 * W_A + 3 * W_B].astype(BF16)
    wf = w_even[:, 3 * W_A + 3 * W_B:]
    w_f = jnp.concatenate([wf, wf, wf, jnp.zeros((d, LANES - 3 * H_A), F32)], axis=1).astype(BF16)
    bf = b_forget[0].astype(F32)
    b_f = jnp.concatenate([bf, bf, bf, jnp.zeros((LANES - 3 * H_A,), F32)])[None, :]
    pq, pk = _placement()
    w_oe = w_out_even[0].astype(BF16)
    w_oo = w_out_odd[0].astype(BF16)
    w_odd = w_in_odd[0].astype(BF16)
    w_fi = _cast_bf16(w_ffn_in, rows=256)
    w_fo = _cast_bf16(w_ffn_out, rows=704)
    g_mix = norm_mix.astype(F32)[:, None, :]
    g_ffn = norm_ffn.astype(F32)[:, None, :]
    g_fin = norm_final.astype(F32)[None, :]

    bias_b_of = functools.partial(_bias_b_of_rel, rel_bias_b[0])
    bias_c_of = functools.partial(_bias_c_of_rel, t5_bias)
    log2_domain = lambda bias_of: (lambda rel: bias_of(rel) * LOG2E)
    variant_offsets = lambda tsub, left: [min(v * tsub, left) for v in range(left // tsub + 1)]
    tq_b, tsub_b, left_b = 512, 256, B_LEFT_CHUNKS * CHUNK
    bias_b = _bias_tiles(log2_domain(bias_b_of), tq=tsub_b, left=left_b, win=tsub_b + left_b,
                         offsets=variant_offsets(tsub_b, left_b), left_chunks=B_LEFT_CHUNKS)
    tq_c, tsub_c, left_c = 512, 128, C_LEFT_CHUNKS * CHUNK
    bias_c = _bias_tiles(log2_domain(bias_c_of), tq=tsub_c, left=left_c, win=tsub_c + left_c,
                         offsets=variant_offsets(tsub_c, left_c), left_chunks=C_LEFT_CHUNKS)

    xp = x_prompt
    w_all = jnp.concatenate([w_main[:, 3 * W_A:3 * W_A + W_B], w_f, w_main[:, :3 * W_A],
                             w_main[:, 3 * W_A + W_B:]], axis=1)
    qaug, kaug, vaug, ka, va, logf, qb, kb, vb, kbt, vbt = _even_proj(
        xp, g_mix[0], w_all, b_f, pq, pk, tm=512)
    oa = _fox(qaug, kaug, vaug, blk=512, nsub=2)
    ob = _band(qb, kb, vb, bias_b, None, tq=tq_b, left=left_b, npairs=H_B // 2, pairs_per_kv=1,
               name="band_b")
    xp1 = _out_ffn(xp.reshape(n_p, d),
                   [(oa.reshape(n_p, W_A), w_oe[:W_A]), (ob.reshape(n_p, W_B), w_oe[W_A:])],
                   g_ffn[0], w_fi, w_fo, g_fin, tm=512, layer=0, final_norm=False)

    qc, kcd, vcd, kct, vct = _odd_proj(xp1.reshape(b, s, d), g_mix[1], w_odd, tm=512)
    oc = _band(qc, kcd, vcd, bias_c, sinks_c[0].astype(F32) * LOG2E, tq=tq_c, left=left_c, npairs=H_C // 2,
               pairs_per_kv=G_C // 2, name="band_c")
    y_prompt = _out_ffn(xp1, [(oc.reshape(n_p, H_C * HEAD_DIM), w_oo)],
                        g_ffn[1], w_fi, w_fo, g_fin, tm=512, layer=1, final_norm=True)

    xs = x_sample.reshape(n_s, d)
    ps, logf_sp = _sample_proj(xs, g_mix[0], w_main, tm=256, w_f=w_f, b_f=b_f)
    logf_s = logf_sp[:, :H_A]
    kpad = LANES * -(-(past + t) // LANES)
    logf_all = jnp.concatenate([cache_a_logf[0].astype(F32), logf_s.reshape(nb, t, H_A)], axis=1)
    logf_t = jnp.pad(jnp.swapaxes(logf_all, 1, 2), ((0, 0), (0, 0), (0, kpad - past - t)))
    lbs = cache_b_k.shape[2]
    bias_sb = _sample_bias(bias_b_of, t, lbs)
    e3 = np.zeros((H_A * t, LANES), np.float32)
    for j in range(3):
        e3[np.arange(H_A * t), 8 * j + np.arange(H_A * t) // t] = 1.0
    keys_minor = lambda cache: jnp.transpose(cache[0], (0, 2, 3, 1))
    oa_s, ob_s = _sample_even(ps, keys_minor(cache_a_k), keys_minor(cache_a_v), logf_t,
                              keys_minor(cache_b_k), keys_minor(cache_b_v),
                              jnp.asarray(e3, BF16), bias_sb, t=t, kc=4096)
    xs1 = _out_ffn(xs, [(oa_s, w_oe[:W_A]), (ob_s, w_oe[W_A:])], g_ffn[0], w_fi, w_fo, g_fin,
                   tm=256, layer=0, final_norm=False)

    ps2 = _sample_proj(xs1, g_mix[1], w_odd, tm=256)
    lcs = cache_c_k.shape[2]
    bias_sc = _sample_bias(bias_c_of, t, lcs)
    sink_col = jnp.repeat(sinks_c[0].astype(F32), t)[:, None]
    lane_head = np.arange(H_C * HEAD_DIM) // HEAD_DIM
    src_lane = (lane_head // G_C) * HEAD_DIM + np.arange(H_C * HEAD_DIM) % HEAD_DIM
    xexp = jnp.asarray(np.arange(LANES)[:, None] == src_lane[None, :], BF16)
    oc_s = _sample_odd(ps2, cache_c_k[0].reshape(nb, lcs, LANES), cache_c_v[0].reshape(nb, lcs, LANES),
                       xexp, bias_sc, sink_col, t=t, nbs=4)
    y_sample = _out_ffn(xs1, [(oc_s, w_oo)], g_ffn[1], w_fi, w_fo, g_fin, tm=256, layer=1, final_norm=True)

    wq = H_C * HEAD_DIM
    hd = lambda a, lead, h: a.reshape((1,) + lead + (h, HEAD_DIM))
    return (
        y_prompt.reshape(b, s, d), y_sample.reshape(nb, t, d),
        hd(ka, (b, s), H_A), hd(va, (b, s), H_A), logf[None],
        hd(kbt, (b, kbt.shape[1]), H_B), hd(vbt, (b, vbt.shape[1]), H_B),
        hd(kct, (b, kct.shape[1]), HKV_C), hd(vct, (b, vct.shape[1]), HKV_C),
        hd(ps[:, W_A:2 * W_A], (nb, t), H_A), hd(ps[:, 2 * W_A:3 * W_A], (nb, t), H_A),
        logf_s.reshape(1, nb, t, H_A),
        hd(ps[:, 3 * W_A + W_B:3 * W_A + 2 * W_B], (nb, t), H_B),
        hd(ps[:, 3 * W_A + 2 * W_B:3 * W_A + 3 * W_B], (nb, t), H_B),
        hd(ps2[:, wq:wq + LANES], (nb, t), HKV_C), hd(ps2[:, wq + LANES:wq + 2 * LANES], (nb, t), HKV_C),
    )
```

```python
import functools
import math

import jax
import jax.numpy as jnp
import numpy as np
from jax import lax
from jax.experimental import pallas as pl
from jax.experimental.pallas import tpu as pltpu

D_MODEL = 1024
HEAD_DIM = 64
CHUNK = 64
H_A = 8
H_B = 8
B_LEFT_CHUNKS = 8
B_REL_CLIP = 128
H_C = 16
HKV_C = 2
G_C = H_C // HKV_C
WINDOW = 128
C_LEFT_CHUNKS = WINDOW // CHUNK
T5_BUCKETS = 32
T5_MAX_DIST = 128
EPS = 1e-6
W_A = H_A * HEAD_DIM
W_B = H_B * HEAD_DIM
SCALE = HEAD_DIM ** -0.5
LOG2E = math.log2(math.e)

LANES = 128
VMEM_LIMIT = 60 * 1024 * 1024

F32 = jnp.float32
BF16 = jnp.bfloat16
NEG_INF = float("-inf")

_NT = (((1,), (1,)), ((), ()))


def _dot(a, b):
    return jnp.dot(a, b, preferred_element_type=F32)


def _dot_nt(a, b):
    return lax.dot_general(a, b, _NT, preferred_element_type=F32)


def _rmsnorm(x, g):
    ms = jnp.mean(x * x, axis=-1, keepdims=True)
    return x * lax.rsqrt(ms + EPS) * g


def _log_sigmoid(x):
    return jnp.minimum(x, 0.0) - jnp.log1p(jnp.exp(-jnp.abs(x)))


def _split3(x):
    hi = x.astype(BF16).astype(F32)
    r1 = x - hi
    mid = r1.astype(BF16).astype(F32)
    lo = (r1 - mid).astype(BF16).astype(F32)
    return hi, mid, lo


def _cumsum_rows(x):
    n = x.shape[0]
    row = lax.broadcasted_iota(jnp.int32, x.shape, 0)
    s = 1
    while s < n:
        x = x + jnp.where(row >= s, pltpu.roll(x, s, 0), 0.0)
        s *= 2
    return x


def _cumsum_lanes(x):
    n = x.shape[1]
    col = lax.broadcasted_iota(jnp.int32, x.shape, 1)
    s = 1
    while s < n:
        x = x + jnp.where(col >= s, pltpu.roll(x, s, 1), 0.0)
        s *= 2
    return x


def _staggered(items, stage_a, stage_b, stage_c):
    n = len(items)
    a_out, b_out = {}, {}
    for step in range(n + 2):
        if step < n:
            a_out[step] = stage_a(items[step])
        if 0 <= step - 1 < n:
            b_out[step - 1] = stage_b(items[step - 1], a_out.pop(step - 1))
        if 0 <= step - 2 < n:
            stage_c(items[step - 2], b_out.pop(step - 2))


def _const_spec(shape):
    nd = len(shape)
    return pl.BlockSpec(shape, lambda *_: (0,) * nd, pipeline_mode=pl.Buffered(1))


def _params(sem):
    return pltpu.CompilerParams(dimension_semantics=sem, vmem_limit_bytes=VMEM_LIMIT)


def _even_proj_kernel(x_ref, g_ref, w_ref, bf_ref, pq_ref, pk_ref,
                      qaug_ref, kaug_ref, vaug_ref, ka_ref, va_ref, logf_ref,
                      qb_ref, kb_ref, vb_ref, kbt_ref, vbt_ref, carry_ref):
    @pl.when(pl.program_id(1) == 0)
    def _():
        carry_ref[...] = jnp.zeros_like(carry_ref)

    h = _rmsnorm(x_ref[0], g_ref[...]).astype(BF16)
    tm = h.shape[0]
    lane = lax.broadcasted_iota(jnp.int32, (tm, LANES), 1)
    low = lane < HEAD_DIM
    first = W_B + LANES

    def chunk(n):
        return _dot(h, w_ref[:, first + W_A * n:first + W_A * (n + 1)])

    pf = _dot(h, w_ref[:, :first])
    qb_ref[0] = (pf[:, :W_B] * (SCALE * LOG2E)).astype(BF16)
    fa = pf[:, W_B:]
    pc = chunk(3)
    kb_ref[0] = pc.astype(BF16)
    kbt_ref[0] = pc
    pc = chunk(4)
    vb_ref[0] = pc.astype(BF16)
    vbt_ref[0] = pc
    logf = _log_sigmoid(fa + bf_ref[...])
    logf_ref[0] = logf[:, :H_A]
    c = _cumsum_rows(logf) + carry_ref[...]
    carry_ref[...] = c[tm - 1:tm, :]
    hi, mid, lo = _split3(c * LOG2E)
    a3 = jnp.where(lane < 8, hi, jnp.where(lane < 16, mid, jnp.where(lane < 24, lo,
                   jnp.where(lane == 24, 1.0, 0.0)))).astype(BF16)
    augq = _dot(a3, pq_ref[...])
    augk = _dot(a3, pk_ref[...])
    vone = jnp.where(lane == HEAD_DIM, 1.0, 0.0)

    def per_head(dst_ref, pc, spare):
        for j in range(H_A // 2):
            pair = pc[:, LANES * j:LANES * (j + 1)]
            for hh, val in enumerate((pair, pltpu.roll(pair, HEAD_DIM, 1))):
                o = slice(LANES * (2 * j + hh), LANES * (2 * j + hh + 1))
                dst_ref[0, :, o] = jnp.where(low, val, vone if spare is None else spare[:, o]).astype(BF16)

    pc = chunk(0)
    per_head(qaug_ref, pc * (SCALE * LOG2E), augq)
    pc = chunk(1)
    ka_ref[0] = pc
    per_head(kaug_ref, pc, augk)
    pc = chunk(2)
    va_ref[0] = pc
    per_head(vaug_ref, pc, None)


def _even_proj(x, g, w_all, b_f, pq, pk, tm):
    b, s, d = x.shape
    nt = s // tm
    lb = min(B_LEFT_CHUNKS * CHUNK, s)
    assert lb == tm, "band-state tail must be exactly one row tile"
    row = lambda w: pl.BlockSpec((1, tm, w), lambda bi, ti: (bi, ti, 0))
    tail = pl.BlockSpec((1, lb, W_B), lambda bi, ti: (bi, 0, 0))
    outs = (
        jax.ShapeDtypeStruct((b, s, H_A * LANES), BF16),
        jax.ShapeDtypeStruct((b, s, H_A * LANES), BF16),
        jax.ShapeDtypeStruct((b, s, H_A * LANES), BF16),
        jax.ShapeDtypeStruct((b, s, W_A), F32),
        jax.ShapeDtypeStruct((b, s, W_A), F32),
        jax.ShapeDtypeStruct((b, s, H_A), F32),
        jax.ShapeDtypeStruct((b, s, W_B), BF16),
        jax.ShapeDtypeStruct((b, s, W_B), BF16),
        jax.ShapeDtypeStruct((b, s, W_B), BF16),
        jax.ShapeDtypeStruct((b, lb, W_B), F32),
        jax.ShapeDtypeStruct((b, lb, W_B), F32),
    )
    return pl.pallas_call(
        _even_proj_kernel,
        grid=(b, nt),
        in_specs=[row(d), _const_spec((1, d)), _const_spec(w_all.shape),
                  _const_spec((1, LANES)), _const_spec(pq.shape), _const_spec(pk.shape)],
        out_specs=(row(H_A * LANES), row(H_A * LANES), row(H_A * LANES), row(W_A), row(W_A), row(H_A),
                   row(W_B), row(W_B), row(W_B), tail, tail),
        out_shape=outs,
        scratch_shapes=[pltpu.VMEM((1, LANES), F32)],
        compiler_params=_params(("arbitrary", "arbitrary")),
        name="even_proj",
    )(x, g, w_all, b_f, pq, pk)


def _fox_kernel(q_ref, k_ref, v_ref, o_ref, m_sc, acc_sc, *, blk, nsub):
    i = pl.program_id(2)
    nh = 2
    chains = [(sub, hh) for sub in range(nsub) for hh in range(nh)]
    qs = {(sub, hh): q_ref[0, sub * blk:(sub + 1) * blk, LANES * hh:LANES * (hh + 1)] for sub, hh in chains}
    r = lax.broadcasted_iota(jnp.int32, (blk, blk), 0)
    c = lax.broadcasted_iota(jnp.int32, (blk, blk), 1)
    causal = c <= r
    m_sc[...] = jnp.full_like(m_sc, NEG_INF)
    acc_sc[...] = jnp.zeros_like(acc_sc)

    def blocks(specs):
        items = [(pl.multiple_of(j * blk, blk), ch, modes[ch[0]])
                 for j, modes in specs for ch in chains if modes[ch[0]] is not None]

        def stage_scores(item):
            start, ch, mode = item
            k = k_ref[0, pl.ds(start, blk), LANES * ch[1]:LANES * (ch[1] + 1)]
            s = _dot_nt(qs[ch], k)
            if mode == "masked":
                s = jnp.where(causal, s, NEG_INF)
            m = m_sc[chains.index(ch)]
            m_new = jnp.maximum(m, jnp.max(s, axis=1, keepdims=True))
            m_sc[chains.index(ch)] = m_new
            return s, m, m_new

        def stage_exp(item, sm):
            s, m, m_new = sm
            p = jnp.concatenate([jnp.exp2(s[:, LANES * cb:LANES * (cb + 1)] - m_new)
                                 for cb in range(blk // LANES)], axis=1)
            return p.astype(BF16), jnp.exp2(m - m_new)

        def stage_pv(item, pa):
            start, ch, _ = item
            p, alpha = pa
            v = v_ref[0, pl.ds(start, blk), LANES * ch[1]:LANES * (ch[1] + 1)]
            n = chains.index(ch)
            acc_sc[n] = alpha * acc_sc[n] + _dot(p, v)

        _staggered(items, stage_scores, stage_exp, stage_pv)

    @pl.loop(0, i)
    def _(jj):
        blocks([(nsub * jj + d, ["full"] * nsub) for d in range(nsub)])

    blocks([(nsub * i + d, [None if sub < d else ("masked" if sub == d else "full") for sub in range(nsub)])
            for d in range(nsub)])

    lane = lax.broadcasted_iota(jnp.int32, (blk, LANES), 1)
    for sub in range(nsub):
        a0 = acc_sc[chains.index((sub, 0))]
        a1 = acc_sc[chains.index((sub, 1))]
        o0 = a0 / a0[:, HEAD_DIM:HEAD_DIM + 1]
        o1 = a1 / a1[:, HEAD_DIM:HEAD_DIM + 1]
        o_ref[0, sub * blk:(sub + 1) * blk, :] = jnp.where(
            lane < HEAD_DIM, o0, pltpu.roll(o1, HEAD_DIM, 1)).astype(BF16)


def _fox(qaug, kaug, vaug, blk, nsub):
    b, s, _ = qaug.shape
    pairs = H_A // 2
    tq = blk * nsub
    return pl.pallas_call(
        functools.partial(_fox_kernel, blk=blk, nsub=nsub),
        grid=(b, pairs, s // tq),
        in_specs=[pl.BlockSpec((1, tq, 2 * LANES), lambda bi, hp, i: (bi, i, hp)),
                  pl.BlockSpec((1, s, 2 * LANES), lambda bi, hp, i: (bi, 0, hp)),
                  pl.BlockSpec((1, s, 2 * LANES), lambda bi, hp, i: (bi, 0, hp))],
        out_specs=pl.BlockSpec((1, tq, LANES), lambda bi, hp, i: (bi, i, hp)),
        out_shape=jax.ShapeDtypeStruct((b, s, W_A), BF16),
        scratch_shapes=[pltpu.VMEM((2 * nsub, blk, LANES), F32), pltpu.VMEM((2 * nsub, blk, LANES), F32)],
        compiler_params=_params(("arbitrary", "arbitrary", "arbitrary")),
        name="fox",
    )(qaug, kaug, vaug)


def _band_kernel(*refs, tq, left, npairs, heads_per_group, use_sinks, pairs_per_kv):
    mxu_rowsum = not use_sinks
    if use_sinks:
        sink_ref, q_ref, k_ref, v_ref, bias_ref, o_ref = refs
    else:
        q_ref, k_ref, v_ref, bias_ref, o_ref = refs
    g = pl.program_id(1)
    i = pl.program_id(2)
    nvar, _, tsub, win = bias_ref.shape
    nsq = tq // tsub
    lane = lax.broadcasted_iota(jnp.int32, (tsub, LANES), 1)
    low = lane < HEAD_DIM
    items = [(sq, pr, hh) for sq in range(nsq) for pr in range(npairs) for hh in range(2)]

    def window(sq):
        gsub = i * nsq + sq
        return pl.multiple_of(jnp.maximum(gsub * tsub - left, 0), LANES), jnp.minimum(gsub, nvar - 1)

    def kv_lanes(pr):
        kl = LANES * (pr // pairs_per_kv)
        return slice(kl, kl + LANES)

    def sink(pr, hh):
        return sink_ref[g * heads_per_group + 2 * pr + hh]

    def stage_scores(item):
        sq, pr, hh = item
        start, var = window(sq)
        qp = q_ref[0, sq * tsub:(sq + 1) * tsub, LANES * pr:LANES * (pr + 1)]
        qm = jnp.where(low if hh == 0 else jnp.logical_not(low), qp, jnp.zeros_like(qp))
        s = _dot_nt(qm, k_ref[0, pl.ds(start, win), kv_lanes(pr)]) + bias_ref[var, 2 * pr + hh]
        m = jnp.max(s, axis=1, keepdims=True)
        return s, (jnp.maximum(m, sink(pr, hh)) if use_sinks else m)

    def stage_exp(item, sm):
        s, m = sm
        e = jnp.exp2(s - m)
        l = None if mxu_rowsum else jnp.sum(e, axis=1, keepdims=True)
        if use_sinks:
            l = l + jnp.exp2(sink(item[1], item[2]) - m)
        return e.astype(BF16), l

    outs = {}

    def stage_pv(item, pe):
        p, l = pe
        sq, pr, hh = item
        v = v_ref[0, pl.ds(window(sq)[0], win), kv_lanes(pr)]
        if mxu_rowsum:
            ov = _dot(p, jnp.concatenate([v, jnp.ones((win, LANES), BF16)], axis=1))
            outs[hh] = ov[:, :LANES] / ov[:, LANES:]
        else:
            outs[hh] = _dot(p, v) / l
        if hh == 1:
            o_ref[0, sq * tsub:(sq + 1) * tsub, LANES * pr:LANES * (pr + 1)] = jnp.where(
                low, outs[0], outs[1]).astype(BF16)

    _staggered(items, stage_scores, stage_exp, stage_pv)


def _band(q, k, v, bias, sinks, *, tq, left, npairs, pairs_per_kv, name):
    b, s, wq = q.shape
    tsub, win = bias.shape[-2:]
    wblk = npairs * LANES
    groups = wq // wblk
    hpg = 2 * npairs
    use_sinks = sinks is not None
    kern = functools.partial(_band_kernel, tq=tq, left=left, npairs=npairs,
                             heads_per_group=hpg, use_sinks=use_sinks, pairs_per_kv=pairs_per_kv)
    kvw = wblk // pairs_per_kv
    resident = lambda shape, imap: pl.BlockSpec(shape, imap, pipeline_mode=pl.Buffered(1))
    in_specs = [pl.BlockSpec((1, tq, wblk), lambda bi, g, i: (bi, i, g)),
                resident((1, s, kvw), lambda bi, g, i: (bi, 0, g)),
                resident((1, s, kvw), lambda bi, g, i: (bi, 0, g)),
                resident((bias.shape[0], hpg, tsub, win), lambda bi, g, i: (0, g, 0, 0))]
    args = [q, k, v, bias]
    if use_sinks:
        in_specs = [pl.BlockSpec(memory_space=pltpu.SMEM)] + in_specs
        args = [sinks] + args
    return pl.pallas_call(
        kern,
        grid=(b, groups, s // tq),
        in_specs=in_specs,
        out_specs=pl.BlockSpec((1, tq, wblk), lambda bi, g, i: (bi, i, g)),
        out_shape=jax.ShapeDtypeStruct((b, s, wq), BF16),
        compiler_params=_params(("arbitrary", "arbitrary", "arbitrary")),
        name=name,
    )(*args)


def _out_ffn_kernel(*refs, n_attn, d_ff, final_norm):
    x_ref = refs[0]
    attn = refs[1:1 + 2 * n_attn]
    gffn_ref, win_ref, wout_ref, gfin_ref, o_ref = refs[1 + 2 * n_attn:]
    y = x_ref[...]
    for a in range(n_attn):
        y = y + _dot(attn[2 * a][...], attn[2 * a + 1][...])
    h = _rmsnorm(y, gffn_ref[...]).astype(BF16)
    gu = _dot(h, win_ref[...])
    gate = gu[:, :d_ff]
    up = gu[:, d_ff:]
    act = (gate * (1.0 / (1.0 + jnp.exp(-gate))) * up).astype(BF16)
    y = y + _dot(act, wout_ref[...])
    o_ref[...] = _rmsnorm(y, gfin_ref[...]) if final_norm else y


def _out_ffn(x, attn_pairs, g_ffn, w_in, w_out, g_fin, tm, layer, final_norm):
    n, d = x.shape
    d_ff = w_out.shape[1]
    row = lambda w: pl.BlockSpec((tm, w), lambda i: (i, 0))
    slab = lambda w: pl.BlockSpec((None,) + w.shape[1:], lambda i: (layer, 0, 0), pipeline_mode=pl.Buffered(1))
    args, specs = [x], [row(d)]
    for o, w in attn_pairs:
        args += [o, w]
        specs += [row(o.shape[1]), _const_spec(w.shape)]
    args += [g_ffn, w_in, w_out, g_fin]
    specs += [_const_spec((1, d)), slab(w_in), slab(w_out), _const_spec((1, d))]
    return pl.pallas_call(
        functools.partial(_out_ffn_kernel, n_attn=len(attn_pairs), d_ff=d_ff, final_norm=final_norm),
        grid=(n // tm,),
        in_specs=specs,
        out_specs=row(d),
        out_shape=jax.ShapeDtypeStruct((n, d), F32),
        compiler_params=_params(("arbitrary",)),
        name="out_ffn",
    )(*args)


def _odd_proj_kernel(x_ref, g_ref, w_ref, q_ref, k_ref, v_ref, kt_ref, vt_ref, *, tail):
    h = _rmsnorm(x_ref[0], g_ref[...]).astype(BF16)
    wq = H_C * HEAD_DIM
    tm = h.shape[0]
    lane = lax.broadcasted_iota(jnp.int32, (tm, LANES), 1)
    low = lane < HEAD_DIM
    kv = _dot(h, w_ref[:, wq:])
    for src, dst, tail_ref in ((kv[:, :LANES], k_ref, kt_ref), (kv[:, LANES:], v_ref, vt_ref)):
        rolled = pltpu.roll(src, HEAD_DIM, 1)
        dst[0, :, :LANES] = jnp.where(low, src, rolled).astype(BF16)
        dst[0, :, LANES:] = jnp.where(low, rolled, src).astype(BF16)
        tail_ref[0] = src[tm - tail:, :]
    half = wq // 2
    for c in range(2):
        q_ref[0, :, half * c:half * (c + 1)] = (
            _dot(h, w_ref[:, half * c:half * (c + 1)]) * (SCALE * LOG2E)).astype(BF16)


def _odd_proj(x, g, w, tm):
    b, s, d = x.shape
    nt = s // tm
    lc = min(C_LEFT_CHUNKS * CHUNK, s)
    assert lc <= tm
    wq = H_C * HEAD_DIM
    row = lambda w_: pl.BlockSpec((1, tm, w_), lambda bi, ti: (bi, ti, 0))
    tail = pl.BlockSpec((1, lc, LANES), lambda bi, ti: (bi, 0, 0))
    return pl.pallas_call(
        functools.partial(_odd_proj_kernel, tail=lc),
        grid=(b, nt),
        in_specs=[row(d), _const_spec((1, d)), _const_spec(w.shape)],
        out_specs=(row(wq), row(2 * LANES), row(2 * LANES), tail, tail),
        out_shape=(jax.ShapeDtypeStruct((b, s, wq), BF16),
                   jax.ShapeDtypeStruct((b, s, 2 * LANES), BF16),
                   jax.ShapeDtypeStruct((b, s, 2 * LANES), BF16),
                   jax.ShapeDtypeStruct((b, lc, LANES), F32),
                   jax.ShapeDtypeStruct((b, lc, LANES), F32)),
        compiler_params=_params(("arbitrary", "arbitrary")),
        name="odd_proj",
    )(x, g, w)


def _sample_proj_kernel(*refs, gated):
    if gated:
        x_ref, g_ref, w_ref, wf_ref, bf_ref, p_ref, logf_ref = refs
    else:
        x_ref, g_ref, w_ref, p_ref = refs
    h = _rmsnorm(x_ref[...], g_ref[...]).astype(BF16)
    p_ref[...] = _dot(h, w_ref[...])
    if gated:
        logf_ref[...] = _log_sigmoid(_dot(h, wf_ref[...]) + bf_ref[...])


def _sample_proj(x, g, w, tm, w_f=None, b_f=None):
    n, d = x.shape
    row = lambda w_: pl.BlockSpec((tm, w_), lambda i: (i, 0))
    in_specs = [row(d), _const_spec((1, d)), _const_spec(w.shape)]
    args = [x, g, w]
    out_specs = row(w.shape[1])
    out_shape = jax.ShapeDtypeStruct((n, w.shape[1]), F32)
    if w_f is not None:
        in_specs += [_const_spec(w_f.shape), _const_spec((1, LANES))]
        args += [w_f, b_f]
        out_specs = (out_specs, row(LANES))
        out_shape = (out_shape, jax.ShapeDtypeStruct((n, LANES), F32))
    return pl.pallas_call(
        functools.partial(_sample_proj_kernel, gated=w_f is not None),
        grid=(n // tm,),
        in_specs=in_specs,
        out_specs=out_specs,
        out_shape=out_shape,
        compiler_params=_params(("arbitrary",)),
        name="sample_proj",
    )(*args)


def _block_diag_q(q, nheads, scale):
    t, w = q.shape
    tiled = jnp.concatenate([q] * nheads, axis=0)
    r = lax.broadcasted_iota(jnp.int32, (nheads * t, w), 0)
    c = lax.broadcasted_iota(jnp.int32, (nheads * t, w), 1)
    return jnp.where(r // t == c // HEAD_DIM, tiled * scale, 0.0).astype(BF16)


def _block_diag_extract(o_all, nheads, t):
    w = o_all.shape[1]
    c = lax.broadcasted_iota(jnp.int32, (t, w), 1)
    out = jnp.zeros((t, w), F32)
    for h in range(nheads):
        out = jnp.where(c // HEAD_DIM == h, o_all[h * t:(h + 1) * t, :], out)
    return out


def _heads_major(ref):
    _, nh, hd, nk = ref.shape
    return ref[0].reshape(nh * hd, nk).astype(BF16)


def _pad_rows(x, rows):
    return jnp.concatenate([x, jnp.zeros((rows - x.shape[0], x.shape[1]), x.dtype)], axis=0)


def _sample_even_kernel(qa_ref, kan_ref, van_ref, qb_ref, kbn_ref, vbn_ref, ck_ref, cv_ref, lft_ref,
                        cbk_ref, cbv_ref, e3_ref, biasb_ref, oa_ref, ob_ref,
                        bias_sc, m_sc, acc_sc, *, t, past, kc, nkc):
    c = pl.program_id(1)
    rows = H_A * t
    qbd = _block_diag_q(qa_ref[...], H_A, SCALE)

    @pl.when(c == 0)
    def _():
        cum = _cumsum_lanes(lft_ref[0])
        last = past + t - 1
        suffix = cum[:, last:last + 1] - cum
        hi, mid, lo = _split3(suffix)
        s3 = jnp.concatenate([hi, mid, lo, jnp.zeros((LANES - 3 * H_A, suffix.shape[1]), F32)], axis=0)
        s3 = s3.astype(BF16)
        for ch in range(nkc + 1):
            w = kc if ch < nkc else LANES
            bias_sc[ch, :, :w] = _dot(e3_ref[...], s3[:, ch * kc:ch * kc + w])
        m_sc[...] = jnp.full_like(m_sc, NEG_INF)
        acc_sc[...] = jnp.zeros_like(acc_sc)

    def update(s, pv):
        m = m_sc[...]
        m_new = jnp.maximum(m, jnp.max(s, axis=1, keepdims=True))
        p = jnp.exp(s - m_new)
        alpha = jnp.exp(m - m_new)
        acc = acc_sc[...]
        lsum = alpha * acc[:, W_A:W_A + 1] + jnp.sum(p, axis=1, keepdims=True)
        o = alpha * acc[:, :W_A] + pv(p.astype(BF16))
        acc_sc[:, :W_A] = o
        acc_sc[:, W_A:] = jnp.broadcast_to(lsum, (rows, LANES))
        m_sc[...] = m_new

    s = _dot(qbd, _heads_major(ck_ref)) + bias_sc[c]
    vt = _heads_major(cv_ref)
    update(s, lambda p: _dot_nt(p, vt))

    @pl.when(c == nkc - 1)
    def _():
        kn = _pad_rows(kan_ref[...], LANES).astype(BF16)
        vn = _pad_rows(van_ref[...], LANES).astype(BF16)
        sn = _dot_nt(qbd, kn) + bias_sc[nkc, :, :LANES]
        r = lax.broadcasted_iota(jnp.int32, sn.shape, 0)
        col = lax.broadcasted_iota(jnp.int32, sn.shape, 1)
        sn = jnp.where(col <= r % t, sn, NEG_INF)
        update(sn, lambda p: _dot(p, vn))
        acc = acc_sc[...]
        o_all = acc[:, :W_A] / acc[:, W_A:W_A + 1]
        oa_ref[...] = _block_diag_extract(o_all, H_A, t).astype(BF16)

        qbd_b = _block_diag_q(qb_ref[...], H_B, SCALE)
        lbk = cbk_ref.shape[3]
        kbn = _pad_rows(kbn_ref[...], LANES).astype(BF16)
        vbn = _pad_rows(vbn_ref[...], LANES).astype(BF16)
        sb = jnp.concatenate([_dot(qbd_b, _heads_major(cbk_ref)), _dot_nt(qbd_b, kbn)], axis=1) + biasb_ref[...]
        mb = jnp.max(sb, axis=1, keepdims=True)
        eb = jnp.exp(sb - mb)
        lb = jnp.sum(eb, axis=1, keepdims=True)
        eb = eb.astype(BF16)
        ob_all = (_dot_nt(eb[:, :lbk], _heads_major(cbv_ref)) + _dot(eb[:, lbk:], vbn)) / lb
        ob_ref[...] = _block_diag_extract(ob_all, H_B, t).astype(BF16)


def _sample_even(p, cache_k, cache_v, logf_t, cache_bk, cache_bv, e3, bias_b, *, t, kc):
    n = p.shape[0]
    nb = n // t
    past = cache_k.shape[3]
    nkc = past // kc
    lbk = cache_bk.shape[3]
    pcol = lambda j: pl.BlockSpec((t, W_A), lambda bi, c: (bi, j))
    rows = H_A * t
    return pl.pallas_call(
        functools.partial(_sample_even_kernel, t=t, past=past, kc=kc, nkc=nkc),
        grid=(nb, nkc),
        in_specs=[pcol(0), pcol(1), pcol(2), pcol(3), pcol(4), pcol(5),
                  pl.BlockSpec((1, H_A, HEAD_DIM, kc), lambda bi, c: (bi, 0, 0, c)),
                  pl.BlockSpec((1, H_A, HEAD_DIM, kc), lambda bi, c: (bi, 0, 0, c)),
                  pl.BlockSpec((1, H_A, logf_t.shape[2]), lambda bi, c: (bi, 0, 0)),
                  pl.BlockSpec((1, H_B, HEAD_DIM, lbk), lambda bi, c: (bi, 0, 0, 0)),
                  pl.BlockSpec((1, H_B, HEAD_DIM, lbk), lambda bi, c: (bi, 0, 0, 0)),
                  _const_spec(e3.shape), _const_spec(bias_b.shape)],
        out_specs=(pl.BlockSpec((t, W_A), lambda bi, c: (bi, 0)),
                   pl.BlockSpec((t, W_B), lambda bi, c: (bi, 0))),
        out_shape=(jax.ShapeDtypeStruct((n, W_A), BF16), jax.ShapeDtypeStruct((n, W_B), BF16)),
        scratch_shapes=[pltpu.VMEM((nkc + 1, rows, kc), F32),
                        pltpu.VMEM((rows, 1), F32),
                        pltpu.VMEM((rows, W_A + LANES), F32)],
        compiler_params=_params(("arbitrary", "arbitrary")),
        name="sample_even",
    )(p, p, p, p, p, p, cache_k, cache_v, logf_t, cache_bk, cache_bv, e3, bias_b)


def _sample_odd_kernel(q_ref, kn_ref, vn_ref, ck_ref, cv_ref, x_ref, bias_ref, sink_ref, o_ref, *, t, nbs):
    sk = sink_ref[...]
    for bb in range(nbs):
        rows = slice(bb * t, (bb + 1) * t)
        qbd = _block_diag_q(q_ref[rows, :], H_C, SCALE)
        kall = jnp.concatenate([ck_ref[bb], _pad_rows(kn_ref[rows, :], LANES)], axis=0).astype(BF16)
        vall = jnp.concatenate([cv_ref[bb], _pad_rows(vn_ref[rows, :], LANES)], axis=0).astype(BF16)
        kexp = _dot(kall, x_ref[...]).astype(BF16)
        vexp = _dot(vall, x_ref[...]).astype(BF16)
        s = _dot_nt(qbd, kexp) + bias_ref[...]
        m = jnp.maximum(jnp.max(s, axis=1, keepdims=True), sk)
        e = jnp.exp(s - m)
        l = jnp.sum(e, axis=1, keepdims=True) + jnp.exp(sk - m)
        o_all = _dot(e.astype(BF16), vexp) / l
        o_ref[rows, :] = _block_diag_extract(o_all, H_C, t).astype(BF16)


def _sample_odd(p, cache_k, cache_v, xexp, bias, sink_col, *, t, nbs):
    n = p.shape[0]
    nb = n // t
    wq = H_C * HEAD_DIM
    lc = cache_k.shape[1]
    rows = t * nbs
    return pl.pallas_call(
        functools.partial(_sample_odd_kernel, t=t, nbs=nbs),
        grid=(nb // nbs,),
        in_specs=[pl.BlockSpec((rows, wq), lambda bi: (bi, 0)),
                  pl.BlockSpec((rows, LANES), lambda bi: (bi, wq // LANES)),
                  pl.BlockSpec((rows, LANES), lambda bi: (bi, wq // LANES + 1)),
                  pl.BlockSpec((nbs, lc, LANES), lambda bi: (bi, 0, 0)),
                  pl.BlockSpec((nbs, lc, LANES), lambda bi: (bi, 0, 0)),
                  _const_spec(xexp.shape), _const_spec(bias.shape), _const_spec(sink_col.shape)],
        out_specs=pl.BlockSpec((rows, wq), lambda bi: (bi, 0)),
        out_shape=jax.ShapeDtypeStruct((n, wq), BF16),
        compiler_params=_params(("arbitrary",)),
        name="sample_odd",
    )(p, p, p, cache_k, cache_v, xexp, bias, sink_col)


def _t5_bucket(rel_mem):
    nb = T5_BUCKETS // 2
    max_exact = nb // 2
    n = jnp.abs(rel_mem)
    large = max_exact + (jnp.log(jnp.maximum(n, 1).astype(F32) / max_exact)
                         / math.log(T5_MAX_DIST / max_exact) * (nb - max_exact)).astype(jnp.int32)
    large = jnp.minimum(large, nb - 1)
    return jnp.where(rel_mem > 0, nb, 0) + jnp.where(n < max_exact, n, large)


def _bias_b_of_rel(table, rel):
    idx = np.clip(rel, -B_REL_CLIP, B_REL_CLIP) + B_REL_CLIP
    return table.astype(F32)[idx].T


def _bias_c_of_rel(table, rel):
    return table.astype(F32)[_t5_bucket(-jnp.asarray(rel, jnp.int32))].T


def _bias_tiles_kernel(f_ref, o_ref, *, tq, left, offsets, left_chunks, valid_cols):
    n = f_ref.shape[2]
    win = o_ref.shape[3]
    rows = jnp.broadcast_to(f_ref[0], (tq, n))
    wide = pltpu.roll(rows, n - (tq - 1), 1, stride=1, stride_axis=0)
    q = lax.broadcasted_iota(jnp.int32, (tq, win), 0)
    k = lax.broadcasted_iota(jnp.int32, (tq, win), 1)
    shift = CHUNK.bit_length() - 1
    for v, off in enumerate(offsets):
        if valid_cols is None:
            d = off // CHUNK + jnp.right_shift(q, shift) - jnp.right_shift(k, shift)
            ok = (d >= 0) & (d <= left_chunks)
        else:
            ok = k < valid_cols
        o_ref[v, 0] = jnp.where(ok, wide[:, left - off:left - off + win], NEG_INF)


def _bias_tiles(bias_of_rel, *, tq, left, win, offsets, left_chunks=None, valid_cols=None):
    assert CHUNK & (CHUNK - 1) == 0 and all((left - o) % LANES == 0 for o in offsets)
    ncols = win + left - min(offsets)
    n = tq + ncols - 1
    n_pad = -(-n // LANES) * LANES
    f_rev = bias_of_rel(tq - 1 + left - np.arange(n))
    nheads = f_rev.shape[0]
    f_rev = jnp.pad(f_rev, ((0, 0), (0, n_pad - n)))[:, None, :]
    return pl.pallas_call(
        functools.partial(_bias_tiles_kernel, tq=tq, left=left, offsets=tuple(offsets),
                          left_chunks=left_chunks, valid_cols=valid_cols),
        grid=(nheads,),
        in_specs=[pl.BlockSpec((1, 1, n_pad), lambda h: (h, 0, 0))],
        out_specs=pl.BlockSpec((len(offsets), 1, tq, win), lambda h: (0, h, 0, 0)),
        out_shape=jax.ShapeDtypeStruct((len(offsets), nheads, tq, win), F32),
        compiler_params=_params(("arbitrary",)),
        name="bias_tiles",
    )(f_rev)


def _sample_bias(bias_of_rel, t, cache_len):
    ncols = cache_len + LANES
    tiles = _bias_tiles(bias_of_rel, tq=t, left=cache_len, win=ncols, offsets=[cache_len],
                        valid_cols=cache_len + t)
    return tiles.reshape(-1, ncols)


def _cast_kernel(x_ref, o_ref):
    o_ref[...] = x_ref[...].astype(o_ref.dtype)


def _cast_bf16(w, rows):
    nl, r, c = w.shape
    w2 = w.reshape(nl * r, c)
    out = pl.pallas_call(
        _cast_kernel,
        grid=(nl * r // rows,),
        in_specs=[pl.BlockSpec((rows, c), lambda i: (i, 0))],
        out_specs=pl.BlockSpec((rows, c), lambda i: (i, 0)),
        out_shape=jax.ShapeDtypeStruct(w2.shape, BF16),
        compiler_params=_params(("arbitrary",)),
        name="cast_bf16",
    )(w2)
    return out.reshape(nl, r, c)


def _placement():
    pq = np.zeros((LANES, H_A * LANES), np.float32)
    pk = np.zeros((LANES, H_A * LANES), np.float32)
    for h in range(H_A):
        for j in range(3):
            pq[8 * j + h, LANES * h + HEAD_DIM + j] = 1.0
            pq[24, LANES * h + HEAD_DIM + 3 + j] = 1.0
            pk[24, LANES * h + HEAD_DIM + j] = 1.0
            pk[8 * j + h, LANES * h + HEAD_DIM + 3 + j] = -1.0
    return jnp.asarray(pq, BF16), jnp.asarray(pk, BF16)


def kernel(x_prompt, x_sample, cache_a_k, cache_a_v, cache_a_logf, cache_b_k, cache_b_v, cache_c_k, cache_c_v,
           norm_mix, norm_ffn, norm_final, w_in_even, b_forget, rel_bias_b, w_out_even, w_in_odd, sinks_c,
           w_out_odd, t5_bias, w_ffn_in, w_ffn_out):
    b, s, d = x_prompt.shape
    nb, t, _ = x_sample.shape
    past = cache_a_k.shape[2]
    n_p, n_s = b * s, nb * t

    w_even = w_in_even[0]
    w_main = w_even[:, :3 * W_A + 3 * W_B].astype(BF16)
    wf = w_even[:, 3 * W_A + 3 * W_B:]
    w_f = jnp.concatenate([wf, wf, wf, jnp.zeros((d, LANES - 3 * H_A), F32)], axis=1).astype(BF16)
    bf = b_forget[0].astype(F32)
    b_f = jnp.concatenate([bf, bf, bf, jnp.zeros((LANES - 3 * H_A,), F32)])[None, :]
    pq, pk = _placement()
    w_oe = w_out_even[0].astype(BF16)
    w_oo = w_out_odd[0].astype(BF16)
    w_odd = w_in_odd[0].astype(BF16)
    w_fi = _cast_bf16(w_ffn_in, rows=512)
    w_fo = _cast_bf16(w_ffn_out, rows=1408)
    g_mix = norm_mix.astype(F32)[:, None, :]
    g_ffn = norm_ffn.astype(F32)[:, None, :]
    g_fin = norm_final.astype(F32)[None, :]

    bias_b_of = functools.partial(_bias_b_of_rel, rel_bias_b[0])
    bias_c_of = functools.partial(_bias_c_of_rel, t5_bias)
    log2_domain = lambda bias_of: (lambda rel: bias_of(rel) * LOG2E)
    variant_offsets = lambda tsub, left: [min(v * tsub, left) for v in range(left // tsub + 1)]
    tq_b, tsub_b, left_b = 1024, 256, B_LEFT_CHUNKS * CHUNK
    bias_b = _bias_tiles(log2_domain(bias_b_of), tq=tsub_b, left=left_b, win=tsub_b + left_b,
                         offsets=variant_offsets(tsub_b, left_b), left_chunks=B_LEFT_CHUNKS)
    tq_c, tsub_c, left_c = 512, 128, C_LEFT_CHUNKS * CHUNK
    bias_c = _bias_tiles(log2_domain(bias_c_of), tq=tsub_c, left=left_c, win=tsub_c + left_c,
                         offsets=variant_offsets(tsub_c, left_c), left_chunks=C_LEFT_CHUNKS)

    xp = x_prompt
    w_all = jnp.concatenate([w_main[:, 3 * W_A:3 * W_A + W_B], w_f, w_main[:, :3 * W_A],
                             w_main[:, 3 * W_A + W_B:]], axis=1)
    qaug, kaug, vaug, ka, va, logf, qb, kb, vb, kbt, vbt = _even_proj(
        xp, g_mix[0], w_all, b_f, pq, pk, tm=512)
    oa = _fox(qaug, kaug, vaug, blk=512, nsub=2)
    ob = _band(qb, kb, vb, bias_b, None, tq=tq_b, left=left_b, npairs=H_B // 2, pairs_per_kv=1,
               name="band_b")
    xp1 = _out_ffn(xp.reshape(n_p, d),
                   [(oa.reshape(n_p, W_A), w_oe[:W_A]), (ob.reshape(n_p, W_B), w_oe[W_A:])],
                   g_ffn[0], w_fi, w_fo, g_fin, tm=512, layer=0, final_norm=False)

    qc, kcd, vcd, kct, vct = _odd_proj(xp1.reshape(b, s, d), g_mix[1], w_odd, tm=512)
    oc = _band(qc, kcd, vcd, bias_c, sinks_c[0].astype(F32) * LOG2E, tq=tq_c, left=left_c, npairs=H_C // 2,
               pairs_per_kv=G_C // 2, name="band_c")
    y_prompt = _out_ffn(xp1, [(oc.reshape(n_p, H_C * HEAD_DIM), w_oo)],
                        g_ffn[1], w_fi, w_fo, g_fin, tm=512, layer=1, final_norm=True)

    xs = x_sample.reshape(n_s, d)
    ps, logf_sp = _sample_proj(xs, g_mix[0], w_main, tm=256, w_f=w_f, b_f=b_f)
    logf_s = logf_sp[:, :H_A]
    kpad = LANES * -(-(past + t) // LANES)
    logf_all = jnp.concatenate([cache_a_logf[0].astype(F32), logf_s.reshape(nb, t, H_A)], axis=1)
    logf_t = jnp.pad(jnp.swapaxes(logf_all, 1, 2), ((0, 0), (0, 0), (0, kpad - past - t)))
    lbs = cache_b_k.shape[2]
    bias_sb = _sample_bias(bias_b_of, t, lbs)
    e3 = np.zeros((H_A * t, LANES), np.float32)
    for j in range(3):
        e3[np.arange(H_A * t), 8 * j + np.arange(H_A * t) // t] = 1.0
    keys_minor = lambda cache: jnp.transpose(cache[0], (0, 2, 3, 1))
    oa_s, ob_s = _sample_even(ps, keys_minor(cache_a_k), keys_minor(cache_a_v), logf_t,
                              keys_minor(cache_b_k), keys_minor(cache_b_v),
                              jnp.asarray(e3, BF16), bias_sb, t=t, kc=4096)
    xs1 = _out_ffn(xs, [(oa_s, w_oe[:W_A]), (ob_s, w_oe[W_A:])], g_ffn[0], w_fi, w_fo, g_fin,
                   tm=256, layer=0, final_norm=False)

    ps2 = _sample_proj(xs1, g_mix[1], w_odd, tm=256)
    lcs = cache_c_k.shape[2]
    bias_sc = _sample_bias(bias_c_of, t, lcs)
    sink_col = jnp.repeat(sinks_c[0].astype(F32), t)[:, None]
    lane_head = np.arange(H_C * HEAD_DIM) // HEAD_DIM
    src_lane = (lane_head // G_C) * HEAD_DIM + np.arange(H_C * HEAD_DIM) % HEAD_DIM
    xexp = jnp.asarray(np.arange(LANES)[:, None] == src_lane[None, :], BF16)
    oc_s = _sample_odd(ps2, cache_c_k[0].reshape(nb, lcs, LANES), cache_c_v[0].reshape(nb, lcs, LANES),
                       xexp, bias_sc, sink_col, t=t, nbs=4)
    y_sample = _out_ffn(xs1, [(oc_s, w_oo)], g_ffn[1], w_fi, w_fo, g_fin, tm=256, layer=1, final_norm=True)

    wq = H_C * HEAD_DIM
    hd = lambda a, lead, h: a.reshape((1,) + lead + (h, HEAD_DIM))
    return (
        y_prompt.reshape(b, s, d), y_sample.reshape(nb, t, d),
        hd(ka, (b, s), H_A), hd(va, (b, s), H_A), logf[None],
        hd(kbt, (b, kbt.shape[1]), H_B), hd(vbt, (b, vbt.shape[1]), H_B),
        hd(kct, (b, kct.shape[1]), HKV_C), hd(vct, (b, vct.shape[1]), HKV_C),
        hd(ps[:, W_A:2 * W_A], (nb, t), H_A), hd(ps[:, 2 * W_A:3 * W_A], (nb, t), H_A),
        logf_s.reshape(1, nb, t, H_A),
        hd(ps[:, 3 * W_A + W_B:3 * W_A + 2 * W_B], (nb, t), H_B),
        hd(ps[:, 3 * W_A + 2 * W_B:3 * W_A + 3 * W_B], (nb, t), H_B),
        hd(ps2[:, wq:wq + LANES], (nb, t), HKV_C), hd(ps2[:, wq + LANES:wq + 2 * LANES], (nb, t), HKV_C),
    )
```

```python
import functools
import math

import jax
import jax.numpy as jnp
import numpy as np
from jax import lax
from jax.experimental import pallas as pl
from jax.experimental.pallas import tpu as pltpu

D_MODEL = 1024
HEAD_DIM = 64
CHUNK = 64
H_A = 8
H_B = 8
B_LEFT_CHUNKS = 8
B_REL_CLIP = 128
H_C = 16
HKV_C = 2
G_C = H_C // HKV_C
WINDOW = 128
C_LEFT_CHUNKS = WINDOW // CHUNK
T5_BUCKETS = 32
T5_MAX_DIST = 128
EPS = 1e-6
W_A = H_A * HEAD_DIM
W_B = H_B * HEAD_DIM
SCALE = HEAD_DIM ** -0.5
LOG2E = math.log2(math.e)

LANES = 128
VMEM_LIMIT = 60 * 1024 * 1024

F32 = jnp.float32
BF16 = jnp.bfloat16
NEG_INF = float("-inf")

_NT = (((1,), (1,)), ((), ()))


def _dot(a, b):
    return jnp.dot(a, b, preferred_element_type=F32)


def _dot_nt(a, b):
    return lax.dot_general(a, b, _NT, preferred_element_type=F32)


def _rmsnorm(x, g):
    ms = jnp.mean(x * x, axis=-1, keepdims=True)
    return x * lax.rsqrt(ms + EPS) * g


def _log_sigmoid(x):
    return jnp.minimum(x, 0.0) - jnp.log1p(jnp.exp(-jnp.abs(x)))


def _split3(x):
    hi = x.astype(BF16).astype(F32)
    r1 = x - hi
    mid = r1.astype(BF16).astype(F32)
    lo = (r1 - mid).astype(BF16).astype(F32)
    return hi, mid, lo


def _cumsum_rows(x):
    n = x.shape[0]
    row = lax.broadcasted_iota(jnp.int32, x.shape, 0)
    s = 1
    while s < n:
        x = x + jnp.where(row >= s, pltpu.roll(x, s, 0), 0.0)
        s *= 2
    return x


def _cumsum_lanes(x):
    n = x.shape[1]
    col = lax.broadcasted_iota(jnp.int32, x.shape, 1)
    s = 1
    while s < n:
        x = x + jnp.where(col >= s, pltpu.roll(x, s, 1), 0.0)
        s *= 2
    return x


def _staggered(items, stage_a, stage_b, stage_c):
    n = len(items)
    a_out, b_out = {}, {}
    for step in range(n + 2):
        if step < n:
            a_out[step] = stage_a(items[step])
        if 0 <= step - 1 < n:
            b_out[step - 1] = stage_b(items[step - 1], a_out.pop(step - 1))
        if 0 <= step - 2 < n:
            stage_c(items[step - 2], b_out.pop(step - 2))


def _const_spec(shape):
    nd = len(shape)
    return pl.BlockSpec(shape, lambda *_: (0,) * nd, pipeline_mode=pl.Buffered(1))


def _params(sem):
    return pltpu.CompilerParams(dimension_semantics=sem, vmem_limit_bytes=VMEM_LIMIT)


def _even_proj_kernel(x_ref, g_ref, w_ref, bf_ref, pq_ref, pk_ref,
                      qaug_ref, kaug_ref, vaug_ref, ka_ref, va_ref, logf_ref,
                      qb_ref, kb_ref, vb_ref, kbt_ref, vbt_ref, carry_ref):
    @pl.when(pl.program_id(1) == 0)
    def _():
        carry_ref[...] = jnp.zeros_like(carry_ref)

    h = _rmsnorm(x_ref[0], g_ref[...]).astype(BF16)
    tm = h.shape[0]
    lane = lax.broadcasted_iota(jnp.int32, (tm, LANES), 1)
    low = lane < HEAD_DIM
    first = W_B + LANES

    def chunk(n):
        return _dot(h, w_ref[:, first + W_A * n:first + W_A * (n + 1)])

    pf = _dot(h, w_ref[:, :first])
    qb_ref[0] = (pf[:, :W_B] * (SCALE * LOG2E)).astype(BF16)
    fa = pf[:, W_B:]
    pc = chunk(3)
    kb_ref[0] = pc.astype(BF16)
    kbt_ref[0] = pc
    pc = chunk(4)
    vb_ref[0] = pc.astype(BF16)
    vbt_ref[0] = pc
    logf = _log_sigmoid(fa + bf_ref[...])
    logf_ref[0] = logf[:, :H_A]
    c = _cumsum_rows(logf) + carry_ref[...]
    carry_ref[...] = c[tm - 1:tm, :]
    hi, mid, lo = _split3(c * LOG2E)
    a3 = jnp.where(lane < 8, hi, jnp.where(lane < 16, mid, jnp.where(lane < 24, lo,
                   jnp.where(lane == 24, 1.0, 0.0)))).astype(BF16)
    augq = _dot(a3, pq_ref[...])
    augk = _dot(a3, pk_ref[...])
    vone = jnp.where(lane == HEAD_DIM, 1.0, 0.0)

    def per_head(dst_ref, pc, spare):
        for j in range(H_A // 2):
            pair = pc[:, LANES * j:LANES * (j + 1)]
            for hh, val in enumerate((pair, pltpu.roll(pair, HEAD_DIM, 1))):
                o = slice(LANES * (2 * j + hh), LANES * (2 * j + hh + 1))
                dst_ref[0, :, o] = jnp.where(low, val, vone if spare is None else spare[:, o]).astype(BF16)

    pc = chunk(0)
    per_head(qaug_ref, pc * (SCALE * LOG2E), augq)
    pc = chunk(1)
    ka_ref[0] = pc
    per_head(kaug_ref, pc, augk)
    pc = chunk(2)
    va_ref[0] = pc
    per_head(vaug_ref, pc, None)


def _even_proj(x, g, w_all, b_f, pq, pk, tm):
    b, s, d = x.shape
    nt = s // tm
    lb = min(B_LEFT_CHUNKS * CHUNK, s)
    assert lb == tm, "band-state tail must be exactly one row tile"
    row = lambda w: pl.BlockSpec((1, tm, w), lambda bi, ti: (bi, ti, 0))
    tail = pl.BlockSpec((1, lb, W_B), lambda bi, ti: (bi, 0, 0))
    outs = (
        jax.ShapeDtypeStruct((b, s, H_A * LANES), BF16),
        jax.ShapeDtypeStruct((b, s, H_A * LANES), BF16),
        jax.ShapeDtypeStruct((b, s, H_A * LANES), BF16),
        jax.ShapeDtypeStruct((b, s, W_A), F32),
        jax.ShapeDtypeStruct((b, s, W_A), F32),
        jax.ShapeDtypeStruct((b, s, H_A), F32),
        jax.ShapeDtypeStruct((b, s, W_B), BF16),
        jax.ShapeDtypeStruct((b, s, W_B), BF16),
        jax.ShapeDtypeStruct((b, s, W_B), BF16),
        jax.ShapeDtypeStruct((b, lb, W_B), F32),
        jax.ShapeDtypeStruct((b, lb, W_B), F32),
    )
    return pl.pallas_call(
        _even_proj_kernel,
        grid=(b, nt),
        in_specs=[row(d), _const_spec((1, d)), _const_spec(w_all.shape),
                  _const_spec((1, LANES)), _const_spec(pq.shape), _const_spec(pk.shape)],
        out_specs=(row(H_A * LANES), row(H_A * LANES), row(H_A * LANES), row(W_A), row(W_A), row(H_A),
                   row(W_B), row(W_B), row(W_B), tail, tail),
        out_shape=outs,
        scratch_shapes=[pltpu.VMEM((1, LANES), F32)],
        compiler_params=_params(("arbitrary", "arbitrary")),
        name="even_proj",
    )(x, g, w_all, b_f, pq, pk)


def _fox_kernel(q_ref, k_ref, v_ref, o_ref, m_sc, acc_sc, *, blk, nsub):
    i = pl.program_id(2)
    nh = 2
    chains = [(sub, hh) for sub in range(nsub) for hh in range(nh)]
    qs = {(sub, hh): q_ref[0, sub * blk:(sub + 1) * blk, LANES * hh:LANES * (hh + 1)] for sub, hh in chains}
    r = lax.broadcasted_iota(jnp.int32, (blk, blk), 0)
    c = lax.broadcasted_iota(jnp.int32, (blk, blk), 1)
    causal = c <= r
    lane = lax.broadcasted_iota(jnp.int32, (blk, LANES), 1)

    def reset(n):
        m_sc[n] = jnp.full((blk, LANES), NEG_INF, F32)
        acc_sc[n] = jnp.zeros((blk, LANES), F32)

    @pl.when((pl.program_id(0) == 0) & (pl.program_id(1) == 0) & (i == 0))
    def _():
        for n in range(len(chains)):
            reset(n)

    def finish(sub):
        n0, n1 = chains.index((sub, 0)), chains.index((sub, 1))
        a0, a1 = acc_sc[n0], acc_sc[n1]
        o0 = a0 / a0[:, HEAD_DIM:HEAD_DIM + 1]
        o1 = a1 / a1[:, HEAD_DIM:HEAD_DIM + 1]
        o_ref[0, sub * blk:(sub + 1) * blk, :] = jnp.where(
            lane < HEAD_DIM, o0, pltpu.roll(o1, HEAD_DIM, 1)).astype(BF16)
        reset(n0)
        reset(n1)

    def blocks(specs):
        items = [(pl.multiple_of(j * blk, blk), ch, modes[ch[0]])
                 for j, modes in specs for ch in chains if modes[ch[0]] is not None]

        def stage_scores(item):
            start, ch, mode = item
            k = k_ref[0, pl.ds(start, blk), LANES * ch[1]:LANES * (ch[1] + 1)]
            s = _dot_nt(qs[ch], k)
            if mode == "masked":
                s = jnp.where(causal, s, NEG_INF)
            m = m_sc[chains.index(ch)]
            m_new = jnp.maximum(m, jnp.max(s, axis=1, keepdims=True))
            m_sc[chains.index(ch)] = m_new
            return s, m, m_new

        def stage_exp(item, sm):
            s, m, m_new = sm
            p = jnp.concatenate([jnp.exp2(s[:, LANES * cb:LANES * (cb + 1)] - m_new)
                                 for cb in range(blk // LANES)], axis=1)
            return p.astype(BF16), jnp.exp2(m - m_new)

        def stage_pv(item, pa):
            start, ch, mode = item
            p, alpha = pa
            v = v_ref[0, pl.ds(start, blk), LANES * ch[1]:LANES * (ch[1] + 1)]
            n = chains.index(ch)
            acc_sc[n] = alpha * acc_sc[n] + _dot(p, v)
            if mode == "masked" and ch[1] == nh - 1:
                finish(ch[0])

        _staggered(items, stage_scores, stage_exp, stage_pv)

    @pl.loop(0, i)
    def _(jj):
        blocks([(nsub * jj + d, ["full"] * nsub) for d in range(nsub)])

    blocks([(nsub * i + d, [None if sub < d else ("masked" if sub == d else "full") for sub in range(nsub)])
            for d in range(nsub)])


def _fox(qaug, kaug, vaug, blk, nsub):
    b, s, _ = qaug.shape
    pairs = H_A // 2
    tq = blk * nsub
    return pl.pallas_call(
        functools.partial(_fox_kernel, blk=blk, nsub=nsub),
        grid=(b, pairs, s // tq),
        in_specs=[pl.BlockSpec((1, tq, 2 * LANES), lambda bi, hp, i: (bi, i, hp)),
                  pl.BlockSpec((1, s, 2 * LANES), lambda bi, hp, i: (bi, 0, hp)),
                  pl.BlockSpec((1, s, 2 * LANES), lambda bi, hp, i: (bi, 0, hp))],
        out_specs=pl.BlockSpec((1, tq, LANES), lambda bi, hp, i: (bi, i, hp)),
        out_shape=jax.ShapeDtypeStruct((b, s, W_A), BF16),
        scratch_shapes=[pltpu.VMEM((2 * nsub, blk, LANES), F32), pltpu.VMEM((2 * nsub, blk, LANES), F32)],
        compiler_params=_params(("arbitrary", "arbitrary", "arbitrary")),
        name="fox",
    )(qaug, kaug, vaug)


def _band_kernel(*refs, tq, left, npairs, heads_per_group, use_sinks, pairs_per_kv):
    mxu_rowsum = not use_sinks
    if use_sinks:
        sink_ref, q_ref, k_ref, v_ref, bias_ref, o_ref = refs
    else:
        q_ref, k_ref, v_ref, bias_ref, o_ref = refs
    g = pl.program_id(1)
    i = pl.program_id(2)
    nvar, _, tsub, win = bias_ref.shape
    nsq = tq // tsub
    lane = lax.broadcasted_iota(jnp.int32, (tsub, LANES), 1)
    low = lane < HEAD_DIM
    items = [(sq, pr, hh) for sq in range(nsq) for pr in range(npairs) for hh in range(2)]

    def window(sq):
        gsub = i * nsq + sq
        return pl.multiple_of(jnp.maximum(gsub * tsub - left, 0), LANES), jnp.minimum(gsub, nvar - 1)

    def kv_lanes(pr):
        kl = LANES * (pr // pairs_per_kv)
        return slice(kl, kl + LANES)

    def sink(pr, hh):
        return sink_ref[g * heads_per_group + 2 * pr + hh]

    def stage_scores(item):
        sq, pr, hh = item
        start, var = window(sq)
        qp = q_ref[0, sq * tsub:(sq + 1) * tsub, LANES * pr:LANES * (pr + 1)]
        qm = jnp.where(low if hh == 0 else jnp.logical_not(low), qp, jnp.zeros_like(qp))
        s = _dot_nt(qm, k_ref[0, pl.ds(start, win), kv_lanes(pr)]) + bias_ref[var, 2 * pr + hh]
        m = jnp.max(s, axis=1, keepdims=True)
        return s, (jnp.maximum(m, sink(pr, hh)) if use_sinks else m)

    def stage_exp(item, sm):
        s, m = sm
        e = jnp.exp2(s - m)
        l = None if mxu_rowsum else jnp.sum(e, axis=1, keepdims=True)
        if use_sinks:
            l = l + jnp.exp2(sink(item[1], item[2]) - m)
        return e.astype(BF16), l

    outs = {}

    def stage_pv(item, pe):
        p, l = pe
        sq, pr, hh = item
        v = v_ref[0, pl.ds(window(sq)[0], win), kv_lanes(pr)]
        if mxu_rowsum:
            ov = _dot(p, jnp.concatenate([v, jnp.ones((win, LANES), BF16)], axis=1))
            outs[hh] = ov[:, :LANES] / ov[:, LANES:]
        else:
            outs[hh] = _dot(p, v) / l
        if hh == 1:
            o_ref[0, sq * tsub:(sq + 1) * tsub, LANES * pr:LANES * (pr + 1)] = jnp.where(
                low, outs[0], outs[1]).astype(BF16)

    _staggered(items, stage_scores, stage_exp, stage_pv)


def _band(q, k, v, bias, sinks, *, tq, left, npairs, pairs_per_kv, name):
    b, s, wq = q.shape
    tsub, win = bias.shape[-2:]
    wblk = npairs * LANES
    groups = wq // wblk
    hpg = 2 * npairs
    use_sinks = sinks is not None
    kern = functools.partial(_band_kernel, tq=tq, left=left, npairs=npairs,
                             heads_per_group=hpg, use_sinks=use_sinks, pairs_per_kv=pairs_per_kv)
    kvw = wblk // pairs_per_kv
    resident = lambda shape, imap: pl.BlockSpec(shape, imap, pipeline_mode=pl.Buffered(1))
    in_specs = [pl.BlockSpec((1, tq, wblk), lambda bi, g, i: (bi, i, g)),
                resident((1, s, kvw), lambda bi, g, i: (bi, 0, g)),
                resident((1, s, kvw), lambda bi, g, i: (bi, 0, g)),
                resident((bias.shape[0], hpg, tsub, win), lambda bi, g, i: (0, g, 0, 0))]
    args = [q, k, v, bias]
    if use_sinks:
        in_specs = [pl.BlockSpec(memory_space=pltpu.SMEM)] + in_specs
        args = [sinks] + args
    return pl.pallas_call(
        kern,
        grid=(b, groups, s // tq),
        in_specs=in_specs,
        out_specs=pl.BlockSpec((1, tq, wblk), lambda bi, g, i: (bi, i, g)),
        out_shape=jax.ShapeDtypeStruct((b, s, wq), BF16),
        compiler_params=_params(("arbitrary", "arbitrary", "arbitrary")),
        name=name,
    )(*args)


def _out_ffn_kernel(*refs, n_attn, d_ff, final_norm):
    x_ref = refs[0]
    attn = refs[1:1 + 2 * n_attn]
    gffn_ref, win_ref, wout_ref, gfin_ref, o_ref = refs[1 + 2 * n_attn:]
    y = x_ref[...]
    for a in range(n_attn):
        y = y + _dot(attn[2 * a][...], attn[2 * a + 1][...])
    h = _rmsnorm(y, gffn_ref[...]).astype(BF16)
    gu = _dot(h, win_ref[...])
    gate = gu[:, :d_ff]
    up = gu[:, d_ff:]
    act = (gate * (1.0 / (1.0 + jnp.exp(-gate))) * up).astype(BF16)
    y = y + _dot(act, wout_ref[...])
    o_ref[...] = _rmsnorm(y, gfin_ref[...]) if final_norm else y


def _out_ffn(x, attn_pairs, g_ffn, w_in, w_out, g_fin, tm, layer, final_norm):
    n, d = x.shape
    d_ff = w_out.shape[1]
    row = lambda w: pl.BlockSpec((tm, w), lambda i: (i, 0))
    slab = lambda w: pl.BlockSpec((None,) + w.shape[1:], lambda i: (layer, 0, 0), pipeline_mode=pl.Buffered(1))
    args, specs = [x], [row(d)]
    for o, w in attn_pairs:
        args += [o, w]
        specs += [row(o.shape[1]), _const_spec(w.shape)]
    args += [g_ffn, w_in, w_out, g_fin]
    specs += [_const_spec((1, d)), slab(w_in), slab(w_out), _const_spec((1, d))]
    return pl.pallas_call(
        functools.partial(_out_ffn_kernel, n_attn=len(attn_pairs), d_ff=d_ff, final_norm=final_norm),
        grid=(n // tm,),
        in_specs=specs,
        out_specs=row(d),
        out_shape=jax.ShapeDtypeStruct((n, d), F32),
        compiler_params=_params(("arbitrary",)),
        name="out_ffn",
    )(*args)


def _odd_proj_kernel(x_ref, g_ref, w_ref, q_ref, k_ref, v_ref, kt_ref, vt_ref, *, tail):
    h = _rmsnorm(x_ref[0], g_ref[...]).astype(BF16)
    wq = H_C * HEAD_DIM
    tm = h.shape[0]
    lane = lax.broadcasted_iota(jnp.int32, (tm, LANES), 1)
    low = lane < HEAD_DIM
    kv = _dot(h, w_ref[:, wq:])
    for src, dst, tail_ref in ((kv[:, :LANES], k_ref, kt_ref), (kv[:, LANES:], v_ref, vt_ref)):
        rolled = pltpu.roll(src, HEAD_DIM, 1)
        dst[0, :, :LANES] = jnp.where(low, src, rolled).astype(BF16)
        dst[0, :, LANES:] = jnp.where(low, rolled, src).astype(BF16)
        tail_ref[0] = src[tm - tail:, :]
    half = wq // 2
    for c in range(2):
        q_ref[0, :, half * c:half * (c + 1)] = (
            _dot(h, w_ref[:, half * c:half * (c + 1)]) * (SCALE * LOG2E)).astype(BF16)


def _odd_proj(x, g, w, tm):
    b, s, d = x.shape
    nt = s // tm
    lc = min(C_LEFT_CHUNKS * CHUNK, s)
    assert lc <= tm
    wq = H_C * HEAD_DIM
    row = lambda w_: pl.BlockSpec((1, tm, w_), lambda bi, ti: (bi, ti, 0))
    tail = pl.BlockSpec((1, lc, LANES), lambda bi, ti: (bi, 0, 0))
    return pl.pallas_call(
        functools.partial(_odd_proj_kernel, tail=lc),
        grid=(b, nt),
        in_specs=[row(d), _const_spec((1, d)), _const_spec(w.shape)],
        out_specs=(row(wq), row(2 * LANES), row(2 * LANES), tail, tail),
        out_shape=(jax.ShapeDtypeStruct((b, s, wq), BF16),
                   jax.ShapeDtypeStruct((b, s, 2 * LANES), BF16),
                   jax.ShapeDtypeStruct((b, s, 2 * LANES), BF16),
                   jax.ShapeDtypeStruct((b, lc, LANES), F32),
                   jax.ShapeDtypeStruct((b, lc, LANES), F32)),
        compiler_params=_params(("arbitrary", "arbitrary")),
        name="odd_proj",
    )(x, g, w)


def _sample_proj_kernel(*refs, gated):
    if gated:
        x_ref, g_ref, w_ref, wf_ref, bf_ref, p_ref, logf_ref = refs
    else:
        x_ref, g_ref, w_ref, p_ref = refs
    h = _rmsnorm(x_ref[...], g_ref[...]).astype(BF16)
    p_ref[...] = _dot(h, w_ref[...])
    if gated:
        logf_ref[...] = _log_sigmoid(_dot(h, wf_ref[...]) + bf_ref[...])


def _sample_proj(x, g, w, tm, w_f=None, b_f=None):
    n, d = x.shape
    row = lambda w_: pl.BlockSpec((tm, w_), lambda i: (i, 0))
    in_specs = [row(d), _const_spec((1, d)), _const_spec(w.shape)]
    args = [x, g, w]
    out_specs = row(w.shape[1])
    out_shape = jax.ShapeDtypeStruct((n, w.shape[1]), F32)
    if w_f is not None:
        in_specs += [_const_spec(w_f.shape), _const_spec((1, LANES))]
        args += [w_f, b_f]
        out_specs = (out_specs, row(LANES))
        out_shape = (out_shape, jax.ShapeDtypeStruct((n, LANES), F32))
    return pl.pallas_call(
        functools.partial(_sample_proj_kernel, gated=w_f is not None),
        grid=(n // tm,),
        in_specs=in_specs,
        out_specs=out_specs,
        out_shape=out_shape,
        compiler_params=_params(("arbitrary",)),
        name="sample_proj",
    )(*args)


def _block_diag_q(q, nheads, scale):
    t, w = q.shape
    tiled = jnp.concatenate([q] * nheads, axis=0)
    r = lax.broadcasted_iota(jnp.int32, (nheads * t, w), 0)
    c = lax.broadcasted_iota(jnp.int32, (nheads * t, w), 1)
    return jnp.where(r // t == c // HEAD_DIM, tiled * scale, 0.0).astype(BF16)


def _block_diag_extract(o_all, nheads, t):
    w = o_all.shape[1]
    c = lax.broadcasted_iota(jnp.int32, (t, w), 1)
    out = jnp.zeros((t, w), F32)
    for h in range(nheads):
        out = jnp.where(c // HEAD_DIM == h, o_all[h * t:(h + 1) * t, :], out)
    return out


def _heads_major(ref):
    _, nh, hd, nk = ref.shape
    return ref[0].reshape(nh * hd, nk).astype(BF16)


def _pad_rows(x, rows):
    return jnp.concatenate([x, jnp.zeros((rows - x.shape[0], x.shape[1]), x.dtype)], axis=0)


def _sample_even_kernel(qa_ref, kan_ref, van_ref, qb_ref, kbn_ref, vbn_ref, ck_ref, cv_ref, lft_ref,
                        cbk_ref, cbv_ref, e3_ref, biasb_ref, oa_ref, ob_ref,
                        bias_sc, m_sc, acc_sc, *, t, past, kc, nkc):
    c = pl.program_id(1)
    rows = H_A * t
    qbd = _block_diag_q(qa_ref[...], H_A, SCALE)

    @pl.when(c == 0)
    def _():
        cum = _cumsum_lanes(lft_ref[0])
        last = past + t - 1
        suffix = cum[:, last:last + 1] - cum
        hi, mid, lo = _split3(suffix)
        s3 = jnp.concatenate([hi, mid, lo, jnp.zeros((LANES - 3 * H_A, suffix.shape[1]), F32)], axis=0)
        s3 = s3.astype(BF16)
        for ch in range(nkc + 1):
            w = kc if ch < nkc else LANES
            bias_sc[ch, :, :w] = _dot(e3_ref[...], s3[:, ch * kc:ch * kc + w])
        m_sc[...] = jnp.full_like(m_sc, NEG_INF)
        acc_sc[...] = jnp.zeros_like(acc_sc)

    def update(s, pv):
        m = m_sc[...]
        m_new = jnp.maximum(m, jnp.max(s, axis=1, keepdims=True))
        p = jnp.exp(s - m_new)
        alpha = jnp.exp(m - m_new)
        acc = acc_sc[...]
        lsum = alpha * acc[:, W_A:W_A + 1] + jnp.sum(p, axis=1, keepdims=True)
        o = alpha * acc[:, :W_A] + pv(p.astype(BF16))
        acc_sc[:, :W_A] = o
        acc_sc[:, W_A:] = jnp.broadcast_to(lsum, (rows, LANES))
        m_sc[...] = m_new

    s = _dot(qbd, _heads_major(ck_ref)) + bias_sc[c]
    vt = _heads_major(cv_ref)
    update(s, lambda p: _dot_nt(p, vt))

    @pl.when(c == nkc - 1)
    def _():
        kn = _pad_rows(kan_ref[...], LANES).astype(BF16)
        vn = _pad_rows(van_ref[...], LANES).astype(BF16)
        sn = _dot_nt(qbd, kn) + bias_sc[nkc, :, :LANES]
        r = lax.broadcasted_iota(jnp.int32, sn.shape, 0)
        col = lax.broadcasted_iota(jnp.int32, sn.shape, 1)
        sn = jnp.where(col <= r % t, sn, NEG_INF)
        update(sn, lambda p: _dot(p, vn))
        acc = acc_sc[...]
        o_all = acc[:, :W_A] / acc[:, W_A:W_A + 1]
        oa_ref[...] = _block_diag_extract(o_all, H_A, t).astype(BF16)

        qbd_b = _block_diag_q(qb_ref[...], H_B, SCALE)
        lbk = cbk_ref.shape[3]
        kbn = _pad_rows(kbn_ref[...], LANES).astype(BF16)
        vbn = _pad_rows(vbn_ref[...], LANES).astype(BF16)
        sb = jnp.concatenate([_dot(qbd_b, _heads_major(cbk_ref)), _dot_nt(qbd_b, kbn)], axis=1) + biasb_ref[...]
        mb = jnp.max(sb, axis=1, keepdims=True)
        eb = jnp.exp(sb - mb)
        lb = jnp.sum(eb, axis=1, keepdims=True)
        eb = eb.astype(BF16)
        ob_all = (_dot_nt(eb[:, :lbk], _heads_major(cbv_ref)) + _dot(eb[:, lbk:], vbn)) / lb
        ob_ref[...] = _block_diag_extract(ob_all, H_B, t).astype(BF16)


def _sample_even(p, cache_k, cache_v, logf_t, cache_bk, cache_bv, e3, bias_b, *, t, kc):
    n = p.shape[0]
    nb = n // t
    past = cache_k.shape[3]
    nkc = past // kc
    lbk = cache_bk.shape[3]
    pcol = lambda j: pl.BlockSpec((t, W_A), lambda bi, c: (bi, j))
    rows = H_A * t
    return pl.pallas_call(
        functools.partial(_sample_even_kernel, t=t, past=past, kc=kc, nkc=nkc),
        grid=(nb, nkc),
        in_specs=[pcol(0), pcol(1), pcol(2), pcol(3), pcol(4), pcol(5),
                  pl.BlockSpec((1, H_A, HEAD_DIM, kc), lambda bi, c: (bi, 0, 0, c)),
                  pl.BlockSpec((1, H_A, HEAD_DIM, kc), lambda bi, c: (bi, 0, 0, c)),
                  pl.BlockSpec((1, H_A, logf_t.shape[2]), lambda bi, c: (bi, 0, 0)),
                  pl.BlockSpec((1, H_B, HEAD_DIM, lbk), lambda bi, c: (bi, 0, 0, 0)),
                  pl.BlockSpec((1, H_B, HEAD_DIM, lbk), lambda bi, c: (bi, 0, 0, 0)),
                  _const_spec(e3.shape), _const_spec(bias_b.shape)],
        out_specs=(pl.BlockSpec((t, W_A), lambda bi, c: (bi, 0)),
                   pl.BlockSpec((t, W_B), lambda bi, c: (bi, 0))),
        out_shape=(jax.ShapeDtypeStruct((n, W_A), BF16), jax.ShapeDtypeStruct((n, W_B), BF16)),
        scratch_shapes=[pltpu.VMEM((nkc + 1, rows, kc), F32),
                        pltpu.VMEM((rows, 1), F32),
                        pltpu.VMEM((rows, W_A + LANES), F32)],
        compiler_params=_params(("arbitrary", "arbitrary")),
        name="sample_even",
    )(p, p, p, p, p, p, cache_k, cache_v, logf_t, cache_bk, cache_bv, e3, bias_b)


def _sample_odd_kernel(q_ref, kn_ref, vn_ref, ck_ref, cv_ref, x_ref, bias_ref, sink_ref, o_ref, *, t, nbs):
    sk = sink_ref[...]
    for bb in range(nbs):
        rows = slice(bb * t, (bb + 1) * t)
        qbd = _block_diag_q(q_ref[rows, :], H_C, SCALE)
        kall = jnp.concatenate([ck_ref[bb], _pad_rows(kn_ref[rows, :], LANES)], axis=0).astype(BF16)
        vall = jnp.concatenate([cv_ref[bb], _pad_rows(vn_ref[rows, :], LANES)], axis=0).astype(BF16)
        kexp = _dot(kall, x_ref[...]).astype(BF16)
        vexp = _dot(vall, x_ref[...]).astype(BF16)
        s = _dot_nt(qbd, kexp) + bias_ref[...]
        m = jnp.maximum(jnp.max(s, axis=1, keepdims=True), sk)
        e = jnp.exp(s - m)
        l = jnp.sum(e, axis=1, keepdims=True) + jnp.exp(sk - m)
        o_all = _dot(e.astype(BF16), vexp) / l
        o_ref[rows, :] = _block_diag_extract(o_all, H_C, t).astype(BF16)


def _sample_odd(p, cache_k, cache_v, xexp, bias, sink_col, *, t, nbs):
    n = p.shape[0]
    nb = n // t
    wq = H_C * HEAD_DIM
    lc = cache_k.shape[1]
    rows = t * nbs
    return pl.pallas_call(
        functools.partial(_sample_odd_kernel, t=t, nbs=nbs),
        grid=(nb // nbs,),
        in_specs=[pl.BlockSpec((rows, wq), lambda bi: (bi, 0)),
                  pl.BlockSpec((rows, LANES), lambda bi: (bi, wq // LANES)),
                  pl.BlockSpec((rows, LANES), lambda bi: (bi, wq // LANES + 1)),
                  pl.BlockSpec((nbs, lc, LANES), lambda bi: (bi, 0, 0)),
                  pl.BlockSpec((nbs, lc, LANES), lambda bi: (bi, 0, 0)),
                  _const_spec(xexp.shape), _const_spec(bias.shape), _const_spec(sink_col.shape)],
        out_specs=pl.BlockSpec((rows, wq), lambda bi: (bi, 0)),
        out_shape=jax.ShapeDtypeStruct((n, wq), BF16),
        compiler_params=_params(("arbitrary",)),
        name="sample_odd",
    )(p, p, p, cache_k, cache_v, xexp, bias, sink_col)


def _t5_bucket(rel_mem):
    nb = T5_BUCKETS // 2
    max_exact = nb // 2
    n = jnp.abs(rel_mem)
    large = max_exact + (jnp.log(jnp.maximum(n, 1).astype(F32) / max_exact)
                         / math.log(T5_MAX_DIST / max_exact) * (nb - max_exact)).astype(jnp.int32)
    large = jnp.minimum(large, nb - 1)
    return jnp.where(rel_mem > 0, nb, 0) + jnp.where(n < max_exact, n, large)


def _bias_b_of_rel(table, rel):
    idx = np.clip(rel, -B_REL_CLIP, B_REL_CLIP) + B_REL_CLIP
    return table.astype(F32)[idx].T


def _bias_c_of_rel(table, rel):
    return table.astype(F32)[_t5_bucket(-jnp.asarray(rel, jnp.int32))].T


def _bias_tiles_kernel(f_ref, o_ref, *, tq, left, offsets, left_chunks, valid_cols):
    n = f_ref.shape[2]
    win = o_ref.shape[3]
    rows = jnp.broadcast_to(f_ref[0], (tq, n))
    wide = pltpu.roll(rows, n - (tq - 1), 1, stride=1, stride_axis=0)
    q = lax.broadcasted_iota(jnp.int32, (tq, win), 0)
    k = lax.broadcasted_iota(jnp.int32, (tq, win), 1)
    shift = CHUNK.bit_length() - 1
    for v, off in enumerate(offsets):
        if valid_cols is None:
            d = off // CHUNK + jnp.right_shift(q, shift) - jnp.right_shift(k, shift)
            ok = (d >= 0) & (d <= left_chunks)
        else:
            ok = k < valid_cols
        o_ref[v, 0] = jnp.where(ok, wide[:, left - off:left - off + win], NEG_INF)


def _bias_tiles(bias_of_rel, *, tq, left, win, offsets, left_chunks=None, valid_cols=None):
    assert CHUNK & (CHUNK - 1) == 0 and all((left - o) % LANES == 0 for o in offsets)
    ncols = win + left - min(offsets)
    n = tq + ncols - 1
    n_pad = -(-n // LANES) * LANES
    f_rev = bias_of_rel(tq - 1 + left - np.arange(n))
    nheads = f_rev.shape[0]
    f_rev = jnp.pad(f_rev, ((0, 0), (0, n_pad - n)))[:, None, :]
    return pl.pallas_call(
        functools.partial(_bias_tiles_kernel, tq=tq, left=left, offsets=tuple(offsets),
                          left_chunks=left_chunks, valid_cols=valid_cols),
        grid=(nheads,),
        in_specs=[pl.BlockSpec((1, 1, n_pad), lambda h: (h, 0, 0))],
        out_specs=pl.BlockSpec((len(offsets), 1, tq, win), lambda h: (0, h, 0, 0)),
        out_shape=jax.ShapeDtypeStruct((len(offsets), nheads, tq, win), F32),
        compiler_params=_params(("arbitrary",)),
        name="bias_tiles",
    )(f_rev)


def _sample_bias(bias_of_rel, t, cache_len):
    ncols = cache_len + LANES
    tiles = _bias_tiles(bias_of_rel, tq=t, left=cache_len, win=ncols, offsets=[cache_len],
                        valid_cols=cache_len + t)
    return tiles.reshape(-1, ncols)


def _cast_kernel(x_ref, o_ref):
    o_ref[...] = x_ref[...].astype(o_ref.dtype)


def _cast_bf16(w, rows):
    nl, r, c = w.shape
    w2 = w.reshape(nl * r, c)
    out = pl.pallas_call(
        _cast_kernel,
        grid=(nl * r // rows,),
        in_specs=[pl.BlockSpec((rows, c), lambda i: (i, 0))],
        out_specs=pl.BlockSpec((rows, c), lambda i: (i, 0)),
        out_shape=jax.ShapeDtypeStruct(w2.shape, BF16),
        compiler_params=_params(("arbitrary",)),
        name="cast_bf16",
    )(w2)
    return out.reshape(nl, r, c)


def _placement():
    pq = np.zeros((LANES, H_A * LANES), np.float32)
    pk = np.zeros((LANES, H_A * LANES), np.float32)
    for h in range(H_A):
        for j in range(3):
            pq[8 * j + h, LANES * h + HEAD_DIM + j] = 1.0
            pq[24, LANES * h + HEAD_DIM + 3 + j] = 1.0
            pk[24, LANES * h + HEAD_DIM + j] = 1.0
            pk[8 * j + h, LANES * h + HEAD_DIM + 3 + j] = -1.0
    return jnp.asarray(pq, BF16), jnp.asarray(pk, BF16)


def kernel(x_prompt, x_sample, cache_a_k, cache_a_v, cache_a_logf, cache_b_k, cache_b_v, cache_c_k, cache_c_v,
           norm_mix, norm_ffn, norm_final, w_in_even, b_forget, rel_bias_b, w_out_even, w_in_odd, sinks_c,
           w_out_odd, t5_bias, w_ffn_in, w_ffn_out):
    b, s, d = x_prompt.shape
    nb, t, _ = x_sample.shape
    past = cache_a_k.shape[2]
    n_p, n_s = b * s, nb * t

    w_even = w_in_even[0]
    w_main = w_even[:, :3 * W_A + 3 * W_B].astype(BF16)
    wf = w_even[:, 3 * W_A + 3 * W_B:]
    w_f = jnp.concatenate([wf, wf, wf, jnp.zeros((d, LANES - 3 * H_A), F32)], axis=1).astype(BF16)
    bf = b_forget[0].astype(F32)
    b_f = jnp.concatenate([bf, bf, bf, jnp.zeros((LANES - 3 * H_A,), F32)])[None, :]
    pq, pk = _placement()
    w_oe = w_out_even[0].astype(BF16)
    w_oo = w_out_odd[0].astype(BF16)
    w_odd = w_in_odd[0].astype(BF16)
    w_fi = _cast_bf16(w_ffn_in, rows=512)
    w_fo = _cast_bf16(w_ffn_out, rows=1408)
    g_mix = norm_mix.astype(F32)[:, None, :]
    g_ffn = norm_ffn.astype(F32)[:, None, :]
    g_fin = norm_final.astype(F32)[None, :]

    bias_b_of = functools.partial(_bias_b_of_rel, rel_bias_b[0])
    bias_c_of = functools.partial(_bias_c_of_rel, t5_bias)
    log2_domain = lambda bias_of: (lambda rel: bias_of(rel) * LOG2E)
    variant_offsets = lambda tsub, left: [min(v * tsub, left) for v in range(left // tsub + 1)]
    tq_b, tsub_b, left_b = 1024, 256, B_LEFT_CHUNKS * CHUNK
    bias_b = _bias_tiles(log2_domain(bias_b_of), tq=tsub_b, left=left_b, win=tsub_b + left_b,
                         offsets=variant_offsets(tsub_b, left_b), left_chunks=B_LEFT_CHUNKS)
    tq_c, tsub_c, left_c = 512, 128, C_LEFT_CHUNKS * CHUNK
    bias_c = _bias_tiles(log2_domain(bias_c_of), tq=tsub_c, left=left_c, win=tsub_c + left_c,
                         offsets=variant_offsets(tsub_c, left_c), left_chunks=C_LEFT_CHUNKS)

    xp = x_prompt
    w_all = jnp.concatenate([w_main[:, 3 * W_A:3 * W_A + W_B], w_f, w_main[:, :3 * W_A],
                             w_main[:, 3 * W_A + W_B:]], axis=1)
    qaug, kaug, vaug, ka, va, logf, qb, kb, vb, kbt, vbt = _even_proj(
        xp, g_mix[0], w_all, b_f, pq, pk, tm=512)
    oa = _fox(qaug, kaug, vaug, blk=512, nsub=2)
    ob = _band(qb, kb, vb, bias_b, None, tq=tq_b, left=left_b, npairs=H_B // 2, pairs_per_kv=1,
               name="band_b")
    xp1 = _out_ffn(xp.reshape(n_p, d),
                   [(oa.reshape(n_p, W_A), w_oe[:W_A]), (ob.reshape(n_p, W_B), w_oe[W_A:])],
                   g_ffn[0], w_fi, w_fo, g_fin, tm=512, layer=0, final_norm=False)

    qc, kcd, vcd, kct, vct = _odd_proj(xp1.reshape(b, s, d), g_mix[1], w_odd, tm=512)
    oc = _band(qc, kcd, vcd, bias_c, sinks_c[0].astype(F32) * LOG2E, tq=tq_c, left=left_c, npairs=H_C // 2,
               pairs_per_kv=G_C // 2, name="band_c")
    y_prompt = _out_ffn(xp1, [(oc.reshape(n_p, H_C * HEAD_DIM), w_oo)],
                        g_ffn[1], w_fi, w_fo, g_fin, tm=512, layer=1, final_norm=True)

    xs = x_sample.reshape(n_s, d)
    ps, logf_sp = _sample_proj(xs, g_mix[0], w_main, tm=256, w_f=w_f, b_f=b_f)
    logf_s = logf_sp[:, :H_A]
    kpad = LANES * -(-(past + t) // LANES)
    logf_all = jnp.concatenate([cache_a_logf[0].astype(F32), logf_s.reshape(nb, t, H_A)], axis=1)
    logf_t = jnp.pad(jnp.swapaxes(logf_all, 1, 2), ((0, 0), (0, 0), (0, kpad - past - t)))
    lbs = cache_b_k.shape[2]
    bias_sb = _sample_bias(bias_b_of, t, lbs)
    e3 = np.zeros((H_A * t, LANES), np.float32)
    for j in range(3):
        e3[np.arange(H_A * t), 8 * j + np.arange(H_A * t) // t] = 1.0
    keys_minor = lambda cache: jnp.transpose(cache[0], (0, 2, 3, 1))
    oa_s, ob_s = _sample_even(ps, keys_minor(cache_a_k), keys_minor(cache_a_v), logf_t,
                              keys_minor(cache_b_k), keys_minor(cache_b_v),
                              jnp.asarray(e3, BF16), bias_sb, t=t, kc=4096)
    xs1 = _out_ffn(xs, [(oa_s, w_oe[:W_A]), (ob_s, w_oe[W_A:])], g_ffn[0], w_fi, w_fo, g_fin,
                   tm=256, layer=0, final_norm=False)

    ps2 = _sample_proj(xs1, g_mix[1], w_odd, tm=256)
    lcs = cache_c_k.shape[2]
    bias_sc = _sample_bias(bias_c_of, t, lcs)
    sink_col = jnp.repeat(sinks_c[0].astype(F32), t)[:, None]
    lane_head = np.arange(H_C * HEAD_DIM) // HEAD_DIM
    src_lane = (lane_head // G_C) * HEAD_DIM + np.arange(H_C * HEAD_DIM) % HEAD_DIM
    xexp = jnp.asarray(np.arange(LANES)[:, None] == src_lane[None, :], BF16)
    oc_s = _sample_odd(ps2, cache_c_k[0].reshape(nb, lcs, LANES), cache_c_v[0].reshape(nb, lcs, LANES),
                       xexp, bias_sc, sink_col, t=t, nbs=4)
    y_sample = _out_ffn(xs1, [(oc_s, w_oo)], g_ffn[1], w_fi, w_fo, g_fin, tm=256, layer=1, final_norm=True)

    wq = H_C * HEAD_DIM
    hd = lambda a, lead, h: a.reshape((1,) + lead + (h, HEAD_DIM))
    return (
        y_prompt.reshape(b, s, d), y_sample.reshape(nb, t, d),
        hd(ka, (b, s), H_A), hd(va, (b, s), H_A), logf[None],
        hd(kbt, (b, kbt.shape[1]), H_B), hd(vbt, (b, vbt.shape[1]), H_B),
        hd(kct, (b, kct.shape[1]), HKV_C), hd(vct, (b, vct.shape[1]), HKV_C),
        hd(ps[:, W_A:2 * W_A], (nb, t), H_A), hd(ps[:, 2 * W_A:3 * W_A], (nb, t), H_A),
        logf_s.reshape(1, nb, t, H_A),
        hd(ps[:, 3 * W_A + W_B:3 * W_A + 2 * W_B], (nb, t), H_B),
        hd(ps[:, 3 * W_A + 2 * W_B:3 * W_A + 3 * W_B], (nb, t), H_B),
        hd(ps2[:, wq:wq + LANES], (nb, t), HKV_C), hd(ps2[:, wq + LANES:wq + 2 * LANES], (nb, t), HKV_C),
    )
```

```python
import functools
import math

import jax
import jax.numpy as jnp
import numpy as np
from jax import lax
from jax.experimental import pallas as pl
from jax.experimental.pallas import tpu as pltpu

D_MODEL = 1024
HEAD_DIM = 64
CHUNK = 64
H_A = 8
H_B = 8
B_LEFT_CHUNKS = 8
B_REL_CLIP = 128
H_C = 16
HKV_C = 2
G_C = H_C // HKV_C
WINDOW = 128
C_LEFT_CHUNKS = WINDOW // CHUNK
T5_BUCKETS = 32
T5_MAX_DIST = 128
EPS = 1e-6
W_A = H_A * HEAD_DIM
W_B = H_B * HEAD_DIM
SCALE = HEAD_DIM ** -0.5
LOG2E = math.log2(math.e)

LANES = 128
VMEM_LIMIT = 60 * 1024 * 1024

F32 = jnp.float32
BF16 = jnp.bfloat16
NEG_INF = float("-inf")

_NT = (((1,), (1,)), ((), ()))


def _dot(a, b):
    return jnp.dot(a, b, preferred_element_type=F32)


def _dot_nt(a, b):
    return lax.dot_general(a, b, _NT, preferred_element_type=F32)


def _rmsnorm(x, g):
    ms = jnp.mean(x * x, axis=-1, keepdims=True)
    return x * lax.rsqrt(ms + EPS) * g


def _log_sigmoid(x):
    return jnp.minimum(x, 0.0) - jnp.log1p(jnp.exp(-jnp.abs(x)))


def _split3(x):
    hi = x.astype(BF16).astype(F32)
    r1 = x - hi
    mid = r1.astype(BF16).astype(F32)
    lo = (r1 - mid).astype(BF16).astype(F32)
    return hi, mid, lo


def _cumsum_rows(x):
    n = x.shape[0]
    row = lax.broadcasted_iota(jnp.int32, x.shape, 0)
    s = 1
    while s < n:
        x = x + jnp.where(row >= s, pltpu.roll(x, s, 0), 0.0)
        s *= 2
    return x


def _cumsum_lanes(x):
    n = x.shape[1]
    col = lax.broadcasted_iota(jnp.int32, x.shape, 1)
    s = 1
    while s < n:
        x = x + jnp.where(col >= s, pltpu.roll(x, s, 1), 0.0)
        s *= 2
    return x


def _staggered(items, stage_a, stage_b, stage_c):
    n = len(items)
    a_out, b_out = {}, {}
    for step in range(n + 2):
        if step < n:
            a_out[step] = stage_a(items[step])
        if 0 <= step - 1 < n:
            b_out[step - 1] = stage_b(items[step - 1], a_out.pop(step - 1))
        if 0 <= step - 2 < n:
            stage_c(items[step - 2], b_out.pop(step - 2))


def _const_spec(shape):
    nd = len(shape)
    return pl.BlockSpec(shape, lambda *_: (0,) * nd, pipeline_mode=pl.Buffered(1))


def _params(sem):
    return pltpu.CompilerParams(dimension_semantics=sem, vmem_limit_bytes=VMEM_LIMIT)


def _even_proj_kernel(x_ref, g_ref, w_ref, bf_ref, p_ref,
                      qaug_ref, kaug_ref, vaug_ref, ka_ref, va_ref, logf_ref,
                      qb_ref, kb_ref, vb_ref, kbt_ref, vbt_ref, carry_ref):
    @pl.when(pl.program_id(1) == 0)
    def _():
        carry_ref[...] = jnp.zeros_like(carry_ref)

    h = _rmsnorm(x_ref[0], g_ref[...]).astype(BF16)
    tm = h.shape[0]
    lane = lax.broadcasted_iota(jnp.int32, (tm, LANES), 1)
    low = lane < HEAD_DIM
    first = W_B + LANES

    def chunk(n):
        return _dot(h, w_ref[:, first + W_A * n:first + W_A * (n + 1)])

    pf = _dot(h, w_ref[:, :first])
    qb_ref[0] = (pf[:, :W_B] * (SCALE * LOG2E)).astype(BF16)
    fa = pf[:, W_B:]
    pc = chunk(3)
    kb_ref[0] = pc.astype(BF16)
    kbt_ref[0] = pc
    pc = chunk(4)
    vb_ref[0] = pc.astype(BF16)
    vbt_ref[0] = pc
    logf = _log_sigmoid(fa + bf_ref[...])
    logf_ref[0] = logf[:, :H_A]
    c = _cumsum_rows(logf) + carry_ref[...]
    carry_ref[...] = c[tm - 1:tm, :]
    hi, mid, lo = _split3(c * LOG2E)
    a3 = jnp.where(lane < 8, hi, jnp.where(lane < 16, mid, jnp.where(lane < 24, lo, 0.0))).astype(BF16)
    pieces = _dot(a3, p_ref[...])
    first3 = (lane >= HEAD_DIM) & (lane < HEAD_DIM + 3)
    next3 = (lane >= HEAD_DIM + 3) & (lane < HEAD_DIM + 6)
    vone = jnp.where(lane == HEAD_DIM, 1.0, 0.0)
    spare_q = lambda g: jnp.where(next3, 1.0, g)
    spare_k = lambda g: jnp.where(first3, 1.0, -g)

    def per_head(dst_ref, pc, spare):
        for j in range(H_A // 2):
            pair = pc[:, LANES * j:LANES * (j + 1)]
            for hh, val in enumerate((pair, pltpu.roll(pair, HEAD_DIM, 1))):
                o = slice(LANES * (2 * j + hh), LANES * (2 * j + hh + 1))
                upper = vone if spare is None else spare(pieces[:, o])
                dst_ref[0, :, o] = jnp.where(low, val, upper).astype(BF16)

    pc = chunk(0)
    per_head(qaug_ref, pc * (SCALE * LOG2E), spare_q)
    pc = chunk(1)
    ka_ref[0] = pc
    per_head(kaug_ref, pc, spare_k)
    pc = chunk(2)
    va_ref[0] = pc
    per_head(vaug_ref, pc, None)


def _even_proj(x, g, w_all, b_f, place, tm):
    b, s, d = x.shape
    nt = s // tm
    lb = min(B_LEFT_CHUNKS * CHUNK, s)
    assert lb == tm, "band-state tail must be exactly one row tile"
    row = lambda w: pl.BlockSpec((1, tm, w), lambda bi, ti: (bi, ti, 0))
    tail = pl.BlockSpec((1, lb, W_B), lambda bi, ti: (bi, 0, 0))
    outs = (
        jax.ShapeDtypeStruct((b, s, H_A * LANES), BF16),
        jax.ShapeDtypeStruct((b, s, H_A * LANES), BF16),
        jax.ShapeDtypeStruct((b, s, H_A * LANES), BF16),
        jax.ShapeDtypeStruct((b, s, W_A), F32),
        jax.ShapeDtypeStruct((b, s, W_A), F32),
        jax.ShapeDtypeStruct((b, s, H_A), F32),
        jax.ShapeDtypeStruct((b, s, W_B), BF16),
        jax.ShapeDtypeStruct((b, s, W_B), BF16),
        jax.ShapeDtypeStruct((b, s, W_B), BF16),
        jax.ShapeDtypeStruct((b, lb, W_B), F32),
        jax.ShapeDtypeStruct((b, lb, W_B), F32),
    )
    return pl.pallas_call(
        _even_proj_kernel,
        grid=(b, nt),
        in_specs=[row(d), _const_spec((1, d)), _const_spec(w_all.shape),
                  _const_spec((1, LANES)), _const_spec(place.shape)],
        out_specs=(row(H_A * LANES), row(H_A * LANES), row(H_A * LANES), row(W_A), row(W_A), row(H_A),
                   row(W_B), row(W_B), row(W_B), tail, tail),
        out_shape=outs,
        scratch_shapes=[pltpu.VMEM((1, LANES), F32)],
        compiler_params=_params(("arbitrary", "arbitrary")),
        name="even_proj",
    )(x, g, w_all, b_f, place)


def _fox_kernel(q_ref, k_ref, v_ref, o_ref, m_sc, acc_sc, *, blk, nsub):
    i = pl.program_id(2)
    nh = 2
    chains = [(sub, hh) for sub in range(nsub) for hh in range(nh)]
    qs = {(sub, hh): q_ref[0, sub * blk:(sub + 1) * blk, LANES * hh:LANES * (hh + 1)] for sub, hh in chains}
    r = lax.broadcasted_iota(jnp.int32, (blk, blk), 0)
    c = lax.broadcasted_iota(jnp.int32, (blk, blk), 1)
    causal = c <= r
    m_sc[...] = jnp.full_like(m_sc, NEG_INF)
    acc_sc[...] = jnp.zeros_like(acc_sc)

    def blocks(specs):
        items = [(pl.multiple_of(j * blk, blk), ch, modes[ch[0]])
                 for j, modes in specs for ch in chains if modes[ch[0]] is not None]

        def stage_scores(item):
            start, ch, mode = item
            k = k_ref[0, pl.ds(start, blk), LANES * ch[1]:LANES * (ch[1] + 1)]
            s = _dot_nt(qs[ch], k)
            if mode == "masked":
                s = jnp.where(causal, s, NEG_INF)
            m = m_sc[chains.index(ch)]
            m_new = jnp.maximum(m, jnp.max(s, axis=1, keepdims=True))
            m_sc[chains.index(ch)] = m_new
            return s, m, m_new

        def stage_exp(item, sm):
            s, m, m_new = sm
            p = jnp.concatenate([jnp.exp2(s[:, LANES * cb:LANES * (cb + 1)] - m_new)
                                 for cb in range(blk // LANES)], axis=1)
            return p.astype(BF16), jnp.exp2(m - m_new)

        def stage_pv(item, pa):
            start, ch, _ = item
            p, alpha = pa
            v = v_ref[0, pl.ds(start, blk), LANES * ch[1]:LANES * (ch[1] + 1)]
            n = chains.index(ch)
            acc_sc[n] = alpha * acc_sc[n] + _dot(p, v)

        _staggered(items, stage_scores, stage_exp, stage_pv)

    @pl.loop(0, i)
    def _(jj):
        blocks([(nsub * jj + d, ["full"] * nsub) for d in range(nsub)])

    blocks([(nsub * i + d, [None if sub < d else ("masked" if sub == d else "full") for sub in range(nsub)])
            for d in range(nsub)])

    lane = lax.broadcasted_iota(jnp.int32, (blk, LANES), 1)
    for sub in range(nsub):
        a0 = acc_sc[chains.index((sub, 0))]
        a1 = acc_sc[chains.index((sub, 1))]
        o0 = a0 / a0[:, HEAD_DIM:HEAD_DIM + 1]
        o1 = a1 / a1[:, HEAD_DIM:HEAD_DIM + 1]
        o_ref[0, sub * blk:(sub + 1) * blk, :] = jnp.where(
            lane < HEAD_DIM, o0, pltpu.roll(o1, HEAD_DIM, 1)).astype(BF16)


def _fox(qaug, kaug, vaug, blk, nsub):
    b, s, _ = qaug.shape
    pairs = H_A // 2
    tq = blk * nsub
    return pl.pallas_call(
        functools.partial(_fox_kernel, blk=blk, nsub=nsub),
        grid=(b, pairs, s // tq),
        in_specs=[pl.BlockSpec((1, tq, 2 * LANES), lambda bi, hp, i: (bi, i, hp)),
                  pl.BlockSpec((1, s, 2 * LANES), lambda bi, hp, i: (bi, 0, hp)),
                  pl.BlockSpec((1, s, 2 * LANES), lambda bi, hp, i: (bi, 0, hp))],
        out_specs=pl.BlockSpec((1, tq, LANES), lambda bi, hp, i: (bi, i, hp)),
        out_shape=jax.ShapeDtypeStruct((b, s, W_A), BF16),
        scratch_shapes=[pltpu.VMEM((2 * nsub, blk, LANES), F32), pltpu.VMEM((2 * nsub, blk, LANES), F32)],
        compiler_params=_params(("arbitrary", "arbitrary", "arbitrary")),
        name="fox",
    )(qaug, kaug, vaug)


def _band_kernel(*refs, tq, left, npairs, heads_per_group, use_sinks, pairs_per_kv, mxu_rowsum):
    if use_sinks:
        sink_ref, q_ref, k_ref, v_ref, bias_ref, o_ref = refs
    else:
        q_ref, k_ref, v_ref, bias_ref, o_ref = refs
    g = pl.program_id(1)
    i = pl.program_id(2)
    nvar, _, tsub, win = bias_ref.shape
    nsq = tq // tsub
    lane = lax.broadcasted_iota(jnp.int32, (tsub, LANES), 1)
    low = lane < HEAD_DIM
    items = [(sq, pr, hh) for sq in range(nsq) for pr in range(npairs) for hh in range(2)]

    def window(sq):
        gsub = i * nsq + sq
        return pl.multiple_of(jnp.maximum(gsub * tsub - left, 0), LANES), jnp.minimum(gsub, nvar - 1)

    def kv_lanes(pr):
        kl = LANES * (pr // pairs_per_kv)
        return slice(kl, kl + LANES)

    def sink(pr, hh):
        return sink_ref[g * heads_per_group + 2 * pr + hh]

    def stage_scores(item):
        sq, pr, hh = item
        start, var = window(sq)
        qp = q_ref[0, sq * tsub:(sq + 1) * tsub, LANES * pr:LANES * (pr + 1)]
        qm = jnp.where(low if hh == 0 else jnp.logical_not(low), qp, jnp.zeros_like(qp))
        s = _dot_nt(qm, k_ref[0, pl.ds(start, win), kv_lanes(pr)]) + bias_ref[var, 2 * pr + hh]
        m = jnp.max(s, axis=1, keepdims=True)
        return s, (jnp.maximum(m, sink(pr, hh)) if use_sinks else m)

    def stage_exp(item, sm):
        s, m = sm
        e = jnp.exp2(s - m)
        l = None if mxu_rowsum else jnp.sum(e, axis=1, keepdims=True)
        if use_sinks:
            l = l + jnp.exp2(sink(item[1], item[2]) - m)
        return e.astype(BF16), l

    outs = {}

    def stage_pv(item, pe):
        p, l = pe
        sq, pr, hh = item
        v = v_ref[0, pl.ds(window(sq)[0], win), kv_lanes(pr)]
        if mxu_rowsum:
            ov = _dot(p, jnp.concatenate([v, jnp.ones((win, LANES), BF16)], axis=1))
            outs[hh] = ov[:, :LANES] / ov[:, LANES:]
        else:
            outs[hh] = _dot(p, v) / l
        if hh == 1:
            o_ref[0, sq * tsub:(sq + 1) * tsub, LANES * pr:LANES * (pr + 1)] = jnp.where(
                low, outs[0], outs[1]).astype(BF16)

    _staggered(items, stage_scores, stage_exp, stage_pv)


def _band(q, k, v, bias, sinks, *, tq, left, npairs, pairs_per_kv, mxu_rowsum, name):
    assert not (mxu_rowsum and sinks is not None)
    b, s, wq = q.shape
    tsub, win = bias.shape[-2:]
    wblk = npairs * LANES
    groups = wq // wblk
    hpg = 2 * npairs
    use_sinks = sinks is not None
    kern = functools.partial(_band_kernel, tq=tq, left=left, npairs=npairs,
                             heads_per_group=hpg, use_sinks=use_sinks, pairs_per_kv=pairs_per_kv,
                             mxu_rowsum=mxu_rowsum)
    kvw = wblk // pairs_per_kv
    resident = lambda shape, imap: pl.BlockSpec(shape, imap, pipeline_mode=pl.Buffered(1))
    in_specs = [pl.BlockSpec((1, tq, wblk), lambda bi, g, i: (bi, i, g)),
                resident((1, s, kvw), lambda bi, g, i: (bi, 0, g)),
                resident((1, s, kvw), lambda bi, g, i: (bi, 0, g)),
                resident((bias.shape[0], hpg, tsub, win), lambda bi, g, i: (0, g, 0, 0))]
    args = [q, k, v, bias]
    if use_sinks:
        in_specs = [pl.BlockSpec(memory_space=pltpu.SMEM)] + in_specs
        args = [sinks] + args
    return pl.pallas_call(
        kern,
        grid=(b, groups, s // tq),
        in_specs=in_specs,
        out_specs=pl.BlockSpec((1, tq, wblk), lambda bi, g, i: (bi, i, g)),
        out_shape=jax.ShapeDtypeStruct((b, s, wq), BF16),
        compiler_params=_params(("arbitrary", "arbitrary", "arbitrary")),
        name=name,
    )(*args)


def _out_ffn_kernel(*refs, n_attn, d_ff, final_norm):
    x_ref = refs[0]
    attn = refs[1:1 + 2 * n_attn]
    gffn_ref, win_ref, wout_ref, gfin_ref, o_ref = refs[1 + 2 * n_attn:]
    y = x_ref[...]
    for a in range(n_attn):
        y = y + _dot(attn[2 * a][...], attn[2 * a + 1][...])
    h = _rmsnorm(y, gffn_ref[...]).astype(BF16)
    gu = _dot(h, win_ref[...])
    gate = gu[:, :d_ff]
    up = gu[:, d_ff:]
    act = (gate * (1.0 / (1.0 + jnp.exp(-gate))) * up).astype(BF16)
    y = y + _dot(act, wout_ref[...])
    o_ref[...] = _rmsnorm(y, gfin_ref[...]) if final_norm else y


def _out_ffn(x, attn_pairs, g_ffn, w_in, w_out, g_fin, tm, layer, final_norm):
    n, d = x.shape
    d_ff = w_out.shape[1]
    row = lambda w: pl.BlockSpec((tm, w), lambda i: (i, 0))
    slab = lambda w: pl.BlockSpec((None,) + w.shape[1:], lambda i: (layer, 0, 0), pipeline_mode=pl.Buffered(1))
    args, specs = [x], [row(d)]
    for o, w in attn_pairs:
        args += [o, w]
        specs += [row(o.shape[1]), _const_spec(w.shape)]
    args += [g_ffn, w_in, w_out, g_fin]
    specs += [_const_spec((1, d)), slab(w_in), slab(w_out), _const_spec((1, d))]
    return pl.pallas_call(
        functools.partial(_out_ffn_kernel, n_attn=len(attn_pairs), d_ff=d_ff, final_norm=final_norm),
        grid=(n // tm,),
        in_specs=specs,
        out_specs=row(d),
        out_shape=jax.ShapeDtypeStruct((n, d), F32),
        compiler_params=_params(("arbitrary",)),
        name="out_ffn",
    )(*args)


def _odd_proj_kernel(x_ref, g_ref, w_ref, q_ref, k_ref, v_ref, kt_ref, vt_ref, *, tail):
    h = _rmsnorm(x_ref[0], g_ref[...]).astype(BF16)
    wq = H_C * HEAD_DIM
    tm = h.shape[0]
    lane = lax.broadcasted_iota(jnp.int32, (tm, LANES), 1)
    low = lane < HEAD_DIM
    kv = _dot(h, w_ref[:, wq:])
    for src, dst, tail_ref in ((kv[:, :LANES], k_ref, kt_ref), (kv[:, LANES:], v_ref, vt_ref)):
        rolled = pltpu.roll(src, HEAD_DIM, 1)
        dst[0, :, :LANES] = jnp.where(low, src, rolled).astype(BF16)
        dst[0, :, LANES:] = jnp.where(low, rolled, src).astype(BF16)
        tail_ref[0] = src[tm - tail:, :]
    half = wq // 2
    for c in range(2):
        q_ref[0, :, half * c:half * (c + 1)] = (
            _dot(h, w_ref[:, half * c:half * (c + 1)]) * (SCALE * LOG2E)).astype(BF16)


def _odd_proj(x, g, w, tm):
    b, s, d = x.shape
    nt = s // tm
    lc = min(C_LEFT_CHUNKS * CHUNK, s)
    assert lc <= tm
    wq = H_C * HEAD_DIM
    row = lambda w_: pl.BlockSpec((1, tm, w_), lambda bi, ti: (bi, ti, 0))
    tail = pl.BlockSpec((1, lc, LANES), lambda bi, ti: (bi, 0, 0))
    return pl.pallas_call(
        functools.partial(_odd_proj_kernel, tail=lc),
        grid=(b, nt),
        in_specs=[row(d), _const_spec((1, d)), _const_spec(w.shape)],
        out_specs=(row(wq), row(2 * LANES), row(2 * LANES), tail, tail),
        out_shape=(jax.ShapeDtypeStruct((b, s, wq), BF16),
                   jax.ShapeDtypeStruct((b, s, 2 * LANES), BF16),
                   jax.ShapeDtypeStruct((b, s, 2 * LANES), BF16),
                   jax.ShapeDtypeStruct((b, lc, LANES), F32),
                   jax.ShapeDtypeStruct((b, lc, LANES), F32)),
        compiler_params=_params(("arbitrary", "arbitrary")),
        name="odd_proj",
    )(x, g, w)


def _sample_proj_kernel(*refs, gated):
    if gated:
        x_ref, g_ref, w_ref, wf_ref, bf_ref, p_ref, logf_ref = refs
    else:
        x_ref, g_ref, w_ref, p_ref = refs
    h = _rmsnorm(x_ref[...], g_ref[...]).astype(BF16)
    p_ref[...] = _dot(h, w_ref[...])
    if gated:
        logf_ref[...] = _log_sigmoid(_dot(h, wf_ref[...]) + bf_ref[...])


def _sample_proj(x, g, w, tm, w_f=None, b_f=None):
    n, d = x.shape
    row = lambda w_: pl.BlockSpec((tm, w_), lambda i: (i, 0))
    in_specs = [row(d), _const_spec((1, d)), _const_spec(w.shape)]
    args = [x, g, w]
    out_specs = row(w.shape[1])
    out_shape = jax.ShapeDtypeStruct((n, w.shape[1]), F32)
    if w_f is not None:
        in_specs += [_const_spec(w_f.shape), _const_spec((1, LANES))]
        args += [w_f, b_f]
        out_specs = (out_specs, row(LANES))
        out_shape = (out_shape, jax.ShapeDtypeStruct((n, LANES), F32))
    return pl.pallas_call(
        functools.partial(_sample_proj_kernel, gated=w_f is not None),
        grid=(n // tm,),
        in_specs=in_specs,
        out_specs=out_specs,
        out_shape=out_shape,
        compiler_params=_params(("arbitrary",)),
        name="sample_proj",
    )(*args)


def _block_diag_q(q, nheads, scale):
    t, w = q.shape
    tiled = jnp.concatenate([q] * nheads, axis=0)
    r = lax.broadcasted_iota(jnp.int32, (nheads * t, w), 0)
    c = lax.broadcasted_iota(jnp.int32, (nheads * t, w), 1)
    return jnp.where(r // t == c // HEAD_DIM, tiled * scale, 0.0).astype(BF16)


def _block_diag_extract(o_all, nheads, t):
    w = o_all.shape[1]
    c = lax.broadcasted_iota(jnp.int32, (t, w), 1)
    out = jnp.zeros((t, w), F32)
    for h in range(nheads):
        out = jnp.where(c // HEAD_DIM == h, o_all[h * t:(h + 1) * t, :], out)
    return out


def _heads_major(ref):
    _, nh, hd, nk = ref.shape
    return ref[0].reshape(nh * hd, nk).astype(BF16)


def _pad_rows(x, rows):
    return jnp.concatenate([x, jnp.zeros((rows - x.shape[0], x.shape[1]), x.dtype)], axis=0)


def _sample_even_kernel(qa_ref, kan_ref, van_ref, qb_ref, kbn_ref, vbn_ref, ck_ref, cv_ref, lft_ref,
                        cbk_ref, cbv_ref, e3_ref, biasb_ref, oa_ref, ob_ref,
                        bias_sc, m_sc, acc_sc, *, t, past, kc, nkc):
    c = pl.program_id(1)
    rows = H_A * t
    qbd = _block_diag_q(qa_ref[...], H_A, SCALE)

    @pl.when(c == 0)
    def _():
        cum = _cumsum_lanes(lft_ref[0])
        last = past + t - 1
        suffix = cum[:, last:last + 1] - cum
        hi, mid, lo = _split3(suffix)
        s3 = jnp.concatenate([hi, mid, lo, jnp.zeros((LANES - 3 * H_A, suffix.shape[1]), F32)], axis=0)
        s3 = s3.astype(BF16)
        for ch in range(nkc + 1):
            w = kc if ch < nkc else LANES
            bias_sc[ch, :, :w] = _dot(e3_ref[...], s3[:, ch * kc:ch * kc + w])
        m_sc[...] = jnp.full_like(m_sc, NEG_INF)
        acc_sc[...] = jnp.zeros_like(acc_sc)

    def update(s, pv):
        m = m_sc[...]
        m_new = jnp.maximum(m, jnp.max(s, axis=1, keepdims=True))
        p = jnp.exp(s - m_new)
        alpha = jnp.exp(m - m_new)
        acc = acc_sc[...]
        lsum = alpha * acc[:, W_A:W_A + 1] + jnp.sum(p, axis=1, keepdims=True)
        o = alpha * acc[:, :W_A] + pv(p.astype(BF16))
        acc_sc[:, :W_A] = o
        acc_sc[:, W_A:] = jnp.broadcast_to(lsum, (rows, LANES))
        m_sc[...] = m_new

    s = _dot(qbd, _heads_major(ck_ref)) + bias_sc[c]
    vt = _heads_major(cv_ref)
    update(s, lambda p: _dot_nt(p, vt))

    @pl.when(c == nkc - 1)
    def _():
        kn = _pad_rows(kan_ref[...], LANES).astype(BF16)
        vn = _pad_rows(van_ref[...], LANES).astype(BF16)
        sn = _dot_nt(qbd, kn) + bias_sc[nkc, :, :LANES]
        r = lax.broadcasted_iota(jnp.int32, sn.shape, 0)
        col = lax.broadcasted_iota(jnp.int32, sn.shape, 1)
        sn = jnp.where(col <= r % t, sn, NEG_INF)
        update(sn, lambda p: _dot(p, vn))
        acc = acc_sc[...]
        o_all = acc[:, :W_A] / acc[:, W_A:W_A + 1]
        oa_ref[...] = _block_diag_extract(o_all, H_A, t).astype(BF16)

        qbd_b = _block_diag_q(qb_ref[...], H_B, SCALE)
        lbk = cbk_ref.shape[3]
        kbn = _pad_rows(kbn_ref[...], LANES).astype(BF16)
        vbn = _pad_rows(vbn_ref[...], LANES).astype(BF16)
        sb = jnp.concatenate([_dot(qbd_b, _heads_major(cbk_ref)), _dot_nt(qbd_b, kbn)], axis=1) + biasb_ref[...]
        mb = jnp.max(sb, axis=1, keepdims=True)
        eb = jnp.exp(sb - mb)
        lb = jnp.sum(eb, axis=1, keepdims=True)
        eb = eb.astype(BF16)
        ob_all = (_dot_nt(eb[:, :lbk], _heads_major(cbv_ref)) + _dot(eb[:, lbk:], vbn)) / lb
        ob_ref[...] = _block_diag_extract(ob_all, H_B, t).astype(BF16)


def _sample_even(p, cache_k, cache_v, logf_t, cache_bk, cache_bv, e3, bias_b, *, t, kc):
    n = p.shape[0]
    nb = n // t
    past = cache_k.shape[3]
    nkc = past // kc
    lbk = cache_bk.shape[3]
    pcol = lambda j: pl.BlockSpec((t, W_A), lambda bi, c: (bi, j))
    rows = H_A * t
    return pl.pallas_call(
        functools.partial(_sample_even_kernel, t=t, past=past, kc=kc, nkc=nkc),
        grid=(nb, nkc),
        in_specs=[pcol(0), pcol(1), pcol(2), pcol(3), pcol(4), pcol(5),
                  pl.BlockSpec((1, H_A, HEAD_DIM, kc), lambda bi, c: (bi, 0, 0, c)),
                  pl.BlockSpec((1, H_A, HEAD_DIM, kc), lambda bi, c: (bi, 0, 0, c)),
                  pl.BlockSpec((1, H_A, logf_t.shape[2]), lambda bi, c: (bi, 0, 0)),
                  pl.BlockSpec((1, H_B, HEAD_DIM, lbk), lambda bi, c: (bi, 0, 0, 0)),
                  pl.BlockSpec((1, H_B, HEAD_DIM, lbk), lambda bi, c: (bi, 0, 0, 0)),
                  _const_spec(e3.shape), _const_spec(bias_b.shape)],
        out_specs=(pl.BlockSpec((t, W_A), lambda bi, c: (bi, 0)),
                   pl.BlockSpec((t, W_B), lambda bi, c: (bi, 0))),
        out_shape=(jax.ShapeDtypeStruct((n, W_A), BF16), jax.ShapeDtypeStruct((n, W_B), BF16)),
        scratch_shapes=[pltpu.VMEM((nkc + 1, rows, kc), F32),
                        pltpu.VMEM((rows, 1), F32),
                        pltpu.VMEM((rows, W_A + LANES), F32)],
        compiler_params=_params(("arbitrary", "arbitrary")),
        name="sample_even",
    )(p, p, p, p, p, p, cache_k, cache_v, logf_t, cache_bk, cache_bv, e3, bias_b)


def _sample_odd_kernel(q_ref, kn_ref, vn_ref, ck_ref, cv_ref, x_ref, bias_ref, sink_ref, o_ref, *, t, nbs):
    sk = sink_ref[...]
    for bb in range(nbs):
        rows = slice(bb * t, (bb + 1) * t)
        qbd = _block_diag_q(q_ref[rows, :], H_C, SCALE)
        kall = jnp.concatenate([ck_ref[bb], _pad_rows(kn_ref[rows, :], LANES)], axis=0).astype(BF16)
        vall = jnp.concatenate([cv_ref[bb], _pad_rows(vn_ref[rows, :], LANES)], axis=0).astype(BF16)
        kexp = _dot(kall, x_ref[...]).astype(BF16)
        vexp = _dot(vall, x_ref[...]).astype(BF16)
        s = _dot_nt(qbd, kexp) + bias_ref[...]
        m = jnp.maximum(jnp.max(s, axis=1, keepdims=True), sk)
        e = jnp.exp(s - m)
        l = jnp.sum(e, axis=1, keepdims=True) + jnp.exp(sk - m)
        o_all = _dot(e.astype(BF16), vexp) / l
        o_ref[rows, :] = _block_diag_extract(o_all, H_C, t).astype(BF16)


def _sample_odd(p, cache_k, cache_v, xexp, bias, sink_col, *, t, nbs):
    n = p.shape[0]
    nb = n // t
    wq = H_C * HEAD_DIM
    lc = cache_k.shape[1]
    rows = t * nbs
    return pl.pallas_call(
        functools.partial(_sample_odd_kernel, t=t, nbs=nbs),
        grid=(nb // nbs,),
        in_specs=[pl.BlockSpec((rows, wq), lambda bi: (bi, 0)),
                  pl.BlockSpec((rows, LANES), lambda bi: (bi, wq // LANES)),
                  pl.BlockSpec((rows, LANES), lambda bi: (bi, wq // LANES + 1)),
                  pl.BlockSpec((nbs, lc, LANES), lambda bi: (bi, 0, 0)),
                  pl.BlockSpec((nbs, lc, LANES), lambda bi: (bi, 0, 0)),
                  _const_spec(xexp.shape), _const_spec(bias.shape), _const_spec(sink_col.shape)],
        out_specs=pl.BlockSpec((rows, wq), lambda bi: (bi, 0)),
        out_shape=jax.ShapeDtypeStruct((n, wq), BF16),
        compiler_params=_params(("arbitrary",)),
        name="sample_odd",
    )(p, p, p, cache_k, cache_v, xexp, bias, sink_col)


def _t5_bucket(rel_mem):
    nb = T5_BUCKETS // 2
    max_exact = nb // 2
    n = jnp.abs(rel_mem)
    large = max_exact + (jnp.log(jnp.maximum(n, 1).astype(F32) / max_exact)
                         / math.log(T5_MAX_DIST / max_exact) * (nb - max_exact)).astype(jnp.int32)
    large = jnp.minimum(large, nb - 1)
    return jnp.where(rel_mem > 0, nb, 0) + jnp.where(n < max_exact, n, large)


def _bias_b_of_rel(table, rel):
    idx = np.clip(rel, -B_REL_CLIP, B_REL_CLIP) + B_REL_CLIP
    return table.astype(F32)[idx].T


def _bias_c_of_rel(table, rel):
    return table.astype(F32)[_t5_bucket(-jnp.asarray(rel, jnp.int32))].T


def _bias_tiles_kernel(f_ref, o_ref, *, tq, left, offsets, left_chunks, valid_cols):
    n = f_ref.shape[2]
    win = o_ref.shape[3]
    rows = jnp.broadcast_to(f_ref[0], (tq, n))
    wide = pltpu.roll(rows, n - (tq - 1), 1, stride=1, stride_axis=0)
    q = lax.broadcasted_iota(jnp.int32, (tq, win), 0)
    k = lax.broadcasted_iota(jnp.int32, (tq, win), 1)
    shift = CHUNK.bit_length() - 1
    for v, off in enumerate(offsets):
        if valid_cols is None:
            d = off // CHUNK + jnp.right_shift(q, shift) - jnp.right_shift(k, shift)
            ok = (d >= 0) & (d <= left_chunks)
        else:
            ok = k < valid_cols
        o_ref[v, 0] = jnp.where(ok, wide[:, left - off:left - off + win], NEG_INF)


def _bias_tiles(bias_of_rel, *, tq, left, win, offsets, left_chunks=None, valid_cols=None):
    assert CHUNK & (CHUNK - 1) == 0 and all((left - o) % LANES == 0 for o in offsets)
    ncols = win + left - min(offsets)
    n = tq + ncols - 1
    n_pad = -(-n // LANES) * LANES
    f_rev = bias_of_rel(tq - 1 + left - np.arange(n))
    nheads = f_rev.shape[0]
    f_rev = jnp.pad(f_rev, ((0, 0), (0, n_pad - n)))[:, None, :]
    return pl.pallas_call(
        functools.partial(_bias_tiles_kernel, tq=tq, left=left, offsets=tuple(offsets),
                          left_chunks=left_chunks, valid_cols=valid_cols),
        grid=(nheads,),
        in_specs=[pl.BlockSpec((1, 1, n_pad), lambda h: (h, 0, 0))],
        out_specs=pl.BlockSpec((len(offsets), 1, tq, win), lambda h: (0, h, 0, 0)),
        out_shape=jax.ShapeDtypeStruct((len(offsets), nheads, tq, win), F32),
        compiler_params=_params(("arbitrary",)),
        name="bias_tiles",
    )(f_rev)


def _sample_bias(bias_of_rel, t, cache_len):
    ncols = cache_len + LANES
    tiles = _bias_tiles(bias_of_rel, tq=t, left=cache_len, win=ncols, offsets=[cache_len],
                        valid_cols=cache_len + t)
    return tiles.reshape(-1, ncols)


def _cast_kernel(x_ref, o_ref):
    o_ref[...] = x_ref[...].astype(o_ref.dtype)


def _cast_bf16(w, rows):
    nl, r, c = w.shape
    w2 = w.reshape(nl * r, c)
    out = pl.pallas_call(
        _cast_kernel,
        grid=(nl * r // rows,),
        in_specs=[pl.BlockSpec((rows, c), lambda i: (i, 0))],
        out_specs=pl.BlockSpec((rows, c), lambda i: (i, 0)),
        out_shape=jax.ShapeDtypeStruct(w2.shape, BF16),
        compiler_params=_params(("arbitrary",)),
        name="cast_bf16",
    )(w2)
    return out.reshape(nl, r, c)


def _placement():
    place = np.zeros((LANES, H_A * LANES), np.float32)
    for h in range(H_A):
        for j in range(3):
            place[8 * j + h, LANES * h + HEAD_DIM + j] = 1.0
            place[8 * j + h, LANES * h + HEAD_DIM + 3 + j] = 1.0
    return jnp.asarray(place, BF16)


def kernel(x_prompt, x_sample, cache_a_k, cache_a_v, cache_a_logf, cache_b_k, cache_b_v, cache_c_k, cache_c_v,
           norm_mix, norm_ffn, norm_final, w_in_even, b_forget, rel_bias_b, w_out_even, w_in_odd, sinks_c,
           w_out_odd, t5_bias, w_ffn_in, w_ffn_out):
    b, s, d = x_prompt.shape
    nb, t, _ = x_sample.shape
    past = cache_a_k.shape[2]
    n_p, n_s = b * s, nb * t

    w_even = w_in_even[0]
    w_main = w_even[:, :3 * W_A + 3 * W_B].astype(BF16)
    wf = w_even[:, 3 * W_A + 3 * W_B:]
    w_f = jnp.concatenate([wf, wf, wf, jnp.zeros((d, LANES - 3 * H_A), F32)], axis=1).astype(BF16)
    bf = b_forget[0].astype(F32)
    b_f = jnp.concatenate([bf, bf, bf, jnp.zeros((LANES - 3 * H_A,), F32)])[None, :]
    place = _placement()
    w_oe = w_out_even[0].astype(BF16)
    w_oo = w_out_odd[0].astype(BF16)
    w_odd = w_in_odd[0].astype(BF16)
    w_fi = _cast_bf16(w_ffn_in, rows=512)
    w_fo = _cast_bf16(w_ffn_out, rows=1408)
    g_mix = norm_mix.astype(F32)[:, None, :]
    g_ffn = norm_ffn.astype(F32)[:, None, :]
    g_fin = norm_final.astype(F32)[None, :]

    bias_b_of = functools.partial(_bias_b_of_rel, rel_bias_b[0])
    bias_c_of = functools.partial(_bias_c_of_rel, t5_bias)
    log2_domain = lambda bias_of: (lambda rel: bias_of(rel) * LOG2E)
    variant_offsets = lambda tsub, left: [min(v * tsub, left) for v in range(left // tsub + 1)]
    tq_b, tsub_b, left_b = 1024, 256, B_LEFT_CHUNKS * CHUNK
    bias_b = _bias_tiles(log2_domain(bias_b_of), tq=tsub_b, left=left_b, win=tsub_b + left_b,
                         offsets=variant_offsets(tsub_b, left_b), left_chunks=B_LEFT_CHUNKS)
    tq_c, tsub_c, left_c = 512, 128, C_LEFT_CHUNKS * CHUNK
    bias_c = _bias_tiles(log2_domain(bias_c_of), tq=tsub_c, left=left_c, win=tsub_c + left_c,
                         offsets=variant_offsets(tsub_c, left_c), left_chunks=C_LEFT_CHUNKS)

    xp = x_prompt
    w_all = jnp.concatenate([w_main[:, 3 * W_A:3 * W_A + W_B], w_f, w_main[:, :3 * W_A],
                             w_main[:, 3 * W_A + W_B:]], axis=1)
    qaug, kaug, vaug, ka, va, logf, qb, kb, vb, kbt, vbt = _even_proj(
        xp, g_mix[0], w_all, b_f, place, tm=512)
    oa = _fox(qaug, kaug, vaug, blk=512, nsub=2)
    ob = _band(qb, kb, vb, bias_b, None, tq=tq_b, left=left_b, npairs=H_B // 2, pairs_per_kv=1,
               mxu_rowsum=True, name="band_b")
    xp1 = _out_ffn(xp.reshape(n_p, d),
                   [(oa.reshape(n_p, W_A), w_oe[:W_A]), (ob.reshape(n_p, W_B), w_oe[W_A:])],
                   g_ffn[0], w_fi, w_fo, g_fin, tm=512, layer=0, final_norm=False)

    qc, kcd, vcd, kct, vct = _odd_proj(xp1.reshape(b, s, d), g_mix[1], w_odd, tm=512)
    oc = _band(qc, kcd, vcd, bias_c, sinks_c[0].astype(F32) * LOG2E, tq=tq_c, left=left_c, npairs=H_C // 2,
               pairs_per_kv=G_C // 2, mxu_rowsum=False, name="band_c")
    y_prompt = _out_ffn(xp1, [(oc.reshape(n_p, H_C * HEAD_DIM), w_oo)],
                        g_ffn[1], w_fi, w_fo, g_fin, tm=512, layer=1, final_norm=True)

    xs = x_sample.reshape(n_s, d)
    ps, logf_sp = _sample_proj(xs, g_mix[0], w_main, tm=256, w_f=w_f, b_f=b_f)
    logf_s = logf_sp[:, :H_A]
    kpad = LANES * -(-(past + t) // LANES)
    logf_all = jnp.concatenate([cache_a_logf[0].astype(F32), logf_s.reshape(nb, t, H_A)], axis=1)
    logf_t = jnp.pad(jnp.swapaxes(logf_all, 1, 2), ((0, 0), (0, 0), (0, kpad - past - t)))
    lbs = cache_b_k.shape[2]
    bias_sb = _sample_bias(bias_b_of, t, lbs)
    e3 = np.zeros((H_A * t, LANES), np.float32)
    for j in range(3):
        e3[np.arange(H_A * t), 8 * j + np.arange(H_A * t) // t] = 1.0
    keys_minor = lambda cache: jnp.transpose(cache[0], (0, 2, 3, 1))
    oa_s, ob_s = _sample_even(ps, keys_minor(cache_a_k), keys_minor(cache_a_v), logf_t,
                              keys_minor(cache_b_k), keys_minor(cache_b_v),
                              jnp.asarray(e3, BF16), bias_sb, t=t, kc=4096)
    xs1 = _out_ffn(xs, [(oa_s, w_oe[:W_A]), (ob_s, w_oe[W_A:])], g_ffn[0], w_fi, w_fo, g_fin,
                   tm=256, layer=0, final_norm=False)

    ps2 = _sample_proj(xs1, g_mix[1], w_odd, tm=256)
    lcs = cache_c_k.shape[2]
    bias_sc = _sample_bias(bias_c_of, t, lcs)
    sink_col = jnp.repeat(sinks_c[0].astype(F32), t)[:, None]
    lane_head = np.arange(H_C * HEAD_DIM) // HEAD_DIM
    src_lane = (lane_head // G_C) * HEAD_DIM + np.arange(H_C * HEAD_DIM) % HEAD_DIM
    xexp = jnp.asarray(np.arange(LANES)[:, None] == src_lane[None, :], BF16)
    oc_s = _sample_odd(ps2, cache_c_k[0].reshape(nb, lcs, LANES), cache_c_v[0].reshape(nb, lcs, LANES),
                       xexp, bias_sc, sink_col, t=t, nbs=4)
    y_sample = _out_ffn(xs1, [(oc_s, w_oo)], g_ffn[1], w_fi, w_fo, g_fin, tm=256, layer=1, final_norm=True)

    wq = H_C * HEAD_DIM
    hd = lambda a, lead, h: a.reshape((1,) + lead + (h, HEAD_DIM))
    return (
        y_prompt.reshape(b, s, d), y_sample.reshape(nb, t, d),
        hd(ka, (b, s), H_A), hd(va, (b, s), H_A), logf[None],
        hd(kbt, (b, kbt.shape[1]), H_B), hd(vbt, (b, vbt.shape[1]), H_B),
        hd(kct, (b, kct.shape[1]), HKV_C), hd(vct, (b, vct.shape[1]), HKV_C),
        hd(ps[:, W_A:2 * W_A], (nb, t), H_A), hd(ps[:, 2 * W_A:3 * W_A], (nb, t), H_A),
        logf_s.reshape(1, nb, t, H_A),
        hd(ps[:, 3 * W_A + W_B:3 * W_A + 2 * W_B], (nb, t), H_B),
        hd(ps[:, 3 * W_A + 2 * W_B:3 * W_A + 3 * W_B], (nb, t), H_B),
        hd(ps2[:, wq:wq + LANES], (nb, t), HKV_C), hd(ps2[:, wq + LANES:wq + 2 * LANES], (nb, t), HKV_C),
    )
```

```python
import functools
import math

import jax
import jax.numpy as jnp
import numpy as np
from jax import lax
from jax.experimental import pallas as pl
from jax.experimental.pallas import tpu as pltpu

D_MODEL = 1024
HEAD_DIM = 64
CHUNK = 64
H_A = 8
H_B = 8
B_LEFT_CHUNKS = 8
B_REL_CLIP = 128
H_C = 16
HKV_C = 2
G_C = H_C // HKV_C
WINDOW = 128
C_LEFT_CHUNKS = WINDOW // CHUNK
T5_BUCKETS = 32
T5_MAX_DIST = 128
EPS = 1e-6
W_A = H_A * HEAD_DIM
W_B = H_B * HEAD_DIM
SCALE = HEAD_DIM ** -0.5
LOG2E = math.log2(math.e)

LANES = 128
VMEM_LIMIT = 60 * 1024 * 1024

F32 = jnp.float32
BF16 = jnp.bfloat16
NEG_INF = float("-inf")

_NT = (((1,), (1,)), ((), ()))


def _dot(a, b):
    return jnp.dot(a, b, preferred_element_type=F32)


def _dot_nt(a, b):
    return lax.dot_general(a, b, _NT, preferred_element_type=F32)


def _rmsnorm(x, g):
    ms = jnp.mean(x * x, axis=-1, keepdims=True)
    return x * lax.rsqrt(ms + EPS) * g


def _log_sigmoid(x):
    return jnp.minimum(x, 0.0) - jnp.log1p(jnp.exp(-jnp.abs(x)))


def _split3(x):
    hi = x.astype(BF16).astype(F32)
    r1 = x - hi
    mid = r1.astype(BF16).astype(F32)
    lo = (r1 - mid).astype(BF16).astype(F32)
    return hi, mid, lo


def _cumsum_rows(x):
    n = x.shape[0]
    row = lax.broadcasted_iota(jnp.int32, x.shape, 0)
    s = 1
    while s < n:
        x = x + jnp.where(row >= s, pltpu.roll(x, s, 0), 0.0)
        s *= 2
    return x


def _cumsum_lanes(x):
    n = x.shape[1]
    col = lax.broadcasted_iota(jnp.int32, x.shape, 1)
    s = 1
    while s < n:
        x = x + jnp.where(col >= s, pltpu.roll(x, s, 1), 0.0)
        s *= 2
    return x


def _staggered(items, stage_a, stage_b, stage_c):
    n = len(items)
    a_out, b_out = {}, {}
    for step in range(n + 2):
        if step < n:
            a_out[step] = stage_a(items[step])
        if 0 <= step - 1 < n:
            b_out[step - 1] = stage_b(items[step - 1], a_out.pop(step - 1))
        if 0 <= step - 2 < n:
            stage_c(items[step - 2], b_out.pop(step - 2))


def _const_spec(shape):
    nd = len(shape)
    return pl.BlockSpec(shape, lambda *_: (0,) * nd, pipeline_mode=pl.Buffered(1))


def _params(sem):
    return pltpu.CompilerParams(dimension_semantics=sem, vmem_limit_bytes=VMEM_LIMIT)


def _even_proj_kernel(x_ref, g_ref, w_ref, bf_ref, p_ref,
                      qaug_ref, kaug_ref, vaug_ref, ka_ref, va_ref, logf_ref,
                      qb_ref, kb_ref, vb_ref, kbt_ref, vbt_ref, carry_ref):
    @pl.when(pl.program_id(1) == 0)
    def _():
        carry_ref[...] = jnp.zeros_like(carry_ref)

    h = _rmsnorm(x_ref[0], g_ref[...]).astype(BF16)
    tm = h.shape[0]
    lane = lax.broadcasted_iota(jnp.int32, (tm, LANES), 1)
    low = lane < HEAD_DIM
    first = W_B + LANES

    def chunk(n):
        return _dot(h, w_ref[:, first + W_A * n:first + W_A * (n + 1)])

    pf = _dot(h, w_ref[:, :first])
    qb_ref[0] = (pf[:, :W_B] * (SCALE * LOG2E)).astype(BF16)
    fa = pf[:, W_B:]
    pc = chunk(3)
    kb_ref[0] = pc.astype(BF16)
    kbt_ref[0] = pc
    pc = chunk(4)
    vb_ref[0] = pc.astype(BF16)
    vbt_ref[0] = pc
    logf = _log_sigmoid(fa + bf_ref[...])
    logf_ref[0] = logf[:, :H_A]
    c = _cumsum_rows(logf) + carry_ref[...]
    carry_ref[...] = c[tm - 1:tm, :]
    hi, mid, lo = _split3(c * LOG2E)
    a3 = jnp.where(lane < 8, hi, jnp.where(lane < 16, mid, jnp.where(lane < 24, lo, 0.0))).astype(BF16)
    pieces = _dot(a3, p_ref[...])
    first3 = (lane >= HEAD_DIM) & (lane < HEAD_DIM + 3)
    next3 = (lane >= HEAD_DIM + 3) & (lane < HEAD_DIM + 6)
    vone = jnp.where(lane == HEAD_DIM, 1.0, 0.0)
    spare_q = lambda g: jnp.where(next3, 1.0, g)
    spare_k = lambda g: jnp.where(first3, 1.0, -g)

    def per_head(dst_ref, pc, spare):
        for j in range(H_A // 2):
            pair = pc[:, LANES * j:LANES * (j + 1)]
            for hh, val in enumerate((pair, pltpu.roll(pair, HEAD_DIM, 1))):
                o = slice(LANES * (2 * j + hh), LANES * (2 * j + hh + 1))
                upper = vone if spare is None else spare(pieces[:, o])
                dst_ref[0, :, o] = jnp.where(low, val, upper).astype(BF16)

    pc = chunk(0)
    per_head(qaug_ref, pc * (SCALE * LOG2E), spare_q)
    pc = chunk(1)
    ka_ref[0] = pc
    per_head(kaug_ref, pc, spare_k)
    pc = chunk(2)
    va_ref[0] = pc
    per_head(vaug_ref, pc, None)


def _even_proj(x, g, w_all, b_f, place, tm):
    b, s, d = x.shape
    nt = s // tm
    lb = min(B_LEFT_CHUNKS * CHUNK, s)
    assert lb == tm, "band-state tail must be exactly one row tile"
    row = lambda w: pl.BlockSpec((1, tm, w), lambda bi, ti: (bi, ti, 0))
    tail = pl.BlockSpec((1, lb, W_B), lambda bi, ti: (bi, 0, 0))
    outs = (
        jax.ShapeDtypeStruct((b, s, H_A * LANES), BF16),
        jax.ShapeDtypeStruct((b, s, H_A * LANES), BF16),
        jax.ShapeDtypeStruct((b, s, H_A * LANES), BF16),
        jax.ShapeDtypeStruct((b, s, W_A), F32),
        jax.ShapeDtypeStruct((b, s, W_A), F32),
        jax.ShapeDtypeStruct((b, s, H_A), F32),
        jax.ShapeDtypeStruct((b, s, W_B), BF16),
        jax.ShapeDtypeStruct((b, s, W_B), BF16),
        jax.ShapeDtypeStruct((b, s, W_B), BF16),
        jax.ShapeDtypeStruct((b, lb, W_B), F32),
        jax.ShapeDtypeStruct((b, lb, W_B), F32),
    )
    return pl.pallas_call(
        _even_proj_kernel,
        grid=(b, nt),
        in_specs=[row(d), _const_spec((1, d)), _const_spec(w_all.shape),
                  _const_spec((1, LANES)), _const_spec(place.shape)],
        out_specs=(row(H_A * LANES), row(H_A * LANES), row(H_A * LANES), row(W_A), row(W_A), row(H_A),
                   row(W_B), row(W_B), row(W_B), tail, tail),
        out_shape=outs,
        scratch_shapes=[pltpu.VMEM((1, LANES), F32)],
        compiler_params=_params(("arbitrary", "arbitrary")),
        name="even_proj",
    )(x, g, w_all, b_f, place)


def _fox_kernel(q_ref, k_ref, v_ref, o_ref, m_sc, acc_sc, *, blk, nsub):
    i = pl.program_id(2)
    nh = 2
    chains = [(sub, hh) for sub in range(nsub) for hh in range(nh)]
    qs = {(sub, hh): q_ref[0, sub * blk:(sub + 1) * blk, LANES * hh:LANES * (hh + 1)] for sub, hh in chains}
    r = lax.broadcasted_iota(jnp.int32, (blk, blk), 0)
    c = lax.broadcasted_iota(jnp.int32, (blk, blk), 1)
    causal = c <= r
    m_sc[...] = jnp.full_like(m_sc, NEG_INF)
    acc_sc[...] = jnp.zeros_like(acc_sc)

    def blocks(specs):
        items = [(pl.multiple_of(j * blk, blk), ch, modes[ch[0]])
                 for j, modes in specs for ch in chains if modes[ch[0]] is not None]

        def stage_scores(item):
            start, ch, mode = item
            k = k_ref[0, pl.ds(start, blk), LANES * ch[1]:LANES * (ch[1] + 1)]
            s = _dot_nt(qs[ch], k)
            if mode == "masked":
                s = jnp.where(causal, s, NEG_INF)
            m = m_sc[chains.index(ch)]
            m_new = jnp.maximum(m, jnp.max(s, axis=1, keepdims=True))
            m_sc[chains.index(ch)] = m_new
            return s, m, m_new

        def stage_exp(item, sm):
            s, m, m_new = sm
            p = jnp.concatenate([jnp.exp2(s[:, LANES * cb:LANES * (cb + 1)] - m_new)
                                 for cb in range(blk // LANES)], axis=1)
            return p.astype(BF16), jnp.exp2(m - m_new)

        def stage_pv(item, pa):
            start, ch, _ = item
            p, alpha = pa
            v = v_ref[0, pl.ds(start, blk), LANES * ch[1]:LANES * (ch[1] + 1)]
            n = chains.index(ch)
            acc_sc[n] = alpha * acc_sc[n] + _dot(p, v)

        _staggered(items, stage_scores, stage_exp, stage_pv)

    @pl.loop(0, i)
    def _(jj):
        blocks([(nsub * jj + d, ["full"] * nsub) for d in range(nsub)])

    blocks([(nsub * i + d, [None if sub < d else ("masked" if sub == d else "full") for sub in range(nsub)])
            for d in range(nsub)])

    lane = lax.broadcasted_iota(jnp.int32, (blk, LANES), 1)
    for sub in range(nsub):
        a0 = acc_sc[chains.index((sub, 0))]
        a1 = acc_sc[chains.index((sub, 1))]
        o0 = a0 / a0[:, HEAD_DIM:HEAD_DIM + 1]
        o1 = a1 / a1[:, HEAD_DIM:HEAD_DIM + 1]
        o_ref[0, sub * blk:(sub + 1) * blk, :] = jnp.where(
            lane < HEAD_DIM, o0, pltpu.roll(o1, HEAD_DIM, 1)).astype(BF16)


def _fox(qaug, kaug, vaug, blk, nsub):
    b, s, _ = qaug.shape
    pairs = H_A // 2
    tq = blk * nsub
    return pl.pallas_call(
        functools.partial(_fox_kernel, blk=blk, nsub=nsub),
        grid=(b, pairs, s // tq),
        in_specs=[pl.BlockSpec((1, tq, 2 * LANES), lambda bi, hp, i: (bi, i, hp)),
                  pl.BlockSpec((1, s, 2 * LANES), lambda bi, hp, i: (bi, 0, hp)),
                  pl.BlockSpec((1, s, 2 * LANES), lambda bi, hp, i: (bi, 0, hp))],
        out_specs=pl.BlockSpec((1, tq, LANES), lambda bi, hp, i: (bi, i, hp)),
        out_shape=jax.ShapeDtypeStruct((b, s, W_A), BF16),
        scratch_shapes=[pltpu.VMEM((2 * nsub, blk, LANES), F32), pltpu.VMEM((2 * nsub, blk, LANES), F32)],
        compiler_params=_params(("arbitrary", "arbitrary", "arbitrary")),
        name="fox",
    )(qaug, kaug, vaug)


def _band_kernel(*refs, tq, left, npairs, heads_per_group, use_sinks, pairs_per_kv, mxu_rowsum):
    if use_sinks:
        sink_ref, q_ref, k_ref, v_ref, bias_ref, o_ref = refs
    else:
        q_ref, k_ref, v_ref, bias_ref, o_ref = refs
    g = pl.program_id(1)
    i = pl.program_id(2)
    nvar, _, tsub, win = bias_ref.shape
    nsq = tq // tsub
    lane = lax.broadcasted_iota(jnp.int32, (tsub, LANES), 1)
    low = lane < HEAD_DIM
    items = [(sq, pr, hh) for sq in range(nsq) for pr in range(npairs) for hh in range(2)]

    def window(sq):
        gsub = i * nsq + sq
        return pl.multiple_of(jnp.maximum(gsub * tsub - left, 0), LANES), jnp.minimum(gsub, nvar - 1)

    def kv_lanes(pr):
        kl = LANES * (pr // pairs_per_kv)
        return slice(kl, kl + LANES)

    def sink(pr, hh):
        return sink_ref[g * heads_per_group + 2 * pr + hh]

    def stage_scores(item):
        sq, pr, hh = item
        start, var = window(sq)
        qp = q_ref[0, sq * tsub:(sq + 1) * tsub, LANES * pr:LANES * (pr + 1)]
        qm = jnp.where(low if hh == 0 else jnp.logical_not(low), qp, jnp.zeros_like(qp))
        s = _dot_nt(qm, k_ref[0, pl.ds(start, win), kv_lanes(pr)]) + bias_ref[var, 2 * pr + hh]
        m = jnp.max(s, axis=1, keepdims=True)
        return s, (jnp.maximum(m, sink(pr, hh)) if use_sinks else m)

    def stage_exp(item, sm):
        s, m = sm
        e = jnp.exp2(s - m)
        l = None if mxu_rowsum else jnp.sum(e, axis=1, keepdims=True)
        if use_sinks:
            l = l + jnp.exp2(sink(item[1], item[2]) - m)
        return e.astype(BF16), l

    outs = {}

    def stage_pv(item, pe):
        p, l = pe
        sq, pr, hh = item
        v = v_ref[0, pl.ds(window(sq)[0], win), kv_lanes(pr)]
        if mxu_rowsum:
            ov = _dot(p, jnp.concatenate([v, jnp.ones((win, LANES), BF16)], axis=1))
            outs[hh] = ov[:, :LANES] / ov[:, LANES:]
        else:
            outs[hh] = _dot(p, v) / l
        if hh == 1:
            o_ref[0, sq * tsub:(sq + 1) * tsub, LANES * pr:LANES * (pr + 1)] = jnp.where(
                low, outs[0], outs[1]).astype(BF16)

    _staggered(items, stage_scores, stage_exp, stage_pv)


def _band(q, k, v, bias, sinks, *, tq, left, npairs, pairs_per_kv, mxu_rowsum, name):
    assert not (mxu_rowsum and sinks is not None)
    b, s, wq = q.shape
    tsub, win = bias.shape[-2:]
    wblk = npairs * LANES
    groups = wq // wblk
    hpg = 2 * npairs
    use_sinks = sinks is not None
    kern = functools.partial(_band_kernel, tq=tq, left=left, npairs=npairs,
                             heads_per_group=hpg, use_sinks=use_sinks, pairs_per_kv=pairs_per_kv,
                             mxu_rowsum=mxu_rowsum)
    kvw = wblk // pairs_per_kv
    resident = lambda shape, imap: pl.BlockSpec(shape, imap, pipeline_mode=pl.Buffered(1))
    in_specs = [pl.BlockSpec((1, tq, wblk), lambda bi, g, i: (bi, i, g)),
                resident((1, s, kvw), lambda bi, g, i: (bi, 0, g)),
                resident((1, s, kvw), lambda bi, g, i: (bi, 0, g)),
                resident((bias.shape[0], hpg, tsub, win), lambda bi, g, i: (0, g, 0, 0))]
    args = [q, k, v, bias]
    if use_sinks:
        in_specs = [pl.BlockSpec(memory_space=pltpu.SMEM)] + in_specs
        args = [sinks] + args
    return pl.pallas_call(
        kern,
        grid=(b, groups, s // tq),
        in_specs=in_specs,
        out_specs=pl.BlockSpec((1, tq, wblk), lambda bi, g, i: (bi, i, g)),
        out_shape=jax.ShapeDtypeStruct((b, s, wq), BF16),
        compiler_params=_params(("arbitrary", "arbitrary", "arbitrary")),
        name=name,
    )(*args)


def _out_ffn_kernel(*refs, n_attn, d_ff, final_norm):
    x_ref = refs[0]
    attn = refs[1:1 + 2 * n_attn]
    gffn_ref, win_ref, wout_ref, gfin_ref, o_ref = refs[1 + 2 * n_attn:]
    y = x_ref[...]
    for a in range(n_attn):
        y = y + _dot(attn[2 * a][...], attn[2 * a + 1][...])
    h = _rmsnorm(y, gffn_ref[...]).astype(BF16)
    gu = _dot(h, win_ref[...])
    gate = gu[:, :d_ff]
    up = gu[:, d_ff:]
    act = (gate * (1.0 / (1.0 + jnp.exp(-gate))) * up).astype(BF16)
    y = y + _dot(act, wout_ref[...])
    o_ref[...] = _rmsnorm(y, gfin_ref[...]) if final_norm else y


def _out_ffn(x, attn_pairs, g_ffn, w_in, w_out, g_fin, tm, layer, final_norm):
    n, d = x.shape
    d_ff = w_out.shape[1]
    row = lambda w: pl.BlockSpec((tm, w), lambda i: (i, 0))
    slab = lambda w: pl.BlockSpec((None,) + w.shape[1:], lambda i: (layer, 0, 0), pipeline_mode=pl.Buffered(1))
    args, specs = [x], [row(d)]
    for o, w in attn_pairs:
        args += [o, w]
        specs += [row(o.shape[1]), _const_spec(w.shape)]
    args += [g_ffn, w_in, w_out, g_fin]
    specs += [_const_spec((1, d)), slab(w_in), slab(w_out), _const_spec((1, d))]
    return pl.pallas_call(
        functools.partial(_out_ffn_kernel, n_attn=len(attn_pairs), d_ff=d_ff, final_norm=final_norm),
        grid=(n // tm,),
        in_specs=specs,
        out_specs=row(d),
        out_shape=jax.ShapeDtypeStruct((n, d), F32),
        compiler_params=_params(("arbitrary",)),
        name="out_ffn",
    )(*args)


def _odd_proj_kernel(x_ref, g_ref, w_ref, q_ref, k_ref, v_ref, kt_ref, vt_ref, *, tail):
    h = _rmsnorm(x_ref[0], g_ref[...]).astype(BF16)
    wq = H_C * HEAD_DIM
    tm = h.shape[0]
    lane = lax.broadcasted_iota(jnp.int32, (tm, LANES), 1)
    low = lane < HEAD_DIM
    kv = _dot(h, w_ref[:, wq:])
    for src, dst, tail_ref in ((kv[:, :LANES], k_ref, kt_ref), (kv[:, LANES:], v_ref, vt_ref)):
        rolled = pltpu.roll(src, HEAD_DIM, 1)
        dst[0, :, :LANES] = jnp.where(low, src, rolled).astype(BF16)
        dst[0, :, LANES:] = jnp.where(low, rolled, src).astype(BF16)
        tail_ref[0] = src[tm - tail:, :]
    half = wq // 2
    for c in range(2):
        q_ref[0, :, half * c:half * (c + 1)] = (
            _dot(h, w_ref[:, half * c:half * (c + 1)]) * (SCALE * LOG2E)).astype(BF16)


def _odd_proj(x, g, w, tm):
    b, s, d = x.shape
    nt = s // tm
    lc = min(C_LEFT_CHUNKS * CHUNK, s)
    assert lc <= tm
    wq = H_C * HEAD_DIM
    row = lambda w_: pl.BlockSpec((1, tm, w_), lambda bi, ti: (bi, ti, 0))
    tail = pl.BlockSpec((1, lc, LANES), lambda bi, ti: (bi, 0, 0))
    return pl.pallas_call(
        functools.partial(_odd_proj_kernel, tail=lc),
        grid=(b, nt),
        in_specs=[row(d), _const_spec((1, d)), _const_spec(w.shape)],
        out_specs=(row(wq), row(2 * LANES), row(2 * LANES), tail, tail),
        out_shape=(jax.ShapeDtypeStruct((b, s, wq), BF16),
                   jax.ShapeDtypeStruct((b, s, 2 * LANES), BF16),
                   jax.ShapeDtypeStruct((b, s, 2 * LANES), BF16),
                   jax.ShapeDtypeStruct((b, lc, LANES), F32),
                   jax.ShapeDtypeStruct((b, lc, LANES), F32)),
        compiler_params=_params(("arbitrary", "arbitrary")),
        name="odd_proj",
    )(x, g, w)


def _sample_proj_kernel(*refs, gated):
    if gated:
        x_ref, g_ref, w_ref, wf_ref, bf_ref, p_ref, logf_ref = refs
    else:
        x_ref, g_ref, w_ref, p_ref = refs
    h = _rmsnorm(x_ref[...], g_ref[...]).astype(BF16)
    p_ref[...] = _dot(h, w_ref[...])
    if gated:
        logf_ref[...] = _log_sigmoid(_dot(h, wf_ref[...]) + bf_ref[...])


def _sample_proj(x, g, w, tm, w_f=None, b_f=None, cols=None):
    n, d = x.shape
    cols = w.shape[1] if cols is None else cols
    row = lambda w_: pl.BlockSpec((tm, w_), lambda i: (i, 0))
    in_specs = [row(d), _const_spec((1, d)), _const_spec((d, cols))]
    args = [x, g, w]
    out_specs = row(cols)
    out_shape = jax.ShapeDtypeStruct((n, cols), F32)
    if w_f is not None:
        in_specs += [_const_spec(w_f.shape), _const_spec((1, LANES))]
        args += [w_f, b_f]
        out_specs = (out_specs, row(LANES))
        out_shape = (out_shape, jax.ShapeDtypeStruct((n, LANES), F32))
    return pl.pallas_call(
        functools.partial(_sample_proj_kernel, gated=w_f is not None),
        grid=(n // tm,),
        in_specs=in_specs,
        out_specs=out_specs,
        out_shape=out_shape,
        compiler_params=_params(("arbitrary",)),
        name="sample_proj",
    )(*args)


def _block_diag_q(q, nheads, scale):
    t, w = q.shape
    tiled = jnp.concatenate([q] * nheads, axis=0)
    r = lax.broadcasted_iota(jnp.int32, (nheads * t, w), 0)
    c = lax.broadcasted_iota(jnp.int32, (nheads * t, w), 1)
    return jnp.where(r // t == c // HEAD_DIM, tiled * scale, 0.0).astype(BF16)


def _block_diag_extract(o_all, nheads, t):
    w = o_all.shape[1]
    c = lax.broadcasted_iota(jnp.int32, (t, w), 1)
    out = jnp.zeros((t, w), F32)
    for h in range(nheads):
        out = jnp.where(c // HEAD_DIM == h, o_all[h * t:(h + 1) * t, :], out)
    return out


def _heads_major(ref):
    _, nh, hd, nk = ref.shape
    return ref[0].reshape(nh * hd, nk).astype(BF16)


def _pad_rows(x, rows):
    return jnp.concatenate([x, jnp.zeros((rows - x.shape[0], x.shape[1]), x.dtype)], axis=0)


def _sample_even_kernel(qa_ref, kan_ref, van_ref, qb_ref, kbn_ref, vbn_ref, ck_ref, cv_ref, lft_ref,
                        cbk_ref, cbv_ref, e3_ref, biasb_ref, oa_ref, ob_ref,
                        bias_sc, m_sc, acc_sc, *, t, past, kc, nkc):
    c = pl.program_id(1)
    rows = H_A * t
    qbd = _block_diag_q(qa_ref[...], H_A, SCALE)

    @pl.when(c == 0)
    def _():
        cum = _cumsum_lanes(lft_ref[0])
        last = past + t - 1
        suffix = cum[:, last:last + 1] - cum
        hi, mid, lo = _split3(suffix)
        s3 = jnp.concatenate([hi, mid, lo, jnp.zeros((LANES - 3 * H_A, suffix.shape[1]), F32)], axis=0)
        s3 = s3.astype(BF16)
        for ch in range(nkc + 1):
            w = kc if ch < nkc else LANES
            bias_sc[ch, :, :w] = _dot(e3_ref[...], s3[:, ch * kc:ch * kc + w])
        m_sc[...] = jnp.full_like(m_sc, NEG_INF)
        acc_sc[...] = jnp.zeros_like(acc_sc)

    def update(s, pv):
        m = m_sc[...]
        m_new = jnp.maximum(m, jnp.max(s, axis=1, keepdims=True))
        p = jnp.exp(s - m_new)
        alpha = jnp.exp(m - m_new)
        acc = acc_sc[...]
        lsum = alpha * acc[:, W_A:W_A + 1] + jnp.sum(p, axis=1, keepdims=True)
        o = alpha * acc[:, :W_A] + pv(p.astype(BF16))
        acc_sc[:, :W_A] = o
        acc_sc[:, W_A:] = jnp.broadcast_to(lsum, (rows, LANES))
        m_sc[...] = m_new

    s = _dot(qbd, _heads_major(ck_ref)) + bias_sc[c]
    vt = _heads_major(cv_ref)
    update(s, lambda p: _dot_nt(p, vt))

    @pl.when(c == nkc - 1)
    def _():
        kn = _pad_rows(kan_ref[...], LANES).astype(BF16)
        vn = _pad_rows(van_ref[...], LANES).astype(BF16)
        sn = _dot_nt(qbd, kn) + bias_sc[nkc, :, :LANES]
        r = lax.broadcasted_iota(jnp.int32, sn.shape, 0)
        col = lax.broadcasted_iota(jnp.int32, sn.shape, 1)
        sn = jnp.where(col <= r % t, sn, NEG_INF)
        update(sn, lambda p: _dot(p, vn))
        acc = acc_sc[...]
        o_all = acc[:, :W_A] / acc[:, W_A:W_A + 1]
        oa_ref[...] = _block_diag_extract(o_all, H_A, t).astype(BF16)

        qbd_b = _block_diag_q(qb_ref[...], H_B, SCALE)
        lbk = cbk_ref.shape[3]
        kbn = _pad_rows(kbn_ref[...], LANES).astype(BF16)
        vbn = _pad_rows(vbn_ref[...], LANES).astype(BF16)
        sb = jnp.concatenate([_dot(qbd_b, _heads_major(cbk_ref)), _dot_nt(qbd_b, kbn)], axis=1) + biasb_ref[...]
        mb = jnp.max(sb, axis=1, keepdims=True)
        eb = jnp.exp(sb - mb)
        lb = jnp.sum(eb, axis=1, keepdims=True)
        eb = eb.astype(BF16)
        ob_all = (_dot_nt(eb[:, :lbk], _heads_major(cbv_ref)) + _dot(eb[:, lbk:], vbn)) / lb
        ob_ref[...] = _block_diag_extract(ob_all, H_B, t).astype(BF16)


def _sample_even(p, cache_k, cache_v, logf_t, cache_bk, cache_bv, e3, bias_b, *, t, kc):
    n = p.shape[0]
    nb = n // t
    past = cache_k.shape[3]
    nkc = past // kc
    lbk = cache_bk.shape[3]
    pcol = lambda j: pl.BlockSpec((t, W_A), lambda bi, c: (bi, j))
    rows = H_A * t
    return pl.pallas_call(
        functools.partial(_sample_even_kernel, t=t, past=past, kc=kc, nkc=nkc),
        grid=(nb, nkc),
        in_specs=[pcol(0), pcol(1), pcol(2), pcol(3), pcol(4), pcol(5),
                  pl.BlockSpec((1, H_A, HEAD_DIM, kc), lambda bi, c: (bi, 0, 0, c)),
                  pl.BlockSpec((1, H_A, HEAD_DIM, kc), lambda bi, c: (bi, 0, 0, c)),
                  pl.BlockSpec((1, H_A, logf_t.shape[2]), lambda bi, c: (bi, 0, 0)),
                  pl.BlockSpec((1, H_B, HEAD_DIM, lbk), lambda bi, c: (bi, 0, 0, 0)),
                  pl.BlockSpec((1, H_B, HEAD_DIM, lbk), lambda bi, c: (bi, 0, 0, 0)),
                  _const_spec(e3.shape), _const_spec(bias_b.shape)],
        out_specs=(pl.BlockSpec((t, W_A), lambda bi, c: (bi, 0)),
                   pl.BlockSpec((t, W_B), lambda bi, c: (bi, 0))),
        out_shape=(jax.ShapeDtypeStruct((n, W_A), BF16), jax.ShapeDtypeStruct((n, W_B), BF16)),
        scratch_shapes=[pltpu.VMEM((nkc + 1, rows, kc), F32),
                        pltpu.VMEM((rows, 1), F32),
                        pltpu.VMEM((rows, W_A + LANES), F32)],
        compiler_params=_params(("arbitrary", "arbitrary")),
        name="sample_even",
    )(p, p, p, p, p, p, cache_k, cache_v, logf_t, cache_bk, cache_bv, e3, bias_b)


def _sample_odd_kernel(q_ref, kn_ref, vn_ref, ck_ref, cv_ref, x_ref, bias_ref, sink_ref, o_ref, *, t, nbs):
    sk = sink_ref[...]
    for bb in range(nbs):
        rows = slice(bb * t, (bb + 1) * t)
        qbd = _block_diag_q(q_ref[rows, :], H_C, SCALE)
        kall = jnp.concatenate([ck_ref[bb], _pad_rows(kn_ref[rows, :], LANES)], axis=0).astype(BF16)
        vall = jnp.concatenate([cv_ref[bb], _pad_rows(vn_ref[rows, :], LANES)], axis=0).astype(BF16)
        kexp = _dot(kall, x_ref[...]).astype(BF16)
        vexp = _dot(vall, x_ref[...]).astype(BF16)
        s = _dot_nt(qbd, kexp) + bias_ref[...]
        m = jnp.maximum(jnp.max(s, axis=1, keepdims=True), sk)
        e = jnp.exp(s - m)
        l = jnp.sum(e, axis=1, keepdims=True) + jnp.exp(sk - m)
        o_all = _dot(e.astype(BF16), vexp) / l
        o_ref[rows, :] = _block_diag_extract(o_all, H_C, t).astype(BF16)


def _sample_odd(p, cache_k, cache_v, xexp, bias, sink_col, *, t, nbs):
    n = p.shape[0]
    nb = n // t
    wq = H_C * HEAD_DIM
    lc = cache_k.shape[1]
    rows = t * nbs
    return pl.pallas_call(
        functools.partial(_sample_odd_kernel, t=t, nbs=nbs),
        grid=(nb // nbs,),
        in_specs=[pl.BlockSpec((rows, wq), lambda bi: (bi, 0)),
                  pl.BlockSpec((rows, LANES), lambda bi: (bi, wq // LANES)),
                  pl.BlockSpec((rows, LANES), lambda bi: (bi, wq // LANES + 1)),
                  pl.BlockSpec((nbs, lc, LANES), lambda bi: (bi, 0, 0)),
                  pl.BlockSpec((nbs, lc, LANES), lambda bi: (bi, 0, 0)),
                  _const_spec(xexp.shape), _const_spec(bias.shape), _const_spec(sink_col.shape)],
        out_specs=pl.BlockSpec((rows, wq), lambda bi: (bi, 0)),
        out_shape=jax.ShapeDtypeStruct((n, wq), BF16),
        compiler_params=_params(("arbitrary",)),
        name="sample_odd",
    )(p, p, p, cache_k, cache_v, xexp, bias, sink_col)


def _t5_bucket(rel_mem):
    nb = T5_BUCKETS // 2
    max_exact = nb // 2
    n = jnp.abs(rel_mem)
    large = max_exact + (jnp.log(jnp.maximum(n, 1).astype(F32) / max_exact)
                         / math.log(T5_MAX_DIST / max_exact) * (nb - max_exact)).astype(jnp.int32)
    large = jnp.minimum(large, nb - 1)
    return jnp.where(rel_mem > 0, nb, 0) + jnp.where(n < max_exact, n, large)


def _bias_b_of_rel(table, rel):
    idx = np.clip(rel, -B_REL_CLIP, B_REL_CLIP) + B_REL_CLIP
    return table.astype(F32)[idx].T


def _bias_c_of_rel(table, rel):
    return table.astype(F32)[_t5_bucket(-jnp.asarray(rel, jnp.int32))].T


def _bias_tiles_kernel(f_ref, o_ref, *, tq, left, offsets, left_chunks, valid_cols):
    n = f_ref.shape[2]
    win = o_ref.shape[3]
    rows = jnp.broadcast_to(f_ref[0], (tq, n))
    wide = pltpu.roll(rows, n - (tq - 1), 1, stride=1, stride_axis=0)
    q = lax.broadcasted_iota(jnp.int32, (tq, win), 0)
    k = lax.broadcasted_iota(jnp.int32, (tq, win), 1)
    shift = CHUNK.bit_length() - 1
    for v, off in enumerate(offsets):
        if valid_cols is None:
            d = off // CHUNK + jnp.right_shift(q, shift) - jnp.right_shift(k, shift)
            ok = (d >= 0) & (d <= left_chunks)
        else:
            ok = k < valid_cols
        o_ref[v, 0] = jnp.where(ok, wide[:, left - off:left - off + win], NEG_INF)


def _bias_tiles(bias_of_rel, *, tq, left, win, offsets, left_chunks=None, valid_cols=None):
    assert CHUNK & (CHUNK - 1) == 0 and all((left - o) % LANES == 0 for o in offsets)
    ncols = win + left - min(offsets)
    n = tq + ncols - 1
    n_pad = -(-n // LANES) * LANES
    f_rev = bias_of_rel(tq - 1 + left - np.arange(n))
    nheads = f_rev.shape[0]
    f_rev = jnp.pad(f_rev, ((0, 0), (0, n_pad - n)))[:, None, :]
    return pl.pallas_call(
        functools.partial(_bias_tiles_kernel, tq=tq, left=left, offsets=tuple(offsets),
                          left_chunks=left_chunks, valid_cols=valid_cols),
        grid=(nheads,),
        in_specs=[pl.BlockSpec((1, 1, n_pad), lambda h: (h, 0, 0))],
        out_specs=pl.BlockSpec((len(offsets), 1, tq, win), lambda h: (0, h, 0, 0)),
        out_shape=jax.ShapeDtypeStruct((len(offsets), nheads, tq, win), F32),
        compiler_params=_params(("arbitrary",)),
        name="bias_tiles",
    )(f_rev)


def _sample_bias(bias_of_rel, t, cache_len):
    ncols = cache_len + LANES
    tiles = _bias_tiles(bias_of_rel, tq=t, left=cache_len, win=ncols, offsets=[cache_len],
                        valid_cols=cache_len + t)
    return tiles.reshape(-1, ncols)


def _cast_kernel(x_ref, o_ref):
    o_ref[...] = x_ref[...].astype(o_ref.dtype)


def _cast_bf16(w, rows):
    nl, r, c = w.shape
    w2 = w.reshape(nl * r, c)
    out = pl.pallas_call(
        _cast_kernel,
        grid=(nl * r // rows,),
        in_specs=[pl.BlockSpec((rows, c), lambda i: (i, 0))],
        out_specs=pl.BlockSpec((rows, c), lambda i: (i, 0)),
        out_shape=jax.ShapeDtypeStruct(w2.shape, BF16),
        compiler_params=_params(("arbitrary",)),
        name="cast_bf16",
    )(w2)
    return out.reshape(nl, r, c)


def _placement():
    place = np.zeros((LANES, H_A * LANES), np.float32)
    for h in range(H_A):
        for j in range(3):
            place[8 * j + h, LANES * h + HEAD_DIM + j] = 1.0
            place[8 * j + h, LANES * h + HEAD_DIM + 3 + j] = 1.0
    return jnp.asarray(place, BF16)


def kernel(x_prompt, x_sample, cache_a_k, cache_a_v, cache_a_logf, cache_b_k, cache_b_v, cache_c_k, cache_c_v,
           norm_mix, norm_ffn, norm_final, w_in_even, b_forget, rel_bias_b, w_out_even, w_in_odd, sinks_c,
           w_out_odd, t5_bias, w_ffn_in, w_ffn_out):
    b, s, d = x_prompt.shape
    nb, t, _ = x_sample.shape
    past = cache_a_k.shape[2]
    n_p, n_s = b * s, nb * t

    w_even = w_in_even[0]
    n_main = 3 * W_A + 3 * W_B
    w_main = w_even.astype(BF16)
    wf = w_even[:, n_main:]
    w_f = jnp.concatenate([wf, wf, wf, jnp.zeros((d, LANES - 3 * H_A), F32)], axis=1).astype(BF16)
    bf = b_forget[0].astype(F32)
    b_f = jnp.concatenate([bf, bf, bf, jnp.zeros((LANES - 3 * H_A,), F32)])[None, :]
    place = _placement()
    w_oe = w_out_even[0].astype(BF16)
    w_oo = w_out_odd[0].astype(BF16)
    w_odd = w_in_odd[0].astype(BF16)
    w_fi = _cast_bf16(w_ffn_in, rows=512)
    w_fo = _cast_bf16(w_ffn_out, rows=1408)
    g_mix = norm_mix.astype(F32)[:, None, :]
    g_ffn = norm_ffn.astype(F32)[:, None, :]
    g_fin = norm_final.astype(F32)[None, :]

    bias_b_of = functools.partial(_bias_b_of_rel, rel_bias_b[0])
    bias_c_of = functools.partial(_bias_c_of_rel, t5_bias)
    log2_domain = lambda bias_of: (lambda rel: bias_of(rel) * LOG2E)
    variant_offsets = lambda tsub, left: [min(v * tsub, left) for v in range(left // tsub + 1)]
    tq_b, tsub_b, left_b = 1024, 256, B_LEFT_CHUNKS * CHUNK
    bias_b = _bias_tiles(log2_domain(bias_b_of), tq=tsub_b, left=left_b, win=tsub_b + left_b,
                         offsets=variant_offsets(tsub_b, left_b), left_chunks=B_LEFT_CHUNKS)
    tq_c, tsub_c, left_c = 1024, 128, C_LEFT_CHUNKS * CHUNK
    bias_c = _bias_tiles(log2_domain(bias_c_of), tq=tsub_c, left=left_c, win=tsub_c + left_c,
                         offsets=variant_offsets(tsub_c, left_c), left_chunks=C_LEFT_CHUNKS)

    xp = x_prompt
    w_all = jnp.concatenate([w_main[:, 3 * W_A:3 * W_A + W_B], w_f, w_main[:, :3 * W_A],
                             w_main[:, 3 * W_A + W_B:n_main]], axis=1)
    qaug, kaug, vaug, ka, va, logf, qb, kb, vb, kbt, vbt = _even_proj(
        xp, g_mix[0], w_all, b_f, place, tm=512)
    oa = _fox(qaug, kaug, vaug, blk=512, nsub=2)
    ob = _band(qb, kb, vb, bias_b, None, tq=tq_b, left=left_b, npairs=H_B // 2, pairs_per_kv=1,
               mxu_rowsum=True, name="band_b")
    xp1 = _out_ffn(xp.reshape(n_p, d),
                   [(oa.reshape(n_p, W_A), w_oe[:W_A]), (ob.reshape(n_p, W_B), w_oe[W_A:])],
                   g_ffn[0], w_fi, w_fo, g_fin, tm=512, layer=0, final_norm=False)

    qc, kcd, vcd, kct, vct = _odd_proj(xp1.reshape(b, s, d), g_mix[1], w_odd, tm=512)
    oc = _band(qc, kcd, vcd, bias_c, sinks_c[0].astype(F32) * LOG2E, tq=tq_c, left=left_c, npairs=H_C // 2,
               pairs_per_kv=G_C // 2, mxu_rowsum=False, name="band_c")
    y_prompt = _out_ffn(xp1, [(oc.reshape(n_p, H_C * HEAD_DIM), w_oo)],
                        g_ffn[1], w_fi, w_fo, g_fin, tm=512, layer=1, final_norm=True)

    xs = x_sample.reshape(n_s, d)
    ps, logf_sp = _sample_proj(xs, g_mix[0], w_main, tm=256, w_f=w_f, b_f=b_f, cols=n_main)
    logf_s = logf_sp[:, :H_A]
    kpad = LANES * -(-(past + t) // LANES)
    logf_all = jnp.concatenate([cache_a_logf[0].astype(F32), logf_s.reshape(nb, t, H_A)], axis=1)
    logf_t = jnp.pad(jnp.swapaxes(logf_all, 1, 2), ((0, 0), (0, 0), (0, kpad - past - t)))
    lbs = cache_b_k.shape[2]
    bias_sb = _sample_bias(bias_b_of, t, lbs)
    e3 = np.zeros((H_A * t, LANES), np.float32)
    for j in range(3):
        e3[np.arange(H_A * t), 8 * j + np.arange(H_A * t) // t] = 1.0
    keys_minor = lambda cache: jnp.transpose(cache[0], (0, 2, 3, 1))
    oa_s, ob_s = _sample_even(ps, keys_minor(cache_a_k), keys_minor(cache_a_v), logf_t,
                              keys_minor(cache_b_k), keys_minor(cache_b_v),
                              jnp.asarray(e3, BF16), bias_sb, t=t, kc=4096)
    xs1 = _out_ffn(xs, [(oa_s, w_oe[:W_A]), (ob_s, w_oe[W_A:])], g_ffn[0], w_fi, w_fo, g_fin,
                   tm=256, layer=0, final_norm=False)

    ps2 = _sample_proj(xs1, g_mix[1], w_odd, tm=256)
    lcs = cache_c_k.shape[2]
    bias_sc = _sample_bias(bias_c_of, t, lcs)
    sink_col = jnp.repeat(sinks_c[0].astype(F32), t)[:, None]
    lane_head = np.arange(H_C * HEAD_DIM) // HEAD_DIM
    src_lane = (lane_head // G_C) * HEAD_DIM + np.arange(H_C * HEAD_DIM) % HEAD_DIM
    xexp = jnp.asarray(np.arange(LANES)[:, None] == src_lane[None, :], BF16)
    oc_s = _sample_odd(ps2, cache_c_k[0].reshape(nb, lcs, LANES), cache_c_v[0].reshape(nb, lcs, LANES),
                       xexp, bias_sc, sink_col, t=t, nbs=4)
    y_sample = _out_ffn(xs1, [(oc_s, w_oo)], g_ffn[1], w_fi, w_fo, g_fin, tm=256, layer=1, final_norm=True)

    wq = H_C * HEAD_DIM
    hd = lambda a, lead, h: a.reshape((1,) + lead + (h, HEAD_DIM))
    return (
        y_prompt.reshape(b, s, d), y_sample.reshape(nb, t, d),
        hd(ka, (b, s), H_A), hd(va, (b, s), H_A), logf[None],
        hd(kbt, (b, kbt.shape[1]), H_B), hd(vbt, (b, vbt.shape[1]), H_B),
        hd(kct, (b, kct.shape[1]), HKV_C), hd(vct, (b, vct.shape[1]), HKV_C),
        hd(ps[:, W_A:2 * W_A], (nb, t), H_A), hd(ps[:, 2 * W_A:3 * W_A], (nb, t), H_A),
        logf_s.reshape(1, nb, t, H_A),
        hd(ps[:, 3 * W_A + W_B:3 * W_A + 2 * W_B], (nb, t), H_B),
        hd(ps[:, 3 * W_A + 2 * W_B:3 * W_A + 3 * W_B], (nb, t), H_B),
        hd(ps2[:, wq:wq + LANES], (nb, t), HKV_C), hd(ps2[:, wq + LANES:wq + 2 * LANES], (nb, t), HKV_C),
    )
```
